```python
import math
import jax, jax.numpy as jnp
from jax import lax
import numpy as np

D_MODEL = 2048
BATCH = 1
SEQ = 8192
DEPTH = 1
DEC_BATCH = 32
DEC_SEQ = 1
PAST_LEN = 16384
PAGE_SIZE = 128

D_MIX = D_MODEL
D_SSD = D_MIX // 2
D_ATT = D_MIX - D_SSD
SSD_HEAD_DIM = 64
SSD_HEADS = D_SSD // SSD_HEAD_DIM
SSD_GROUPS = 2
D_STATE = 128
CONV_W = 4
CONV_DIM = D_SSD + 2 * SSD_GROUPS * D_STATE
SSD_CHUNK = 128
ATT_HEAD_DIM = 128
ATT_HEADS = D_ATT // ATT_HEAD_DIM
KV_HEADS = 2
Q_PER_KV = ATT_HEADS // KV_HEADS
IDX_HEADS = 16
IDX_DIM = 64
TOPK_MAX = 256
ATT_BLOCK = 128
N_EGROUPS = 4
EXPERTS_PER_GROUP = 8
N_EXPERTS = N_EGROUPS * EXPERTS_PER_GROUP
TOP_K_E = 2
D_EXPERT = 512
EPS = 1e-6
IN_SIZES = (D_SSD, CONV_DIM, SSD_HEADS, D_ATT, KV_HEADS * ATT_HEAD_DIM,
            KV_HEADS * ATT_HEAD_DIM, IDX_HEADS * IDX_DIM, IDX_DIM, IDX_HEADS)
D_IN_PROJ = sum(IN_SIZES)

kernel_name = 'hymba_ssd_dsa_hmoe_step'

F32 = jnp.float32


def _rmsnorm(x, w):
    xf = x.astype(F32)
    y = xf * lax.rsqrt(jnp.mean(xf * xf, axis=-1, keepdims=True) + EPS)
    return (y * w.astype(F32)).astype(x.dtype)


def _layernorm(x, w, b):
    xf = x.astype(F32)
    mu = jnp.mean(xf, axis=-1, keepdims=True)
    var = jnp.mean(jnp.square(xf - mu), axis=-1, keepdims=True)
    y = (xf - mu) * lax.rsqrt(var + EPS)
    return (y * w.astype(F32) + b.astype(F32)).astype(x.dtype)


def _adaln(c, w_ada, b_ada):
    mod = jax.nn.silu(c) @ w_ada + b_ada
    return jnp.split(mod[:, None, :], 6, axis=-1)


def _split_proj(h, w_in):
    points = [int(s) for s in np.cumsum(IN_SIZES)[:-1]]
    return jnp.split(h @ w_in, points, axis=-1)


def _causal_dwconv(xbc, prev, w, b):
    L = xbc.shape[1]
    xp = jnp.concatenate([prev.astype(xbc.dtype), xbc], axis=1)
    out = b
    for k in range(CONV_W):
        out = out + w[k] * xp[:, k:k + L]
    return jax.nn.silu(out), xp[:, L:]


def _ssd_prepare(xbc, dt_raw, conv_prev, p):
    b, L, _ = xbc.shape
    xbc_c, conv_new = _causal_dwconv(xbc, conv_prev, p['conv_w'], p['conv_b'])
    xbc_c = xbc_c.astype(F32)
    gn = SSD_GROUPS * D_STATE
    xs = xbc_c[..., :D_SSD].reshape(b, L, SSD_HEADS, SSD_HEAD_DIM)
    Bm = xbc_c[..., D_SSD:D_SSD + gn].reshape(b, L, SSD_GROUPS, D_STATE)
    Cm = xbc_c[..., D_SSD + gn:].reshape(b, L, SSD_GROUPS, D_STATE)
    dt = jax.nn.softplus(dt_raw.astype(F32) + p['dt_bias'].astype(F32))
    A = -jnp.exp(p['a_log'].astype(F32))
    return xs, dt, A, Bm, Cm, conv_new


def _ssd_chunked(xs, dt, A, Bm, Cm, h0):
    b, L = xs.shape[:2]
    Q = min(SSD_CHUNK, L)
    nc = L // Q
    rep = SSD_HEADS // SSD_GROUPS
    Bh = jnp.repeat(Bm, rep, axis=2).reshape(b, nc, Q, SSD_HEADS, D_STATE)
    Ch = jnp.repeat(Cm, rep, axis=2).reshape(b, nc, Q, SSD_HEADS, D_STATE)
    x = xs.reshape(b, nc, Q, SSD_HEADS, SSD_HEAD_DIM)
    dtc = dt.reshape(b, nc, Q, SSD_HEADS)
    a_h = jnp.moveaxis(jnp.cumsum(dtc * A, axis=2), 2, 3)
    dt_h = jnp.moveaxis(dtc, 2, 3)
    tril = jnp.tril(jnp.ones((Q, Q), dtype=bool))
    decay = jnp.exp(jnp.where(tril, a_h[..., :, None] - a_h[..., None, :], -jnp.inf))
    scores = jnp.einsum('bcqhn,bcshn->bchqs', Ch, Bh) * decay * dt_h[..., None, :]
    y_diag = jnp.einsum('bchqs,bcshp->bcqhp', scores, x)
    w_state = jnp.exp(a_h[..., -1:] - a_h) * dt_h
    states = jnp.einsum('bcshn,bchs,bcshp->bchpn', Bh, w_state, x)
    chunk_decay = jnp.exp(a_h[..., -1])

    def step(hc, inp):
        dec, st = inp
        return dec[..., None, None] * hc + st, hc

    h_fin, h_start = lax.scan(step, h0, (jnp.moveaxis(chunk_decay, 1, 0), jnp.moveaxis(states, 1, 0)))
    h_start = jnp.moveaxis(h_start, 0, 1)
    y_off = jnp.einsum('bcqhn,bchpn,bchq->bcqhp', Ch, h_start, jnp.exp(a_h))
    y = (y_diag + y_off).reshape(b, L, SSD_HEADS, SSD_HEAD_DIM)
    return y, h_fin


def _ssd_recurrent(xs, dt, A, Bm, Cm, h0):
    rep = SSD_HEADS // SSD_GROUPS
    Bh = jnp.repeat(Bm, rep, axis=2)
    Ch = jnp.repeat(Cm, rep, axis=2)

    def step(hc, inp):
        x_t, dt_t, B_t, C_t = inp
        hc = jnp.exp(dt_t * A)[..., None, None] * hc + dt_t[..., None, None] * x_t[..., None] * B_t[:, :, None, :]
        return hc, jnp.einsum('bhpn,bhn->bhp', hc, C_t)

    h_fin, ys = lax.scan(step, h0, (jnp.moveaxis(xs, 1, 0), jnp.moveaxis(dt, 1, 0),
                                    jnp.moveaxis(Bh, 1, 0), jnp.moveaxis(Ch, 1, 0)))
    return jnp.moveaxis(ys, 0, 1), h_fin


def _ssd_finish(y, xs, z, p, dtype):
    b, L = y.shape[:2]
    y = y + xs * p['d_skip'].astype(F32)[:, None]
    y = y.reshape(b, L, D_SSD) * jax.nn.silu(z.astype(F32))
    return _rmsnorm(y, p['norm_ssd_w']).astype(dtype)


def _attn_prepare(q, k, v, qi, ki, wi, p):
    b, L = q.shape[:2]
    q = q.reshape(b, L, KV_HEADS, Q_PER_KV, ATT_HEAD_DIM)
    k = k.reshape(b, L, KV_HEADS, ATT_HEAD_DIM)
    v = v.reshape(b, L, KV_HEADS, ATT_HEAD_DIM)
    qi = qi.reshape(b, L, IDX_HEADS, IDX_DIM)
    ki = _layernorm(ki, p['ln_kidx_w'], p['ln_kidx_b'])
    return q, k, v, qi, ki, wi


def _indexer_scores(qi, wi, ki):
    dots = jnp.einsum('bqhd,bsd->bqhs', qi.astype(F32), ki.astype(F32)) * (IDX_DIM ** -0.5)
    return jnp.einsum('bqhs,bqh->bqs', jax.nn.relu(dots), wi.astype(F32) * (IDX_HEADS ** -0.5))


def _attend_selected(q, k_sel, v_sel, valid):
    b, T = q.shape[:2]
    logits = jnp.einsum('bqkgd,bqskd->bqkgs', q, k_sel).astype(F32) * (ATT_HEAD_DIM ** -0.5)
    logits = jnp.where(valid[:, :, None, None, :], logits, -jnp.inf)
    probs = jax.nn.softmax(logits, axis=-1).astype(v_sel.dtype)
    o = jnp.einsum('bqkgs,bqskd->bqkgd', probs, v_sel)
    return o.reshape(b, T, D_ATT)


def _attn_prompt(q, k, v, qi, ki, wi):
    b, L = q.shape[:2]
    topk = min(TOPK_MAX, L // 4)
    nb = L // ATT_BLOCK
    bidx = jnp.arange(b)[:, None, None]
    key_pos = jnp.arange(L, dtype=jnp.int32)

    def blk(args):
        qb, qib, wib, start = args
        pos = start + jnp.arange(ATT_BLOCK, dtype=jnp.int32)
        scores = _indexer_scores(qib, wib, ki)
        scores = jnp.where(key_pos[None, None, :] <= pos[None, :, None], scores, -jnp.inf)
        _, idx = lax.top_k(scores, topk)
        valid = idx <= pos[None, :, None]
        return _attend_selected(qb, k[bidx, idx], v[bidx, idx], valid)

    def blocks(a):
        return jnp.moveaxis(a.reshape((b, nb, ATT_BLOCK) + a.shape[2:]), 1, 0)

    starts = jnp.arange(nb, dtype=jnp.int32) * ATT_BLOCK
    out = lax.map(blk, (blocks(q), blocks(qi), blocks(wi), starts))
    return jnp.moveaxis(out, 0, 1).reshape(b, L, D_ATT)


def _attn_sample(q, k_new, v_new, qi, ki_new, wi, cache_k, cache_v, cache_ki, page_table):
    db, ds = q.shape[:2]
    n_pages = page_table.shape[1]
    L = PAST_LEN + ds
    topk = min(TOPK_MAX, L // 4)
    ki_past = cache_ki[page_table].reshape(db, PAST_LEN, IDX_DIM)
    ki_all = jnp.concatenate([ki_past.astype(ki_new.dtype), ki_new], axis=1)
    pos = PAST_LEN + jnp.arange(ds, dtype=jnp.int32)
    key_pos = jnp.arange(L, dtype=jnp.int32)
    scores = _indexer_scores(qi, wi, ki_all)
    scores = jnp.where(key_pos[None, None, :] <= pos[None, :, None], scores, -jnp.inf)
    _, idx = lax.top_k(scores, topk)
    valid = idx <= pos[None, :, None]
    bidx = jnp.arange(db)[:, None, None]
    is_past = (idx < PAST_LEN)[..., None, None]
    phys = page_table[bidx, jnp.minimum(idx // PAGE_SIZE, n_pages - 1)]
    off = idx % PAGE_SIZE
    new_i = jnp.clip(idx - PAST_LEN, 0, ds - 1)
    k_sel = jnp.where(is_past, cache_k[phys, off].astype(k_new.dtype), k_new[bidx, new_i])
    v_sel = jnp.where(is_past, cache_v[phys, off].astype(v_new.dtype), v_new[bidx, new_i])
    return _attend_selected(q, k_sel, v_sel, valid)


def _mixer_prompt(h, p):
    z, xbc, dt_raw, q, k, v, qi, ki, wi = _split_proj(h, p['w_in'])
    b = h.shape[0]
    conv0 = jnp.zeros((b, CONV_W - 1, CONV_DIM), h.dtype)
    xs, dt, A, Bm, Cm, conv_new = _ssd_prepare(xbc, dt_raw, conv0, p)
    h0 = jnp.zeros((b, SSD_HEADS, SSD_HEAD_DIM, D_STATE), F32)
    y, h_fin = _ssd_chunked(xs, dt, A, Bm, Cm, h0)
    y_ssd = _ssd_finish(y, xs, z, p, h.dtype)
    q, k, v, qi, ki, wi = _attn_prepare(q, k, v, qi, ki, wi, p)
    y_att = _rmsnorm(_attn_prompt(q, k, v, qi, ki, wi), p['norm_att_w'])
    out = jnp.concatenate([y_ssd, y_att], axis=-1) @ p['w_out']
    return out, (k, v, ki, conv_new, h_fin)


def _mixer_sample(h, cache_k, cache_v, cache_ki, conv_prev, ssm_prev, page_table, p):
    z, xbc, dt_raw, q, k, v, qi, ki, wi = _split_proj(h, p['w_in'])
    xs, dt, A, Bm, Cm, conv_new = _ssd_prepare(xbc, dt_raw, conv_prev, p)
    y, h_fin = _ssd_recurrent(xs, dt, A, Bm, Cm, ssm_prev.astype(F32))
    y_ssd = _ssd_finish(y, xs, z, p, h.dtype)
    q, k, v, qi, ki, wi = _attn_prepare(q, k, v, qi, ki, wi, p)
    y_att = _rmsnorm(_attn_sample(q, k, v, qi, ki, wi, cache_k, cache_v, cache_ki, page_table), p['norm_att_w'])
    out = jnp.concatenate([y_ssd, y_att], axis=-1) @ p['w_out']
    return out, (k, v, ki, conv_new, h_fin)


def _hier_moe(h, p):
    shape = h.shape
    x = h.reshape(-1, D_MODEL)
    T = x.shape[0]
    lg = (x @ p['w_router_g'] + p['b_router_g']).astype(F32)
    pg = jax.nn.softmax(lg, axis=-1)
    gsel = jnp.argmax(lg, axis=-1)
    pgsel = jnp.take_along_axis(pg, gsel[:, None], axis=1)
    le = (x @ p['w_router_e'] + p['b_router_e']).astype(F32).reshape(T, N_EGROUPS, EXPERTS_PER_GROUP)
    le = jnp.take_along_axis(le, gsel[:, None, None], axis=1)[:, 0]
    pe = jax.nn.softmax(le, axis=-1)
    top_p, top_i = lax.top_k(pe, TOP_K_E)
    wts = pgsel * top_p / jnp.sum(top_p, axis=-1, keepdims=True)
    eid = gsel[:, None] * EXPERTS_PER_GROUP + top_i
    comb = jnp.sum(jax.nn.one_hot(eid, N_EXPERTS, dtype=F32) * wts[..., None], axis=1).astype(x.dtype)
    y = jnp.zeros_like(x)
    for e in range(N_EXPERTS):
        g, u = jnp.split(x @ p['w_exp_up'][e], 2, axis=-1)
        y = y + comb[:, e:e + 1] * ((jax.nn.silu(g) * u) @ p['w_exp_down'][e])
    return y.reshape(shape)


def _block(x, c, p, mixer):
    sh1, sc1, g1, sh2, sc2, g2 = _adaln(c, p['w_ada'], p['b_ada'])
    h = _rmsnorm(x, p['norm1_w']) * (1 + sc1) + sh1
    m, st = mixer(h)
    x = x + g1 * m
    h = _rmsnorm(x, p['norm2_w']) * (1 + sc2) + sh2
    x = x + g2 * _hier_moe(h, p)
    return x, st


def setup_inputs(seed: int = 0) -> dict:
    key = jax.random.key(seed)
    ks = jax.random.split(key, 40)
    n_pages = PAST_LEN // PAGE_SIZE
    n_used = DEC_BATCH * n_pages
    n_pool = n_used + max(1, n_used // 4)

    def nrm(k, shape, s):
        return s * jax.random.normal(k, shape, F32)

    page_table = jax.random.permutation(ks[0], n_pool)[:n_used].reshape(DEC_BATCH, n_pages).astype(jnp.int32)
    dt0 = jnp.exp(jax.random.uniform(ks[1], (DEPTH, SSD_HEADS), F32, math.log(1e-3), math.log(1e-1)))
    dt_bias = dt0 + jnp.log(-jnp.expm1(-dt0))
    a_log = jnp.log(jax.random.uniform(ks[2], (DEPTH, SSD_HEADS), F32, 1.0, 16.0))
    return {
        'x_prompt': nrm(ks[3], (BATCH, SEQ, D_MODEL), 1.0),
        'x_sample': nrm(ks[4], (DEC_BATCH, DEC_SEQ, D_MODEL), 1.0),
        'cache_k': nrm(ks[5], (DEPTH, n_pool, PAGE_SIZE, KV_HEADS, ATT_HEAD_DIM), 1.0),
        'cache_v': nrm(ks[6], (DEPTH, n_pool, PAGE_SIZE, KV_HEADS, ATT_HEAD_DIM), 1.0),
        'cache_k_idx': nrm(ks[7], (DEPTH, n_pool, PAGE_SIZE, IDX_DIM), 1.0),
        'state_conv': nrm(ks[8], (DEPTH, DEC_BATCH, CONV_W - 1, CONV_DIM), 1.0),
        'state_ssm': nrm(ks[9], (DEPTH, DEC_BATCH, SSD_HEADS, SSD_HEAD_DIM, D_STATE), 0.1),
        'page_table': page_table,
        'c_prompt': nrm(ks[10], (BATCH, D_MODEL), 1.0),
        'c_sample': nrm(ks[11], (DEC_BATCH, D_MODEL), 1.0),
        'w_ada': nrm(ks[12], (DEPTH, D_MODEL, 6 * D_MODEL), 0.5 * D_MODEL ** -0.5),
        'b_ada': nrm(ks[13], (DEPTH, 6 * D_MODEL), 0.01),
        'norm1_w': 1.0 + nrm(ks[14], (DEPTH, D_MODEL), 0.01),
        'norm2_w': 1.0 + nrm(ks[15], (DEPTH, D_MODEL), 0.01),
        'w_in': nrm(ks[16], (DEPTH, D_MODEL, D_IN_PROJ), D_MODEL ** -0.5),
        'conv_w': nrm(ks[17], (DEPTH, CONV_W, CONV_DIM), CONV_W ** -0.5),
        'conv_b': nrm(ks[18], (DEPTH, CONV_DIM), 0.01),
        'dt_bias': dt_bias,
        'a_log': a_log,
        'd_skip': 1.0 + nrm(ks[19], (DEPTH, SSD_HEADS), 0.1),
        'norm_ssd_w': 1.0 + nrm(ks[20], (DEPTH, D_SSD), 0.01),
        'ln_kidx_w': 1.0 + nrm(ks[21], (DEPTH, IDX_DIM), 0.01),
        'ln_kidx_b': nrm(ks[22], (DEPTH, IDX_DIM), 0.01),
        'norm_att_w': 1.0 + nrm(ks[23], (DEPTH, D_ATT), 0.01),
        'w_out': nrm(ks[24], (DEPTH, D_MIX, D_MODEL), D_MIX ** -0.5),
        'w_router_g': nrm(ks[25], (DEPTH, D_MODEL, N_EGROUPS), D_MODEL ** -0.5),
        'b_router_g': nrm(ks[26], (DEPTH, N_EGROUPS), 0.01),
        'w_router_e': nrm(ks[27], (DEPTH, D_MODEL, N_EXPERTS), D_MODEL ** -0.5),
        'b_router_e': nrm(ks[28], (DEPTH, N_EXPERTS), 0.01),
        'w_exp_up': nrm(ks[29], (DEPTH, N_EXPERTS, D_MODEL, 2 * D_EXPERT), D_MODEL ** -0.5),
        'w_exp_down': nrm(ks[30], (DEPTH, N_EXPERTS, D_EXPERT, D_MODEL), D_EXPERT ** -0.5),
        'norm_f_w': 1.0 + nrm(ks[31], (D_MODEL,), 0.01),
    }


def reference(x_prompt, x_sample, cache_k, cache_v, cache_k_idx, state_conv, state_ssm, page_table,
              c_prompt, c_sample, w_ada, b_ada, norm1_w, norm2_w, w_in, conv_w, conv_b, dt_bias,
              a_log, d_skip, norm_ssd_w, ln_kidx_w, ln_kidx_b, norm_att_w, w_out, w_router_g,
              b_router_g, w_router_e, b_router_e, w_exp_up, w_exp_down, norm_f_w):
    yp, ys = x_prompt, x_sample
    kp_l, vp_l, kip_l, cp_l, sp_l = [], [], [], [], []
    ks_l, vs_l, kis_l, cs_l, ss_l = [], [], [], [], []
    for l in range(DEPTH):
        p = {'w_ada': w_ada[l], 'b_ada': b_ada[l], 'norm1_w': norm1_w[l], 'norm2_w': norm2_w[l],
             'w_in': w_in[l], 'conv_w': conv_w[l], 'conv_b': conv_b[l], 'dt_bias': dt_bias[l],
             'a_log': a_log[l], 'd_skip': d_skip[l], 'norm_ssd_w': norm_ssd_w[l],
             'ln_kidx_w': ln_kidx_w[l], 'ln_kidx_b': ln_kidx_b[l], 'norm_att_w': norm_att_w[l],
             'w_out': w_out[l], 'w_router_g': w_router_g[l], 'b_router_g': b_router_g[l],
             'w_router_e': w_router_e[l], 'b_router_e': b_router_e[l],
             'w_exp_up': w_exp_up[l], 'w_exp_down': w_exp_down[l]}
        yp, (k_p, v_p, ki_p, c_p, s_p) = _block(yp, c_prompt, p, lambda h, p=p: _mixer_prompt(h, p))
        ys, (k_s, v_s, ki_s, c_s, s_s) = _block(
            ys, c_sample, p,
            lambda h, p=p, l=l: _mixer_sample(h, cache_k[l], cache_v[l], cache_k_idx[l],
                                              state_conv[l], state_ssm[l], page_table, p))
        kp_l.append(k_p); vp_l.append(v_p); kip_l.append(ki_p); cp_l.append(c_p); sp_l.append(s_p)
        ks_l.append(k_s); vs_l.append(v_s); kis_l.append(ki_s); cs_l.append(c_s); ss_l.append(s_s)
    y_prompt = _rmsnorm(yp, norm_f_w)
    y_sample = _rmsnorm(ys, norm_f_w)
    return (y_prompt, y_sample,
            jnp.stack(kp_l), jnp.stack(vp_l), jnp.stack(kip_l), jnp.stack(cp_l), jnp.stack(sp_l),
            jnp.stack(ks_l), jnp.stack(vs_l), jnp.stack(kis_l), jnp.stack(cs_l), jnp.stack(ss_l))
```

```python
import functools

import numpy as np
import jax
import jax.numpy as jnp
from jax import lax
from jax.experimental import pallas as pl
from jax.experimental.pallas import tpu as pltpu

F32 = jnp.float32
BF16 = jnp.bfloat16
I32 = jnp.int32

SSD_HEAD_DIM = 64
SSD_GROUPS = 2
D_STATE = 128
CONV_W = 4
SSD_CHUNK = 128
ATT_HEAD_DIM = 128
KV_HEADS = 2
IDX_HEADS = 16
IDX_DIM = 64
TOPK_MAX = 256
N_EGROUPS = 4
EXPERTS_PER_GROUP = 8
N_EXPERTS = N_EGROUPS * EXPERTS_PER_GROUP
D_EXPERT = 512
EPS = 1e-6

LANES = 128
INT_MIN = -2 ** 31
NEG_BIG = -1e30
VMEM_LIMIT = 56 * 1024 * 1024
HIGHEST = lax.Precision.HIGHEST


def _cparams(sem):
    return pltpu.CompilerParams(dimension_semantics=sem, vmem_limit_bytes=VMEM_LIMIT)


def _dot(a, b, precision=None):
    return jnp.dot(a, b, preferred_element_type=F32, precision=precision)


def _dot_nt(a, b, precision=None):
    return lax.dot_general(a, b, (((1,), (1,)), ((), ())), preferred_element_type=F32, precision=precision)


def _silu(x):
    return x * jax.nn.sigmoid(x)


def _softplus(x):
    return jnp.maximum(x, 0.0) + jnp.log(1.0 + jnp.exp(-jnp.abs(x)))


def _rms(x):
    return x * lax.rsqrt(jnp.mean(x * x, axis=-1, keepdims=True) + EPS)


def _pad_cols(a, width):
    return jnp.pad(a, ((0, 0), (0, width - a.shape[1])))


def _ada_kernel(c_ref, w_ref, b_ref, o_ref):
    s = _silu(c_ref[...]).astype(BF16)
    o_ref[...] = _dot(s, w_ref[...].astype(BF16)) + b_ref[...]


def _ada_mod(c_all, w_ada, b_ada):
    r, d = c_all.shape
    n = w_ada.shape[1]
    tn = 1024
    return pl.pallas_call(
        _ada_kernel,
        grid=(n // tn,),
        in_specs=[pl.BlockSpec((r, d), lambda j: (0, 0)),
                  pl.BlockSpec((d, tn), lambda j: (0, j)),
                  pl.BlockSpec((1, tn), lambda j: (0, j))],
        out_specs=pl.BlockSpec((r, tn), lambda j: (0, j)),
        out_shape=jax.ShapeDtypeStruct((r, n), F32),
        compiler_params=_cparams(("arbitrary",)),
    )(c_all, w_ada, b_ada.reshape(1, n))


def _in_layout(d_model):
    d_ssd = d_model // 2
    d_att = d_model - d_ssd
    conv_dim = d_ssd + 2 * SSD_GROUPS * D_STATE
    ssd_heads = d_ssd // SSD_HEAD_DIM
    sizes = dict(z=d_ssd, xbc=conv_dim, dt=ssd_heads, q=d_att, k=KV_HEADS * ATT_HEAD_DIM,
                 v=KV_HEADS * ATT_HEAD_DIM, qi=IDX_HEADS * IDX_DIM, ki=IDX_DIM, wi=IDX_HEADS)
    order = ("z", "xbc", "dt", "q", "k", "v", "qi", "ki", "wi")
    src, dst, off_s, off_d = {}, {}, 0, 0
    for name in order:
        w = sizes[name]
        wp = -(-w // LANES) * LANES
        src[name] = (off_s, w)
        dst[name] = (off_d, wp)
        off_s += w
        off_d += wp
    return order, src, dst, off_d


def _perm_w_in(w_in):
    order, src, dst, _ = _in_layout(w_in.shape[0])
    parts = [_pad_cols(w_in[:, src[n][0]:src[n][0] + src[n][1]], dst[n][1]) for n in order]
    return jnp.concatenate(parts, axis=1).astype(BF16)


def _inproj_kernel(seg, x_ref, nw_ref, sc_ref, sh_ref, w_ref, lnw_ref, lnb_ref,
                   z_ref, xbc_ref, dt_ref, q_ref, k_ref, v_ref, kb_ref, vb_ref, qi_ref, ki_ref, wi_ref):
    h = _rms(x_ref[...]) * nw_ref[...]
    h = h * (1.0 + sc_ref[...]) + sh_ref[...]
    hb = h.astype(BF16)

    def mm(name):
        a, w = seg[name]
        return _dot(hb, w_ref[:, a:a + w])

    z_ref[...] = mm("z")
    xbc_ref[...] = mm("xbc")
    dt_ref[...] = mm("dt")
    q_ref[...] = mm("q").astype(BF16)
    k = mm("k")
    k_ref[...] = k
    kb_ref[...] = k.astype(BF16)
    v = mm("v")
    v_ref[...] = v
    vb_ref[...] = v.astype(BF16)
    qi_ref[...] = mm("qi").astype(BF16)
    wi_ref[...] = mm("wi")
    ki = mm("ki")
    lane = lax.broadcasted_iota(I32, ki.shape, 1)
    ok = lane < IDX_DIM
    mu = jnp.sum(jnp.where(ok, ki, 0.0), axis=-1, keepdims=True) * (1.0 / IDX_DIM)
    cen = jnp.where(ok, ki - mu, 0.0)
    var = jnp.sum(cen * cen, axis=-1, keepdims=True) * (1.0 / IDX_DIM)
    y = cen * lax.rsqrt(var + EPS) * lnw_ref[...] + lnb_ref[...]
    ki_ref[...] = y[:, :IDX_DIM]


def _in_proj(x, nw, sc, sh, w_perm, lnw, lnb, tm):
    t, d = x.shape
    _, _, dst, npad = _in_layout(d)
    tmod = sc.shape[0]
    mod_map = (lambda i: (0, 0)) if tmod == 1 else (lambda i: (i, 0))
    mod_rows = 1 if tmod == 1 else tm
    row = lambda w: pl.BlockSpec((tm, w), lambda i: (i, 0))
    d_ssd, d_att = dst["z"][1], dst["q"][1]
    kvw = KV_HEADS * ATT_HEAD_DIM
    outs = [("z", d_ssd, F32), ("xbc", dst["xbc"][1], F32), ("dt", LANES, F32), ("q", d_att, BF16),
            ("k", kvw, F32), ("v", kvw, F32), ("kb", kvw, BF16), ("vb", kvw, BF16),
            ("qi", IDX_HEADS * IDX_DIM, BF16), ("ki", IDX_DIM, F32), ("wi", LANES, F32)]
    res = pl.pallas_call(
        functools.partial(_inproj_kernel, dst),
        grid=(t // tm,),
        in_specs=[row(d),
                  pl.BlockSpec((1, d), lambda i: (0, 0)),
                  pl.BlockSpec((mod_rows, d), mod_map),
                  pl.BlockSpec((mod_rows, d), mod_map),
                  pl.BlockSpec((d, npad), lambda i: (0, 0)),
                  pl.BlockSpec((1, LANES), lambda i: (0, 0)),
                  pl.BlockSpec((1, LANES), lambda i: (0, 0))],
        out_specs=[row(w) for _, w, _ in outs],
        out_shape=[jax.ShapeDtypeStruct((t, w), dt) for _, w, dt in outs],
        compiler_params=_cparams(("arbitrary",)),
    )(x, nw, sc, sh, w_perm, lnw, lnb)
    return dict(zip([n for n, _, _ in outs], res))


def _ssd_prompt_kernel(n_pairs, xbc_ref, dt_ref, z_ref, cw_ref, cb_ref, dtb_ref, alog_ref, dsk_ref, nw_ref,
                       y_ref, st_ref, xprev, ht, ybuf):
    c = pl.program_id(0)
    q = SSD_CHUNK
    d_ssd = n_pairs * LANES
    gn = SSD_GROUPS * D_STATE

    @pl.when(c == 0)
    def _():
        xprev[...] = jnp.zeros_like(xprev)
        ht[...] = jnp.zeros_like(ht)

    x = xbc_ref[...]
    xp = xprev[...]
    rowi = lax.broadcasted_iota(I32, (q, 1), 0)
    acc = cb_ref[...] + cw_ref[CONV_W - 1:CONV_W, :] * x
    for k in range(1, CONV_W):
        sh = jnp.where(rowi < k, pltpu.roll(xp, k, 0), pltpu.roll(x, k, 0))
        acc = acc + cw_ref[CONV_W - 1 - k:CONV_W - k, :] * sh
    xprev[...] = x
    xc = _silu(acc)

    dt = _softplus(dt_ref[...] + dtb_ref[...])
    a_neg = -jnp.exp(alog_ref[...])
    r2 = lax.broadcasted_iota(I32, (q, q), 0)
    c2 = lax.broadcasted_iota(I32, (q, q), 1)
    tril = c2 <= r2
    a = _dot(tril.astype(F32), dt * a_neg, precision=HIGHEST)
    a_t = a.T
    dt_t = dt.T
    a_last = a[q - 1:q, :]
    wmat = jnp.exp(a_last - a) * dt
    emat = jnp.exp(a)
    cd = jnp.exp(a_last)
    lane = lax.broadcasted_iota(I32, (q, LANES), 1)
    left = lane < SSD_HEAD_DIM
    pairs_per_group = n_pairs // SSD_GROUPS

    bts, cbs, cgs = [], [], []
    for g in range(SSD_GROUPS):
        bg = xc[:, d_ssd + g * D_STATE:d_ssd + (g + 1) * D_STATE]
        cg = xc[:, d_ssd + gn + g * D_STATE:d_ssd + gn + (g + 1) * D_STATE].astype(BF16)
        bt = bg.T.astype(BF16)
        bts.append(bt)
        cgs.append(cg)
        cbs.append(_dot(cg, bt))

    def colb(m, h):
        return jnp.broadcast_to(m[:, h:h + 1], (q, LANES))

    for p in range(n_pairs):
        g = p // pairs_per_group
        h0, h1 = 2 * p, 2 * p + 1
        xpair = xc[:, p * LANES:(p + 1) * LANES]
        xpb = xpair.astype(BF16)
        yd = []
        for h in (h0, h1):
            diff = colb(a, h) - a_t[h:h + 1, :]
            decay = jnp.exp(jnp.where(tril, diff, -jnp.inf))
            sc = cbs[g] * decay * dt_t[h:h + 1, :]
            yd.append(_dot(sc.astype(BF16), xpb))
        y_diag = jnp.where(left, yd[0], yd[1])
        w_pair = jnp.where(left, colb(wmat, h0), colb(wmat, h1))
        e_pair = jnp.where(left, colb(emat, h0), colb(emat, h1))
        cd_pair = jnp.where(left[0:1, :], jnp.broadcast_to(cd[:, h0:h0 + 1], (1, LANES)),
                            jnp.broadcast_to(cd[:, h1:h1 + 1], (1, LANES)))
        hprev = ht[p]
        y_off = _dot(cgs[g], hprev.astype(BF16)) * e_pair
        states = _dot(bts[g], (xpair * w_pair).astype(BF16))
        ht[p] = hprev * cd_pair + states
        ybuf[:, p * LANES:(p + 1) * LANES] = y_diag + y_off + xpair * dsk_ref[:, p * LANES:(p + 1) * LANES]

    y = ybuf[...] * _silu(z_ref[...])
    y_ref[...] = (_rms(y) * nw_ref[...]).astype(BF16)

    @pl.when(c == pl.num_programs(0) - 1)
    def _():
        for p in range(n_pairs):
            st_ref[p * LANES:(p + 1) * LANES, :] = ht[p].T


def _ssd_prompt(xbc, dt_raw, z, conv_w, conv_b, dt_bias_p, a_log_p, dskip_row, norm_w):
    t, conv_dim = xbc.shape
    d_ssd = z.shape[1]
    n_pairs = d_ssd // LANES
    q = SSD_CHUNK
    full = lambda a: pl.BlockSpec(a.shape, lambda c: (0, 0))
    return pl.pallas_call(
        functools.partial(_ssd_prompt_kernel, n_pairs),
        grid=(t // q,),
        in_specs=[pl.BlockSpec((q, conv_dim), lambda c: (c, 0)),
                  pl.BlockSpec((q, LANES), lambda c: (c, 0)),
                  pl.BlockSpec((q, d_ssd), lambda c: (c, 0)),
                  full(conv_w), full(conv_b), full(dt_bias_p), full(a_log_p), full(dskip_row), full(norm_w)],
        out_specs=[pl.BlockSpec((q, d_ssd), lambda c: (c, 0)),
                   pl.BlockSpec((d_ssd, D_STATE), lambda c: (0, 0))],
        out_shape=[jax.ShapeDtypeStruct((t, d_ssd), BF16),
                   jax.ShapeDtypeStruct((d_ssd, D_STATE), F32)],
        scratch_shapes=[pltpu.VMEM((q, conv_dim), F32),
                        pltpu.VMEM((n_pairs, D_STATE, LANES), F32),
                        pltpu.VMEM((q, d_ssd), F32)],
        compiler_params=_cparams(("arbitrary",)),
    )(xbc, dt_raw, z, conv_w, conv_b, dt_bias_p, a_log_p, dskip_row, norm_w)


def _order_key(s):
    i = lax.bitcast_convert_type(s, I32)
    return i ^ ((i >> 31) & 0x7FFFFFFF)


def _attn_prompt_kernel(topk, tq, q_ref, qi_ref, wi_ref, ki2_ref, kb_ref, vb_ref, nw_ref, o_ref,
                        skey, wb, m_scr, l_scr, acc_scr):
    i = pl.program_id(0)
    kc = tq
    n_chunks = i + 1
    n_heads = q_ref.shape[1] // ATT_HEAD_DIM
    q_per_kv = n_heads // KV_HEADS
    scale = ATT_HEAD_DIM ** -0.5
    wscale = (IDX_DIM ** -0.5) * (IDX_HEADS ** -0.5)

    wi = wi_ref[...] * wscale
    for h in range(IDX_HEADS):
        wb[h] = jnp.broadcast_to(wi[:, h:h + 1], (tq, kc))

    row_g = i * tq + lax.broadcasted_iota(I32, (tq, kc), 0)
    col_l = lax.broadcasted_iota(I32, (tq, kc), 1)

    def score_chunk(j, carry):
        k0 = ki2_ref[0, pl.ds(j * kc, kc), :]
        k1 = ki2_ref[1, pl.ds(j * kc, kc), :]
        s = jnp.zeros((tq, kc), F32)
        for p in range(IDX_HEADS // 2):
            qp = qi_ref[:, p * LANES:(p + 1) * LANES]
            s = s + jnp.maximum(_dot_nt(qp, k0), 0.0) * wb[2 * p]
            s = s + jnp.maximum(_dot_nt(qp, k1), 0.0) * wb[2 * p + 1]
        key = jnp.where(j * kc + col_l <= row_g, _order_key(s), INT_MIN)
        skey[:, pl.ds(j * kc, kc)] = key
        return carry

    lax.fori_loop(0, n_chunks, score_chunk, 0)

    n_blk = n_chunks * (kc // LANES)

    def count(pred):
        def body(b, acc):
            blk = skey[:, pl.ds(b * LANES, LANES)]
            return acc + jnp.where(pred(blk, b), 1, 0)
        acc = lax.fori_loop(0, n_blk, body, jnp.zeros((tq, LANES), I32))
        return jnp.sum(acc, axis=-1, keepdims=True)

    def bit_step(t, c):
        trial = c + jnp.left_shift(jnp.int32(1), 31 - t)
        n = count(lambda blk, b: blk >= trial)
        return jnp.where(n >= topk, trial, c)

    c = lax.fori_loop(0, 32, bit_step, jnp.full((tq, 1), INT_MIN, I32))
    thr = jnp.maximum(c, INT_MIN + 1)
    n_ge = count(lambda blk, b: blk >= thr)

    @pl.when(jnp.max(n_ge) > topk)
    def _():
        n_gt = count(lambda blk, b: blk > thr)
        need = topk - n_gt
        lane_i = lax.broadcasted_iota(I32, (tq, LANES), 1)
        n_bits = max(int(skey.shape[1]).bit_length(), 1)

        def idx_step(t, jlo):
            trial = jlo + jnp.left_shift(jnp.int32(1), n_bits - 1 - t)
            f = count(lambda blk, b: (blk == thr) & (b * LANES + lane_i < trial))
            return jnp.where(f <= need - 1, trial, jlo)

        jlo = lax.fori_loop(0, n_bits, idx_step, jnp.zeros((tq, 1), I32))
        cut = jnp.where(n_ge > topk, jlo + 1, jnp.int32(2 ** 30))

        def drop(b, carry):
            blk = skey[:, pl.ds(b * LANES, LANES)]
            gone = (blk == thr) & (b * LANES + lane_i >= cut)
            skey[:, pl.ds(b * LANES, LANES)] = jnp.where(gone, INT_MIN, blk)
            return carry

        lax.fori_loop(0, n_blk, drop, 0)

    m_scr[...] = jnp.full(m_scr.shape, NEG_BIG, F32)
    l_scr[...] = jnp.zeros(l_scr.shape, F32)
    acc_scr[...] = jnp.zeros(acc_scr.shape, F32)

    def attend_chunk(j, carry):
        sel = skey[:, pl.ds(j * kc, kc)] >= thr
        for h in range(n_heads):
            g = h // q_per_kv
            kj = kb_ref[pl.ds(j * kc, kc), g * ATT_HEAD_DIM:(g + 1) * ATT_HEAD_DIM]
            vj = vb_ref[pl.ds(j * kc, kc), g * ATT_HEAD_DIM:(g + 1) * ATT_HEAD_DIM]
            qh = q_ref[:, h * ATT_HEAD_DIM:(h + 1) * ATT_HEAD_DIM]
            lg = jnp.where(sel, _dot_nt(qh, kj) * scale, NEG_BIG)
            m_old = m_scr[h]
            m_new = jnp.maximum(m_old, jnp.max(lg, axis=-1, keepdims=True))
            p = jnp.where(sel, jnp.exp(lg - m_new), 0.0)
            alpha = jnp.exp(m_old - m_new)
            l_scr[h] = alpha * l_scr[h] + jnp.sum(p, axis=-1, keepdims=True)
            acc_scr[h] = alpha * acc_scr[h] + _dot(p.astype(BF16), vj)
            m_scr[h] = m_new
        return carry

    lax.fori_loop(0, n_chunks, attend_chunk, 0)

    ss = jnp.zeros((tq, 1), F32)
    for h in range(n_heads):
        o = acc_scr[h] / l_scr[h]
        acc_scr[h] = o
        ss = ss + jnp.sum(o * o, axis=-1, keepdims=True)
    inv = lax.rsqrt(ss * (1.0 / (n_heads * ATT_HEAD_DIM)) + EPS)
    for h in range(n_heads):
        sl = slice(h * ATT_HEAD_DIM, (h + 1) * ATT_HEAD_DIM)
        o_ref[:, sl] = (acc_scr[h] * inv * nw_ref[:, sl]).astype(BF16)


def _attn_prompt(q, qi, wi, ki2, kb, vb, norm_w, topk, tq):
    t, d_att = q.shape
    n_heads = d_att // ATT_HEAD_DIM
    full = lambda a: pl.BlockSpec(a.shape, lambda i: (0,) * a.ndim)
    return pl.pallas_call(
        functools.partial(_attn_prompt_kernel, topk, tq),
        grid=(t // tq,),
        in_specs=[pl.BlockSpec((tq, d_att), lambda i: (i, 0)),
                  pl.BlockSpec((tq, qi.shape[1]), lambda i: (i, 0)),
                  pl.BlockSpec((tq, LANES), lambda i: (i, 0)),
                  full(ki2), full(kb), full(vb), full(norm_w)],
        out_specs=pl.BlockSpec((tq, d_att), lambda i: (i, 0)),
        out_shape=jax.ShapeDtypeStruct((t, d_att), BF16),
        scratch_shapes=[pltpu.VMEM((tq, t), I32),
                        pltpu.VMEM((IDX_HEADS, tq, tq), F32),
                        pltpu.VMEM((n_heads, tq, 1), F32),
                        pltpu.VMEM((n_heads, tq, 1), F32),
                        pltpu.VMEM((n_heads, tq, ATT_HEAD_DIM), F32)],
        compiler_params=_cparams(("arbitrary",)),
    )(q, qi, wi, ki2, kb, vb, norm_w)


def _outproj_kernel(ya_ref, yb_ref, w_ref, x_ref, g1_ref, nw_ref, sc_ref, sh_ref, wr_ref, br_ref,
                    x1_ref, h2_ref, comb_ref):
    d_a = ya_ref.shape[1]
    m = _dot(ya_ref[...], w_ref[:d_a, :]) + _dot(yb_ref[...], w_ref[d_a:, :])
    x1 = x_ref[...] + g1_ref[...] * m
    x1_ref[...] = x1
    h2 = _rms(x1) * nw_ref[...]
    h2 = h2 * (1.0 + sc_ref[...]) + sh_ref[...]
    hb = h2.astype(BF16)
    h2_ref[...] = hb
    lg = _dot(hb, wr_ref[...]) + br_ref[...]
    lane = lax.broadcasted_iota(I32, lg.shape, 1)
    big = jnp.int32(4 * LANES)

    def rmax(v):
        return jnp.max(v, axis=-1, keepdims=True)

    def rmin(v):
        return jnp.min(v, axis=-1, keepdims=True)

    def rsum(v):
        return jnp.sum(v, axis=-1, keepdims=True)

    is_g = (lane >= N_EXPERTS) & (lane < N_EXPERTS + N_EGROUPS)
    mg = rmax(jnp.where(is_g, lg, -jnp.inf))
    sg = rsum(jnp.where(is_g, jnp.exp(lg - mg), 0.0))
    gsel = rmin(jnp.where(is_g & (lg == mg), lane - N_EXPERTS, big))
    pgsel = 1.0 / sg
    in_grp = (lane < N_EXPERTS) & (jnp.right_shift(lane, EXPERTS_PER_GROUP.bit_length() - 1) == gsel)
    me = rmax(jnp.where(in_grp, lg, -jnp.inf))
    ee = jnp.where(in_grp, jnp.exp(lg - me), 0.0)
    pe = ee / rsum(ee)
    p1 = rmax(jnp.where(in_grp, pe, -1.0))
    i1 = rmin(jnp.where(in_grp & (pe == p1), lane, big))
    rem = in_grp & (lane != i1)
    p2 = rmax(jnp.where(rem, pe, -1.0))
    i2 = rmin(jnp.where(rem & (pe == p2), lane, big))
    den = p1 + p2
    comb_ref[...] = jnp.where(lane == i1, pgsel * p1 / den, jnp.where(lane == i2, pgsel * p2 / den, 0.0))


def _out_proj(ya, yb, w_out_b, x, g1, nw2, sc2, sh2, wr, br, tm):
    t, d = x.shape
    d_a = ya.shape[1]
    tmod = g1.shape[0]
    mod_map = (lambda i: (0, 0)) if tmod == 1 else (lambda i: (i, 0))
    mod_rows = 1 if tmod == 1 else tm
    modspec = pl.BlockSpec((mod_rows, d), mod_map)
    row = lambda w: pl.BlockSpec((tm, w), lambda i: (i, 0))
    full = lambda a: pl.BlockSpec(a.shape, lambda i: (0, 0))
    return pl.pallas_call(
        _outproj_kernel,
        grid=(t // tm,),
        in_specs=[row(d_a), row(yb.shape[1]), full(w_out_b), row(d), modspec, full(nw2), modspec, modspec,
                  full(wr), full(br)],
        out_specs=[row(d), row(d), row(LANES)],
        out_shape=[jax.ShapeDtypeStruct((t, d), F32), jax.ShapeDtypeStruct((t, d), BF16),
                   jax.ShapeDtypeStruct((t, LANES), F32)],
        compiler_params=_cparams(("arbitrary",)),
    )(ya, yb, w_out_b, x, g1, nw2, sc2, sh2, wr, br)


def _moe_kernel(h_ref, comb_ref, wu_ref, wd_ref, o_ref):
    e = pl.program_id(1)

    @pl.when(e == 0)
    def _():
        o_ref[...] = jnp.zeros_like(o_ref)

    gu = _dot(h_ref[...], wu_ref[0].astype(BF16))
    de = gu.shape[1] // 2
    act = _silu(gu[:, :de]) * gu[:, de:]
    comb = comb_ref[...]
    lane = lax.broadcasted_iota(I32, comb.shape, 1)
    col = jnp.sum(jnp.where(lane == e, comb, 0.0), axis=-1, keepdims=True)
    o_ref[...] += col * _dot(act.astype(BF16), wd_ref[0].astype(BF16))


def _moe(h2, comb, w_up, w_down, tm):
    t, d = h2.shape
    ne, _, two_de = w_up.shape
    return pl.pallas_call(
        _moe_kernel,
        grid=(t // tm, ne),
        in_specs=[pl.BlockSpec((tm, d), lambda i, e: (i, 0)),
                  pl.BlockSpec((tm, LANES), lambda i, e: (i, 0)),
                  pl.BlockSpec((1, d, two_de), lambda i, e: (e, 0, 0)),
                  pl.BlockSpec((1, two_de // 2, d), lambda i, e: (e, 0, 0))],
        out_specs=pl.BlockSpec((tm, d), lambda i, e: (i, 0)),
        out_shape=jax.ShapeDtypeStruct((t, d), F32),
        compiler_params=_cparams(("arbitrary", "arbitrary")),
    )(h2, comb, w_up, w_down)


def _final_kernel(x1_ref, y_ref, g2_ref, nf_ref, xo_ref, o_ref):
    x2 = x1_ref[...] + g2_ref[...] * y_ref[...]
    xo_ref[...] = x2
    o_ref[...] = _rms(x2) * nf_ref[...]


def _final(x1, y, g2, nf, tm):
    t, d = x1.shape
    tmod = g2.shape[0]
    mod_map = (lambda i: (0, 0)) if tmod == 1 else (lambda i: (i, 0))
    row = pl.BlockSpec((tm, d), lambda i: (i, 0))
    return pl.pallas_call(
        _final_kernel,
        grid=(t // tm,),
        in_specs=[row, row, pl.BlockSpec((1 if tmod == 1 else tm, d), mod_map),
                  pl.BlockSpec((1, d), lambda i: (0, 0))],
        out_specs=[row, row],
        out_shape=[jax.ShapeDtypeStruct((t, d), F32), jax.ShapeDtypeStruct((t, d), F32)],
        compiler_params=_cparams(("arbitrary",)),
    )(x1, y, g2, nf)


def _ssd_prep_kernel(xbc_ref, p0_ref, p1_ref, p2_ref, cw_ref, cb_ref, dt_ref, dtb_ref, alog_ref, ex_ref,
                     xc_ref, xdt_ref, dec_ref):
    d_ssd = xdt_ref.shape[1]
    acc = (cb_ref[...] + cw_ref[0:1, :] * p0_ref[...] + cw_ref[1:2, :] * p1_ref[...]
           + cw_ref[2:3, :] * p2_ref[...] + cw_ref[3:4, :] * xbc_ref[...])
    xc = _silu(acc)
    xc_ref[...] = xc
    dt = _softplus(dt_ref[...] + dtb_ref[...])
    dec = jnp.exp(dt * (-jnp.exp(alog_ref[...])))
    xdt_ref[...] = _dot(dt, ex_ref[...], precision=HIGHEST) * xc[:, :d_ssd]
    dec_ref[...] = _dot(dec, ex_ref[...], precision=HIGHEST)


def _ssd_prep(xbc, p0, p1, p2, conv_w, conv_b, dt_raw, dt_bias_p, a_log_p, expand):
    b, conv_dim = xbc.shape
    d_ssd = expand.shape[1]
    args = (xbc, p0, p1, p2, conv_w, conv_b, dt_raw, dt_bias_p, a_log_p, expand)
    return pl.pallas_call(
        _ssd_prep_kernel,
        grid=(1,),
        in_specs=[pl.BlockSpec(a.shape, lambda i: (0, 0)) for a in args],
        out_specs=[pl.BlockSpec((b, conv_dim), lambda i: (0, 0)),
                   pl.BlockSpec((b, d_ssd), lambda i: (0, 0)),
                   pl.BlockSpec((b, d_ssd), lambda i: (0, 0))],
        out_shape=[jax.ShapeDtypeStruct((b, conv_dim), F32), jax.ShapeDtypeStruct((b, d_ssd), F32),
                   jax.ShapeDtypeStruct((b, d_ssd), F32)],
        compiler_params=_cparams(("arbitrary",)),
    )(*args)


def _ssd_step_kernel(n_pairs, xdt_ref, dec_ref, bm_ref, cm_ref, s_ref, so_ref, y_ref):
    r2 = lax.broadcasted_iota(I32, (LANES, LANES), 0)
    c2 = lax.broadcasted_iota(I32, (LANES, LANES), 1)
    eye = r2 == c2
    ones = jnp.ones((LANES, LANES), F32)
    pairs_per_group = n_pairs // SSD_GROUPS
    rows_per_pair = LANES // SSD_HEAD_DIM
    for p in range(n_pairs):
        g = p // pairs_per_group
        sl = slice(p * LANES, (p + 1) * LANES)
        hs = slice(p * rows_per_pair, (p + 1) * rows_per_pair)
        hb = s_ref[0, 0, hs].reshape(LANES, D_STATE)
        xd = jnp.where(eye, jnp.broadcast_to(xdt_ref[0, :, sl], (LANES, LANES)), 0.0)
        dd = jnp.where(eye, jnp.broadcast_to(dec_ref[0, :, sl], (LANES, LANES)), 0.0)
        bmat = jnp.broadcast_to(bm_ref[0, :, g * D_STATE:(g + 1) * D_STATE], (LANES, D_STATE))
        upd = _dot(xd, bmat, precision=HIGHEST)
        dcol = _dot(dd, ones, precision=HIGHEST)
        hn = hb * dcol + upd
        so_ref[0, 0, hs] = hn.reshape(rows_per_pair, SSD_HEAD_DIM, D_STATE)
        cmat = jnp.broadcast_to(cm_ref[0, :, g * D_STATE:(g + 1) * D_STATE], (8, D_STATE))
        y_ref[0, :, sl] = _dot_nt(cmat, hn, precision=HIGHEST)[0:1, :]


def _ssd_step(xdt, dec, bm, cm, state):
    b, d_ssd = xdt.shape
    n_pairs = d_ssd // LANES
    heads = d_ssd // SSD_HEAD_DIM
    r3 = lambda a: a.reshape(b, 1, a.shape[1])
    row = lambda w: pl.BlockSpec((1, 1, w), lambda i: (i, 0, 0))
    sspec = pl.BlockSpec((1, 1, heads, SSD_HEAD_DIM, D_STATE), lambda i: (0, i, 0, 0, 0))
    so, y = pl.pallas_call(
        functools.partial(_ssd_step_kernel, n_pairs),
        grid=(b,),
        in_specs=[row(d_ssd), row(d_ssd), row(bm.shape[1]), row(cm.shape[1]), sspec],
        out_specs=[sspec, row(d_ssd)],
        out_shape=[jax.ShapeDtypeStruct(state.shape, F32), jax.ShapeDtypeStruct((b, 1, d_ssd), F32)],
        compiler_params=_cparams(("arbitrary",)),
    )(r3(xdt), r3(dec), r3(bm), r3(cm), state)
    return so, y.reshape(b, d_ssd)


def _ssd_finish_kernel(y_ref, xs_ref, z_ref, dsk_ref, nw_ref, o_ref):
    y = (y_ref[...] + xs_ref[...] * dsk_ref[...]) * _silu(z_ref[...])
    o_ref[...] = (_rms(y) * nw_ref[...]).astype(BF16)


def _ssd_finish(y, xs, z, dskip_row, norm_w):
    args = (y, xs, z, dskip_row, norm_w)
    return pl.pallas_call(
        _ssd_finish_kernel,
        grid=(1,),
        in_specs=[pl.BlockSpec(a.shape, lambda i: (0, 0)) for a in args],
        out_specs=pl.BlockSpec(y.shape, lambda i: (0, 0)),
        out_shape=jax.ShapeDtypeStruct(y.shape, BF16),
        compiler_params=_cparams(("arbitrary",)),
    )(*args)


def _page_copy(cache_ref, buf, sem, pt_ref, b, p, slot):
    return pltpu.make_async_copy(cache_ref.at[0, pt_ref[b, p]], buf.at[slot, p], sem.at[slot])


def _score_sample_kernel(n_pages, pt_ref, qi_ref, wi_ref, kin_ref, cache_ref, s_ref, buf, sem):
    b = pl.program_id(0)
    nb = pl.num_programs(0)
    slot = b % 2

    def start(bb, sl):
        def body(p, carry):
            _page_copy(cache_ref, buf, sem, pt_ref, bb, p, sl).start()
            return carry
        lax.fori_loop(0, n_pages, body, 0)

    @pl.when(b == 0)
    def _():
        start(0, 0)

    @pl.when(b + 1 < nb)
    def _():
        start(b + 1, 1 - slot)

    def wait(p, carry):
        _page_copy(cache_ref, buf, sem, pt_ref, b, p, slot).wait()
        return carry

    lax.fori_loop(0, n_pages, wait, 0)

    qi = qi_ref[0]
    w = wi_ref[0] * ((IDX_DIM ** -0.5) * (IDX_HEADS ** -0.5))
    page_rows = buf.shape[2]

    def score(kp):
        d = _dot_nt(qi, kp.astype(BF16))
        return jnp.sum(jnp.maximum(d, 0.0) * w, axis=0, keepdims=True)

    def page(p, carry):
        s_ref[0, pl.ds(p, 1), :] = score(buf[slot, p])
        return carry

    lax.fori_loop(0, n_pages, page, 0)
    tail = s_ref.shape[1] - n_pages
    kn = jnp.broadcast_to(kin_ref[0], (page_rows, kin_ref.shape[2]))
    sn = score(kn)
    r = lax.broadcasted_iota(I32, (tail, page_rows), 0)
    c = lax.broadcasted_iota(I32, (tail, page_rows), 1)
    s_ref[0, n_pages:, :] = jnp.where((r == 0) & (c == 0), jnp.broadcast_to(sn, (tail, page_rows)), -jnp.inf)


def _score_sample(page_table, qi3, wi3, ki_new3, cache_ki, tail_rows):
    b, n_pages = page_table.shape
    page_rows, idx_dim = cache_ki.shape[2], cache_ki.shape[3]
    gs = pltpu.PrefetchScalarGridSpec(
        num_scalar_prefetch=1,
        grid=(b,),
        in_specs=[pl.BlockSpec((1,) + qi3.shape[1:], lambda i, pt: (i, 0, 0)),
                  pl.BlockSpec((1,) + wi3.shape[1:], lambda i, pt: (i, 0, 0)),
                  pl.BlockSpec((1,) + ki_new3.shape[1:], lambda i, pt: (i, 0, 0)),
                  pl.BlockSpec(memory_space=pl.ANY)],
        out_specs=pl.BlockSpec((1, n_pages + tail_rows, page_rows), lambda i, pt: (i, 0, 0)),
        scratch_shapes=[pltpu.VMEM((2, n_pages, page_rows, idx_dim), F32),
                        pltpu.SemaphoreType.DMA((2,))],
    )
    return pl.pallas_call(
        functools.partial(_score_sample_kernel, n_pages),
        grid_spec=gs,
        out_shape=jax.ShapeDtypeStruct((b, n_pages + tail_rows, page_rows), F32),
        compiler_params=_cparams(("arbitrary",)),
    )(page_table, qi3, wi3, ki_new3, cache_ki)


def _topk_kernel(topk, s_ref, idx_ref, sv):
    sv[...] = s_ref[...]
    nb, n = sv.shape
    col = lax.broadcasted_iota(I32, (nb, n), 1)
    lane = lax.broadcasted_iota(I32, (nb, topk), 1)
    big = jnp.int32(2 ** 30)

    def body(r, out):
        s = sv[...]
        m = jnp.max(s, axis=-1, keepdims=True)
        j = jnp.min(jnp.where(s == m, col, big), axis=-1, keepdims=True)
        sv[...] = jnp.where(col == j, -jnp.inf, s)
        return jnp.where(lane == r, j, out)

    idx_ref[...] = lax.fori_loop(0, topk, body, jnp.zeros((nb, topk), I32))


def _topk_rows(s2d, topk):
    nb, n = s2d.shape
    return pl.pallas_call(
        functools.partial(_topk_kernel, topk),
        grid=(1,),
        in_specs=[pl.BlockSpec((nb, n), lambda i: (0, 0))],
        out_specs=pl.BlockSpec((nb, topk), lambda i: (0, 0)),
        out_shape=jax.ShapeDtypeStruct((nb, topk), I32),
        scratch_shapes=[pltpu.VMEM((nb, n), F32)],
        compiler_params=_cparams(("arbitrary",)),
    )(s2d)


def _row_copy(src, dst, sem, src_row, dst_row):
    return pltpu.make_async_copy(src.at[pl.ds(src_row, KV_HEADS)], dst.at[pl.ds(dst_row, KV_HEADS)], sem)


def _attn_sample_kernel(topk, past_len, page_rows, n_pages, idx_ref, pt_ref, q_ref, nw_ref, ck_ref, cv_ref,
                        kn_ref, vn_ref, o_ref, kbuf, vbuf, sem):
    b = pl.program_id(0)

    def start(r, carry):
        j = idx_ref[b, r]
        pg = pt_ref[b, jnp.minimum(j // page_rows, n_pages - 1)]
        row = (pg * page_rows + j % page_rows) * KV_HEADS

        @pl.when(j < past_len)
        def _():
            _row_copy(ck_ref, kbuf, sem.at[0], row, r * KV_HEADS).start()
            _row_copy(cv_ref, vbuf, sem.at[1], row, r * KV_HEADS).start()

        @pl.when(j >= past_len)
        def _():
            _row_copy(kn_ref, kbuf, sem.at[0], b * KV_HEADS, r * KV_HEADS).start()
            _row_copy(vn_ref, vbuf, sem.at[1], b * KV_HEADS, r * KV_HEADS).start()

        return carry

    lax.fori_loop(0, topk, start, 0)

    def wait(r, carry):
        _row_copy(ck_ref, kbuf, sem.at[0], 0, r * KV_HEADS).wait()
        _row_copy(cv_ref, vbuf, sem.at[1], 0, r * KV_HEADS).wait()
        return carry

    lax.fori_loop(0, topk, wait, 0)

    scale = ATT_HEAD_DIM ** -0.5
    outs = []
    ss = jnp.zeros((1, 1), F32)
    for g in range(KV_HEADS):
        kg = kbuf[pl.ds(g, topk, stride=KV_HEADS), :].astype(BF16)
        vg = vbuf[pl.ds(g, topk, stride=KV_HEADS), :].astype(BF16)
        lg = _dot_nt(q_ref[0, g], kg) * scale
        m = jnp.max(lg, axis=-1, keepdims=True)
        p = jnp.exp(lg - m)
        p = p / jnp.sum(p, axis=-1, keepdims=True)
        o = _dot(p.astype(BF16), vg)
        rows = lax.broadcasted_iota(I32, o.shape, 0)
        o = jnp.where(rows < q_ref.shape[2] // 2, o, 0.0)
        outs.append(o)
        ss = ss + jnp.sum(jnp.sum(o * o, axis=-1, keepdims=True), axis=0, keepdims=True)
    n_feat = KV_HEADS * (q_ref.shape[2] // 2) * ATT_HEAD_DIM
    inv = lax.rsqrt(ss * (1.0 / n_feat) + EPS)
    for g in range(KV_HEADS):
        o_ref[0, g] = (outs[g] * inv * nw_ref[g]).astype(BF16)


def _attn_sample(idx, page_table, q4, nw3, ck2, cv2, kn2, vn2, past_len, page_rows):
    b, topk = idx.shape
    n_pages = page_table.shape[1]
    gs = pltpu.PrefetchScalarGridSpec(
        num_scalar_prefetch=2,
        grid=(b,),
        in_specs=[pl.BlockSpec((1,) + q4.shape[1:], lambda i, a, c: (i, 0, 0, 0)),
                  pl.BlockSpec(nw3.shape, lambda i, a, c: (0, 0, 0)),
                  pl.BlockSpec(memory_space=pl.ANY), pl.BlockSpec(memory_space=pl.ANY),
                  pl.BlockSpec(memory_space=pl.ANY), pl.BlockSpec(memory_space=pl.ANY)],
        out_specs=pl.BlockSpec((1,) + q4.shape[1:], lambda i, a, c: (i, 0, 0, 0)),
        scratch_shapes=[pltpu.VMEM((topk * KV_HEADS, ATT_HEAD_DIM), F32),
                        pltpu.VMEM((topk * KV_HEADS, ATT_HEAD_DIM), F32),
                        pltpu.SemaphoreType.DMA((2,))],
    )
    return pl.pallas_call(
        functools.partial(_attn_sample_kernel, topk, past_len, page_rows, n_pages),
        grid_spec=gs,
        out_shape=jax.ShapeDtypeStruct(q4.shape, BF16),
        compiler_params=_cparams(("arbitrary",)),
    )(idx, page_table, q4, nw3, ck2, cv2, kn2, vn2)


def _row(v, width=None):
    v = v.reshape(1, -1)
    return v if width is None else _pad_cols(v, width)


def _layer_params(p):
    d = p["w_in"].shape[0]
    wr = jnp.concatenate([p["w_router_e"], p["w_router_g"]], axis=1)
    br = jnp.concatenate([p["b_router_e"], p["b_router_g"]])
    return dict(
        w_perm=_perm_w_in(p["w_in"]),
        w_out_b=p["w_out"].astype(BF16),
        wr=_pad_cols(wr, LANES).astype(BF16),
        br=_row(br, LANES),
        nw1=_row(p["norm1_w"]), nw2=_row(p["norm2_w"]),
        lnw=_row(p["ln_kidx_w"], LANES), lnb=_row(p["ln_kidx_b"], LANES),
        dt_bias=_row(p["dt_bias"], LANES), a_log=_row(p["a_log"], LANES),
        dskip=_row(jnp.repeat(p["d_skip"], SSD_HEAD_DIM)),
        norm_ssd=_row(p["norm_ssd_w"]), norm_att=_row(p["norm_att_w"]),
        conv_w=p["conv_w"], conv_b=_row(p["conv_b"]),
        d_ssd=d // 2,
    )


def _ffn(x, ya, yb, mod, lp, p, tm):
    x1, h2, comb = _out_proj(ya, yb, lp["w_out_b"], x, mod[2], lp["nw2"], mod[4], mod[3], lp["wr"], lp["br"], tm)
    y = _moe(h2, comb, p["w_exp_up"], p["w_exp_down"], tm)
    return x1, y


def _prompt_layer(x, mod, lp, p):
    t, d = x.shape
    pr = _in_proj(x, lp["nw1"], mod[1], mod[0], lp["w_perm"], lp["lnw"], lp["lnb"], tm=256)
    y_ssd, st = _ssd_prompt(pr["xbc"], pr["dt"], pr["z"], lp["conv_w"], lp["conv_b"], lp["dt_bias"], lp["a_log"],
                            lp["dskip"], lp["norm_ssd"])
    ki = pr["ki"]
    zeros = jnp.zeros_like(ki)
    ki2 = jnp.stack([jnp.concatenate([ki, zeros], axis=1), jnp.concatenate([zeros, ki], axis=1)]).astype(BF16)
    topk = min(TOPK_MAX, t // 4)
    y_att = _attn_prompt(pr["q"], pr["qi"], pr["wi"], ki2, pr["kb"], pr["vb"], lp["norm_att"], topk, tq=256)
    x1, y = _ffn(x, y_ssd, y_att, mod, lp, p, tm=512)
    conv_new = jnp.concatenate([jnp.zeros((CONV_W - 1, pr["xbc"].shape[1]), F32), pr["xbc"]])[-(CONV_W - 1):]
    return x1, y, (pr["k"], pr["v"], ki, conv_new, st)


def _sample_layer(x, mod, lp, p, cache_k, cache_v, cache_ki, conv_prev, ssm_prev, page_table):
    b, d = x.shape
    d_ssd = lp["d_ssd"]
    heads = d_ssd // SSD_HEAD_DIM
    gn = SSD_GROUPS * D_STATE
    pr = _in_proj(x, lp["nw1"], mod[1], mod[0], lp["w_perm"], lp["lnw"], lp["lnb"], tm=b)
    expand = (jnp.arange(LANES)[:, None] == (jnp.arange(d_ssd)[None, :] // SSD_HEAD_DIM)).astype(F32)
    xc, xdt, dec = _ssd_prep(pr["xbc"], conv_prev[:, 0], conv_prev[:, 1], conv_prev[:, 2], lp["conv_w"], lp["conv_b"],
                             pr["dt"], lp["dt_bias"], lp["a_log"], expand)
    xs, bm, cm = xc[:, :d_ssd], xc[:, d_ssd:d_ssd + gn], xc[:, d_ssd + gn:]
    st5 = ssm_prev.reshape((1, b, heads, SSD_HEAD_DIM, D_STATE))
    st_new, y = _ssd_step(xdt, dec, bm, cm, st5)
    y_ssd = _ssd_finish(y, xs, pr["z"], lp["dskip"], lp["norm_ssd"])
    conv_new = jnp.concatenate([conv_prev[:, 1:], pr["xbc"][:, None, :]], axis=1)
    n_pool, page_rows = cache_k.shape[0], cache_k.shape[1]
    n_pages = page_table.shape[1]
    past_len = n_pages * page_rows
    topk = min(TOPK_MAX, (past_len + 1) // 4)
    qi3 = pr["qi"].reshape(b, IDX_HEADS, IDX_DIM)
    wi3 = pr["wi"][:, :IDX_HEADS].reshape(b, IDX_HEADS, 1)
    tail_rows = 8
    s3 = _score_sample(page_table, qi3, wi3, pr["ki"].reshape(b, 1, IDX_DIM), cache_ki[None], tail_rows)
    idx = _topk_rows(s3.reshape(b, (n_pages + tail_rows) * page_rows), topk)
    n_heads = pr["q"].shape[1] // ATT_HEAD_DIM
    q_per_kv = n_heads // KV_HEADS
    q4 = jnp.pad(pr["q"].reshape(b, KV_HEADS, q_per_kv, ATT_HEAD_DIM), ((0, 0), (0, 0), (0, q_per_kv), (0, 0)))
    nw3 = jnp.pad(lp["norm_att"].reshape(KV_HEADS, q_per_kv, ATT_HEAD_DIM), ((0, 0), (0, q_per_kv), (0, 0)))
    ck2 = cache_k.reshape(n_pool * page_rows * KV_HEADS, ATT_HEAD_DIM)
    cv2 = cache_v.reshape(n_pool * page_rows * KV_HEADS, ATT_HEAD_DIM)
    kn2 = pr["k"].reshape(b * KV_HEADS, ATT_HEAD_DIM)
    vn2 = pr["v"].reshape(b * KV_HEADS, ATT_HEAD_DIM)
    o4 = _attn_sample(idx, page_table, q4, nw3, ck2, cv2, kn2, vn2, past_len, page_rows)
    y_att = o4[:, :, :q_per_kv].reshape(b, n_heads * ATT_HEAD_DIM)
    x1, y = _ffn(x, y_ssd, y_att, mod, lp, p, tm=b)
    return x1, y, (pr["k"], pr["v"], pr["ki"], conv_new, st_new.reshape(ssm_prev.shape))


def kernel(x_prompt, x_sample, cache_k, cache_v, cache_k_idx, state_conv, state_ssm, page_table, c_prompt, c_sample, w_ada, b_ada, norm1_w, norm2_w, w_in, conv_w, conv_b, dt_bias, a_log, d_skip, norm_ssd_w, ln_kidx_w, ln_kidx_b, norm_att_w, w_out, w_router_g, b_router_g, w_router_e, b_router_e, w_exp_up, w_exp_down, norm_f_w):
    batch, seq, d = x_prompt.shape
    dec_batch, dec_seq, _ = x_sample.shape
    assert batch == 1 and dec_seq == 1, "one prompt sequence and one new token per sample sequence"
    depth = w_ada.shape[0]
    heads = (d // 2) // SSD_HEAD_DIM
    xp = x_prompt.reshape(seq, d)
    xs = x_sample.reshape(dec_batch, d)
    n_c = batch + dec_batch
    c_all = jnp.pad(jnp.concatenate([c_prompt, c_sample]), ((0, -n_c % 8), (0, 0)))
    nf = _row(norm_f_w)
    outs_p, outs_s = [], []
    yp = ys = None
    for l in range(depth):
        p = dict(w_in=w_in[l], conv_w=conv_w[l], conv_b=conv_b[l], dt_bias=dt_bias[l], a_log=a_log[l],
                 d_skip=d_skip[l], norm_ssd_w=norm_ssd_w[l], ln_kidx_w=ln_kidx_w[l], ln_kidx_b=ln_kidx_b[l],
                 norm_att_w=norm_att_w[l], w_out=w_out[l], w_router_g=w_router_g[l], b_router_g=b_router_g[l],
                 w_router_e=w_router_e[l], b_router_e=b_router_e[l], w_exp_up=w_exp_up[l],
                 w_exp_down=w_exp_down[l], norm1_w=norm1_w[l], norm2_w=norm2_w[l])
        lp = _layer_params(p)
        mod = _ada_mod(c_all, w_ada[l], b_ada[l])
        mod_p = [mod[0:1, k * d:(k + 1) * d] for k in range(6)]
        mod_s = [mod[batch:n_c, k * d:(k + 1) * d] for k in range(6)]
        x1p, yp_moe, st_p = _prompt_layer(xp, mod_p, lp, p)
        x1s, ys_moe, st_s = _sample_layer(xs, mod_s, lp, p, cache_k[l], cache_v[l], cache_k_idx[l], state_conv[l],
                                          state_ssm[l], page_table)
        xp, yp = _final(x1p, yp_moe, mod_p[5], nf, tm=512)
        xs, ys = _final(x1s, ys_moe, mod_s[5], nf, tm=dec_batch)
        outs_p.append(st_p)
        outs_s.append(st_s)

    def stack(outs, n_rows, lead):
        k = jnp.stack([o[0].reshape(lead + (n_rows, KV_HEADS, ATT_HEAD_DIM)) for o in outs])
        v = jnp.stack([o[1].reshape(lead + (n_rows, KV_HEADS, ATT_HEAD_DIM)) for o in outs])
        ki = jnp.stack([o[2].reshape(lead + (n_rows, IDX_DIM)) for o in outs])
        return k, v, ki

    k_p, v_p, ki_p = stack(outs_p, seq, (batch,))
    conv_p = jnp.stack([o[3][None] for o in outs_p])
    ssm_p = jnp.stack([o[4].reshape(batch, heads, SSD_HEAD_DIM, D_STATE) for o in outs_p])
    k_s = jnp.stack([o[0].reshape(dec_batch, dec_seq, KV_HEADS, ATT_HEAD_DIM) for o in outs_s])
    v_s = jnp.stack([o[1].reshape(dec_batch, dec_seq, KV_HEADS, ATT_HEAD_DIM) for o in outs_s])
    ki_s = jnp.stack([o[2].reshape(dec_batch, dec_seq, IDX_DIM) for o in outs_s])
    conv_s = jnp.stack([o[3] for o in outs_s])
    ssm_s = jnp.stack([o[4] for o in outs_s])
    return (yp.reshape(batch, seq, d), ys.reshape(dec_batch, dec_seq, d), k_p, v_p, ki_p, conv_p, ssm_p,
            k_s, v_s, ki_s, conv_s, ssm_s)
```

```python
import functools

import numpy as np
import jax
import jax.numpy as jnp
from jax import lax
from jax.experimental import pallas as pl
from jax.experimental.pallas import tpu as pltpu

F32 = jnp.float32
BF16 = jnp.bfloat16
I32 = jnp.int32

SSD_HEAD_DIM = 64
SSD_GROUPS = 2
D_STATE = 128
CONV_W = 4
SSD_CHUNK = 128
ATT_HEAD_DIM = 128
KV_HEADS = 2
IDX_HEADS = 16
IDX_DIM = 64
TOPK_MAX = 256
N_EGROUPS = 4
EXPERTS_PER_GROUP = 8
N_EXPERTS = N_EGROUPS * EXPERTS_PER_GROUP
D_EXPERT = 512
EPS = 1e-6

LANES = 128
INT_MIN = -2 ** 31
NEG_BIG = -1e30
VMEM_LIMIT = 56 * 1024 * 1024
HIGHEST = lax.Precision.HIGHEST


def _cparams(sem):
    return pltpu.CompilerParams(dimension_semantics=sem, vmem_limit_bytes=VMEM_LIMIT)


def _dot(a, b, precision=None):
    return jnp.dot(a, b, preferred_element_type=F32, precision=precision)


def _dot_nt(a, b, precision=None):
    return lax.dot_general(a, b, (((1,), (1,)), ((), ())), preferred_element_type=F32, precision=precision)


def _silu(x):
    return x * jax.nn.sigmoid(x)


def _softplus(x):
    return jnp.maximum(x, 0.0) + jnp.log(1.0 + jnp.exp(-jnp.abs(x)))


def _rms(x):
    return x * lax.rsqrt(jnp.mean(x * x, axis=-1, keepdims=True) + EPS)


def _pad_cols(a, width):
    return jnp.pad(a, ((0, 0), (0, width - a.shape[1])))


def _ada_kernel(c_ref, w_ref, b_ref, o_ref):
    s = _silu(c_ref[...]).astype(BF16)
    o_ref[...] = _dot(s, w_ref[...].astype(BF16)) + b_ref[...]


def _ada_mod(c_all, w_ada, b_ada):
    r, d = c_all.shape
    n = w_ada.shape[1]
    tn = 1024
    return pl.pallas_call(
        _ada_kernel,
        grid=(n // tn,),
        in_specs=[pl.BlockSpec((r, d), lambda j: (0, 0)),
                  pl.BlockSpec((d, tn), lambda j: (0, j)),
                  pl.BlockSpec((1, tn), lambda j: (0, j))],
        out_specs=pl.BlockSpec((r, tn), lambda j: (0, j)),
        out_shape=jax.ShapeDtypeStruct((r, n), F32),
        compiler_params=_cparams(("arbitrary",)),
    )(c_all, w_ada, b_ada.reshape(1, n))


def _in_layout(d_model):
    d_ssd = d_model // 2
    d_att = d_model - d_ssd
    conv_dim = d_ssd + 2 * SSD_GROUPS * D_STATE
    ssd_heads = d_ssd // SSD_HEAD_DIM
    sizes = dict(z=d_ssd, xbc=conv_dim, dt=ssd_heads, q=d_att, k=KV_HEADS * ATT_HEAD_DIM,
                 v=KV_HEADS * ATT_HEAD_DIM, qi=IDX_HEADS * IDX_DIM, ki=IDX_DIM, wi=IDX_HEADS)
    order = ("z", "xbc", "dt", "q", "k", "v", "qi", "ki", "wi")
    src, dst, off_s, off_d = {}, {}, 0, 0
    for name in order:
        w = sizes[name]
        wp = -(-w // LANES) * LANES
        src[name] = (off_s, w)
        dst[name] = (off_d, wp)
        off_s += w
        off_d += wp
    return order, src, dst, off_d


def _perm_w_in(w_in):
    order, src, dst, _ = _in_layout(w_in.shape[0])
    parts = [_pad_cols(w_in[:, src[n][0]:src[n][0] + src[n][1]], dst[n][1]) for n in order]
    return jnp.concatenate(parts, axis=1).astype(BF16)


def _inproj_kernel(seg, x_ref, nw_ref, sc_ref, sh_ref, w_ref, lnw_ref, lnb_ref,
                   z_ref, xbc_ref, dt_ref, q_ref, k_ref, v_ref, kb_ref, vb_ref, qi_ref, ki_ref, wi_ref):
    h = _rms(x_ref[...]) * nw_ref[...]
    h = h * (1.0 + sc_ref[...]) + sh_ref[...]
    hb = h.astype(BF16)

    def mm(name):
        a, w = seg[name]
        return _dot(hb, w_ref[:, a:a + w])

    z_ref[...] = mm("z")
    xbc_ref[...] = mm("xbc")
    dt_ref[...] = mm("dt")
    q_ref[...] = mm("q").astype(BF16)
    k = mm("k")
    k_ref[...] = k
    kb_ref[...] = k.astype(BF16)
    v = mm("v")
    v_ref[...] = v
    vb_ref[...] = v.astype(BF16)
    qi_ref[...] = mm("qi").astype(BF16)
    wi_ref[...] = mm("wi")
    ki = mm("ki")
    lane = lax.broadcasted_iota(I32, ki.shape, 1)
    ok = lane < IDX_DIM
    mu = jnp.sum(jnp.where(ok, ki, 0.0), axis=-1, keepdims=True) * (1.0 / IDX_DIM)
    cen = jnp.where(ok, ki - mu, 0.0)
    var = jnp.sum(cen * cen, axis=-1, keepdims=True) * (1.0 / IDX_DIM)
    y = cen * lax.rsqrt(var + EPS) * lnw_ref[...] + lnb_ref[...]
    ki_ref[...] = y[:, :IDX_DIM]


def _in_proj(x, nw, sc, sh, w_perm, lnw, lnb, tm):
    t, d = x.shape
    _, _, dst, npad = _in_layout(d)
    tmod = sc.shape[0]
    mod_map = (lambda i: (0, 0)) if tmod == 1 else (lambda i: (i, 0))
    mod_rows = 1 if tmod == 1 else tm
    row = lambda w: pl.BlockSpec((tm, w), lambda i: (i, 0))
    d_ssd, d_att = dst["z"][1], dst["q"][1]
    kvw = KV_HEADS * ATT_HEAD_DIM
    outs = [("z", d_ssd, F32), ("xbc", dst["xbc"][1], F32), ("dt", LANES, F32), ("q", d_att, BF16),
            ("k", kvw, F32), ("v", kvw, F32), ("kb", kvw, BF16), ("vb", kvw, BF16),
            ("qi", IDX_HEADS * IDX_DIM, BF16), ("ki", IDX_DIM, F32), ("wi", LANES, F32)]
    res = pl.pallas_call(
        functools.partial(_inproj_kernel, dst),
        grid=(t // tm,),
        in_specs=[row(d),
                  pl.BlockSpec((1, d), lambda i: (0, 0)),
                  pl.BlockSpec((mod_rows, d), mod_map),
                  pl.BlockSpec((mod_rows, d), mod_map),
                  pl.BlockSpec((d, npad), lambda i: (0, 0)),
                  pl.BlockSpec((1, LANES), lambda i: (0, 0)),
                  pl.BlockSpec((1, LANES), lambda i: (0, 0))],
        out_specs=[row(w) for _, w, _ in outs],
        out_shape=[jax.ShapeDtypeStruct((t, w), dt) for _, w, dt in outs],
        compiler_params=_cparams(("arbitrary",)),
    )(x, nw, sc, sh, w_perm, lnw, lnb)
    return dict(zip([n for n, _, _ in outs], res))


def _ssd_prompt_kernel(n_pairs, xbc_ref, dt_ref, z_ref, cw_ref, cb_ref, dtb_ref, alog_ref, dsk_ref, nw_ref,
                       y_ref, st_ref, xprev, ht, ybuf):
    c = pl.program_id(0)
    q = SSD_CHUNK
    d_ssd = n_pairs * LANES
    gn = SSD_GROUPS * D_STATE

    @pl.when(c == 0)
    def _():
        xprev[...] = jnp.zeros_like(xprev)
        ht[...] = jnp.zeros_like(ht)

    x = xbc_ref[...]
    xp = xprev[...]
    rowi = lax.broadcasted_iota(I32, (q, 1), 0)
    acc = cb_ref[...] + cw_ref[CONV_W - 1:CONV_W, :] * x
    for k in range(1, CONV_W):
        sh = jnp.where(rowi < k, pltpu.roll(xp, k, 0), pltpu.roll(x, k, 0))
        acc = acc + cw_ref[CONV_W - 1 - k:CONV_W - k, :] * sh
    xprev[...] = x
    xc = _silu(acc)

    dt = _softplus(dt_ref[...] + dtb_ref[...])
    a_neg = -jnp.exp(alog_ref[...])
    r2 = lax.broadcasted_iota(I32, (q, q), 0)
    c2 = lax.broadcasted_iota(I32, (q, q), 1)
    tril = c2 <= r2
    a = _dot(tril.astype(F32), dt * a_neg, precision=HIGHEST)
    a_t = a.T
    dt_t = dt.T
    a_last = a[q - 1:q, :]
    wmat = jnp.exp(a_last - a) * dt
    emat = jnp.exp(a)
    cd = jnp.exp(a_last)
    lane = lax.broadcasted_iota(I32, (q, LANES), 1)
    left = lane < SSD_HEAD_DIM
    pairs_per_group = n_pairs // SSD_GROUPS

    bts, cbs, cgs = [], [], []
    for g in range(SSD_GROUPS):
        bg = xc[:, d_ssd + g * D_STATE:d_ssd + (g + 1) * D_STATE]
        cg = xc[:, d_ssd + gn + g * D_STATE:d_ssd + gn + (g + 1) * D_STATE].astype(BF16)
        bt = bg.T.astype(BF16)
        bts.append(bt)
        cgs.append(cg)
        cbs.append(_dot(cg, bt))

    def colb(m, h):
        return jnp.broadcast_to(m[:, h:h + 1], (q, LANES))

    for p in range(n_pairs):
        g = p // pairs_per_group
        h0, h1 = 2 * p, 2 * p + 1
        xpair = xc[:, p * LANES:(p + 1) * LANES]
        xpb = xpair.astype(BF16)
        yd = []
        for h in (h0, h1):
            diff = colb(a, h) - a_t[h:h + 1, :]
            decay = jnp.exp(jnp.where(tril, diff, -jnp.inf))
            sc = cbs[g] * decay * dt_t[h:h + 1, :]
            yd.append(_dot(sc.astype(BF16), xpb))
        y_diag = jnp.where(left, yd[0], yd[1])
        w_pair = jnp.where(left, colb(wmat, h0), colb(wmat, h1))
        e_pair = jnp.where(left, colb(emat, h0), colb(emat, h1))
        cd_pair = jnp.where(left[0:1, :], jnp.broadcast_to(cd[:, h0:h0 + 1], (1, LANES)),
                            jnp.broadcast_to(cd[:, h1:h1 + 1], (1, LANES)))
        hprev = ht[p]
        y_off = _dot(cgs[g], hprev.astype(BF16)) * e_pair
        states = _dot(bts[g], (xpair * w_pair).astype(BF16))
        ht[p] = hprev * cd_pair + states
        ybuf[:, p * LANES:(p + 1) * LANES] = y_diag + y_off + xpair * dsk_ref[:, p * LANES:(p + 1) * LANES]

    y = ybuf[...] * _silu(z_ref[...])
    y_ref[...] = (_rms(y) * nw_ref[...]).astype(BF16)

    @pl.when(c == pl.num_programs(0) - 1)
    def _():
        for p in range(n_pairs):
            st_ref[p * LANES:(p + 1) * LANES, :] = ht[p].T


def _ssd_prompt(xbc, dt_raw, z, conv_w, conv_b, dt_bias_p, a_log_p, dskip_row, norm_w):
    t, conv_dim = xbc.shape
    d_ssd = z.shape[1]
    n_pairs = d_ssd // LANES
    q = SSD_CHUNK
    full = lambda a: pl.BlockSpec(a.shape, lambda c: (0, 0))
    return pl.pallas_call(
        functools.partial(_ssd_prompt_kernel, n_pairs),
        grid=(t // q,),
        in_specs=[pl.BlockSpec((q, conv_dim), lambda c: (c, 0)),
                  pl.BlockSpec((q, LANES), lambda c: (c, 0)),
                  pl.BlockSpec((q, d_ssd), lambda c: (c, 0)),
                  full(conv_w), full(conv_b), full(dt_bias_p), full(a_log_p), full(dskip_row), full(norm_w)],
        out_specs=[pl.BlockSpec((q, d_ssd), lambda c: (c, 0)),
                   pl.BlockSpec((d_ssd, D_STATE), lambda c: (0, 0))],
        out_shape=[jax.ShapeDtypeStruct((t, d_ssd), BF16),
                   jax.ShapeDtypeStruct((d_ssd, D_STATE), F32)],
        scratch_shapes=[pltpu.VMEM((q, conv_dim), F32),
                        pltpu.VMEM((n_pairs, D_STATE, LANES), F32),
                        pltpu.VMEM((q, d_ssd), F32)],
        compiler_params=_cparams(("arbitrary",)),
    )(xbc, dt_raw, z, conv_w, conv_b, dt_bias_p, a_log_p, dskip_row, norm_w)


ROW_SUB = 64
BISECT_CAP = 320


def _attn_prompt_kernel(topk, tq, q_ref, qi_ref, wi_ref, ki2_ref, kb_ref, vb_ref, nw_ref, o_ref,
                        sc, wb, thr_b, mrun, lrun, acc_scr):
    i = pl.program_id(0)
    kc = tq
    n_chunks = i + 1
    n_heads = q_ref.shape[1] // ATT_HEAD_DIM
    q_per_kv = n_heads // KV_HEADS
    scale = ATT_HEAD_DIM ** -0.5
    wscale = (IDX_DIM ** -0.5) * (IDX_HEADS ** -0.5)
    n_sub = tq // ROW_SUB

    wi = wi_ref[...] * wscale
    for h in range(IDX_HEADS):
        wb[h] = jnp.broadcast_to(wi[:, h:h + 1], (tq, kc))

    row_g = i * tq + lax.broadcasted_iota(I32, (tq, kc), 0)
    col_l = lax.broadcasted_iota(I32, (tq, kc), 1)

    def score_chunk(j, carry):
        k0 = ki2_ref[0, pl.ds(j * kc, kc), :]
        k1 = ki2_ref[1, pl.ds(j * kc, kc), :]
        s = jnp.zeros((tq, kc), F32)
        for p in range(IDX_HEADS // 2):
            qp = qi_ref[:, p * LANES:(p + 1) * LANES]
            s = s + jnp.maximum(_dot_nt(qp, k0), 0.0) * wb[2 * p]
            s = s + jnp.maximum(_dot_nt(qp, k1), 0.0) * wb[2 * p + 1]
        sc[:, pl.ds(j * kc, kc)] = jnp.where(j * kc + col_l <= row_g, s, -jnp.inf)
        return carry

    lax.fori_loop(0, n_chunks, score_chunk, 0)

    def fold(fn, init, dtype):
        outs = []
        for r in range(n_sub):
            rows = slice(r * ROW_SUB, (r + 1) * ROW_SUB)

            def body(j, acc, rows=rows, r=r):
                for half in range(kc // LANES):
                    c0 = j * kc + half * LANES
                    acc = fn(acc, sc[rows, pl.ds(c0, LANES)], c0, r)
                return acc

            outs.append(lax.fori_loop(0, n_chunks, body, jnp.full((ROW_SUB, LANES), init, dtype)))
        return outs

    def rows_of(col, r):
        return col[r * ROW_SUB:(r + 1) * ROW_SUB]

    def count(pred):
        parts = fold(lambda acc, blk, c0, r: acc + jnp.where(pred(blk, c0, r), 1, 0), 0, I32)
        return jnp.concatenate([jnp.sum(p, axis=-1, keepdims=True) for p in parts], axis=0)

    def count_ge(t):
        tb = [jnp.broadcast_to(rows_of(t, r), (ROW_SUB, LANES)) for r in range(n_sub)]
        return count(lambda blk, c0, r: blk >= tb[r])

    big = jnp.float32(3e38)
    mins = fold(lambda acc, blk, c0, r: jnp.minimum(acc, jnp.where(blk == -jnp.inf, big, blk)), 3e38, F32)
    maxs = fold(lambda acc, blk, c0, r: jnp.maximum(acc, blk), -3e38, F32)
    lo0 = jnp.concatenate([jnp.min(p, axis=-1, keepdims=True) for p in mins], axis=0)
    hi0 = jnp.concatenate([jnp.max(p, axis=-1, keepdims=True) for p in maxs], axis=0)

    n_valid = i * tq + lax.broadcasted_iota(I32, (tq, 1), 0) + 1
    done0 = (n_valid <= topk).astype(I32)

    def cond(st):
        it, lo, hi, thr, done, stalled = st
        return (it < BISECT_CAP) & (jnp.min(done) == 0)

    def halve(st):
        it, lo, hi, thr, done, stalled = st
        mid = 0.5 * lo + 0.5 * hi
        n = count_ge(mid)
        live = done == 0
        exact = live & (n == topk)
        stall = live & jnp.logical_not(exact) & ((mid <= lo) | (mid >= hi))
        move = live & jnp.logical_not(exact) & jnp.logical_not(stall)
        up = n >= topk
        return (it + 1,
                jnp.where(move & up, mid, lo),
                jnp.where(move & jnp.logical_not(up), mid, hi),
                jnp.where(exact, mid, thr),
                jnp.where(exact | stall, 1, done),
                jnp.where(stall, 1, stalled))

    st = lax.while_loop(cond, halve, (jnp.int32(0), lo0, hi0, lo0, done0, jnp.zeros((tq, 1), I32)))
    _, lo, hi, thr, _, stalled = st
    n_hi = count_ge(hi)
    thr = jnp.where(stalled == 1, jnp.where(n_hi >= topk, hi, lo), thr)
    n_ge = count_ge(thr)

    @pl.when(jnp.max(n_ge) > topk)
    def _():
        tb = [jnp.broadcast_to(rows_of(thr, r), (ROW_SUB, LANES)) for r in range(n_sub)]
        n_gt = count(lambda blk, c0, r: blk > tb[r])
        need = topk - n_gt
        lane_i = lax.broadcasted_iota(I32, (ROW_SUB, LANES), 1)
        n_bits = max(int(sc.shape[1]).bit_length(), 1)

        def idx_step(t, jlo):
            trial = jlo + jnp.left_shift(jnp.int32(1), n_bits - 1 - t)
            trb = [jnp.broadcast_to(rows_of(trial, r), (ROW_SUB, LANES)) for r in range(n_sub)]
            f = count(lambda blk, c0, r: (blk == tb[r]) & (c0 + lane_i < trb[r]))
            return jnp.where(f <= need - 1, trial, jlo)

        jlo = lax.fori_loop(0, n_bits, idx_step, jnp.zeros((tq, 1), I32))
        cut = jnp.where(n_ge > topk, jlo + 1, jnp.int32(2 ** 30))
        lane_t = lax.broadcasted_iota(I32, (tq, LANES), 1)
        thr_t = jnp.broadcast_to(thr, (tq, LANES))
        cut_t = jnp.broadcast_to(cut, (tq, LANES))

        def drop(b, carry):
            blk = sc[:, pl.ds(b * LANES, LANES)]
            gone = (blk == thr_t) & (b * LANES + lane_t >= cut_t)
            sc[:, pl.ds(b * LANES, LANES)] = jnp.where(gone, -jnp.inf, blk)
            return carry

        lax.fori_loop(0, n_chunks * (kc // LANES), drop, 0)

    thr_b[...] = jnp.broadcast_to(thr, (tq, kc))

    mrun[...] = jnp.full(mrun.shape, NEG_BIG, F32)
    lrun[...] = jnp.zeros(lrun.shape, F32)
    acc_scr[...] = jnp.zeros(acc_scr.shape, F32)

    def halves(x):
        return [x[:, k * LANES:(k + 1) * LANES] for k in range(kc // LANES)]

    def logits(j, h):
        g = h // q_per_kv
        kj = kb_ref[pl.ds(j * kc, kc), g * ATT_HEAD_DIM:(g + 1) * ATT_HEAD_DIM]
        qh = q_ref[:, h * ATT_HEAD_DIM:(h + 1) * ATT_HEAD_DIM]
        return _dot_nt(qh, kj) * scale

    def max_chunk(j, carry):
        sel = sc[:, pl.ds(j * kc, kc)] >= thr_b[...]
        for h in range(n_heads):
            lg = jnp.where(sel, logits(j, h), NEG_BIG)
            mrun[h] = functools.reduce(jnp.maximum, halves(lg), mrun[h])
        return carry

    lax.fori_loop(0, n_chunks, max_chunk, 0)
    for h in range(n_heads):
        mrun[h] = jnp.broadcast_to(jnp.max(mrun[h], axis=-1, keepdims=True), (tq, LANES))

    def sum_chunk(j, carry):
        sel = sc[:, pl.ds(j * kc, kc)] >= thr_b[...]
        for h in range(n_heads):
            g = h // q_per_kv
            vj = vb_ref[pl.ds(j * kc, kc), g * ATT_HEAD_DIM:(g + 1) * ATT_HEAD_DIM]
            m = mrun[h]
            lg = logits(j, h)
            p = jnp.where(sel, jnp.exp(lg - jnp.concatenate([m] * (kc // LANES), axis=1)), 0.0)
            lrun[h] = functools.reduce(jnp.add, halves(p), lrun[h])
            acc_scr[h] = acc_scr[h] + _dot(p.astype(BF16), vj)
        return carry

    lax.fori_loop(0, n_chunks, sum_chunk, 0)

    ss = jnp.zeros((tq, 1), F32)
    for h in range(n_heads):
        o = acc_scr[h] / jnp.sum(lrun[h], axis=-1, keepdims=True)
        acc_scr[h] = o
        ss = ss + jnp.sum(o * o, axis=-1, keepdims=True)
    inv = lax.rsqrt(ss * (1.0 / (n_heads * ATT_HEAD_DIM)) + EPS)
    for h in range(n_heads):
        sl = slice(h * ATT_HEAD_DIM, (h + 1) * ATT_HEAD_DIM)
        o_ref[:, sl] = (acc_scr[h] * inv * nw_ref[:, sl]).astype(BF16)


def _attn_prompt(q, qi, wi, ki2, kb, vb, norm_w, topk, tq):
    t, d_att = q.shape
    n_heads = d_att // ATT_HEAD_DIM
    full = lambda a: pl.BlockSpec(a.shape, lambda i: (0,) * a.ndim)
    return pl.pallas_call(
        functools.partial(_attn_prompt_kernel, topk, tq),
        grid=(t // tq,),
        in_specs=[pl.BlockSpec((tq, d_att), lambda i: (i, 0)),
                  pl.BlockSpec((tq, qi.shape[1]), lambda i: (i, 0)),
                  pl.BlockSpec((tq, LANES), lambda i: (i, 0)),
                  full(ki2), full(kb), full(vb), full(norm_w)],
        out_specs=pl.BlockSpec((tq, d_att), lambda i: (i, 0)),
        out_shape=jax.ShapeDtypeStruct((t, d_att), BF16),
        scratch_shapes=[pltpu.VMEM((tq, t), F32),
                        pltpu.VMEM((IDX_HEADS, tq, tq), F32),
                        pltpu.VMEM((tq, tq), F32),
                        pltpu.VMEM((n_heads, tq, LANES), F32),
                        pltpu.VMEM((n_heads, tq, LANES), F32),
                        pltpu.VMEM((n_heads, tq, ATT_HEAD_DIM), F32)],
        compiler_params=_cparams(("arbitrary",)),
    )(q, qi, wi, ki2, kb, vb, norm_w)


def _outproj_kernel(ya_ref, yb_ref, w_ref, x_ref, g1_ref, nw_ref, sc_ref, sh_ref, wr_ref, br_ref,
                    x1_ref, h2_ref, comb_ref):
    d_a = ya_ref.shape[1]
    m = _dot(ya_ref[...], w_ref[:d_a, :]) + _dot(yb_ref[...], w_ref[d_a:, :])
    x1 = x_ref[...] + g1_ref[...] * m
    x1_ref[...] = x1
    h2 = _rms(x1) * nw_ref[...]
    h2 = h2 * (1.0 + sc_ref[...]) + sh_ref[...]
    hb = h2.astype(BF16)
    h2_ref[...] = hb
    lg = _dot(hb, wr_ref[...]) + br_ref[...]
    lane = lax.broadcasted_iota(I32, lg.shape, 1)
    big = jnp.int32(4 * LANES)

    def rmax(v):
        return jnp.max(v, axis=-1, keepdims=True)

    def rmin(v):
        return jnp.min(v, axis=-1, keepdims=True)

    def rsum(v):
        return jnp.sum(v, axis=-1, keepdims=True)

    is_g = (lane >= N_EXPERTS) & (lane < N_EXPERTS + N_EGROUPS)
    mg = rmax(jnp.where(is_g, lg, -jnp.inf))
    sg = rsum(jnp.where(is_g, jnp.exp(lg - mg), 0.0))
    gsel = rmin(jnp.where(is_g & (lg == mg), lane - N_EXPERTS, big))
    pgsel = 1.0 / sg
    in_grp = (lane < N_EXPERTS) & (jnp.right_shift(lane, EXPERTS_PER_GROUP.bit_length() - 1) == gsel)
    me = rmax(jnp.where(in_grp, lg, -jnp.inf))
    ee = jnp.where(in_grp, jnp.exp(lg - me), 0.0)
    pe = ee / rsum(ee)
    p1 = rmax(jnp.where(in_grp, pe, -1.0))
    i1 = rmin(jnp.where(in_grp & (pe == p1), lane, big))
    rem = in_grp & (lane != i1)
    p2 = rmax(jnp.where(rem, pe, -1.0))
    i2 = rmin(jnp.where(rem & (pe == p2), lane, big))
    den = p1 + p2
    comb_ref[...] = jnp.where(lane == i1, pgsel * p1 / den, jnp.where(lane == i2, pgsel * p2 / den, 0.0))


def _out_proj(ya, yb, w_out_b, x, g1, nw2, sc2, sh2, wr, br, tm):
    t, d = x.shape
    d_a = ya.shape[1]
    tmod = g1.shape[0]
    mod_map = (lambda i: (0, 0)) if tmod == 1 else (lambda i: (i, 0))
    mod_rows = 1 if tmod == 1 else tm
    modspec = pl.BlockSpec((mod_rows, d), mod_map)
    row = lambda w: pl.BlockSpec((tm, w), lambda i: (i, 0))
    full = lambda a: pl.BlockSpec(a.shape, lambda i: (0, 0))
    return pl.pallas_call(
        _outproj_kernel,
        grid=(t // tm,),
        in_specs=[row(d_a), row(yb.shape[1]), full(w_out_b), row(d), modspec, full(nw2), modspec, modspec,
                  full(wr), full(br)],
        out_specs=[row(d), row(d), row(LANES)],
        out_shape=[jax.ShapeDtypeStruct((t, d), F32), jax.ShapeDtypeStruct((t, d), BF16),
                   jax.ShapeDtypeStruct((t, LANES), F32)],
        compiler_params=_cparams(("arbitrary",)),
    )(ya, yb, w_out_b, x, g1, nw2, sc2, sh2, wr, br)


def _moe_kernel(h_ref, comb_ref, wu_ref, wd_ref, o_ref):
    e = pl.program_id(1)

    @pl.when(e == 0)
    def _():
        o_ref[...] = jnp.zeros_like(o_ref)

    gu = _dot(h_ref[...], wu_ref[0].astype(BF16))
    de = gu.shape[1] // 2
    act = _silu(gu[:, :de]) * gu[:, de:]
    comb = comb_ref[...]
    lane = lax.broadcasted_iota(I32, comb.shape, 1)
    col = jnp.sum(jnp.where(lane == e, comb, 0.0), axis=-1, keepdims=True)
    o_ref[...] += col * _dot(act.astype(BF16), wd_ref[0].astype(BF16))


def _moe(h2, comb, w_up, w_down, tm):
    t, d = h2.shape
    ne, _, two_de = w_up.shape
    return pl.pallas_call(
        _moe_kernel,
        grid=(t // tm, ne),
        in_specs=[pl.BlockSpec((tm, d), lambda i, e: (i, 0)),
                  pl.BlockSpec((tm, LANES), lambda i, e: (i, 0)),
                  pl.BlockSpec((1, d, two_de), lambda i, e: (e, 0, 0)),
                  pl.BlockSpec((1, two_de // 2, d), lambda i, e: (e, 0, 0))],
        out_specs=pl.BlockSpec((tm, d), lambda i, e: (i, 0)),
        out_shape=jax.ShapeDtypeStruct((t, d), F32),
        compiler_params=_cparams(("arbitrary", "arbitrary")),
    )(h2, comb, w_up, w_down)


def _final_kernel(x1_ref, y_ref, g2_ref, nf_ref, xo_ref, o_ref):
    x2 = x1_ref[...] + g2_ref[...] * y_ref[...]
    xo_ref[...] = x2
    o_ref[...] = _rms(x2) * nf_ref[...]


def _final(x1, y, g2, nf, tm):
    t, d = x1.shape
    tmod = g2.shape[0]
    mod_map = (lambda i: (0, 0)) if tmod == 1 else (lambda i: (i, 0))
    row = pl.BlockSpec((tm, d), lambda i: (i, 0))
    return pl.pallas_call(
        _final_kernel,
        grid=(t // tm,),
        in_specs=[row, row, pl.BlockSpec((1 if tmod == 1 else tm, d), mod_map),
                  pl.BlockSpec((1, d), lambda i: (0, 0))],
        out_specs=[row, row],
        out_shape=[jax.ShapeDtypeStruct((t, d), F32), jax.ShapeDtypeStruct((t, d), F32)],
        compiler_params=_cparams(("arbitrary",)),
    )(x1, y, g2, nf)


def _ssd_prep_kernel(xbc_ref, p0_ref, p1_ref, p2_ref, cw_ref, cb_ref, dt_ref, dtb_ref, alog_ref, ex_ref,
                     xc_ref, xdt_ref, dec_ref):
    d_ssd = xdt_ref.shape[1]
    acc = (cb_ref[...] + cw_ref[0:1, :] * p0_ref[...] + cw_ref[1:2, :] * p1_ref[...]
           + cw_ref[2:3, :] * p2_ref[...] + cw_ref[3:4, :] * xbc_ref[...])
    xc = _silu(acc)
    xc_ref[...] = xc
    dt = _softplus(dt_ref[...] + dtb_ref[...])
    dec = jnp.exp(dt * (-jnp.exp(alog_ref[...])))
    xdt_ref[...] = _dot(dt, ex_ref[...], precision=HIGHEST) * xc[:, :d_ssd]
    dec_ref[...] = _dot(dec, ex_ref[...], precision=HIGHEST)


def _ssd_prep(xbc, p0, p1, p2, conv_w, conv_b, dt_raw, dt_bias_p, a_log_p, expand):
    b, conv_dim = xbc.shape
    d_ssd = expand.shape[1]
    args = (xbc, p0, p1, p2, conv_w, conv_b, dt_raw, dt_bias_p, a_log_p, expand)
    return pl.pallas_call(
        _ssd_prep_kernel,
        grid=(1,),
        in_specs=[pl.BlockSpec(a.shape, lambda i: (0, 0)) for a in args],
        out_specs=[pl.BlockSpec((b, conv_dim), lambda i: (0, 0)),
                   pl.BlockSpec((b, d_ssd), lambda i: (0, 0)),
                   pl.BlockSpec((b, d_ssd), lambda i: (0, 0))],
        out_shape=[jax.ShapeDtypeStruct((b, conv_dim), F32), jax.ShapeDtypeStruct((b, d_ssd), F32),
                   jax.ShapeDtypeStruct((b, d_ssd), F32)],
        compiler_params=_cparams(("arbitrary",)),
    )(*args)


def _ssd_step_kernel(n_pairs, xdt_ref, dec_ref, bm_ref, cm_ref, s_ref, so_ref, y_ref):
    r2 = lax.broadcasted_iota(I32, (LANES, LANES), 0)
    c2 = lax.broadcasted_iota(I32, (LANES, LANES), 1)
    eye = r2 == c2
    ones = jnp.ones((LANES, LANES), F32)
    pairs_per_group = n_pairs // SSD_GROUPS
    rows_per_pair = LANES // SSD_HEAD_DIM
    for p in range(n_pairs):
        g = p // pairs_per_group
        sl = slice(p * LANES, (p + 1) * LANES)
        hs = slice(p * rows_per_pair, (p + 1) * rows_per_pair)
        hb = s_ref[0, 0, hs].reshape(LANES, D_STATE)
        xd = jnp.where(eye, jnp.broadcast_to(xdt_ref[0, :, sl], (LANES, LANES)), 0.0)
        dd = jnp.where(eye, jnp.broadcast_to(dec_ref[0, :, sl], (LANES, LANES)), 0.0)
        bmat = jnp.broadcast_to(bm_ref[0, :, g * D_STATE:(g + 1) * D_STATE], (LANES, D_STATE))
        upd = _dot(xd, bmat, precision=HIGHEST)
        dcol = _dot(dd, ones, precision=HIGHEST)
        hn = hb * dcol + upd
        so_ref[0, 0, hs] = hn.reshape(rows_per_pair, SSD_HEAD_DIM, D_STATE)
        cmat = jnp.broadcast_to(cm_ref[0, :, g * D_STATE:(g + 1) * D_STATE], (8, D_STATE))
        y_ref[0, :, sl] = _dot_nt(cmat, hn, precision=HIGHEST)[0:1, :]


def _ssd_step(xdt, dec, bm, cm, state):
    b, d_ssd = xdt.shape
    n_pairs = d_ssd // LANES
    heads = d_ssd // SSD_HEAD_DIM
    r3 = lambda a: a.reshape(b, 1, a.shape[1])
    row = lambda w: pl.BlockSpec((1, 1, w), lambda i: (i, 0, 0))
    sspec = pl.BlockSpec((1, 1, heads, SSD_HEAD_DIM, D_STATE), lambda i: (0, i, 0, 0, 0))
    so, y = pl.pallas_call(
        functools.partial(_ssd_step_kernel, n_pairs),
        grid=(b,),
        in_specs=[row(d_ssd), row(d_ssd), row(bm.shape[1]), row(cm.shape[1]), sspec],
        out_specs=[sspec, row(d_ssd)],
        out_shape=[jax.ShapeDtypeStruct(state.shape, F32), jax.ShapeDtypeStruct((b, 1, d_ssd), F32)],
        compiler_params=_cparams(("arbitrary",)),
    )(r3(xdt), r3(dec), r3(bm), r3(cm), state)
    return so, y.reshape(b, d_ssd)


def _ssd_finish_kernel(y_ref, xs_ref, z_ref, dsk_ref, nw_ref, o_ref):
    y = (y_ref[...] + xs_ref[...] * dsk_ref[...]) * _silu(z_ref[...])
    o_ref[...] = (_rms(y) * nw_ref[...]).astype(BF16)


def _ssd_finish(y, xs, z, dskip_row, norm_w):
    args = (y, xs, z, dskip_row, norm_w)
    return pl.pallas_call(
        _ssd_finish_kernel,
        grid=(1,),
        in_specs=[pl.BlockSpec(a.shape, lambda i: (0, 0)) for a in args],
        out_specs=pl.BlockSpec(y.shape, lambda i: (0, 0)),
        out_shape=jax.ShapeDtypeStruct(y.shape, BF16),
        compiler_params=_cparams(("arbitrary",)),
    )(*args)


PAGE_PACK = 8


def _page_copy(cache_ref, buf, sem, pt_ref, b, p, slot):
    rows = cache_ref.shape[2]
    return pltpu.make_async_copy(cache_ref.at[0, pt_ref[b, p]], buf.at[slot, pl.ds(p * rows, rows)], sem.at[slot])


def _score_sample_kernel(n_pages, pt_ref, q8_ref, w8_ref, qi_ref, wi_ref, kin_ref, cache_ref, s_ref, buf, sem):
    b = pl.program_id(0)
    nb = pl.num_programs(0)
    slot = b % 2

    def start(bb, sl):
        def body(p, carry):
            _page_copy(cache_ref, buf, sem, pt_ref, bb, p, sl).start()
            return carry
        lax.fori_loop(0, n_pages, body, 0)

    @pl.when(b == 0)
    def _():
        start(0, 0)

    @pl.when(b + 1 < nb)
    def _():
        start(b + 1, 1 - slot)

    def wait(p, carry):
        _page_copy(cache_ref, buf, sem, pt_ref, b, p, slot).wait()
        return carry

    lax.fori_loop(0, n_pages, wait, 0)

    wscale = (IDX_DIM ** -0.5) * (IDX_HEADS ** -0.5)
    q8 = q8_ref[0]
    w8 = w8_ref[0] * wscale
    kdim = q8.shape[1]
    page_rows = buf.shape[2]

    def group(gi, carry):
        keys_t = buf[slot, pl.ds(gi * kdim, kdim), :].astype(BF16)
        r = jnp.maximum(_dot(q8, keys_t), 0.0) * w8
        s_ref[0, pl.ds(gi * PAGE_PACK, PAGE_PACK), :] = jnp.sum(
            r.reshape(PAGE_PACK, IDX_HEADS, page_rows), axis=1)
        return carry

    lax.fori_loop(0, n_pages // PAGE_PACK, group, 0)
    tail = s_ref.shape[1] - n_pages
    kn = jnp.broadcast_to(kin_ref[0], (page_rows, kin_ref.shape[2])).astype(BF16)
    dn = _dot_nt(qi_ref[0], kn)
    sn = jnp.sum(jnp.maximum(dn, 0.0) * (wi_ref[0] * wscale), axis=0, keepdims=True)
    r = lax.broadcasted_iota(I32, (tail, page_rows), 0)
    c = lax.broadcasted_iota(I32, (tail, page_rows), 1)
    s_ref[0, n_pages:, :] = jnp.where((r == 0) & (c == 0), jnp.broadcast_to(sn, (tail, page_rows)), -jnp.inf)


def _score_sample(page_table, q8, w8, qi3, wi3, ki_new3, cache_kit, tail_rows):
    b, n_pages = page_table.shape
    idx_dim, page_rows = cache_kit.shape[2], cache_kit.shape[3]
    blk = lambda a: pl.BlockSpec((1,) + a.shape[1:], lambda i, pt: (i, 0, 0))
    gs = pltpu.PrefetchScalarGridSpec(
        num_scalar_prefetch=1,
        grid=(b,),
        in_specs=[blk(q8), blk(w8), blk(qi3), blk(wi3), blk(ki_new3), pl.BlockSpec(memory_space=pl.ANY)],
        out_specs=pl.BlockSpec((1, n_pages + tail_rows, page_rows), lambda i, pt: (i, 0, 0)),
        scratch_shapes=[pltpu.VMEM((2, n_pages * idx_dim, page_rows), F32),
                        pltpu.SemaphoreType.DMA((2,))],
    )
    return pl.pallas_call(
        functools.partial(_score_sample_kernel, n_pages),
        grid_spec=gs,
        out_shape=jax.ShapeDtypeStruct((b, n_pages + tail_rows, page_rows), F32),
        compiler_params=_cparams(("arbitrary",)),
    )(page_table, q8, w8, qi3, wi3, ki_new3, cache_kit)


def _select_sample_kernel(topk, s_ref, idx_ref, rm):
    s = s_ref[0]
    n_rows, width = s.shape
    pos = lax.broadcasted_iota(I32, s.shape, 0) * width + lax.broadcasted_iota(I32, s.shape, 1)

    def total(v):
        return jnp.sum(jnp.sum(v, axis=-1, keepdims=True), axis=0, keepdims=True)

    def count(mask):
        return total(jnp.where(mask, 1, 0))

    valid = s > -jnp.inf
    lo0 = jnp.min(jnp.min(jnp.where(valid, s, 3e38), axis=-1, keepdims=True), axis=0, keepdims=True)
    hi0 = jnp.max(jnp.max(s, axis=-1, keepdims=True), axis=0, keepdims=True)
    done0 = (count(valid) <= topk).astype(I32)

    def cond(st):
        it, lo, hi, thr, done, stalled = st
        return (it < BISECT_CAP) & (jnp.min(done) == 0)

    def halve(st):
        it, lo, hi, thr, done, stalled = st
        mid = 0.5 * lo + 0.5 * hi
        n = count(s >= mid)
        live = done == 0
        exact = live & (n == topk)
        stall = live & jnp.logical_not(exact) & ((mid <= lo) | (mid >= hi))
        move = live & jnp.logical_not(exact) & jnp.logical_not(stall)
        up = n >= topk
        return (it + 1,
                jnp.where(move & up, mid, lo),
                jnp.where(move & jnp.logical_not(up), mid, hi),
                jnp.where(exact, mid, thr),
                jnp.where(exact | stall, 1, done),
                jnp.where(stall, 1, stalled))

    st = lax.while_loop(cond, halve, (jnp.int32(0), lo0, hi0, lo0, done0, jnp.zeros((1, 1), I32)))
    _, lo, hi, thr, _, stalled = st
    thr = jnp.where(stalled == 1, jnp.where(count(s >= hi) >= topk, hi, lo), thr)
    need = topk - count(s > thr)
    tied = s == thr
    n_bits = max(int(n_rows * width).bit_length(), 1)

    def idx_step(t, jlo):
        trial = jlo + jnp.left_shift(jnp.int32(1), n_bits - 1 - t)
        return jnp.where(count(tied & (pos < trial)) <= need - 1, trial, jlo)

    jlo = lax.fori_loop(0, n_bits, idx_step, jnp.zeros((1, 1), I32))
    sel = (s > thr) | (tied & (pos <= jlo))
    self = jnp.where(sel, 1.0, 0.0)
    ra = lax.broadcasted_iota(I32, (width, width), 0)
    ca = lax.broadcasted_iota(I32, (width, width), 1)
    before = _dot(self.astype(BF16), jnp.where(ra < ca, 1.0, 0.0).astype(BF16))
    rb = lax.broadcasted_iota(I32, (n_rows, n_rows), 0)
    cb = lax.broadcasted_iota(I32, (n_rows, n_rows), 1)
    row_cnt = jnp.broadcast_to(jnp.sum(self, axis=-1, keepdims=True), s.shape)
    above = _dot(jnp.where(cb < rb, 1.0, 0.0).astype(BF16), row_cnt.astype(BF16))
    rm[...] = jnp.where(sel, before + above, -1.0)
    rank = lax.broadcasted_iota(I32, (topk, width), 0).astype(F32)
    lane = lax.broadcasted_iota(I32, (topk, width), 1)

    def place(p, acc):
        hit = rank == jnp.broadcast_to(rm[pl.ds(p, 1), :], (topk, width))
        return acc + jnp.where(hit, p * width + lane, 0)

    acc = lax.fori_loop(0, n_rows, place, jnp.zeros((topk, width), I32))
    idx_ref[0] = jnp.sum(acc, axis=-1, keepdims=True)


def _select_sample(s3, topk):
    b, n_rows, width = s3.shape
    return pl.pallas_call(
        functools.partial(_select_sample_kernel, topk),
        grid=(b,),
        in_specs=[pl.BlockSpec((1, n_rows, width), lambda i: (i, 0, 0))],
        out_specs=pl.BlockSpec((1, topk, 1), lambda i: (i, 0, 0)),
        out_shape=jax.ShapeDtypeStruct((b, topk, 1), I32),
        scratch_shapes=[pltpu.VMEM((n_rows, width), F32)],
        compiler_params=_cparams(("arbitrary",)),
    )(s3)


def _row_copy(src, dst, sem, src_row, dst_row):
    return pltpu.make_async_copy(src.at[pl.ds(src_row, KV_HEADS)], dst.at[pl.ds(dst_row, KV_HEADS)], sem)


def _attn_sample_kernel(topk, past_len, page_rows, n_pages, idx_ref, pt_ref, q_ref, nw_ref, ck_ref, cv_ref,
                        kn_ref, vn_ref, o_ref, kbuf, vbuf, sem):
    b = pl.program_id(0)

    def start(r, carry):
        j = idx_ref[b, r]
        pg = pt_ref[b, jnp.minimum(j // page_rows, n_pages - 1)]
        row = (pg * page_rows + j % page_rows) * KV_HEADS

        @pl.when(j < past_len)
        def _():
            _row_copy(ck_ref, kbuf, sem.at[0], row, r * KV_HEADS).start()
            _row_copy(cv_ref, vbuf, sem.at[1], row, r * KV_HEADS).start()

        @pl.when(j >= past_len)
        def _():
            _row_copy(kn_ref, kbuf, sem.at[0], b * KV_HEADS, r * KV_HEADS).start()
            _row_copy(vn_ref, vbuf, sem.at[1], b * KV_HEADS, r * KV_HEADS).start()

        return carry

    lax.fori_loop(0, topk, start, 0, unroll=8)

    def wait(r, carry):
        _row_copy(ck_ref, kbuf, sem.at[0], 0, r * KV_HEADS).wait()
        _row_copy(cv_ref, vbuf, sem.at[1], 0, r * KV_HEADS).wait()
        return carry

    lax.fori_loop(0, topk, wait, 0, unroll=8)

    scale = ATT_HEAD_DIM ** -0.5
    outs = []
    ss = jnp.zeros((1, 1), F32)
    for g in range(KV_HEADS):
        kg = kbuf[pl.ds(g, topk, stride=KV_HEADS), :].astype(BF16)
        vg = vbuf[pl.ds(g, topk, stride=KV_HEADS), :].astype(BF16)
        lg = _dot_nt(q_ref[0, g], kg) * scale
        m = jnp.max(lg, axis=-1, keepdims=True)
        p = jnp.exp(lg - m)
        p = p / jnp.sum(p, axis=-1, keepdims=True)
        o = _dot(p.astype(BF16), vg)
        rows = lax.broadcasted_iota(I32, o.shape, 0)
        o = jnp.where(rows < q_ref.shape[2] // 2, o, 0.0)
        outs.append(o)
        ss = ss + jnp.sum(jnp.sum(o * o, axis=-1, keepdims=True), axis=0, keepdims=True)
    n_feat = KV_HEADS * (q_ref.shape[2] // 2) * ATT_HEAD_DIM
    inv = lax.rsqrt(ss * (1.0 / n_feat) + EPS)
    for g in range(KV_HEADS):
        o_ref[0, g] = (outs[g] * inv * nw_ref[g]).astype(BF16)


def _attn_sample(idx, page_table, q4, nw3, ck2, cv2, kn2, vn2, past_len, page_rows):
    b, topk = idx.shape
    n_pages = page_table.shape[1]
    gs = pltpu.PrefetchScalarGridSpec(
        num_scalar_prefetch=2,
        grid=(b,),
        in_specs=[pl.BlockSpec((1,) + q4.shape[1:], lambda i, a, c: (i, 0, 0, 0)),
                  pl.BlockSpec(nw3.shape, lambda i, a, c: (0, 0, 0)),
                  pl.BlockSpec(memory_space=pl.ANY), pl.BlockSpec(memory_space=pl.ANY),
                  pl.BlockSpec(memory_space=pl.ANY), pl.BlockSpec(memory_space=pl.ANY)],
        out_specs=pl.BlockSpec((1,) + q4.shape[1:], lambda i, a, c: (i, 0, 0, 0)),
        scratch_shapes=[pltpu.VMEM((topk * KV_HEADS, ATT_HEAD_DIM), F32),
                        pltpu.VMEM((topk * KV_HEADS, ATT_HEAD_DIM), F32),
                        pltpu.SemaphoreType.DMA((2,))],
    )
    return pl.pallas_call(
        functools.partial(_attn_sample_kernel, topk, past_len, page_rows, n_pages),
        grid_spec=gs,
        out_shape=jax.ShapeDtypeStruct(q4.shape, BF16),
        compiler_params=_cparams(("arbitrary",)),
    )(idx, page_table, q4, nw3, ck2, cv2, kn2, vn2)


def _row(v, width=None):
    v = v.reshape(1, -1)
    return v if width is None else _pad_cols(v, width)


def _layer_params(p):
    d = p["w_in"].shape[0]
    wr = jnp.concatenate([p["w_router_e"], p["w_router_g"]], axis=1)
    br = jnp.concatenate([p["b_router_e"], p["b_router_g"]])
    return dict(
        w_perm=_perm_w_in(p["w_in"]),
        w_out_b=p["w_out"].astype(BF16),
        wr=_pad_cols(wr, LANES).astype(BF16),
        br=_row(br, LANES),
        nw1=_row(p["norm1_w"]), nw2=_row(p["norm2_w"]),
        lnw=_row(p["ln_kidx_w"], LANES), lnb=_row(p["ln_kidx_b"], LANES),
        dt_bias=_row(p["dt_bias"], LANES), a_log=_row(p["a_log"], LANES),
        dskip=_row(jnp.repeat(p["d_skip"], SSD_HEAD_DIM)),
        norm_ssd=_row(p["norm_ssd_w"]), norm_att=_row(p["norm_att_w"]),
        conv_w=p["conv_w"], conv_b=_row(p["conv_b"]),
        d_ssd=d // 2,
    )


def _ffn(x, ya, yb, mod, lp, p, tm):
    x1, h2, comb = _out_proj(ya, yb, lp["w_out_b"], x, mod[2], lp["nw2"], mod[4], mod[3], lp["wr"], lp["br"], tm)
    y = _moe(h2, comb, p["w_exp_up"], p["w_exp_down"], tm)
    return x1, y


def _prompt_layer(x, mod, lp, p):
    t, d = x.shape
    pr = _in_proj(x, lp["nw1"], mod[1], mod[0], lp["w_perm"], lp["lnw"], lp["lnb"], tm=256)
    y_ssd, st = _ssd_prompt(pr["xbc"], pr["dt"], pr["z"], lp["conv_w"], lp["conv_b"], lp["dt_bias"], lp["a_log"],
                            lp["dskip"], lp["norm_ssd"])
    ki = pr["ki"]
    zeros = jnp.zeros_like(ki)
    ki2 = jnp.stack([jnp.concatenate([ki, zeros], axis=1), jnp.concatenate([zeros, ki], axis=1)]).astype(BF16)
    topk = min(TOPK_MAX, t // 4)
    y_att = _attn_prompt(pr["q"], pr["qi"], pr["wi"], ki2, pr["kb"], pr["vb"], lp["norm_att"], topk, tq=256)
    x1, y = _ffn(x, y_ssd, y_att, mod, lp, p, tm=512)
    conv_new = jnp.concatenate([jnp.zeros((CONV_W - 1, pr["xbc"].shape[1]), F32), pr["xbc"]])[-(CONV_W - 1):]
    return x1, y, (pr["k"], pr["v"], ki, conv_new, st)


def _sample_layer(x, mod, lp, p, cache_k, cache_v, cache_ki, conv_prev, ssm_prev, page_table):
    b, d = x.shape
    d_ssd = lp["d_ssd"]
    heads = d_ssd // SSD_HEAD_DIM
    gn = SSD_GROUPS * D_STATE
    pr = _in_proj(x, lp["nw1"], mod[1], mod[0], lp["w_perm"], lp["lnw"], lp["lnb"], tm=b)
    expand = (jnp.arange(LANES)[:, None] == (jnp.arange(d_ssd)[None, :] // SSD_HEAD_DIM)).astype(F32)
    xc, xdt, dec = _ssd_prep(pr["xbc"], conv_prev[:, 0], conv_prev[:, 1], conv_prev[:, 2], lp["conv_w"], lp["conv_b"],
                             pr["dt"], lp["dt_bias"], lp["a_log"], expand)
    xs, bm, cm = xc[:, :d_ssd], xc[:, d_ssd:d_ssd + gn], xc[:, d_ssd + gn:]
    st5 = ssm_prev.reshape((1, b, heads, SSD_HEAD_DIM, D_STATE))
    st_new, y = _ssd_step(xdt, dec, bm, cm, st5)
    y_ssd = _ssd_finish(y, xs, pr["z"], lp["dskip"], lp["norm_ssd"])
    conv_new = jnp.concatenate([conv_prev[:, 1:], pr["xbc"][:, None, :]], axis=1)
    n_pool, page_rows = cache_k.shape[0], cache_k.shape[1]
    n_pages = page_table.shape[1]
    past_len = n_pages * page_rows
    topk = min(TOPK_MAX, (past_len + 1) // 4)
    qi3 = pr["qi"].reshape(b, IDX_HEADS, IDX_DIM)
    wi3 = pr["wi"][:, :IDX_HEADS].reshape(b, IDX_HEADS, 1)
    eye = jnp.eye(PAGE_PACK, dtype=BF16)
    q8 = (eye[None, :, None, :, None] * qi3[:, None, :, None, :]).reshape(b, PAGE_PACK * IDX_HEADS,
                                                                         PAGE_PACK * IDX_DIM)
    w8 = jnp.tile(wi3, (1, PAGE_PACK, 1))
    tail_rows = 8
    cache_kit = jnp.swapaxes(cache_ki, -1, -2)[None]
    s3 = _score_sample(page_table, q8, w8, qi3, wi3, pr["ki"].reshape(b, 1, IDX_DIM), cache_kit, tail_rows)
    idx = _select_sample(s3, topk).reshape(b, topk)
    n_heads = pr["q"].shape[1] // ATT_HEAD_DIM
    q_per_kv = n_heads // KV_HEADS
    q4 = jnp.pad(pr["q"].reshape(b, KV_HEADS, q_per_kv, ATT_HEAD_DIM), ((0, 0), (0, 0), (0, q_per_kv), (0, 0)))
    nw3 = jnp.pad(lp["norm_att"].reshape(KV_HEADS, q_per_kv, ATT_HEAD_DIM), ((0, 0), (0, q_per_kv), (0, 0)))
    ck2 = cache_k.reshape(n_pool * page_rows * KV_HEADS, ATT_HEAD_DIM)
    cv2 = cache_v.reshape(n_pool * page_rows * KV_HEADS, ATT_HEAD_DIM)
    kn2 = pr["k"].reshape(b * KV_HEADS, ATT_HEAD_DIM)
    vn2 = pr["v"].reshape(b * KV_HEADS, ATT_HEAD_DIM)
    o4 = _attn_sample(idx, page_table, q4, nw3, ck2, cv2, kn2, vn2, past_len, page_rows)
    y_att = o4[:, :, :q_per_kv].reshape(b, n_heads * ATT_HEAD_DIM)
    x1, y = _ffn(x, y_ssd, y_att, mod, lp, p, tm=b)
    return x1, y, (pr["k"], pr["v"], pr["ki"], conv_new, st_new.reshape(ssm_prev.shape))


def kernel(x_prompt, x_sample, cache_k, cache_v, cache_k_idx, state_conv, state_ssm, page_table, c_prompt, c_sample, w_ada, b_ada, norm1_w, norm2_w, w_in, conv_w, conv_b, dt_bias, a_log, d_skip, norm_ssd_w, ln_kidx_w, ln_kidx_b, norm_att_w, w_out, w_router_g, b_router_g, w_router_e, b_router_e, w_exp_up, w_exp_down, norm_f_w):
    batch, seq, d = x_prompt.shape
    dec_batch, dec_seq, _ = x_sample.shape
    assert batch == 1 and dec_seq == 1, "one prompt sequence and one new token per sample sequence"
    depth = w_ada.shape[0]
    heads = (d // 2) // SSD_HEAD_DIM
    xp = x_prompt.reshape(seq, d)
    xs = x_sample.reshape(dec_batch, d)
    n_c = batch + dec_batch
    c_all = jnp.pad(jnp.concatenate([c_prompt, c_sample]), ((0, -n_c % 8), (0, 0)))
    nf = _row(norm_f_w)
    outs_p, outs_s = [], []
    yp = ys = None
    for l in range(depth):
        p = dict(w_in=w_in[l], conv_w=conv_w[l], conv_b=conv_b[l], dt_bias=dt_bias[l], a_log=a_log[l],
                 d_skip=d_skip[l], norm_ssd_w=norm_ssd_w[l], ln_kidx_w=ln_kidx_w[l], ln_kidx_b=ln_kidx_b[l],
                 norm_att_w=norm_att_w[l], w_out=w_out[l], w_router_g=w_router_g[l], b_router_g=b_router_g[l],
                 w_router_e=w_router_e[l], b_router_e=b_router_e[l], w_exp_up=w_exp_up[l],
                 w_exp_down=w_exp_down[l], norm1_w=norm1_w[l], norm2_w=norm2_w[l])
        lp = _layer_params(p)
        mod = _ada_mod(c_all, w_ada[l], b_ada[l])
        mod_p = [mod[0:1, k * d:(k + 1) * d] for k in range(6)]
        mod_s = [mod[batch:n_c, k * d:(k + 1) * d] for k in range(6)]
        x1p, yp_moe, st_p = _prompt_layer(xp, mod_p, lp, p)
        x1s, ys_moe, st_s = _sample_layer(xs, mod_s, lp, p, cache_k[l], cache_v[l], cache_k_idx[l], state_conv[l],
                                          state_ssm[l], page_table)
        xp, yp = _final(x1p, yp_moe, mod_p[5], nf, tm=512)
        xs, ys = _final(x1s, ys_moe, mod_s[5], nf, tm=dec_batch)
        outs_p.append(st_p)
        outs_s.append(st_s)

    def stack(outs, n_rows, lead):
        k = jnp.stack([o[0].reshape(lead + (n_rows, KV_HEADS, ATT_HEAD_DIM)) for o in outs])
        v = jnp.stack([o[1].reshape(lead + (n_rows, KV_HEADS, ATT_HEAD_DIM)) for o in outs])
        ki = jnp.stack([o[2].reshape(lead + (n_rows, IDX_DIM)) for o in outs])
        return k, v, ki

    k_p, v_p, ki_p = stack(outs_p, seq, (batch,))
    conv_p = jnp.stack([o[3][None] for o in outs_p])
    ssm_p = jnp.stack([o[4].reshape(batch, heads, SSD_HEAD_DIM, D_STATE) for o in outs_p])
    k_s = jnp.stack([o[0].reshape(dec_batch, dec_seq, KV_HEADS, ATT_HEAD_DIM) for o in outs_s])
    v_s = jnp.stack([o[1].reshape(dec_batch, dec_seq, KV_HEADS, ATT_HEAD_DIM) for o in outs_s])
    ki_s = jnp.stack([o[2].reshape(dec_batch, dec_seq, IDX_DIM) for o in outs_s])
    conv_s = jnp.stack([o[3] for o in outs_s])
    ssm_s = jnp.stack([o[4] for o in outs_s])
    return (yp.reshape(batch, seq, d), ys.reshape(dec_batch, dec_seq, d), k_p, v_p, ki_p, conv_p, ssm_p,
            k_s, v_s, ki_s, conv_s, ssm_s)
```

```python
import functools

import numpy as np
import jax
import jax.numpy as jnp
from jax import lax
from jax.experimental import pallas as pl
from jax.experimental.pallas import tpu as pltpu

F32 = jnp.float32
BF16 = jnp.bfloat16
I32 = jnp.int32

SSD_HEAD_DIM = 64
SSD_GROUPS = 2
D_STATE = 128
CONV_W = 4
SSD_CHUNK = 128
ATT_HEAD_DIM = 128
KV_HEADS = 2
IDX_HEADS = 16
IDX_DIM = 64
TOPK_MAX = 256
N_EGROUPS = 4
EXPERTS_PER_GROUP = 8
N_EXPERTS = N_EGROUPS * EXPERTS_PER_GROUP
D_EXPERT = 512
EPS = 1e-6

LANES = 128
INT_MIN = -2 ** 31
NEG_BIG = -1e30
VMEM_LIMIT = 56 * 1024 * 1024
HIGHEST = lax.Precision.HIGHEST
Q_SCALE = ATT_HEAD_DIM ** -0.5 * 1.4426950408889634


def _cparams(sem):
    return pltpu.CompilerParams(dimension_semantics=sem, vmem_limit_bytes=VMEM_LIMIT)


def _dot(a, b, precision=None):
    return jnp.dot(a, b, preferred_element_type=F32, precision=precision)


def _dot_nt(a, b, precision=None):
    return lax.dot_general(a, b, (((1,), (1,)), ((), ())), preferred_element_type=F32, precision=precision)


def _silu(x):
    return x * jax.nn.sigmoid(x)


def _softplus(x):
    return jnp.maximum(x, 0.0) + jnp.log(1.0 + jnp.exp(-jnp.abs(x)))


def _rms(x):
    return x * lax.rsqrt(jnp.mean(x * x, axis=-1, keepdims=True) + EPS)


def _pad_cols(a, width):
    return jnp.pad(a, ((0, 0), (0, width - a.shape[1])))


def _ada_kernel(c_ref, w_ref, b_ref, o_ref):
    s = _silu(c_ref[...]).astype(BF16)
    o_ref[...] = _dot(s, w_ref[...].astype(BF16)) + b_ref[...]


def _ada_mod(c_all, w_ada, b_ada):
    r, d = c_all.shape
    n = w_ada.shape[1]
    tn = 1024
    return pl.pallas_call(
        _ada_kernel,
        grid=(n // tn,),
        in_specs=[pl.BlockSpec((r, d), lambda j: (0, 0)),
                  pl.BlockSpec((d, tn), lambda j: (0, j)),
                  pl.BlockSpec((1, tn), lambda j: (0, j))],
        out_specs=pl.BlockSpec((r, tn), lambda j: (0, j)),
        out_shape=jax.ShapeDtypeStruct((r, n), F32),
        compiler_params=_cparams(("arbitrary",)),
    )(c_all, w_ada, b_ada.reshape(1, n))


def _in_layout(d_model):
    d_ssd = d_model // 2
    d_att = d_model - d_ssd
    conv_dim = d_ssd + 2 * SSD_GROUPS * D_STATE
    ssd_heads = d_ssd // SSD_HEAD_DIM
    sizes = dict(z=d_ssd, xbc=conv_dim, dt=ssd_heads, q=d_att, k=KV_HEADS * ATT_HEAD_DIM,
                 v=KV_HEADS * ATT_HEAD_DIM, qi=IDX_HEADS * IDX_DIM, ki=IDX_DIM, wi=IDX_HEADS)
    order = ("z", "xbc", "dt", "q", "k", "v", "qi", "ki", "wi")
    src, dst, off_s, off_d = {}, {}, 0, 0
    for name in order:
        w = sizes[name]
        wp = -(-w // LANES) * LANES
        src[name] = (off_s, w)
        dst[name] = (off_d, wp)
        off_s += w
        off_d += wp
    return order, src, dst, off_d


def _perm_w_in(w_in):
    order, src, dst, _ = _in_layout(w_in.shape[0])
    parts = [_pad_cols(w_in[:, src[n][0]:src[n][0] + src[n][1]], dst[n][1]) for n in order]
    return jnp.concatenate(parts, axis=1).astype(BF16)


def _inproj_kernel(seg, x_ref, nw_ref, sc_ref, sh_ref, w_ref, lnw_ref, lnb_ref,
                   z_ref, xbc_ref, dt_ref, q_ref, k_ref, v_ref, kb_ref, vb_ref, qi_ref, ki_ref, wi_ref):
    h = _rms(x_ref[...]) * nw_ref[...]
    h = h * (1.0 + sc_ref[...]) + sh_ref[...]
    hb = h.astype(BF16)

    def mm(name):
        a, w = seg[name]
        return _dot(hb, w_ref[:, a:a + w])

    z_ref[...] = mm("z")
    xbc_ref[...] = mm("xbc")
    dt_ref[...] = mm("dt")
    q_ref[...] = (mm("q") * Q_SCALE).astype(BF16)
    k = mm("k")
    k_ref[...] = k
    kb_ref[...] = k.astype(BF16)
    v = mm("v")
    v_ref[...] = v
    vb_ref[...] = v.astype(BF16)
    qi_ref[...] = mm("qi").astype(BF16)
    wi_ref[...] = mm("wi")
    ki = mm("ki")
    lane = lax.broadcasted_iota(I32, ki.shape, 1)
    ok = lane < IDX_DIM
    mu = jnp.sum(jnp.where(ok, ki, 0.0), axis=-1, keepdims=True) * (1.0 / IDX_DIM)
    cen = jnp.where(ok, ki - mu, 0.0)
    var = jnp.sum(cen * cen, axis=-1, keepdims=True) * (1.0 / IDX_DIM)
    y = cen * lax.rsqrt(var + EPS) * lnw_ref[...] + lnb_ref[...]
    ki_ref[...] = y[:, :IDX_DIM]


def _in_proj(x, nw, sc, sh, w_perm, lnw, lnb, tm):
    t, d = x.shape
    _, _, dst, npad = _in_layout(d)
    tmod = sc.shape[0]
    mod_map = (lambda i: (0, 0)) if tmod == 1 else (lambda i: (i, 0))
    mod_rows = 1 if tmod == 1 else tm
    row = lambda w: pl.BlockSpec((tm, w), lambda i: (i, 0))
    d_ssd, d_att = dst["z"][1], dst["q"][1]
    kvw = KV_HEADS * ATT_HEAD_DIM
    outs = [("z", d_ssd, F32), ("xbc", dst["xbc"][1], F32), ("dt", LANES, F32), ("q", d_att, BF16),
            ("k", kvw, F32), ("v", kvw, F32), ("kb", kvw, BF16), ("vb", kvw, BF16),
            ("qi", IDX_HEADS * IDX_DIM, BF16), ("ki", IDX_DIM, F32), ("wi", LANES, F32)]
    res = pl.pallas_call(
        functools.partial(_inproj_kernel, dst),
        grid=(t // tm,),
        in_specs=[row(d),
                  pl.BlockSpec((1, d), lambda i: (0, 0)),
                  pl.BlockSpec((mod_rows, d), mod_map),
                  pl.BlockSpec((mod_rows, d), mod_map),
                  pl.BlockSpec((d, npad), lambda i: (0, 0)),
                  pl.BlockSpec((1, LANES), lambda i: (0, 0)),
                  pl.BlockSpec((1, LANES), lambda i: (0, 0))],
        out_specs=[row(w) for _, w, _ in outs],
        out_shape=[jax.ShapeDtypeStruct((t, w), dt) for _, w, dt in outs],
        compiler_params=_cparams(("arbitrary",)),
    )(x, nw, sc, sh, w_perm, lnw, lnb)
    return dict(zip([n for n, _, _ in outs], res))


def _ssd_prompt_kernel(n_pairs, xbc_ref, dt_ref, z_ref, cw_ref, cb_ref, dtb_ref, alog_ref, dsk_ref, nw_ref,
                       y_ref, st_ref, xprev, ht, ybuf):
    c = pl.program_id(0)
    q = SSD_CHUNK
    d_ssd = n_pairs * LANES
    gn = SSD_GROUPS * D_STATE

    @pl.when(c == 0)
    def _():
        xprev[...] = jnp.zeros_like(xprev)
        ht[...] = jnp.zeros_like(ht)

    x = xbc_ref[...]
    xp = xprev[...]
    rowi = lax.broadcasted_iota(I32, (q, 1), 0)
    acc = cb_ref[...] + cw_ref[CONV_W - 1:CONV_W, :] * x
    for k in range(1, CONV_W):
        sh = jnp.where(rowi < k, pltpu.roll(xp, k, 0), pltpu.roll(x, k, 0))
        acc = acc + cw_ref[CONV_W - 1 - k:CONV_W - k, :] * sh
    xprev[...] = x
    xc = _silu(acc)

    dt = _softplus(dt_ref[...] + dtb_ref[...])
    a_neg = -jnp.exp(alog_ref[...])
    r2 = lax.broadcasted_iota(I32, (q, q), 0)
    c2 = lax.broadcasted_iota(I32, (q, q), 1)
    tril = c2 <= r2
    a = _dot(tril.astype(F32), dt * a_neg, precision=HIGHEST)
    a_t = a.T
    dt_t = dt.T
    a_last = a[q - 1:q, :]
    wmat = jnp.exp(a_last - a) * dt
    emat = jnp.exp(a)
    cd = jnp.exp(a_last)
    lane = lax.broadcasted_iota(I32, (q, LANES), 1)
    left = lane < SSD_HEAD_DIM
    pairs_per_group = n_pairs // SSD_GROUPS

    bts, cbs, cgs = [], [], []
    for g in range(SSD_GROUPS):
        bg = xc[:, d_ssd + g * D_STATE:d_ssd + (g + 1) * D_STATE]
        cg = xc[:, d_ssd + gn + g * D_STATE:d_ssd + gn + (g + 1) * D_STATE].astype(BF16)
        bt = bg.T.astype(BF16)
        bts.append(bt)
        cgs.append(cg)
        cbs.append(_dot(cg, bt))

    def colb(m, h):
        return jnp.broadcast_to(m[:, h:h + 1], (q, LANES))

    for p in range(n_pairs):
        g = p // pairs_per_group
        h0, h1 = 2 * p, 2 * p + 1
        xpair = xc[:, p * LANES:(p + 1) * LANES]
        xpb = xpair.astype(BF16)
        yd = []
        for h in (h0, h1):
            diff = colb(a, h) - a_t[h:h + 1, :]
            decay = jnp.exp(jnp.where(tril, diff, -jnp.inf))
            sc = cbs[g] * decay * dt_t[h:h + 1, :]
            yd.append(_dot(sc.astype(BF16), xpb))
        y_diag = jnp.where(left, yd[0], yd[1])
        w_pair = jnp.where(left, colb(wmat, h0), colb(wmat, h1))
        e_pair = jnp.where(left, colb(emat, h0), colb(emat, h1))
        cd_pair = jnp.where(left[0:1, :], jnp.broadcast_to(cd[:, h0:h0 + 1], (1, LANES)),
                            jnp.broadcast_to(cd[:, h1:h1 + 1], (1, LANES)))
        hprev = ht[p]
        y_off = _dot(cgs[g], hprev.astype(BF16)) * e_pair
        states = _dot(bts[g], (xpair * w_pair).astype(BF16))
        ht[p] = hprev * cd_pair + states
        ybuf[:, p * LANES:(p + 1) * LANES] = y_diag + y_off + xpair * dsk_ref[:, p * LANES:(p + 1) * LANES]

    y = ybuf[...] * _silu(z_ref[...])
    y_ref[...] = (_rms(y) * nw_ref[...]).astype(BF16)

    @pl.when(c == pl.num_programs(0) - 1)
    def _():
        for p in range(n_pairs):
            st_ref[p * LANES:(p + 1) * LANES, :] = ht[p].T


def _ssd_prompt(xbc, dt_raw, z, conv_w, conv_b, dt_bias_p, a_log_p, dskip_row, norm_w):
    t, conv_dim = xbc.shape
    d_ssd = z.shape[1]
    n_pairs = d_ssd // LANES
    q = SSD_CHUNK
    full = lambda a: pl.BlockSpec(a.shape, lambda c: (0, 0))
    return pl.pallas_call(
        functools.partial(_ssd_prompt_kernel, n_pairs),
        grid=(t // q,),
        in_specs=[pl.BlockSpec((q, conv_dim), lambda c: (c, 0)),
                  pl.BlockSpec((q, LANES), lambda c: (c, 0)),
                  pl.BlockSpec((q, d_ssd), lambda c: (c, 0)),
                  full(conv_w), full(conv_b), full(dt_bias_p), full(a_log_p), full(dskip_row), full(norm_w)],
        out_specs=[pl.BlockSpec((q, d_ssd), lambda c: (c, 0)),
                   pl.BlockSpec((d_ssd, D_STATE), lambda c: (0, 0))],
        out_shape=[jax.ShapeDtypeStruct((t, d_ssd), BF16),
                   jax.ShapeDtypeStruct((d_ssd, D_STATE), F32)],
        scratch_shapes=[pltpu.VMEM((q, conv_dim), F32),
                        pltpu.VMEM((n_pairs, D_STATE, LANES), F32),
                        pltpu.VMEM((q, d_ssd), F32)],
        compiler_params=_cparams(("arbitrary",)),
    )(xbc, dt_raw, z, conv_w, conv_b, dt_bias_p, a_log_p, dskip_row, norm_w)


ROW_SUB = 64
BISECT_CAP = 320


def _attn_prompt_kernel(topk, tq, q_ref, qi_ref, wi_ref, ki2_ref, kb_ref, vx_ref, nw_ref, o_ref,
                        sc, wb, thr_b, mrun, acc_scr):
    i = pl.program_id(0)
    kc = tq
    n_chunks = i + 1
    n_heads = q_ref.shape[1] // ATT_HEAD_DIM
    q_per_kv = n_heads // KV_HEADS
    wscale = (IDX_DIM ** -0.5) * (IDX_HEADS ** -0.5)
    n_sub = tq // ROW_SUB

    wi = wi_ref[...] * wscale
    for h in range(IDX_HEADS):
        wb[h] = jnp.broadcast_to(wi[:, h:h + 1], (tq, kc))

    row_g = i * tq + lax.broadcasted_iota(I32, (tq, kc), 0)
    col_l = lax.broadcasted_iota(I32, (tq, kc), 1)

    def score_chunk(j, carry):
        k0 = ki2_ref[0, pl.ds(j * kc, kc), :]
        k1 = ki2_ref[1, pl.ds(j * kc, kc), :]
        s = jnp.zeros((tq, kc), F32)
        for p in range(IDX_HEADS // 2):
            qp = qi_ref[:, p * LANES:(p + 1) * LANES]
            s = s + jnp.maximum(_dot_nt(qp, k0), 0.0) * wb[2 * p]
            s = s + jnp.maximum(_dot_nt(qp, k1), 0.0) * wb[2 * p + 1]
        sc[:, pl.ds(j * kc, kc)] = jnp.where(j * kc + col_l <= row_g, s, -jnp.inf)
        return carry

    lax.fori_loop(0, n_chunks, score_chunk, 0)
    sc[:, pl.ds(n_chunks * kc, kc)] = jnp.full((tq, kc), -jnp.inf, F32)
    n_steps = (n_chunks + 1) // 2

    def fold(fn, init, dtype):
        outs = []
        for r in range(n_sub):
            rows = slice(r * ROW_SUB, (r + 1) * ROW_SUB)

            def body(j, acc, rows=rows, r=r):
                for part in range(2 * kc // LANES):
                    c0 = j * (2 * kc) + part * LANES
                    acc = fn(acc, sc[rows, pl.ds(c0, LANES)], c0, r)
                return acc

            outs.append(lax.fori_loop(0, n_steps, body, jnp.full((ROW_SUB, LANES), init, dtype)))
        return outs

    def rows_of(col, r):
        return col[r * ROW_SUB:(r + 1) * ROW_SUB]

    def count(pred):
        parts = fold(lambda acc, blk, c0, r: acc + jnp.where(pred(blk, c0, r), 1, 0), 0, I32)
        return jnp.concatenate([jnp.sum(p, axis=-1, keepdims=True) for p in parts], axis=0)

    def count_ge(t):
        tb = [jnp.broadcast_to(rows_of(t, r), (ROW_SUB, LANES)) for r in range(n_sub)]
        return count(lambda blk, c0, r: blk >= tb[r])

    big = jnp.float32(3e38)
    mins = fold(lambda acc, blk, c0, r: jnp.minimum(acc, jnp.where(blk == -jnp.inf, big, blk)), 3e38, F32)
    maxs = fold(lambda acc, blk, c0, r: jnp.maximum(acc, blk), -3e38, F32)
    lo0 = jnp.concatenate([jnp.min(p, axis=-1, keepdims=True) for p in mins], axis=0)
    hi0 = jnp.concatenate([jnp.max(p, axis=-1, keepdims=True) for p in maxs], axis=0)

    n_valid = i * tq + lax.broadcasted_iota(I32, (tq, 1), 0) + 1
    done0 = (n_valid <= topk).astype(I32)

    def cond(st):
        it, lo, hi, thr, done, stalled = st
        return (it < BISECT_CAP) & (jnp.min(done) == 0)

    def halve(st):
        it, lo, hi, thr, done, stalled = st
        mid = 0.5 * lo + 0.5 * hi
        n = count_ge(mid)
        live = done == 0
        exact = live & (n == topk)
        stall = live & jnp.logical_not(exact) & ((mid <= lo) | (mid >= hi))
        move = live & jnp.logical_not(exact) & jnp.logical_not(stall)
        up = n >= topk
        return (it + 1,
                jnp.where(move & up, mid, lo),
                jnp.where(move & jnp.logical_not(up), mid, hi),
                jnp.where(exact, mid, thr),
                jnp.where(exact | stall, 1, done),
                jnp.where(stall, 1, stalled))

    st = lax.while_loop(cond, halve, (jnp.int32(0), lo0, hi0, lo0, done0, jnp.zeros((tq, 1), I32)))
    _, lo, hi, thr, _, stalled = st
    n_hi = count_ge(hi)
    thr = jnp.where(stalled == 1, jnp.where(n_hi >= topk, hi, lo), thr)
    n_ge = count_ge(thr)

    @pl.when(jnp.max(n_ge) > topk)
    def _():
        tb = [jnp.broadcast_to(rows_of(thr, r), (ROW_SUB, LANES)) for r in range(n_sub)]
        n_gt = count(lambda blk, c0, r: blk > tb[r])
        need = topk - n_gt
        lane_i = lax.broadcasted_iota(I32, (ROW_SUB, LANES), 1)
        n_bits = max(int(sc.shape[1]).bit_length(), 1)

        def idx_step(t, jlo):
            trial = jlo + jnp.left_shift(jnp.int32(1), n_bits - 1 - t)
            trb = [jnp.broadcast_to(rows_of(trial, r), (ROW_SUB, LANES)) for r in range(n_sub)]
            f = count(lambda blk, c0, r: (blk == tb[r]) & (c0 + lane_i < trb[r]))
            return jnp.where(f <= need - 1, trial, jlo)

        jlo = lax.fori_loop(0, n_bits, idx_step, jnp.zeros((tq, 1), I32))
        cut = jnp.where(n_ge > topk, jlo + 1, jnp.int32(2 ** 30))
        lane_t = lax.broadcasted_iota(I32, (tq, LANES), 1)
        thr_t = jnp.broadcast_to(thr, (tq, LANES))
        cut_t = jnp.broadcast_to(cut, (tq, LANES))

        def drop(b, carry):
            blk = sc[:, pl.ds(b * LANES, LANES)]
            gone = (blk == thr_t) & (b * LANES + lane_t >= cut_t)
            sc[:, pl.ds(b * LANES, LANES)] = jnp.where(gone, -jnp.inf, blk)
            return carry

        lax.fori_loop(0, n_chunks * (kc // LANES), drop, 0)

    thr_b[...] = jnp.broadcast_to(thr, (tq, kc))

    mrun[...] = jnp.full(mrun.shape, NEG_BIG, F32)
    acc_scr[...] = jnp.zeros(acc_scr.shape, F32)
    vw = 2 * ATT_HEAD_DIM

    def halves(x):
        return [x[:, k * LANES:(k + 1) * LANES] for k in range(kc // LANES)]

    def logits(j, h):
        g = h // q_per_kv
        kj = kb_ref[pl.ds(j * kc, kc), g * ATT_HEAD_DIM:(g + 1) * ATT_HEAD_DIM]
        qh = q_ref[:, h * ATT_HEAD_DIM:(h + 1) * ATT_HEAD_DIM]
        return _dot_nt(qh, kj)

    def max_chunk(j, carry):
        sel = sc[:, pl.ds(j * kc, kc)] >= thr_b[...]
        for h in range(n_heads):
            lg = jnp.where(sel, logits(j, h), NEG_BIG)
            mrun[h] = functools.reduce(jnp.maximum, halves(lg), mrun[h])
        return carry

    lax.fori_loop(0, n_chunks, max_chunk, 0)
    for h in range(n_heads):
        mrun[h] = jnp.broadcast_to(jnp.max(mrun[h], axis=-1, keepdims=True), (tq, LANES))

    def sum_chunk(j, carry):
        sel = sc[:, pl.ds(j * kc, kc)] >= thr_b[...]
        for h in range(n_heads):
            g = h // q_per_kv
            vj = vx_ref[pl.ds(j * kc, kc), g * vw:(g + 1) * vw]
            m = mrun[h]
            p = jnp.where(sel, jnp.exp2(logits(j, h) - jnp.concatenate([m] * (kc // LANES), axis=1)), 0.0)
            acc_scr[h] = acc_scr[h] + _dot(p.astype(BF16), vj)
        return carry

    lax.fori_loop(0, n_chunks, sum_chunk, 0)

    ss = jnp.zeros((tq, 1), F32)
    for h in range(n_heads):
        o = acc_scr[h, :, :ATT_HEAD_DIM] / acc_scr[h, :, ATT_HEAD_DIM:]
        acc_scr[h, :, :ATT_HEAD_DIM] = o
        ss = ss + jnp.sum(o * o, axis=-1, keepdims=True)
    inv = lax.rsqrt(ss * (1.0 / (n_heads * ATT_HEAD_DIM)) + EPS)
    for h in range(n_heads):
        sl = slice(h * ATT_HEAD_DIM, (h + 1) * ATT_HEAD_DIM)
        o_ref[:, sl] = (acc_scr[h, :, :ATT_HEAD_DIM] * inv * nw_ref[:, sl]).astype(BF16)


def _attn_prompt(q, qi, wi, ki2, kb, vx, norm_w, topk, tq):
    t, d_att = q.shape
    n_heads = d_att // ATT_HEAD_DIM
    full = lambda a: pl.BlockSpec(a.shape, lambda i: (0,) * a.ndim)
    return pl.pallas_call(
        functools.partial(_attn_prompt_kernel, topk, tq),
        grid=(t // tq,),
        in_specs=[pl.BlockSpec((tq, d_att), lambda i: (i, 0)),
                  pl.BlockSpec((tq, qi.shape[1]), lambda i: (i, 0)),
                  pl.BlockSpec((tq, LANES), lambda i: (i, 0)),
                  full(ki2), full(kb), full(vx), full(norm_w)],
        out_specs=pl.BlockSpec((tq, d_att), lambda i: (i, 0)),
        out_shape=jax.ShapeDtypeStruct((t, d_att), BF16),
        scratch_shapes=[pltpu.VMEM((tq, t + 2 * tq), F32),
                        pltpu.VMEM((IDX_HEADS, tq, tq), F32),
                        pltpu.VMEM((tq, tq), F32),
                        pltpu.VMEM((n_heads, tq, LANES), F32),
                        pltpu.VMEM((n_heads, tq, 2 * ATT_HEAD_DIM), F32)],
        compiler_params=_cparams(("arbitrary",)),
    )(q, qi, wi, ki2, kb, vx, norm_w)


def _outproj_kernel(ya_ref, yb_ref, w_ref, x_ref, g1_ref, nw_ref, sc_ref, sh_ref, wr_ref, br_ref,
                    x1_ref, h2_ref, eid_ref, wts_ref, cnt_ref):
    d_a = ya_ref.shape[1]
    m = _dot(ya_ref[...], w_ref[:d_a, :]) + _dot(yb_ref[...], w_ref[d_a:, :])
    x1 = x_ref[...] + g1_ref[...] * m
    x1_ref[...] = x1
    h2 = _rms(x1) * nw_ref[...]
    h2 = h2 * (1.0 + sc_ref[...]) + sh_ref[...]
    h2_ref[...] = h2
    lg = _dot(h2.astype(BF16), wr_ref[...]) + br_ref[...]
    lane = lax.broadcasted_iota(I32, lg.shape, 1)
    big = jnp.int32(4 * LANES)

    def rmax(v):
        return jnp.max(v, axis=-1, keepdims=True)

    def rmin(v):
        return jnp.min(v, axis=-1, keepdims=True)

    def rsum(v):
        return jnp.sum(v, axis=-1, keepdims=True)

    is_g = (lane >= N_EXPERTS) & (lane < N_EXPERTS + N_EGROUPS)
    mg = rmax(jnp.where(is_g, lg, -jnp.inf))
    sg = rsum(jnp.where(is_g, jnp.exp(lg - mg), 0.0))
    gsel = rmin(jnp.where(is_g & (lg == mg), lane - N_EXPERTS, big))
    pgsel = 1.0 / sg
    in_grp = (lane < N_EXPERTS) & (jnp.right_shift(lane, EXPERTS_PER_GROUP.bit_length() - 1) == gsel)
    me = rmax(jnp.where(in_grp, lg, -jnp.inf))
    ee = jnp.where(in_grp, jnp.exp(lg - me), 0.0)
    pe = ee / rsum(ee)
    p1 = rmax(jnp.where(in_grp, pe, -1.0))
    i1 = rmin(jnp.where(in_grp & (pe == p1), lane, big))
    rem = in_grp & (lane != i1)
    p2 = rmax(jnp.where(rem, pe, -1.0))
    i2 = rmin(jnp.where(rem & (pe == p2), lane, big))
    den = p1 + p2
    eid_ref[...] = jnp.where(lane == 0, i1, jnp.where(lane == 1, i2, 0))
    wts_ref[...] = jnp.where(lane == 0, pgsel * p1 / den, jnp.where(lane == 1, pgsel * p2 / den, 0.0))

    @pl.when(pl.program_id(0) == 0)
    def _():
        cnt_ref[...] = jnp.zeros_like(cnt_ref)

    chosen = jnp.where((lane == i1) | (lane == i2), 1.0, 0.0)
    cnt_ref[...] += jnp.sum(chosen, axis=0, keepdims=True)


def _out_proj(ya, yb, w_out_b, x, g1, nw2, sc2, sh2, wr, br, tm):
    t, d = x.shape
    d_a = ya.shape[1]
    tmod = g1.shape[0]
    mod_map = (lambda i: (0, 0)) if tmod == 1 else (lambda i: (i, 0))
    mod_rows = 1 if tmod == 1 else tm
    modspec = pl.BlockSpec((mod_rows, d), mod_map)
    row = lambda w: pl.BlockSpec((tm, w), lambda i: (i, 0))
    full = lambda a: pl.BlockSpec(a.shape, lambda i: (0, 0))
    return pl.pallas_call(
        _outproj_kernel,
        grid=(t // tm,),
        in_specs=[row(d_a), row(yb.shape[1]), full(w_out_b), row(d), modspec, full(nw2), modspec, modspec,
                  full(wr), full(br)],
        out_specs=[row(d), row(d), row(LANES), row(LANES), pl.BlockSpec((1, LANES), lambda i: (0, 0))],
        out_shape=[jax.ShapeDtypeStruct((t, d), F32), jax.ShapeDtypeStruct((t, d), F32),
                   jax.ShapeDtypeStruct((t, LANES), I32), jax.ShapeDtypeStruct((t, LANES), F32),
                   jax.ShapeDtypeStruct((1, LANES), F32)],
        compiler_params=_cparams(("arbitrary",)),
    )(ya, yb, w_out_b, x, g1, nw2, sc2, sh2, wr, br)


MOE_TILE = 256


def _moe_pos_kernel(eid_ref, off_ref, pos_ref, carry):
    @pl.when(pl.program_id(0) == 0)
    def _():
        carry[...] = jnp.zeros_like(carry)

    eid = eid_ref[...]
    tm = eid.shape[0]
    i1, i2 = eid[:, 0:1], eid[:, 1:2]
    lane = lax.broadcasted_iota(I32, eid.shape, 1)
    chosen = jnp.where((lane == i1) | (lane == i2), 1.0, 0.0)
    r = lax.broadcasted_iota(I32, (tm, tm), 0)
    c = lax.broadcasted_iota(I32, (tm, tm), 1)
    earlier = _dot(jnp.where(c < r, 1.0, 0.0).astype(BF16), chosen.astype(BF16))
    row = earlier + carry[...] + off_ref[...]
    p1 = jnp.sum(jnp.where(lane == i1, row, 0.0), axis=-1, keepdims=True)
    p2 = jnp.sum(jnp.where(lane == i2, row, 0.0), axis=-1, keepdims=True)
    out = jnp.where(lane == 0, p1, jnp.where(lane == 1, p2, 0.0))
    pos_ref[...] = jnp.where(i1 >= 0, out, -1.0).astype(I32)
    carry[...] += jnp.sum(chosen, axis=0, keepdims=True)


def _moe_positions(eid_all, off_row, tm):
    t = eid_all.shape[0]
    return pl.pallas_call(
        _moe_pos_kernel,
        grid=(t // tm,),
        in_specs=[pl.BlockSpec((tm, LANES), lambda i: (i, 0)), pl.BlockSpec((1, LANES), lambda i: (0, 0))],
        out_specs=pl.BlockSpec((tm, LANES), lambda i: (i, 0)),
        out_shape=jax.ShapeDtypeStruct((t, LANES), I32),
        scratch_shapes=[pltpu.VMEM((1, LANES), F32)],
        compiler_params=_cparams(("arbitrary",)),
    )(eid_all, off_row)


def _token_copy(hp_ref, hs_ref, xbuf, sem, t_prompt, tok, slot, r, wait):
    dst = xbuf.at[slot, pl.ds(r, 1)]
    if wait:
        pltpu.make_async_copy(hp_ref.at[pl.ds(0, 1)], dst, sem.at[slot]).wait()
        return

    @pl.when(tok < t_prompt)
    def _():
        pltpu.make_async_copy(hp_ref.at[pl.ds(tok, 1)], dst, sem.at[slot]).start()

    @pl.when(tok >= t_prompt)
    def _():
        pltpu.make_async_copy(hs_ref.at[pl.ds(tok - t_prompt, 1)], dst, sem.at[slot]).start()


def _moe_grouped_kernel(t_prompt, te_ref, nu_ref, pos_ref, hp_ref, hs_ref, wu_ref, wd_ref, o_ref,
                        src, xbuf, wub, wdb, sem):
    g = pl.program_id(0)
    n_used = nu_ref[0]
    tmg = xbuf.shape[1]
    n_tok = pos_ref.shape[0] // 2
    slot = g % 2

    def gather(tile, sl, wait):
        def body(r, carry):
            tok = src[tile * tmg + r]

            @pl.when(tok >= 0)
            def _():
                _token_copy(hp_ref, hs_ref, xbuf, sem, t_prompt, tok, sl, r, wait)

            return carry

        lax.fori_loop(0, tmg, body, 0, unroll=8)

    @pl.when(g == 0)
    def _():
        def clear(r, carry):
            src[r] = -1
            return carry

        lax.fori_loop(0, src.shape[0], clear, 0, unroll=8)

        def fill(t, carry):
            p1 = pos_ref[2 * t]

            @pl.when(p1 >= 0)
            def _():
                src[p1] = t
                src[pos_ref[2 * t + 1]] = t

            return carry

        lax.fori_loop(0, n_tok, fill, 0)
        xbuf[...] = jnp.zeros_like(xbuf)
        gather(0, 0, False)

    @pl.when(g < n_used)
    def _():
        @pl.when(g + 1 < n_used)
        def _():
            gather(g + 1, 1 - slot, False)

        gather(g, slot, True)
        fresh = (g == 0) | (te_ref[g] != te_ref[jnp.maximum(g - 1, 0)])

        @pl.when(fresh)
        def _():
            wub[...] = wu_ref[0].astype(BF16)
            wdb[...] = wd_ref[0].astype(BF16)

        gu = _dot(xbuf[slot].astype(BF16), wub[...])
        de = gu.shape[1] // 2
        act = _silu(gu[:, :de]) * gu[:, de:]
        o_ref[...] = _dot(act.astype(BF16), wdb[...])

    @pl.when(g >= n_used)
    def _():
        o_ref[...] = jnp.zeros_like(o_ref)


def _moe_grouped(tile_expert, n_used, pos_flat, h2_p, h2_s, w_up, w_down):
    n_tiles = tile_expert.shape[0]
    t_prompt, d = h2_p.shape
    _, _, two_de = w_up.shape
    tmg = MOE_TILE
    gs = pltpu.PrefetchScalarGridSpec(
        num_scalar_prefetch=3,
        grid=(n_tiles,),
        in_specs=[pl.BlockSpec(memory_space=pl.ANY), pl.BlockSpec(memory_space=pl.ANY),
                  pl.BlockSpec((1, d, two_de), lambda g, te, nu, ps: (te[g], 0, 0)),
                  pl.BlockSpec((1, two_de // 2, d), lambda g, te, nu, ps: (te[g], 0, 0))],
        out_specs=pl.BlockSpec((tmg, d), lambda g, te, nu, ps: (g, 0)),
        scratch_shapes=[pltpu.SMEM((n_tiles * tmg,), I32),
                        pltpu.VMEM((2, tmg, d), F32),
                        pltpu.VMEM((d, two_de), BF16),
                        pltpu.VMEM((two_de // 2, d), BF16),
                        pltpu.SemaphoreType.DMA((2,))],
    )
    return pl.pallas_call(
        functools.partial(_moe_grouped_kernel, t_prompt),
        grid_spec=gs,
        out_shape=jax.ShapeDtypeStruct((n_tiles * tmg, d), F32),
        compiler_params=_cparams(("arbitrary",)),
    )(tile_expert, n_used, pos_flat, h2_p, h2_s, w_up, w_down)


def _combine_kernel(tok0, pos_ref, x1_ref, wts_ref, g2_ref, nf_ref, ys_ref, xo_ref, o_ref, ybuf, sem):
    i = pl.program_id(0)
    n = pl.num_programs(0)
    tm = x1_ref.shape[0]
    slot = i % 2

    def gather(tile, sl, wait):
        def body(r, carry):
            tok = tok0 + tile * tm + r
            for k in range(2):
                src_row = 0 if wait else pos_ref[2 * tok + k]
                cp = pltpu.make_async_copy(ys_ref.at[pl.ds(src_row, 1)], ybuf.at[sl, k, pl.ds(r, 1)], sem.at[sl])
                cp.wait() if wait else cp.start()
            return carry

        lax.fori_loop(0, tm, body, 0, unroll=8)

    @pl.when(i == 0)
    def _():
        gather(0, 0, False)

    @pl.when(i + 1 < n)
    def _():
        gather(i + 1, 1 - slot, False)

    gather(i, slot, True)
    w = wts_ref[...]
    y = w[:, 0:1] * ybuf[slot, 0] + w[:, 1:2] * ybuf[slot, 1]
    x2 = x1_ref[...] + g2_ref[...] * y
    xo_ref[...] = x2
    o_ref[...] = _rms(x2) * nf_ref[...]


def _combine(pos_flat, x1, wts, g2, nf, ys, tok0, tm):
    t, d = x1.shape
    tmod = g2.shape[0]
    mod_map = (lambda i, ps: (0, 0)) if tmod == 1 else (lambda i, ps: (i, 0))
    row = lambda w: pl.BlockSpec((tm, w), lambda i, ps: (i, 0))
    gs = pltpu.PrefetchScalarGridSpec(
        num_scalar_prefetch=1,
        grid=(t // tm,),
        in_specs=[row(d), row(LANES), pl.BlockSpec((1 if tmod == 1 else tm, d), mod_map),
                  pl.BlockSpec((1, d), lambda i, ps: (0, 0)), pl.BlockSpec(memory_space=pl.ANY)],
        out_specs=[row(d), row(d)],
        scratch_shapes=[pltpu.VMEM((2, 2, tm, d), F32), pltpu.SemaphoreType.DMA((2,))],
    )
    return pl.pallas_call(
        functools.partial(_combine_kernel, tok0),
        grid_spec=gs,
        out_shape=[jax.ShapeDtypeStruct((t, d), F32), jax.ShapeDtypeStruct((t, d), F32)],
        compiler_params=_cparams(("arbitrary",)),
    )(pos_flat, x1, wts, g2, nf, ys)


def _moe_plan(cnt, n_tiles):
    cnt = cnt[0, :N_EXPERTS].astype(I32)
    padded = (cnt + MOE_TILE - 1) // MOE_TILE * MOE_TILE
    ends = jnp.cumsum(padded)
    off_row = _pad_cols((ends - padded).astype(F32).reshape(1, N_EXPERTS), LANES)
    starts = jnp.arange(n_tiles, dtype=I32) * MOE_TILE
    tile_expert = jnp.minimum(jnp.sum(starts[:, None] >= ends[None, :], axis=1), N_EXPERTS - 1).astype(I32)
    return off_row, tile_expert, (ends[-1:] // MOE_TILE).astype(I32)


def _ssd_prep_kernel(xbc_ref, p0_ref, p1_ref, p2_ref, cw_ref, cb_ref, dt_ref, dtb_ref, alog_ref, ex_ref,
                     xc_ref, xdt_ref, dec_ref):
    d_ssd = xdt_ref.shape[1]
    acc = (cb_ref[...] + cw_ref[0:1, :] * p0_ref[...] + cw_ref[1:2, :] * p1_ref[...]
           + cw_ref[2:3, :] * p2_ref[...] + cw_ref[3:4, :] * xbc_ref[...])
    xc = _silu(acc)
    xc_ref[...] = xc
    dt = _softplus(dt_ref[...] + dtb_ref[...])
    dec = jnp.exp(dt * (-jnp.exp(alog_ref[...])))
    xdt_ref[...] = _dot(dt, ex_ref[...], precision=HIGHEST) * xc[:, :d_ssd]
    dec_ref[...] = _dot(dec, ex_ref[...], precision=HIGHEST)


def _ssd_prep(xbc, p0, p1, p2, conv_w, conv_b, dt_raw, dt_bias_p, a_log_p, expand):
    b, conv_dim = xbc.shape
    d_ssd = expand.shape[1]
    args = (xbc, p0, p1, p2, conv_w, conv_b, dt_raw, dt_bias_p, a_log_p, expand)
    return pl.pallas_call(
        _ssd_prep_kernel,
        grid=(1,),
        in_specs=[pl.BlockSpec(a.shape, lambda i: (0, 0)) for a in args],
        out_specs=[pl.BlockSpec((b, conv_dim), lambda i: (0, 0)),
                   pl.BlockSpec((b, d_ssd), lambda i: (0, 0)),
                   pl.BlockSpec((b, d_ssd), lambda i: (0, 0))],
        out_shape=[jax.ShapeDtypeStruct((b, conv_dim), F32), jax.ShapeDtypeStruct((b, d_ssd), F32),
                   jax.ShapeDtypeStruct((b, d_ssd), F32)],
        compiler_params=_cparams(("arbitrary",)),
    )(*args)


def _ssd_step_kernel(n_pairs, xdt_ref, dec_ref, bm_ref, cm_ref, s_ref, so_ref, y_ref):
    r2 = lax.broadcasted_iota(I32, (LANES, LANES), 0)
    c2 = lax.broadcasted_iota(I32, (LANES, LANES), 1)
    eye = r2 == c2
    ones = jnp.ones((LANES, LANES), F32)
    pairs_per_group = n_pairs // SSD_GROUPS
    rows_per_pair = LANES // SSD_HEAD_DIM
    for p in range(n_pairs):
        g = p // pairs_per_group
        sl = slice(p * LANES, (p + 1) * LANES)
        hs = slice(p * rows_per_pair, (p + 1) * rows_per_pair)
        hb = s_ref[0, 0, hs].reshape(LANES, D_STATE)
        xd = jnp.where(eye, jnp.broadcast_to(xdt_ref[0, :, sl], (LANES, LANES)), 0.0)
        dd = jnp.where(eye, jnp.broadcast_to(dec_ref[0, :, sl], (LANES, LANES)), 0.0)
        bmat = jnp.broadcast_to(bm_ref[0, :, g * D_STATE:(g + 1) * D_STATE], (LANES, D_STATE))
        upd = _dot(xd, bmat, precision=HIGHEST)
        dcol = _dot(dd, ones, precision=HIGHEST)
        hn = hb * dcol + upd
        so_ref[0, 0, hs] = hn.reshape(rows_per_pair, SSD_HEAD_DIM, D_STATE)
        cmat = jnp.broadcast_to(cm_ref[0, :, g * D_STATE:(g + 1) * D_STATE], (8, D_STATE))
        y_ref[0, :, sl] = _dot_nt(cmat, hn, precision=HIGHEST)[0:1, :]


def _ssd_step(xdt, dec, bm, cm, state):
    b, d_ssd = xdt.shape
    n_pairs = d_ssd // LANES
    heads = d_ssd // SSD_HEAD_DIM
    r3 = lambda a: a.reshape(b, 1, a.shape[1])
    row = lambda w: pl.BlockSpec((1, 1, w), lambda i: (i, 0, 0))
    sspec = pl.BlockSpec((1, 1, heads, SSD_HEAD_DIM, D_STATE), lambda i: (0, i, 0, 0, 0))
    so, y = pl.pallas_call(
        functools.partial(_ssd_step_kernel, n_pairs),
        grid=(b,),
        in_specs=[row(d_ssd), row(d_ssd), row(bm.shape[1]), row(cm.shape[1]), sspec],
        out_specs=[sspec, row(d_ssd)],
        out_shape=[jax.ShapeDtypeStruct(state.shape, F32), jax.ShapeDtypeStruct((b, 1, d_ssd), F32)],
        compiler_params=_cparams(("arbitrary",)),
    )(r3(xdt), r3(dec), r3(bm), r3(cm), state)
    return so, y.reshape(b, d_ssd)


def _ssd_finish_kernel(y_ref, xs_ref, z_ref, dsk_ref, nw_ref, o_ref):
    y = (y_ref[...] + xs_ref[...] * dsk_ref[...]) * _silu(z_ref[...])
    o_ref[...] = (_rms(y) * nw_ref[...]).astype(BF16)


def _ssd_finish(y, xs, z, dskip_row, norm_w):
    args = (y, xs, z, dskip_row, norm_w)
    return pl.pallas_call(
        _ssd_finish_kernel,
        grid=(1,),
        in_specs=[pl.BlockSpec(a.shape, lambda i: (0, 0)) for a in args],
        out_specs=pl.BlockSpec(y.shape, lambda i: (0, 0)),
        out_shape=jax.ShapeDtypeStruct(y.shape, BF16),
        compiler_params=_cparams(("arbitrary",)),
    )(*args)


PAGE_PACK = 8


def _page_copy(cache_ref, buf, sem, pt_ref, b, p, slot):
    rows = cache_ref.shape[2]
    return pltpu.make_async_copy(cache_ref.at[0, pt_ref[b, p]], buf.at[slot, pl.ds(p * rows, rows)], sem.at[slot])


def _score_sample_kernel(n_pages, pt_ref, q8_ref, w8_ref, qi_ref, wi_ref, kin_ref, cache_ref, s_ref, buf, sem):
    b = pl.program_id(0)
    nb = pl.num_programs(0)
    slot = b % 2

    def start(bb, sl):
        def body(p, carry):
            _page_copy(cache_ref, buf, sem, pt_ref, bb, p, sl).start()
            return carry
        lax.fori_loop(0, n_pages, body, 0)

    @pl.when(b == 0)
    def _():
        start(0, 0)

    @pl.when(b + 1 < nb)
    def _():
        start(b + 1, 1 - slot)

    def wait(p, carry):
        _page_copy(cache_ref, buf, sem, pt_ref, b, p, slot).wait()
        return carry

    lax.fori_loop(0, n_pages, wait, 0)

    wscale = (IDX_DIM ** -0.5) * (IDX_HEADS ** -0.5)
    q8 = q8_ref[0]
    w8 = w8_ref[0] * wscale
    kdim = q8.shape[1]
    page_rows = buf.shape[2]

    def group(gi, carry):
        keys_t = buf[slot, pl.ds(gi * kdim, kdim), :].astype(BF16)
        r = jnp.maximum(_dot(q8, keys_t), 0.0) * w8
        s_ref[0, pl.ds(gi * PAGE_PACK, PAGE_PACK), :] = jnp.sum(
            r.reshape(PAGE_PACK, IDX_HEADS, page_rows), axis=1)
        return carry

    lax.fori_loop(0, n_pages // PAGE_PACK, group, 0)
    tail = s_ref.shape[1] - n_pages
    kn = jnp.broadcast_to(kin_ref[0], (page_rows, kin_ref.shape[2])).astype(BF16)
    dn = _dot_nt(qi_ref[0], kn)
    sn = jnp.sum(jnp.maximum(dn, 0.0) * (wi_ref[0] * wscale), axis=0, keepdims=True)
    r = lax.broadcasted_iota(I32, (tail, page_rows), 0)
    c = lax.broadcasted_iota(I32, (tail, page_rows), 1)
    s_ref[0, n_pages:, :] = jnp.where((r == 0) & (c == 0), jnp.broadcast_to(sn, (tail, page_rows)), -jnp.inf)


def _score_sample(page_table, q8, w8, qi3, wi3, ki_new3, cache_kit, tail_rows):
    b, n_pages = page_table.shape
    idx_dim, page_rows = cache_kit.shape[2], cache_kit.shape[3]
    blk = lambda a: pl.BlockSpec((1,) + a.shape[1:], lambda i, pt: (i, 0, 0))
    gs = pltpu.PrefetchScalarGridSpec(
        num_scalar_prefetch=1,
        grid=(b,),
        in_specs=[blk(q8), blk(w8), blk(qi3), blk(wi3), blk(ki_new3), pl.BlockSpec(memory_space=pl.ANY)],
        out_specs=pl.BlockSpec((1, n_pages + tail_rows, page_rows), lambda i, pt: (i, 0, 0)),
        scratch_shapes=[pltpu.VMEM((2, n_pages * idx_dim, page_rows), F32),
                        pltpu.SemaphoreType.DMA((2,))],
    )
    return pl.pallas_call(
        functools.partial(_score_sample_kernel, n_pages),
        grid_spec=gs,
        out_shape=jax.ShapeDtypeStruct((b, n_pages + tail_rows, page_rows), F32),
        compiler_params=_cparams(("arbitrary",)),
    )(page_table, q8, w8, qi3, wi3, ki_new3, cache_kit)


def _select_sample_kernel(topk, s_ref, idx_ref):
    s = s_ref[0]
    n_rows, width = s.shape
    pos = lax.broadcasted_iota(I32, s.shape, 0) * width + lax.broadcasted_iota(I32, s.shape, 1)

    def total(v):
        return jnp.sum(jnp.sum(v, axis=-1, keepdims=True), axis=0, keepdims=True)

    def count(mask):
        return total(jnp.where(mask, 1, 0))

    valid = s > -jnp.inf
    lo0 = jnp.min(jnp.min(jnp.where(valid, s, 3e38), axis=-1, keepdims=True), axis=0, keepdims=True)
    hi0 = jnp.max(jnp.max(s, axis=-1, keepdims=True), axis=0, keepdims=True)
    done0 = (count(valid) <= topk).astype(I32)

    def cond(st):
        it, lo, hi, thr, done, stalled = st
        return (it < BISECT_CAP) & (jnp.min(done) == 0)

    def halve(st):
        it, lo, hi, thr, done, stalled = st
        mid = 0.5 * lo + 0.5 * hi
        n = count(s >= mid)
        live = done == 0
        exact = live & (n == topk)
        stall = live & jnp.logical_not(exact) & ((mid <= lo) | (mid >= hi))
        move = live & jnp.logical_not(exact) & jnp.logical_not(stall)
        up = n >= topk
        return (it + 1,
                jnp.where(move & up, mid, lo),
                jnp.where(move & jnp.logical_not(up), mid, hi),
                jnp.where(exact, mid, thr),
                jnp.where(exact | stall, 1, done),
                jnp.where(stall, 1, stalled))

    st = lax.while_loop(cond, halve, (jnp.int32(0), lo0, hi0, lo0, done0, jnp.zeros((1, 1), I32)))
    _, lo, hi, thr, _, stalled = st
    thr = jnp.where(stalled == 1, jnp.where(count(s >= hi) >= topk, hi, lo), thr)
    tied = s == thr
    n_bits = max(int(n_rows * width).bit_length(), 1)

    def tie_cut():
        need = topk - count(s > thr)

        def idx_step(t, jlo):
            trial = jlo + jnp.left_shift(jnp.int32(1), n_bits - 1 - t)
            return jnp.where(count(tied & (pos < trial)) <= need - 1, trial, jlo)

        return lax.fori_loop(0, n_bits, idx_step, jnp.zeros((1, 1), I32)) + 1

    cut = lax.cond(count(s >= thr)[0, 0] > topk, tie_cut, lambda: jnp.full((1, 1), n_rows * width, I32))
    sel = (s > thr) | (tied & (pos < cut))
    self = jnp.where(sel, 1.0, 0.0).astype(BF16)
    ra = lax.broadcasted_iota(I32, (width, width), 0)
    ca = lax.broadcasted_iota(I32, (width, width), 1)
    local = jnp.where(sel, _dot(self, jnp.where(ra < ca, 1.0, 0.0).astype(BF16)), -1.0)
    cnt_row = _dot_nt(jnp.ones((8, width), BF16), self)
    rb = lax.broadcasted_iota(I32, (n_rows, n_rows), 0)
    cb = lax.broadcasted_iota(I32, (n_rows, n_rows), 1)
    end_row = _dot(cnt_row.astype(BF16), jnp.where(rb <= cb, 1.0, 0.0).astype(BF16))
    rank = lax.broadcasted_iota(I32, (topk, n_rows), 0).astype(F32)
    row_id = lax.broadcasted_iota(I32, (topk, n_rows), 1).astype(F32)
    passed = jnp.broadcast_to(end_row[0:1, :], (topk, n_rows)) <= rank
    row_of = jnp.sum(jnp.where(passed, 1.0, 0.0), axis=-1, keepdims=True)
    start = jnp.sum(jnp.where(passed, jnp.broadcast_to(cnt_row[0:1, :], (topk, n_rows)), 0.0), axis=-1,
                    keepdims=True)
    picked = _dot(jnp.where(row_id == row_of, 1.0, 0.0).astype(BF16), local.astype(BF16))
    lane = lax.broadcasted_iota(I32, (topk, width), 1).astype(F32)
    lane_of = jnp.sum(jnp.where(picked == rank[:, 0:1] - start, lane, 0.0), axis=-1, keepdims=True)
    idx_ref[0] = (row_of * width + lane_of).astype(I32)


def _select_sample(s3, topk):
    b, n_rows, width = s3.shape
    return pl.pallas_call(
        functools.partial(_select_sample_kernel, topk),
        grid=(b,),
        in_specs=[pl.BlockSpec((1, n_rows, width), lambda i: (i, 0, 0))],
        out_specs=pl.BlockSpec((1, topk, 1), lambda i: (i, 0, 0)),
        out_shape=jax.ShapeDtypeStruct((b, topk, 1), I32),
        compiler_params=_cparams(("arbitrary",)),
    )(s3)


def _row_copy(src, dst, sem, src_row, dst_row):
    return pltpu.make_async_copy(src.at[pl.ds(src_row, KV_HEADS)], dst.at[pl.ds(dst_row, KV_HEADS)], sem)


def _attn_sample_kernel(topk, past_len, page_rows, n_pages, idx_ref, pt_ref, q_ref, nw_ref, ck_ref, cv_ref,
                        kn_ref, vn_ref, o_ref, kbuf, vbuf, sem):
    b = pl.program_id(0)

    def start(r, carry):
        j = idx_ref[b, r]
        pg = pt_ref[b, jnp.minimum(j // page_rows, n_pages - 1)]
        row = (pg * page_rows + j % page_rows) * KV_HEADS

        @pl.when(j < past_len)
        def _():
            _row_copy(ck_ref, kbuf, sem.at[0], row, r * KV_HEADS).start()
            _row_copy(cv_ref, vbuf, sem.at[1], row, r * KV_HEADS).start()

        @pl.when(j >= past_len)
        def _():
            _row_copy(kn_ref, kbuf, sem.at[0], b * KV_HEADS, r * KV_HEADS).start()
            _row_copy(vn_ref, vbuf, sem.at[1], b * KV_HEADS, r * KV_HEADS).start()

        return carry

    lax.fori_loop(0, topk, start, 0, unroll=8)

    def wait(r, carry):
        _row_copy(ck_ref, kbuf, sem.at[0], 0, r * KV_HEADS).wait()
        _row_copy(cv_ref, vbuf, sem.at[1], 0, r * KV_HEADS).wait()
        return carry

    lax.fori_loop(0, topk, wait, 0, unroll=8)

    outs = []
    ss = jnp.zeros((1, 1), F32)
    for g in range(KV_HEADS):
        kg = kbuf[pl.ds(g, topk, stride=KV_HEADS), :].astype(BF16)
        vg = vbuf[pl.ds(g, topk, stride=KV_HEADS), :].astype(BF16)
        lg = _dot_nt(q_ref[0, g], kg)
        m = jnp.max(lg, axis=-1, keepdims=True)
        p = jnp.exp2(lg - m)
        p = p / jnp.sum(p, axis=-1, keepdims=True)
        o = _dot(p.astype(BF16), vg)
        rows = lax.broadcasted_iota(I32, o.shape, 0)
        o = jnp.where(rows < q_ref.shape[2] // 2, o, 0.0)
        outs.append(o)
        ss = ss + jnp.sum(jnp.sum(o * o, axis=-1, keepdims=True), axis=0, keepdims=True)
    n_feat = KV_HEADS * (q_ref.shape[2] // 2) * ATT_HEAD_DIM
    inv = lax.rsqrt(ss * (1.0 / n_feat) + EPS)
    for g in range(KV_HEADS):
        o_ref[0, g] = (outs[g] * inv * nw_ref[g]).astype(BF16)


def _attn_sample(idx, page_table, q4, nw3, ck2, cv2, kn2, vn2, past_len, page_rows):
    b, topk = idx.shape
    n_pages = page_table.shape[1]
    gs = pltpu.PrefetchScalarGridSpec(
        num_scalar_prefetch=2,
        grid=(b,),
        in_specs=[pl.BlockSpec((1,) + q4.shape[1:], lambda i, a, c: (i, 0, 0, 0)),
                  pl.BlockSpec(nw3.shape, lambda i, a, c: (0, 0, 0)),
                  pl.BlockSpec(memory_space=pl.ANY), pl.BlockSpec(memory_space=pl.ANY),
                  pl.BlockSpec(memory_space=pl.ANY), pl.BlockSpec(memory_space=pl.ANY)],
        out_specs=pl.BlockSpec((1,) + q4.shape[1:], lambda i, a, c: (i, 0, 0, 0)),
        scratch_shapes=[pltpu.VMEM((topk * KV_HEADS, ATT_HEAD_DIM), F32),
                        pltpu.VMEM((topk * KV_HEADS, ATT_HEAD_DIM), F32),
                        pltpu.SemaphoreType.DMA((2,))],
    )
    return pl.pallas_call(
        functools.partial(_attn_sample_kernel, topk, past_len, page_rows, n_pages),
        grid_spec=gs,
        out_shape=jax.ShapeDtypeStruct(q4.shape, BF16),
        compiler_params=_cparams(("arbitrary",)),
    )(idx, page_table, q4, nw3, ck2, cv2, kn2, vn2)


def _row(v, width=None):
    v = v.reshape(1, -1)
    return v if width is None else _pad_cols(v, width)


def _layer_params(p):
    d = p["w_in"].shape[0]
    wr = jnp.concatenate([p["w_router_e"], p["w_router_g"]], axis=1)
    br = jnp.concatenate([p["b_router_e"], p["b_router_g"]])
    return dict(
        w_perm=_perm_w_in(p["w_in"]),
        w_out_b=p["w_out"].astype(BF16),
        wr=_pad_cols(wr, LANES).astype(BF16),
        br=_row(br, LANES),
        nw1=_row(p["norm1_w"]), nw2=_row(p["norm2_w"]),
        lnw=_row(p["ln_kidx_w"], LANES), lnb=_row(p["ln_kidx_b"], LANES),
        dt_bias=_row(p["dt_bias"], LANES), a_log=_row(p["a_log"], LANES),
        dskip=_row(jnp.repeat(p["d_skip"], SSD_HEAD_DIM)),
        norm_ssd=_row(p["norm_ssd_w"]), norm_att=_row(p["norm_att_w"]),
        conv_w=p["conv_w"], conv_b=_row(p["conv_b"]),
        d_ssd=d // 2,
    )


def _route(x, ya, yb, mod, lp, tm):
    return _out_proj(ya, yb, lp["w_out_b"], x, mod[2], lp["nw2"], mod[4], mod[3], lp["wr"], lp["br"], tm)


def _moe_and_norm(routed_p, routed_s, g2_p, g2_s, p, nf):
    x1p, h2p, eidp, wtsp, cntp = routed_p
    x1s, h2s, eids, wtss, cnts = routed_s
    tp, ts = x1p.shape[0], x1s.shape[0]
    tt = tp + ts
    tpos = 256
    tt_pad = -(-tt // tpos) * tpos
    eid_all = jnp.concatenate([eidp, eids, jnp.full((tt_pad - tt, LANES), -1, I32)])
    n_tiles = -(-(2 * tt + N_EXPERTS * (MOE_TILE - 1)) // MOE_TILE)
    off_row, tile_expert, n_used = _moe_plan(cntp + cnts, n_tiles)
    pos_flat = _moe_positions(eid_all, off_row, tpos)[:, :2].reshape(-1)
    ys = _moe_grouped(tile_expert, n_used, pos_flat, h2p, h2s, p["w_exp_up"], p["w_exp_down"])
    out_p = _combine(pos_flat, x1p, wtsp, g2_p, nf, ys, tok0=0, tm=256)
    out_s = _combine(pos_flat, x1s, wtss, g2_s, nf, ys, tok0=tp, tm=ts)
    return out_p, out_s


def _prompt_layer(x, mod, lp, p):
    t, d = x.shape
    pr = _in_proj(x, lp["nw1"], mod[1], mod[0], lp["w_perm"], lp["lnw"], lp["lnb"], tm=256)
    y_ssd, st = _ssd_prompt(pr["xbc"], pr["dt"], pr["z"], lp["conv_w"], lp["conv_b"], lp["dt_bias"], lp["a_log"],
                            lp["dskip"], lp["norm_ssd"])
    ki = pr["ki"]
    zeros = jnp.zeros_like(ki)
    ki2 = jnp.stack([jnp.concatenate([ki, zeros], axis=1), jnp.concatenate([zeros, ki], axis=1)]).astype(BF16)
    topk = min(TOPK_MAX, t // 4)
    v3 = pr["vb"].reshape(t, KV_HEADS, ATT_HEAD_DIM)
    vx = jnp.concatenate([v3, jnp.ones_like(v3)], axis=-1).reshape(t, 2 * KV_HEADS * ATT_HEAD_DIM)
    y_att = _attn_prompt(pr["q"], pr["qi"], pr["wi"], ki2, pr["kb"], vx, lp["norm_att"], topk, tq=256)
    routed = _route(x, y_ssd, y_att, mod, lp, tm=256)
    conv_new = jnp.concatenate([jnp.zeros((CONV_W - 1, pr["xbc"].shape[1]), F32), pr["xbc"]])[-(CONV_W - 1):]
    return routed, (pr["k"], pr["v"], ki, conv_new, st)


def _sample_layer(x, mod, lp, p, cache_k, cache_v, cache_ki, conv_prev, ssm_prev, page_table):
    b, d = x.shape
    d_ssd = lp["d_ssd"]
    heads = d_ssd // SSD_HEAD_DIM
    gn = SSD_GROUPS * D_STATE
    pr = _in_proj(x, lp["nw1"], mod[1], mod[0], lp["w_perm"], lp["lnw"], lp["lnb"], tm=b)
    expand = (jnp.arange(LANES)[:, None] == (jnp.arange(d_ssd)[None, :] // SSD_HEAD_DIM)).astype(F32)
    xc, xdt, dec = _ssd_prep(pr["xbc"], conv_prev[:, 0], conv_prev[:, 1], conv_prev[:, 2], lp["conv_w"], lp["conv_b"],
                             pr["dt"], lp["dt_bias"], lp["a_log"], expand)
    xs, bm, cm = xc[:, :d_ssd], xc[:, d_ssd:d_ssd + gn], xc[:, d_ssd + gn:]
    st5 = ssm_prev.reshape((1, b, heads, SSD_HEAD_DIM, D_STATE))
    st_new, y = _ssd_step(xdt, dec, bm, cm, st5)
    y_ssd = _ssd_finish(y, xs, pr["z"], lp["dskip"], lp["norm_ssd"])
    conv_new = jnp.concatenate([conv_prev[:, 1:], pr["xbc"][:, None, :]], axis=1)
    n_pool, page_rows = cache_k.shape[0], cache_k.shape[1]
    n_pages = page_table.shape[1]
    past_len = n_pages * page_rows
    topk = min(TOPK_MAX, (past_len + 1) // 4)
    qi3 = pr["qi"].reshape(b, IDX_HEADS, IDX_DIM)
    wi3 = pr["wi"][:, :IDX_HEADS].reshape(b, IDX_HEADS, 1)
    eye = jnp.eye(PAGE_PACK, dtype=BF16)
    q8 = (eye[None, :, None, :, None] * qi3[:, None, :, None, :]).reshape(b, PAGE_PACK * IDX_HEADS,
                                                                         PAGE_PACK * IDX_DIM)
    w8 = jnp.tile(wi3, (1, PAGE_PACK, 1))
    tail_rows = -(n_pages + 1) % LANES + 1
    cache_kit = jnp.swapaxes(cache_ki, -1, -2)[None]
    s3 = _score_sample(page_table, q8, w8, qi3, wi3, pr["ki"].reshape(b, 1, IDX_DIM), cache_kit, tail_rows)
    idx = _select_sample(s3, topk).reshape(b, topk)
    n_heads = pr["q"].shape[1] // ATT_HEAD_DIM
    q_per_kv = n_heads // KV_HEADS
    q4 = jnp.pad(pr["q"].reshape(b, KV_HEADS, q_per_kv, ATT_HEAD_DIM), ((0, 0), (0, 0), (0, q_per_kv), (0, 0)))
    nw3 = jnp.pad(lp["norm_att"].reshape(KV_HEADS, q_per_kv, ATT_HEAD_DIM), ((0, 0), (0, q_per_kv), (0, 0)))
    ck2 = cache_k.reshape(n_pool * page_rows * KV_HEADS, ATT_HEAD_DIM)
    cv2 = cache_v.reshape(n_pool * page_rows * KV_HEADS, ATT_HEAD_DIM)
    kn2 = pr["k"].reshape(b * KV_HEADS, ATT_HEAD_DIM)
    vn2 = pr["v"].reshape(b * KV_HEADS, ATT_HEAD_DIM)
    o4 = _attn_sample(idx, page_table, q4, nw3, ck2, cv2, kn2, vn2, past_len, page_rows)
    y_att = o4[:, :, :q_per_kv].reshape(b, n_heads * ATT_HEAD_DIM)
    routed = _route(x, y_ssd, y_att, mod, lp, tm=b)
    return routed, (pr["k"], pr["v"], pr["ki"], conv_new, st_new.reshape(ssm_prev.shape))


def kernel(x_prompt, x_sample, cache_k, cache_v, cache_k_idx, state_conv, state_ssm, page_table, c_prompt, c_sample, w_ada, b_ada, norm1_w, norm2_w, w_in, conv_w, conv_b, dt_bias, a_log, d_skip, norm_ssd_w, ln_kidx_w, ln_kidx_b, norm_att_w, w_out, w_router_g, b_router_g, w_router_e, b_router_e, w_exp_up, w_exp_down, norm_f_w):
    batch, seq, d = x_prompt.shape
    dec_batch, dec_seq, _ = x_sample.shape
    assert batch == 1 and dec_seq == 1, "one prompt sequence and one new token per sample sequence"
    depth = w_ada.shape[0]
    heads = (d // 2) // SSD_HEAD_DIM
    xp = x_prompt.reshape(seq, d)
    xs = x_sample.reshape(dec_batch, d)
    n_c = batch + dec_batch
    c_all = jnp.pad(jnp.concatenate([c_prompt, c_sample]), ((0, -n_c % 8), (0, 0)))
    nf = _row(norm_f_w)
    outs_p, outs_s = [], []
    yp = ys = None
    for l in range(depth):
        p = dict(w_in=w_in[l], conv_w=conv_w[l], conv_b=conv_b[l], dt_bias=dt_bias[l], a_log=a_log[l],
                 d_skip=d_skip[l], norm_ssd_w=norm_ssd_w[l], ln_kidx_w=ln_kidx_w[l], ln_kidx_b=ln_kidx_b[l],
                 norm_att_w=norm_att_w[l], w_out=w_out[l], w_router_g=w_router_g[l], b_router_g=b_router_g[l],
                 w_router_e=w_router_e[l], b_router_e=b_router_e[l], w_exp_up=w_exp_up[l],
                 w_exp_down=w_exp_down[l], norm1_w=norm1_w[l], norm2_w=norm2_w[l])
        lp = _layer_params(p)
        mod = _ada_mod(c_all, w_ada[l], b_ada[l])
        mod_p = [mod[0:1, k * d:(k + 1) * d] for k in range(6)]
        mod_s = [mod[batch:n_c, k * d:(k + 1) * d] for k in range(6)]
        routed_p, st_p = _prompt_layer(xp, mod_p, lp, p)
        routed_s, st_s = _sample_layer(xs, mod_s, lp, p, cache_k[l], cache_v[l], cache_k_idx[l], state_conv[l],
                                       state_ssm[l], page_table)
        (xp, yp), (xs, ys) = _moe_and_norm(routed_p, routed_s, mod_p[5], mod_s[5], p, nf)
        outs_p.append(st_p)
        outs_s.append(st_s)

    def stack(outs, n_rows, lead):
        k = jnp.stack([o[0].reshape(lead + (n_rows, KV_HEADS, ATT_HEAD_DIM)) for o in outs])
        v = jnp.stack([o[1].reshape(lead + (n_rows, KV_HEADS, ATT_HEAD_DIM)) for o in outs])
        ki = jnp.stack([o[2].reshape(lead + (n_rows, IDX_DIM)) for o in outs])
        return k, v, ki

    k_p, v_p, ki_p = stack(outs_p, seq, (batch,))
    conv_p = jnp.stack([o[3][None] for o in outs_p])
    ssm_p = jnp.stack([o[4].reshape(batch, heads, SSD_HEAD_DIM, D_STATE) for o in outs_p])
    k_s = jnp.stack([o[0].reshape(dec_batch, dec_seq, KV_HEADS, ATT_HEAD_DIM) for o in outs_s])
    v_s = jnp.stack([o[1].reshape(dec_batch, dec_seq, KV_HEADS, ATT_HEAD_DIM) for o in outs_s])
    ki_s = jnp.stack([o[2].reshape(dec_batch, dec_seq, IDX_DIM) for o in outs_s])
    conv_s = jnp.stack([o[3] for o in outs_s])
    ssm_s = jnp.stack([o[4] for o in outs_s])
    return (yp.reshape(batch, seq, d), ys.reshape(dec_batch, dec_seq, d), k_p, v_p, ki_p, conv_p, ssm_p,
            k_s, v_s, ki_s, conv_s, ssm_s)
```

```python
import functools

import numpy as np
import jax
import jax.numpy as jnp
from jax import lax
from jax.experimental import pallas as pl
from jax.experimental.pallas import tpu as pltpu

F32 = jnp.float32
BF16 = jnp.bfloat16
I32 = jnp.int32

SSD_HEAD_DIM = 64
SSD_GROUPS = 2
D_STATE = 128
CONV_W = 4
SSD_CHUNK = 128
ATT_HEAD_DIM = 128
KV_HEADS = 2
IDX_HEADS = 16
IDX_DIM = 64
TOPK_MAX = 256
N_EGROUPS = 4
EXPERTS_PER_GROUP = 8
N_EXPERTS = N_EGROUPS * EXPERTS_PER_GROUP
D_EXPERT = 512
EPS = 1e-6

LANES = 128
INT_MIN = -2 ** 31
NEG_BIG = -1e30
VMEM_LIMIT = 56 * 1024 * 1024
HIGHEST = lax.Precision.HIGHEST
Q_SCALE = ATT_HEAD_DIM ** -0.5 * 1.4426950408889634


def _cparams(sem):
    return pltpu.CompilerParams(dimension_semantics=sem, vmem_limit_bytes=VMEM_LIMIT)


def _dot(a, b, precision=None):
    return jnp.dot(a, b, preferred_element_type=F32, precision=precision)


def _dot_nt(a, b, precision=None):
    return lax.dot_general(a, b, (((1,), (1,)), ((), ())), preferred_element_type=F32, precision=precision)


def _silu(x):
    return x * jax.nn.sigmoid(x)


def _softplus(x):
    return jnp.maximum(x, 0.0) + jnp.log(1.0 + jnp.exp(-jnp.abs(x)))


def _rms(x):
    return x * lax.rsqrt(jnp.mean(x * x, axis=-1, keepdims=True) + EPS)


def _pad_cols(a, width):
    return jnp.pad(a, ((0, 0), (0, width - a.shape[1])))


def _ada_kernel(c_ref, w_ref, b_ref, o_ref):
    s = _silu(c_ref[...]).astype(BF16)
    o_ref[...] = _dot(s, w_ref[...].astype(BF16)) + b_ref[...]


def _ada_mod(c_all, w_ada, b_ada):
    r, d = c_all.shape
    n = w_ada.shape[1]
    tn = 1024
    return pl.pallas_call(
        _ada_kernel,
        grid=(n // tn,),
        in_specs=[pl.BlockSpec((r, d), lambda j: (0, 0)),
                  pl.BlockSpec((d, tn), lambda j: (0, j)),
                  pl.BlockSpec((1, tn), lambda j: (0, j))],
        out_specs=pl.BlockSpec((r, tn), lambda j: (0, j)),
        out_shape=jax.ShapeDtypeStruct((r, n), F32),
        compiler_params=_cparams(("arbitrary",)),
    )(c_all, w_ada, b_ada.reshape(1, n))


def _in_layout(d_model):
    d_ssd = d_model // 2
    d_att = d_model - d_ssd
    conv_dim = d_ssd + 2 * SSD_GROUPS * D_STATE
    ssd_heads = d_ssd // SSD_HEAD_DIM
    sizes = dict(z=d_ssd, xbc=conv_dim, dt=ssd_heads, q=d_att, k=KV_HEADS * ATT_HEAD_DIM,
                 v=KV_HEADS * ATT_HEAD_DIM, qi=IDX_HEADS * IDX_DIM, ki=IDX_DIM, wi=IDX_HEADS)
    order = ("z", "xbc", "dt", "q", "k", "v", "qi", "ki", "wi")
    src, dst, off_s, off_d = {}, {}, 0, 0
    for name in order:
        w = sizes[name]
        wp = -(-w // LANES) * LANES
        src[name] = (off_s, w)
        dst[name] = (off_d, wp)
        off_s += w
        off_d += wp
    return order, src, dst, off_d


def _perm_w_in(w_in):
    order, src, dst, _ = _in_layout(w_in.shape[0])
    parts = [_pad_cols(w_in[:, src[n][0]:src[n][0] + src[n][1]], dst[n][1]) for n in order]
    return jnp.concatenate(parts, axis=1).astype(BF16)


def _inproj_kernel(seg, x_ref, nw_ref, sc_ref, sh_ref, w_ref, lnw_ref, lnb_ref,
                   z_ref, xbc_ref, dt_ref, q_ref, k_ref, v_ref, kb_ref, vb_ref, qi_ref, ki_ref, wi_ref):
    h = _rms(x_ref[...]) * nw_ref[...]
    h = h * (1.0 + sc_ref[...]) + sh_ref[...]
    hb = h.astype(BF16)

    def mm(name):
        a, w = seg[name]
        return _dot(hb, w_ref[:, a:a + w])

    z_ref[...] = mm("z")
    xbc_ref[...] = mm("xbc")
    dt_ref[...] = mm("dt")
    q_ref[...] = (mm("q") * Q_SCALE).astype(BF16)
    k = mm("k")
    k_ref[...] = k
    kb_ref[...] = k.astype(BF16)
    v = mm("v")
    v_ref[...] = v
    vb_ref[...] = v.astype(BF16)
    qi_ref[...] = mm("qi").astype(BF16)
    wi_ref[...] = mm("wi")
    ki = mm("ki")
    lane = lax.broadcasted_iota(I32, ki.shape, 1)
    ok = lane < IDX_DIM
    mu = jnp.sum(jnp.where(ok, ki, 0.0), axis=-1, keepdims=True) * (1.0 / IDX_DIM)
    cen = jnp.where(ok, ki - mu, 0.0)
    var = jnp.sum(cen * cen, axis=-1, keepdims=True) * (1.0 / IDX_DIM)
    y = cen * lax.rsqrt(var + EPS) * lnw_ref[...] + lnb_ref[...]
    ki_ref[...] = y[:, :IDX_DIM]


def _in_proj(x, nw, sc, sh, w_perm, lnw, lnb, tm):
    t, d = x.shape
    _, _, dst, npad = _in_layout(d)
    tmod = sc.shape[0]
    mod_map = (lambda i: (0, 0)) if tmod == 1 else (lambda i: (i, 0))
    mod_rows = 1 if tmod == 1 else tm
    row = lambda w: pl.BlockSpec((tm, w), lambda i: (i, 0))
    d_ssd, d_att = dst["z"][1], dst["q"][1]
    kvw = KV_HEADS * ATT_HEAD_DIM
    outs = [("z", d_ssd, F32), ("xbc", dst["xbc"][1], F32), ("dt", LANES, F32), ("q", d_att, BF16),
            ("k", kvw, F32), ("v", kvw, F32), ("kb", kvw, BF16), ("vb", kvw, BF16),
            ("qi", IDX_HEADS * IDX_DIM, BF16), ("ki", IDX_DIM, F32), ("wi", LANES, F32)]
    res = pl.pallas_call(
        functools.partial(_inproj_kernel, dst),
        grid=(t // tm,),
        in_specs=[row(d),
                  pl.BlockSpec((1, d), lambda i: (0, 0)),
                  pl.BlockSpec((mod_rows, d), mod_map),
                  pl.BlockSpec((mod_rows, d), mod_map),
                  pl.BlockSpec((d, npad), lambda i: (0, 0)),
                  pl.BlockSpec((1, LANES), lambda i: (0, 0)),
                  pl.BlockSpec((1, LANES), lambda i: (0, 0))],
        out_specs=[row(w) for _, w, _ in outs],
        out_shape=[jax.ShapeDtypeStruct((t, w), dt) for _, w, dt in outs],
        compiler_params=_cparams(("arbitrary",)),
    )(x, nw, sc, sh, w_perm, lnw, lnb)
    return dict(zip([n for n, _, _ in outs], res))


def _ssd_prompt_kernel(n_pairs, xbc_ref, dt_ref, z_ref, cw_ref, cb_ref, dtb_ref, alog_ref, dsk_ref, nw_ref,
                       y_ref, st_ref, xprev, ht, ybuf):
    c = pl.program_id(0)
    q = SSD_CHUNK
    d_ssd = n_pairs * LANES
    gn = SSD_GROUPS * D_STATE

    @pl.when(c == 0)
    def _():
        xprev[...] = jnp.zeros_like(xprev)
        ht[...] = jnp.zeros_like(ht)

    x = xbc_ref[...]
    xp = xprev[...]
    rowi = lax.broadcasted_iota(I32, (q, 1), 0)
    acc = cb_ref[...] + cw_ref[CONV_W - 1:CONV_W, :] * x
    for k in range(1, CONV_W):
        sh = jnp.where(rowi < k, pltpu.roll(xp, k, 0), pltpu.roll(x, k, 0))
        acc = acc + cw_ref[CONV_W - 1 - k:CONV_W - k, :] * sh
    xprev[...] = x
    xc = _silu(acc)

    dt = _softplus(dt_ref[...] + dtb_ref[...])
    a_neg = -jnp.exp(alog_ref[...])
    r2 = lax.broadcasted_iota(I32, (q, q), 0)
    c2 = lax.broadcasted_iota(I32, (q, q), 1)
    tril = c2 <= r2
    a = _dot(tril.astype(F32), dt * a_neg, precision=HIGHEST)
    a_t = a.T
    dt_t = dt.T
    a_last = a[q - 1:q, :]
    wmat = jnp.exp(a_last - a) * dt
    emat = jnp.exp(a)
    cd = jnp.exp(a_last)
    lane = lax.broadcasted_iota(I32, (q, LANES), 1)
    left = lane < SSD_HEAD_DIM
    pairs_per_group = n_pairs // SSD_GROUPS

    bts, cbs, cgs = [], [], []
    for g in range(SSD_GROUPS):
        bg = xc[:, d_ssd + g * D_STATE:d_ssd + (g + 1) * D_STATE]
        cg = xc[:, d_ssd + gn + g * D_STATE:d_ssd + gn + (g + 1) * D_STATE].astype(BF16)
        bt = bg.T.astype(BF16)
        bts.append(bt)
        cgs.append(cg)
        cbs.append(_dot(cg, bt))

    def colb(m, h):
        return jnp.broadcast_to(m[:, h:h + 1], (q, LANES))

    for p in range(n_pairs):
        g = p // pairs_per_group
        h0, h1 = 2 * p, 2 * p + 1
        xpair = xc[:, p * LANES:(p + 1) * LANES]
        xpb = xpair.astype(BF16)
        yd = []
        for h in (h0, h1):
            diff = colb(a, h) - a_t[h:h + 1, :]
            decay = jnp.exp(jnp.where(tril, diff, -jnp.inf))
            sc = cbs[g] * decay * dt_t[h:h + 1, :]
            yd.append(_dot(sc.astype(BF16), xpb))
        y_diag = jnp.where(left, yd[0], yd[1])
        w_pair = jnp.where(left, colb(wmat, h0), colb(wmat, h1))
        e_pair = jnp.where(left, colb(emat, h0), colb(emat, h1))
        cd_pair = jnp.where(left[0:1, :], jnp.broadcast_to(cd[:, h0:h0 + 1], (1, LANES)),
                            jnp.broadcast_to(cd[:, h1:h1 + 1], (1, LANES)))
        hprev = ht[p]
        y_off = _dot(cgs[g], hprev.astype(BF16)) * e_pair
        states = _dot(bts[g], (xpair * w_pair).astype(BF16))
        ht[p] = hprev * cd_pair + states
        ybuf[:, p * LANES:(p + 1) * LANES] = y_diag + y_off + xpair * dsk_ref[:, p * LANES:(p + 1) * LANES]

    y = ybuf[...] * _silu(z_ref[...])
    y_ref[...] = (_rms(y) * nw_ref[...]).astype(BF16)

    @pl.when(c == pl.num_programs(0) - 1)
    def _():
        for p in range(n_pairs):
            st_ref[p * LANES:(p + 1) * LANES, :] = ht[p].T


def _ssd_prompt(xbc, dt_raw, z, conv_w, conv_b, dt_bias_p, a_log_p, dskip_row, norm_w):
    t, conv_dim = xbc.shape
    d_ssd = z.shape[1]
    n_pairs = d_ssd // LANES
    q = SSD_CHUNK
    full = lambda a: pl.BlockSpec(a.shape, lambda c: (0, 0))
    return pl.pallas_call(
        functools.partial(_ssd_prompt_kernel, n_pairs),
        grid=(t // q,),
        in_specs=[pl.BlockSpec((q, conv_dim), lambda c: (c, 0)),
                  pl.BlockSpec((q, LANES), lambda c: (c, 0)),
                  pl.BlockSpec((q, d_ssd), lambda c: (c, 0)),
                  full(conv_w), full(conv_b), full(dt_bias_p), full(a_log_p), full(dskip_row), full(norm_w)],
        out_specs=[pl.BlockSpec((q, d_ssd), lambda c: (c, 0)),
                   pl.BlockSpec((d_ssd, D_STATE), lambda c: (0, 0))],
        out_shape=[jax.ShapeDtypeStruct((t, d_ssd), BF16),
                   jax.ShapeDtypeStruct((d_ssd, D_STATE), F32)],
        scratch_shapes=[pltpu.VMEM((q, conv_dim), F32),
                        pltpu.VMEM((n_pairs, D_STATE, LANES), F32),
                        pltpu.VMEM((q, d_ssd), F32)],
        compiler_params=_cparams(("arbitrary",)),
    )(xbc, dt_raw, z, conv_w, conv_b, dt_bias_p, a_log_p, dskip_row, norm_w)


ROW_SUB = 128
FOLD_CHUNKS = 4
BISECT_CAP = 320


def _attn_prompt_kernel(topk, tq, q_ref, qi_ref, wi_ref, ki2_ref, kb_ref, vx_ref, nw_ref, o_ref,
                        sc, wb, thr_b, mrun, acc_scr):
    i = pl.program_id(0)
    kc = tq
    n_chunks = i + 1
    n_heads = q_ref.shape[1] // ATT_HEAD_DIM
    q_per_kv = n_heads // KV_HEADS
    wscale = (IDX_DIM ** -0.5) * (IDX_HEADS ** -0.5)
    n_sub = tq // ROW_SUB

    wi = wi_ref[...] * wscale
    for h in range(IDX_HEADS):
        wb[h] = jnp.broadcast_to(wi[:, h:h + 1], (tq, kc))

    row_g = i * tq + lax.broadcasted_iota(I32, (tq, kc), 0)
    col_l = lax.broadcasted_iota(I32, (tq, kc), 1)

    def score_chunk(j, carry):
        k0 = ki2_ref[0, pl.ds(j * kc, kc), :]
        k1 = ki2_ref[1, pl.ds(j * kc, kc), :]
        s = jnp.zeros((tq, kc), F32)
        for p in range(IDX_HEADS // 2):
            qp = qi_ref[:, p * LANES:(p + 1) * LANES]
            s = s + jnp.maximum(_dot_nt(qp, k0), 0.0) * wb[2 * p]
            s = s + jnp.maximum(_dot_nt(qp, k1), 0.0) * wb[2 * p + 1]
        sc[:, pl.ds(j * kc, kc)] = jnp.where(j * kc + col_l <= row_g, s, -jnp.inf)
        return carry

    lax.fori_loop(0, n_chunks, score_chunk, 0)
    for extra in range(FOLD_CHUNKS - 1):
        sc[:, pl.ds((n_chunks + extra) * kc, kc)] = jnp.full((tq, kc), -jnp.inf, F32)
    n_steps = (n_chunks + FOLD_CHUNKS - 1) // FOLD_CHUNKS

    def fold(fn, init):
        outs = []
        for r in range(n_sub):
            rows = slice(r * ROW_SUB, (r + 1) * ROW_SUB)

            def body(j, acc, rows=rows, r=r):
                for part in range(FOLD_CHUNKS * kc // LANES):
                    c0 = j * (FOLD_CHUNKS * kc) + part * LANES
                    acc = fn(acc, sc[rows, pl.ds(c0, LANES)], c0, r)
                return acc

            outs.append(lax.fori_loop(0, n_steps, body, jnp.full((ROW_SUB, LANES), init, F32)))
        return outs

    def spread(row):
        return [jnp.broadcast_to(row[:, r * ROW_SUB:(r + 1) * ROW_SUB], (ROW_SUB, ROW_SUB)).T for r in range(n_sub)]

    def collect(parts, op):
        return jnp.concatenate([op(p.T, axis=0, keepdims=True) for p in parts], axis=1)

    def count(pred):
        return collect(fold(lambda acc, blk, c0, r: acc + jnp.where(pred(blk, c0, r), 1.0, 0.0), 0.0), jnp.sum)

    def count_ge(t):
        tb = spread(t)
        return count(lambda blk, c0, r: blk >= tb[r])

    big = jnp.float32(3e38)
    lo0 = collect(fold(lambda acc, blk, c0, r: jnp.minimum(acc, jnp.where(blk == -jnp.inf, big, blk)), 3e38), jnp.min)
    hi0 = collect(fold(lambda acc, blk, c0, r: jnp.maximum(acc, blk), -3e38), jnp.max)

    kf = jnp.float32(topk)
    n_valid = (i * tq + lax.broadcasted_iota(I32, (1, tq), 1) + 1).astype(F32)
    done0 = jnp.where(n_valid <= kf, 1.0, 0.0)

    def cond(st):
        it, lo, hi, thr, done, stalled = st
        return (it < BISECT_CAP) & (jnp.min(done) == 0.0)

    def halve(st):
        it, lo, hi, thr, done, stalled = st
        mid = 0.5 * lo + 0.5 * hi
        n = count_ge(mid)
        live = done == 0.0
        exact = live & (n == kf)
        stall = live & jnp.logical_not(exact) & ((mid <= lo) | (mid >= hi))
        move = live & jnp.logical_not(exact) & jnp.logical_not(stall)
        up = n >= kf
        return (it + 1,
                jnp.where(move & up, mid, lo),
                jnp.where(move & jnp.logical_not(up), mid, hi),
                jnp.where(exact, mid, thr),
                jnp.where(exact | stall, 1.0, done),
                jnp.where(stall, 1.0, stalled))

    st = lax.while_loop(cond, halve, (jnp.int32(0), lo0, hi0, lo0, done0, jnp.zeros((1, tq), F32)))
    _, lo, hi, thr, _, stalled = st
    n_hi = count_ge(hi)
    thr = jnp.where(stalled == 1.0, jnp.where(n_hi >= kf, hi, lo), thr)
    n_ge = count_ge(thr)
    tb = spread(thr)

    @pl.when(jnp.max(n_ge) > kf)
    def _():
        n_gt = count(lambda blk, c0, r: blk > tb[r])
        need = kf - n_gt
        lane_i = lax.broadcasted_iota(I32, (ROW_SUB, LANES), 1)
        n_bits = max(int(sc.shape[1]).bit_length(), 1)

        def idx_step(t, jlo):
            trial = jlo + jnp.left_shift(jnp.int32(1), n_bits - 1 - t).astype(F32)
            trb = spread(trial)
            f = count(lambda blk, c0, r: (blk == tb[r]) & ((c0 + lane_i).astype(F32) < trb[r]))
            return jnp.where(f <= need - 1.0, trial, jlo)

        jlo = lax.fori_loop(0, n_bits, idx_step, jnp.zeros((1, tq), F32))
        cut = spread(jnp.where(n_ge > kf, jlo + 1.0, jnp.float32(2 ** 30)))

        def drop(b, carry):
            for r in range(n_sub):
                rows = slice(r * ROW_SUB, (r + 1) * ROW_SUB)
                blk = sc[rows, pl.ds(b * LANES, LANES)]
                gone = (blk == tb[r]) & ((b * LANES + lane_i).astype(F32) >= cut[r])
                sc[rows, pl.ds(b * LANES, LANES)] = jnp.where(gone, -jnp.inf, blk)
            return carry

        lax.fori_loop(0, n_chunks * (kc // LANES), drop, 0)

    for r in range(n_sub):
        thr_b[r * ROW_SUB:(r + 1) * ROW_SUB, :] = jnp.concatenate([tb[r]] * (kc // LANES), axis=1)

    mrun[...] = jnp.full(mrun.shape, NEG_BIG, F32)
    acc_scr[...] = jnp.zeros(acc_scr.shape, F32)
    vw = 2 * ATT_HEAD_DIM

    def halves(x):
        return [x[:, k * LANES:(k + 1) * LANES] for k in range(kc // LANES)]

    def logits(j, h):
        g = h // q_per_kv
        kj = kb_ref[pl.ds(j * kc, kc), g * ATT_HEAD_DIM:(g + 1) * ATT_HEAD_DIM]
        qh = q_ref[:, h * ATT_HEAD_DIM:(h + 1) * ATT_HEAD_DIM]
        return _dot_nt(qh, kj)

    def max_chunk(j, carry):
        sel = sc[:, pl.ds(j * kc, kc)] >= thr_b[...]
        for h in range(n_heads):
            lg = jnp.where(sel, logits(j, h), NEG_BIG)
            mrun[h] = functools.reduce(jnp.maximum, halves(lg), mrun[h])
        return carry

    lax.fori_loop(0, n_chunks, max_chunk, 0)
    for h in range(n_heads):
        mrun[h] = jnp.broadcast_to(jnp.max(mrun[h], axis=-1, keepdims=True), (tq, LANES))

    def sum_chunk(j, carry):
        sel = sc[:, pl.ds(j * kc, kc)] >= thr_b[...]
        for h in range(n_heads):
            g = h // q_per_kv
            vj = vx_ref[pl.ds(j * kc, kc), g * vw:(g + 1) * vw]
            m = mrun[h]
            p = jnp.where(sel, jnp.exp2(logits(j, h) - jnp.concatenate([m] * (kc // LANES), axis=1)), 0.0)
            acc_scr[h] = acc_scr[h] + _dot(p.astype(BF16), vj)
        return carry

    lax.fori_loop(0, n_chunks, sum_chunk, 0)

    ss = jnp.zeros((tq, 1), F32)
    for h in range(n_heads):
        o = acc_scr[h, :, :ATT_HEAD_DIM] / acc_scr[h, :, ATT_HEAD_DIM:]
        acc_scr[h, :, :ATT_HEAD_DIM] = o
        ss = ss + jnp.sum(o * o, axis=-1, keepdims=True)
    inv = lax.rsqrt(ss * (1.0 / (n_heads * ATT_HEAD_DIM)) + EPS)
    for h in range(n_heads):
        sl = slice(h * ATT_HEAD_DIM, (h + 1) * ATT_HEAD_DIM)
        o_ref[:, sl] = (acc_scr[h, :, :ATT_HEAD_DIM] * inv * nw_ref[:, sl]).astype(BF16)


def _attn_prompt(q, qi, wi, ki2, kb, vx, norm_w, topk, tq):
    t, d_att = q.shape
    n_heads = d_att // ATT_HEAD_DIM
    full = lambda a: pl.BlockSpec(a.shape, lambda i: (0,) * a.ndim)
    return pl.pallas_call(
        functools.partial(_attn_prompt_kernel, topk, tq),
        grid=(t // tq,),
        in_specs=[pl.BlockSpec((tq, d_att), lambda i: (i, 0)),
                  pl.BlockSpec((tq, qi.shape[1]), lambda i: (i, 0)),
                  pl.BlockSpec((tq, LANES), lambda i: (i, 0)),
                  full(ki2), full(kb), full(vx), full(norm_w)],
        out_specs=pl.BlockSpec((tq, d_att), lambda i: (i, 0)),
        out_shape=jax.ShapeDtypeStruct((t, d_att), BF16),
        scratch_shapes=[pltpu.VMEM((tq, t + FOLD_CHUNKS * tq), F32),
                        pltpu.VMEM((IDX_HEADS, tq, tq), F32),
                        pltpu.VMEM((tq, tq), F32),
                        pltpu.VMEM((n_heads, tq, LANES), F32),
                        pltpu.VMEM((n_heads, tq, 2 * ATT_HEAD_DIM), F32)],
        compiler_params=_cparams(("arbitrary",)),
    )(q, qi, wi, ki2, kb, vx, norm_w)


def _outproj_kernel(ya_ref, yb_ref, w_ref, x_ref, g1_ref, nw_ref, sc_ref, sh_ref, wr_ref, br_ref,
                    x1_ref, h2_ref, eid_ref, wts_ref, cnt_ref):
    d_a = ya_ref.shape[1]
    m = _dot(ya_ref[...], w_ref[:d_a, :]) + _dot(yb_ref[...], w_ref[d_a:, :])
    x1 = x_ref[...] + g1_ref[...] * m
    x1_ref[...] = x1
    h2 = _rms(x1) * nw_ref[...]
    h2 = h2 * (1.0 + sc_ref[...]) + sh_ref[...]
    h2_ref[...] = h2
    lg = _dot(h2.astype(BF16), wr_ref[...]) + br_ref[...]
    lane = lax.broadcasted_iota(I32, lg.shape, 1)
    big = jnp.int32(4 * LANES)

    def rmax(v):
        return jnp.max(v, axis=-1, keepdims=True)

    def rmin(v):
        return jnp.min(v, axis=-1, keepdims=True)

    def rsum(v):
        return jnp.sum(v, axis=-1, keepdims=True)

    is_g = (lane >= N_EXPERTS) & (lane < N_EXPERTS + N_EGROUPS)
    mg = rmax(jnp.where(is_g, lg, -jnp.inf))
    sg = rsum(jnp.where(is_g, jnp.exp(lg - mg), 0.0))
    gsel = rmin(jnp.where(is_g & (lg == mg), lane - N_EXPERTS, big))
    pgsel = 1.0 / sg
    in_grp = (lane < N_EXPERTS) & (jnp.right_shift(lane, EXPERTS_PER_GROUP.bit_length() - 1) == gsel)
    me = rmax(jnp.where(in_grp, lg, -jnp.inf))
    ee = jnp.where(in_grp, jnp.exp(lg - me), 0.0)
    pe = ee / rsum(ee)
    p1 = rmax(jnp.where(in_grp, pe, -1.0))
    i1 = rmin(jnp.where(in_grp & (pe == p1), lane, big))
    rem = in_grp & (lane != i1)
    p2 = rmax(jnp.where(rem, pe, -1.0))
    i2 = rmin(jnp.where(rem & (pe == p2), lane, big))
    den = p1 + p2
    eid_ref[...] = jnp.where(lane == 0, i1, jnp.where(lane == 1, i2, 0))
    wts_ref[...] = jnp.where(lane == 0, pgsel * p1 / den, jnp.where(lane == 1, pgsel * p2 / den, 0.0))

    @pl.when(pl.program_id(0) == 0)
    def _():
        cnt_ref[...] = jnp.zeros_like(cnt_ref)

    chosen = jnp.where((lane == i1) | (lane == i2), 1.0, 0.0)
    cnt_ref[...] += jnp.sum(chosen, axis=0, keepdims=True)


def _out_proj(ya, yb, w_out_b, x, g1, nw2, sc2, sh2, wr, br, tm):
    t, d = x.shape
    d_a = ya.shape[1]
    tmod = g1.shape[0]
    mod_map = (lambda i: (0, 0)) if tmod == 1 else (lambda i: (i, 0))
    mod_rows = 1 if tmod == 1 else tm
    modspec = pl.BlockSpec((mod_rows, d), mod_map)
    row = lambda w: pl.BlockSpec((tm, w), lambda i: (i, 0))
    full = lambda a: pl.BlockSpec(a.shape, lambda i: (0, 0))
    return pl.pallas_call(
        _outproj_kernel,
        grid=(t // tm,),
        in_specs=[row(d_a), row(yb.shape[1]), full(w_out_b), row(d), modspec, full(nw2), modspec, modspec,
                  full(wr), full(br)],
        out_specs=[row(d), row(d), row(LANES), row(LANES), pl.BlockSpec((1, LANES), lambda i: (0, 0))],
        out_shape=[jax.ShapeDtypeStruct((t, d), F32), jax.ShapeDtypeStruct((t, d), F32),
                   jax.ShapeDtypeStruct((t, LANES), I32), jax.ShapeDtypeStruct((t, LANES), F32),
                   jax.ShapeDtypeStruct((1, LANES), F32)],
        compiler_params=_cparams(("arbitrary",)),
    )(ya, yb, w_out_b, x, g1, nw2, sc2, sh2, wr, br)


MOE_TILE = 256


def _moe_pos_kernel(eid_ref, off_ref, pos_ref, carry):
    @pl.when(pl.program_id(0) == 0)
    def _():
        carry[...] = jnp.zeros_like(carry)

    eid = eid_ref[...]
    tm = eid.shape[0]
    i1, i2 = eid[:, 0:1], eid[:, 1:2]
    lane = lax.broadcasted_iota(I32, eid.shape, 1)
    chosen = jnp.where((lane == i1) | (lane == i2), 1.0, 0.0)
    r = lax.broadcasted_iota(I32, (tm, tm), 0)
    c = lax.broadcasted_iota(I32, (tm, tm), 1)
    earlier = _dot(jnp.where(c < r, 1.0, 0.0).astype(BF16), chosen.astype(BF16))
    row = earlier + carry[...] + off_ref[...]
    p1 = jnp.sum(jnp.where(lane == i1, row, 0.0), axis=-1, keepdims=True)
    p2 = jnp.sum(jnp.where(lane == i2, row, 0.0), axis=-1, keepdims=True)
    out = jnp.where(lane == 0, p1, jnp.where(lane == 1, p2, 0.0))
    pos_ref[...] = jnp.where(i1 >= 0, out, -1.0).astype(I32)
    carry[...] += jnp.sum(chosen, axis=0, keepdims=True)


def _moe_positions(eid_all, off_row, tm):
    t = eid_all.shape[0]
    return pl.pallas_call(
        _moe_pos_kernel,
        grid=(t // tm,),
        in_specs=[pl.BlockSpec((tm, LANES), lambda i: (i, 0)), pl.BlockSpec((1, LANES), lambda i: (0, 0))],
        out_specs=pl.BlockSpec((tm, LANES), lambda i: (i, 0)),
        out_shape=jax.ShapeDtypeStruct((t, LANES), I32),
        scratch_shapes=[pltpu.VMEM((1, LANES), F32)],
        compiler_params=_cparams(("arbitrary",)),
    )(eid_all, off_row)


def _dyn_loop(lo, hi, fn, unroll=4):
    shift = unroll.bit_length() - 1
    n_blk = jnp.right_shift(hi - lo, shift)

    def blk(k, carry):
        for u in range(unroll):
            fn(lo + k * unroll + u)
        return carry

    def one(i, carry):
        fn(i)
        return carry

    lax.fori_loop(0, n_blk, blk, 0)
    lax.fori_loop(lo + n_blk * unroll, hi, one, 0)


def _moe_grouped_kernel(t_prompt, te_ref, nu_ref, np_ref, nv_ref, pos_ref, hp_ref, hs_ref, wu_ref, wd_ref, o_ref,
                        src, xbuf, wub, wdb, sem):
    g = pl.program_id(0)
    n_used = nu_ref[0]
    tmg = xbuf.shape[1]
    n_tok = pos_ref.shape[0] // 2
    n_rows = o_ref.shape[0] * pl.num_programs(0)
    slot = g % 2

    def row_copy(h_ref, tok, sl, r):
        return pltpu.make_async_copy(h_ref.at[pl.ds(tok, 1)], xbuf.at[sl, pl.ds(r, 1)], sem.at[sl])

    def gather_start(tile, sl):
        base = tile * tmg
        _dyn_loop(0, np_ref[tile], lambda r: row_copy(hp_ref, src[base + r], sl, r).start())
        _dyn_loop(np_ref[tile], nv_ref[tile], lambda r: row_copy(hs_ref, src[base + r] - t_prompt, sl, r).start())

    def gather_wait(tile, sl):
        _dyn_loop(0, nv_ref[tile], lambda r: row_copy(hp_ref, 0, sl, r).wait())

    @pl.when(g == 0)
    def _():
        def fill(t, carry):
            for k in range(2):
                p = pos_ref[2 * t + k]
                src[jnp.where(p < 0, n_rows, p)] = t
            return carry

        lax.fori_loop(0, n_tok, fill, 0, unroll=4)
        xbuf[...] = jnp.zeros_like(xbuf)
        gather_start(0, 0)

    @pl.when(g < n_used)
    def _():
        @pl.when(g + 1 < n_used)
        def _():
            gather_start(g + 1, 1 - slot)

        gather_wait(g, slot)
        fresh = (g == 0) | (te_ref[g] != te_ref[jnp.maximum(g - 1, 0)])

        @pl.when(fresh)
        def _():
            wub[...] = wu_ref[0].astype(BF16)
            wdb[...] = wd_ref[0].astype(BF16)

        gu = _dot(xbuf[slot].astype(BF16), wub[...])
        de = gu.shape[1] // 2
        act = _silu(gu[:, :de]) * gu[:, de:]
        o_ref[...] = _dot(act.astype(BF16), wdb[...])

    @pl.when(g >= n_used)
    def _():
        o_ref[...] = jnp.zeros_like(o_ref)


def _moe_grouped(plan, pos_flat, h2_p, h2_s, w_up, w_down):
    tile_expert, n_used, tile_np, tile_nv = plan
    n_tiles = tile_expert.shape[0]
    t_prompt, d = h2_p.shape
    _, _, two_de = w_up.shape
    tmg = MOE_TILE
    wmap = lambda g, te, nu, tp, tv, ps: (te[g], 0, 0)
    gs = pltpu.PrefetchScalarGridSpec(
        num_scalar_prefetch=5,
        grid=(n_tiles,),
        in_specs=[pl.BlockSpec(memory_space=pl.ANY), pl.BlockSpec(memory_space=pl.ANY),
                  pl.BlockSpec((1, d, two_de), wmap),
                  pl.BlockSpec((1, two_de // 2, d), wmap)],
        out_specs=pl.BlockSpec((tmg, d), lambda g, te, nu, tp, tv, ps: (g, 0)),
        scratch_shapes=[pltpu.SMEM((n_tiles * tmg + 8,), I32),
                        pltpu.VMEM((2, tmg, d), F32),
                        pltpu.VMEM((d, two_de), BF16),
                        pltpu.VMEM((two_de // 2, d), BF16),
                        pltpu.SemaphoreType.DMA((2,))],
    )
    return pl.pallas_call(
        functools.partial(_moe_grouped_kernel, t_prompt),
        grid_spec=gs,
        out_shape=jax.ShapeDtypeStruct((n_tiles * tmg, d), F32),
        compiler_params=_cparams(("arbitrary",)),
    )(tile_expert, n_used, tile_np, tile_nv, pos_flat, h2_p, h2_s, w_up, w_down)


def _combine_kernel(tok0, pos_ref, x1_ref, wts_ref, g2_ref, nf_ref, ys_ref, xo_ref, o_ref, ybuf, sem):
    i = pl.program_id(0)
    n = pl.num_programs(0)
    tm = x1_ref.shape[0]
    slot = i % 2

    def gather(tile, sl, wait):
        def body(r, carry):
            tok = tok0 + tile * tm + r
            for k in range(2):
                src_row = 0 if wait else pos_ref[2 * tok + k]
                cp = pltpu.make_async_copy(ys_ref.at[pl.ds(src_row, 1)], ybuf.at[sl, k, pl.ds(r, 1)], sem.at[sl])
                cp.wait() if wait else cp.start()
            return carry

        lax.fori_loop(0, tm, body, 0, unroll=8)

    @pl.when(i == 0)
    def _():
        gather(0, 0, False)

    @pl.when(i + 1 < n)
    def _():
        gather(i + 1, 1 - slot, False)

    gather(i, slot, True)
    w = wts_ref[...]
    y = w[:, 0:1] * ybuf[slot, 0] + w[:, 1:2] * ybuf[slot, 1]
    x2 = x1_ref[...] + g2_ref[...] * y
    xo_ref[...] = x2
    o_ref[...] = _rms(x2) * nf_ref[...]


def _combine(pos_flat, x1, wts, g2, nf, ys, tok0, tm):
    t, d = x1.shape
    tmod = g2.shape[0]
    mod_map = (lambda i, ps: (0, 0)) if tmod == 1 else (lambda i, ps: (i, 0))
    row = lambda w: pl.BlockSpec((tm, w), lambda i, ps: (i, 0))
    gs = pltpu.PrefetchScalarGridSpec(
        num_scalar_prefetch=1,
        grid=(t // tm,),
        in_specs=[row(d), row(LANES), pl.BlockSpec((1 if tmod == 1 else tm, d), mod_map),
                  pl.BlockSpec((1, d), lambda i, ps: (0, 0)), pl.BlockSpec(memory_space=pl.ANY)],
        out_specs=[row(d), row(d)],
        scratch_shapes=[pltpu.VMEM((2, 2, tm, d), F32), pltpu.SemaphoreType.DMA((2,))],
    )
    return pl.pallas_call(
        functools.partial(_combine_kernel, tok0),
        grid_spec=gs,
        out_shape=[jax.ShapeDtypeStruct((t, d), F32), jax.ShapeDtypeStruct((t, d), F32)],
        compiler_params=_cparams(("arbitrary",)),
    )(pos_flat, x1, wts, g2, nf, ys)


def _moe_plan(cnt_p, cnt_s, n_tiles):
    cp = cnt_p[0, :N_EXPERTS].astype(I32)
    cnt = cp + cnt_s[0, :N_EXPERTS].astype(I32)
    padded = (cnt + MOE_TILE - 1) // MOE_TILE * MOE_TILE
    ends = jnp.cumsum(padded)
    off = ends - padded
    off_row = _pad_cols(off.astype(F32).reshape(1, N_EXPERTS), LANES)
    starts = jnp.arange(n_tiles, dtype=I32) * MOE_TILE
    te = jnp.minimum(jnp.sum(starts[:, None] >= ends[None, :], axis=1), N_EXPERTS - 1).astype(I32)
    tile_np = jnp.clip(off[te] + cp[te] - starts, 0, MOE_TILE).astype(I32)
    tile_nv = jnp.clip(off[te] + cnt[te] - starts, 0, MOE_TILE).astype(I32)
    return off_row, (te, (ends[-1:] // MOE_TILE).astype(I32), tile_np, tile_nv)


def _ssd_prep_kernel(xbc_ref, p0_ref, p1_ref, p2_ref, cw_ref, cb_ref, dt_ref, dtb_ref, alog_ref, ex_ref,
                     xc_ref, xdt_ref, dec_ref):
    d_ssd = xdt_ref.shape[1]
    acc = (cb_ref[...] + cw_ref[0:1, :] * p0_ref[...] + cw_ref[1:2, :] * p1_ref[...]
           + cw_ref[2:3, :] * p2_ref[...] + cw_ref[3:4, :] * xbc_ref[...])
    xc = _silu(acc)
    xc_ref[...] = xc
    dt = _softplus(dt_ref[...] + dtb_ref[...])
    dec = jnp.exp(dt * (-jnp.exp(alog_ref[...])))
    xdt_ref[...] = _dot(dt, ex_ref[...], precision=HIGHEST) * xc[:, :d_ssd]
    dec_ref[...] = _dot(dec, ex_ref[...], precision=HIGHEST)


def _ssd_prep(xbc, p0, p1, p2, conv_w, conv_b, dt_raw, dt_bias_p, a_log_p, expand):
    b, conv_dim = xbc.shape
    d_ssd = expand.shape[1]
    args = (xbc, p0, p1, p2, conv_w, conv_b, dt_raw, dt_bias_p, a_log_p, expand)
    return pl.pallas_call(
        _ssd_prep_kernel,
        grid=(1,),
        in_specs=[pl.BlockSpec(a.shape, lambda i: (0, 0)) for a in args],
        out_specs=[pl.BlockSpec((b, conv_dim), lambda i: (0, 0)),
                   pl.BlockSpec((b, d_ssd), lambda i: (0, 0)),
                   pl.BlockSpec((b, d_ssd), lambda i: (0, 0))],
        out_shape=[jax.ShapeDtypeStruct((b, conv_dim), F32), jax.ShapeDtypeStruct((b, d_ssd), F32),
                   jax.ShapeDtypeStruct((b, d_ssd), F32)],
        compiler_params=_cparams(("arbitrary",)),
    )(*args)


def _ssd_step_kernel(n_pairs, xdt_ref, dec_ref, bm_ref, cm_ref, s_ref, so_ref, y_ref):
    r2 = lax.broadcasted_iota(I32, (LANES, LANES), 0)
    c2 = lax.broadcasted_iota(I32, (LANES, LANES), 1)
    eye = r2 == c2
    ones = jnp.ones((LANES, LANES), F32)
    pairs_per_group = n_pairs // SSD_GROUPS
    rows_per_pair = LANES // SSD_HEAD_DIM
    for p in range(n_pairs):
        g = p // pairs_per_group
        sl = slice(p * LANES, (p + 1) * LANES)
        hs = slice(p * rows_per_pair, (p + 1) * rows_per_pair)
        hb = s_ref[0, 0, hs].reshape(LANES, D_STATE)
        xd = jnp.where(eye, jnp.broadcast_to(xdt_ref[0, :, sl], (LANES, LANES)), 0.0)
        dd = jnp.where(eye, jnp.broadcast_to(dec_ref[0, :, sl], (LANES, LANES)), 0.0)
        bmat = jnp.broadcast_to(bm_ref[0, :, g * D_STATE:(g + 1) * D_STATE], (LANES, D_STATE))
        upd = _dot(xd, bmat, precision=HIGHEST)
        dcol = _dot(dd, ones, precision=HIGHEST)
        hn = hb * dcol + upd
        so_ref[0, 0, hs] = hn.reshape(rows_per_pair, SSD_HEAD_DIM, D_STATE)
        cmat = jnp.broadcast_to(cm_ref[0, :, g * D_STATE:(g + 1) * D_STATE], (8, D_STATE))
        y_ref[0, :, sl] = _dot_nt(cmat, hn, precision=HIGHEST)[0:1, :]


def _ssd_step(xdt, dec, bm, cm, state):
    b, d_ssd = xdt.shape
    n_pairs = d_ssd // LANES
    heads = d_ssd // SSD_HEAD_DIM
    r3 = lambda a: a.reshape(b, 1, a.shape[1])
    row = lambda w: pl.BlockSpec((1, 1, w), lambda i: (i, 0, 0))
    sspec = pl.BlockSpec((1, 1, heads, SSD_HEAD_DIM, D_STATE), lambda i: (0, i, 0, 0, 0))
    so, y = pl.pallas_call(
        functools.partial(_ssd_step_kernel, n_pairs),
        grid=(b,),
        in_specs=[row(d_ssd), row(d_ssd), row(bm.shape[1]), row(cm.shape[1]), sspec],
        out_specs=[sspec, row(d_ssd)],
        out_shape=[jax.ShapeDtypeStruct(state.shape, F32), jax.ShapeDtypeStruct((b, 1, d_ssd), F32)],
        compiler_params=_cparams(("arbitrary",)),
    )(r3(xdt), r3(dec), r3(bm), r3(cm), state)
    return so, y.reshape(b, d_ssd)


def _ssd_finish_kernel(y_ref, xs_ref, z_ref, dsk_ref, nw_ref, o_ref):
    y = (y_ref[...] + xs_ref[...] * dsk_ref[...]) * _silu(z_ref[...])
    o_ref[...] = (_rms(y) * nw_ref[...]).astype(BF16)


def _ssd_finish(y, xs, z, dskip_row, norm_w):
    args = (y, xs, z, dskip_row, norm_w)
    return pl.pallas_call(
        _ssd_finish_kernel,
        grid=(1,),
        in_specs=[pl.BlockSpec(a.shape, lambda i: (0, 0)) for a in args],
        out_specs=pl.BlockSpec(y.shape, lambda i: (0, 0)),
        out_shape=jax.ShapeDtypeStruct(y.shape, BF16),
        compiler_params=_cparams(("arbitrary",)),
    )(*args)


PAGE_PACK = 8


def _page_copy(cache_ref, buf, sem, pt_ref, b, p, slot):
    rows = cache_ref.shape[2]
    return pltpu.make_async_copy(cache_ref.at[0, pt_ref[b, p]], buf.at[slot, pl.ds(p * rows, rows)], sem.at[slot])


def _score_sample_kernel(n_pages, pt_ref, q8_ref, w8_ref, qi_ref, wi_ref, kin_ref, cache_ref, s_ref, buf, sem):
    b = pl.program_id(0)
    nb = pl.num_programs(0)
    slot = b % 2

    def start(bb, sl):
        def body(p, carry):
            _page_copy(cache_ref, buf, sem, pt_ref, bb, p, sl).start()
            return carry
        lax.fori_loop(0, n_pages, body, 0)

    @pl.when(b == 0)
    def _():
        start(0, 0)

    @pl.when(b + 1 < nb)
    def _():
        start(b + 1, 1 - slot)

    def wait(p, carry):
        _page_copy(cache_ref, buf, sem, pt_ref, b, p, slot).wait()
        return carry

    lax.fori_loop(0, n_pages, wait, 0)

    wscale = (IDX_DIM ** -0.5) * (IDX_HEADS ** -0.5)
    q8 = q8_ref[0]
    w8 = w8_ref[0] * wscale
    kdim = q8.shape[1]
    page_rows = buf.shape[2]

    def group(gi, carry):
        keys_t = buf[slot, pl.ds(gi * kdim, kdim), :].astype(BF16)
        r = jnp.maximum(_dot(q8, keys_t), 0.0) * w8
        s_ref[0, pl.ds(gi * PAGE_PACK, PAGE_PACK), :] = jnp.sum(
            r.reshape(PAGE_PACK, IDX_HEADS, page_rows), axis=1)
        return carry

    lax.fori_loop(0, n_pages // PAGE_PACK, group, 0)
    tail = s_ref.shape[1] - n_pages
    kn = jnp.broadcast_to(kin_ref[0], (page_rows, kin_ref.shape[2])).astype(BF16)
    dn = _dot_nt(qi_ref[0], kn)
    sn = jnp.sum(jnp.maximum(dn, 0.0) * (wi_ref[0] * wscale), axis=0, keepdims=True)
    r = lax.broadcasted_iota(I32, (tail, page_rows), 0)
    c = lax.broadcasted_iota(I32, (tail, page_rows), 1)
    s_ref[0, n_pages:, :] = jnp.where((r == 0) & (c == 0), jnp.broadcast_to(sn, (tail, page_rows)), -jnp.inf)


def _score_sample(page_table, q8, w8, qi3, wi3, ki_new3, cache_kit, tail_rows):
    b, n_pages = page_table.shape
    idx_dim, page_rows = cache_kit.shape[2], cache_kit.shape[3]
    blk = lambda a: pl.BlockSpec((1,) + a.shape[1:], lambda i, pt: (i, 0, 0))
    gs = pltpu.PrefetchScalarGridSpec(
        num_scalar_prefetch=1,
        grid=(b,),
        in_specs=[blk(q8), blk(w8), blk(qi3), blk(wi3), blk(ki_new3), pl.BlockSpec(memory_space=pl.ANY)],
        out_specs=pl.BlockSpec((1, n_pages + tail_rows, page_rows), lambda i, pt: (i, 0, 0)),
        scratch_shapes=[pltpu.VMEM((2, n_pages * idx_dim, page_rows), F32),
                        pltpu.SemaphoreType.DMA((2,))],
    )
    return pl.pallas_call(
        functools.partial(_score_sample_kernel, n_pages),
        grid_spec=gs,
        out_shape=jax.ShapeDtypeStruct((b, n_pages + tail_rows, page_rows), F32),
        compiler_params=_cparams(("arbitrary",)),
    )(page_table, q8, w8, qi3, wi3, ki_new3, cache_kit)


def _select_sample_kernel(topk, s_ref, idx_ref):
    s = s_ref[0]
    n_rows, width = s.shape
    pos = lax.broadcasted_iota(I32, s.shape, 0) * width + lax.broadcasted_iota(I32, s.shape, 1)

    def total(v):
        return jnp.sum(jnp.sum(v, axis=-1, keepdims=True), axis=0, keepdims=True)

    def count(mask):
        return total(jnp.where(mask, 1, 0))

    valid = s > -jnp.inf
    lo0 = jnp.min(jnp.min(jnp.where(valid, s, 3e38), axis=-1, keepdims=True), axis=0, keepdims=True)
    hi0 = jnp.max(jnp.max(s, axis=-1, keepdims=True), axis=0, keepdims=True)
    done0 = (count(valid) <= topk).astype(I32)

    def cond(st):
        it, lo, hi, thr, done, stalled = st
        return (it < BISECT_CAP) & (jnp.min(done) == 0)

    def halve(st):
        it, lo, hi, thr, done, stalled = st
        mid = 0.5 * lo + 0.5 * hi
        n = count(s >= mid)
        live = done == 0
        exact = live & (n == topk)
        stall = live & jnp.logical_not(exact) & ((mid <= lo) | (mid >= hi))
        move = live & jnp.logical_not(exact) & jnp.logical_not(stall)
        up = n >= topk
        return (it + 1,
                jnp.where(move & up, mid, lo),
                jnp.where(move & jnp.logical_not(up), mid, hi),
                jnp.where(exact, mid, thr),
                jnp.where(exact | stall, 1, done),
                jnp.where(stall, 1, stalled))

    st = lax.while_loop(cond, halve, (jnp.int32(0), lo0, hi0, lo0, done0, jnp.zeros((1, 1), I32)))
    _, lo, hi, thr, _, stalled = st
    thr = jnp.where(stalled == 1, jnp.where(count(s >= hi) >= topk, hi, lo), thr)
    tied = s == thr
    n_bits = max(int(n_rows * width).bit_length(), 1)

    def tie_cut():
        need = topk - count(s > thr)

        def idx_step(t, jlo):
            trial = jlo + jnp.left_shift(jnp.int32(1), n_bits - 1 - t)
            return jnp.where(count(tied & (pos < trial)) <= need - 1, trial, jlo)

        return lax.fori_loop(0, n_bits, idx_step, jnp.zeros((1, 1), I32)) + 1

    cut = lax.cond(count(s >= thr)[0, 0] > topk, tie_cut, lambda: jnp.full((1, 1), n_rows * width, I32))
    sel = (s > thr) | (tied & (pos < cut))
    self = jnp.where(sel, 1.0, 0.0).astype(BF16)
    ra = lax.broadcasted_iota(I32, (width, width), 0)
    ca = lax.broadcasted_iota(I32, (width, width), 1)
    local = jnp.where(sel, _dot(self, jnp.where(ra < ca, 1.0, 0.0).astype(BF16)), -1.0)
    cnt_row = _dot_nt(jnp.ones((8, width), BF16), self)
    rb = lax.broadcasted_iota(I32, (n_rows, n_rows), 0)
    cb = lax.broadcasted_iota(I32, (n_rows, n_rows), 1)
    end_row = _dot(cnt_row.astype(BF16), jnp.where(rb <= cb, 1.0, 0.0).astype(BF16))
    rank = lax.broadcasted_iota(I32, (topk, n_rows), 0).astype(F32)
    row_id = lax.broadcasted_iota(I32, (topk, n_rows), 1).astype(F32)
    passed = jnp.broadcast_to(end_row[0:1, :], (topk, n_rows)) <= rank
    row_of = jnp.sum(jnp.where(passed, 1.0, 0.0), axis=-1, keepdims=True)
    start = jnp.sum(jnp.where(passed, jnp.broadcast_to(cnt_row[0:1, :], (topk, n_rows)), 0.0), axis=-1,
                    keepdims=True)
    picked = _dot(jnp.where(row_id == row_of, 1.0, 0.0).astype(BF16), local.astype(BF16))
    lane = lax.broadcasted_iota(I32, (topk, width), 1).astype(F32)
    lane_of = jnp.sum(jnp.where(picked == rank[:, 0:1] - start, lane, 0.0), axis=-1, keepdims=True)
    idx_ref[0] = (row_of * width + lane_of).astype(I32)


def _select_sample(s3, topk):
    b, n_rows, width = s3.shape
    return pl.pallas_call(
        functools.partial(_select_sample_kernel, topk),
        grid=(b,),
        in_specs=[pl.BlockSpec((1, n_rows, width), lambda i: (i, 0, 0))],
        out_specs=pl.BlockSpec((1, topk, 1), lambda i: (i, 0, 0)),
        out_shape=jax.ShapeDtypeStruct((b, topk, 1), I32),
        compiler_params=_cparams(("arbitrary",)),
    )(s3)


def _row_copy(src, dst, sem, src_row, dst_row):
    return pltpu.make_async_copy(src.at[pl.ds(src_row, KV_HEADS)], dst.at[pl.ds(dst_row, KV_HEADS)], sem)


def _attn_sample_kernel(topk, past_len, page_rows, n_pages, idx_ref, pt_ref, q_ref, nw_ref, ck_ref, cv_ref,
                        kn_ref, vn_ref, o_ref, kbuf, vbuf, sem):
    b = pl.program_id(0)

    pow2 = page_rows & (page_rows - 1) == 0

    def start(r, carry):
        j = jnp.minimum(idx_ref[b, r], past_len - 1)
        if pow2:
            page, off = jnp.right_shift(j, page_rows.bit_length() - 1), j & (page_rows - 1)
        else:
            page, off = j // page_rows, j % page_rows
        row = (pt_ref[b, page] * page_rows + off) * KV_HEADS
        _row_copy(ck_ref, kbuf, sem.at[0], row, r * KV_HEADS).start()
        _row_copy(cv_ref, vbuf, sem.at[1], row, r * KV_HEADS).start()
        return carry

    lax.fori_loop(0, topk, start, 0, unroll=8)

    def wait(r, carry):
        _row_copy(ck_ref, kbuf, sem.at[0], 0, r * KV_HEADS).wait()
        _row_copy(cv_ref, vbuf, sem.at[1], 0, r * KV_HEADS).wait()
        return carry

    lax.fori_loop(0, topk, wait, 0, unroll=8)

    @pl.when(idx_ref[b, topk - 1] >= past_len)
    def _():
        last = (topk - 1) * KV_HEADS
        for src_ref, buf, s in ((kn_ref, kbuf, sem.at[0]), (vn_ref, vbuf, sem.at[1])):
            cp = _row_copy(src_ref, buf, s, b * KV_HEADS, last)
            cp.start()
            cp.wait()

    outs = []
    ss = jnp.zeros((1, 1), F32)
    for g in range(KV_HEADS):
        kg = kbuf[pl.ds(g, topk, stride=KV_HEADS), :].astype(BF16)
        vg = vbuf[pl.ds(g, topk, stride=KV_HEADS), :].astype(BF16)
        lg = _dot_nt(q_ref[0, g], kg)
        m = jnp.max(lg, axis=-1, keepdims=True)
        p = jnp.exp2(lg - m)
        p = p / jnp.sum(p, axis=-1, keepdims=True)
        o = _dot(p.astype(BF16), vg)
        rows = lax.broadcasted_iota(I32, o.shape, 0)
        o = jnp.where(rows < q_ref.shape[2] // 2, o, 0.0)
        outs.append(o)
        ss = ss + jnp.sum(jnp.sum(o * o, axis=-1, keepdims=True), axis=0, keepdims=True)
    n_feat = KV_HEADS * (q_ref.shape[2] // 2) * ATT_HEAD_DIM
    inv = lax.rsqrt(ss * (1.0 / n_feat) + EPS)
    for g in range(KV_HEADS):
        o_ref[0, g] = (outs[g] * inv * nw_ref[g]).astype(BF16)


def _attn_sample(idx, page_table, q4, nw3, ck2, cv2, kn2, vn2, past_len, page_rows):
    b, topk = idx.shape
    n_pages = page_table.shape[1]
    gs = pltpu.PrefetchScalarGridSpec(
        num_scalar_prefetch=2,
        grid=(b,),
        in_specs=[pl.BlockSpec((1,) + q4.shape[1:], lambda i, a, c: (i, 0, 0, 0)),
                  pl.BlockSpec(nw3.shape, lambda i, a, c: (0, 0, 0)),
                  pl.BlockSpec(memory_space=pl.ANY), pl.BlockSpec(memory_space=pl.ANY),
                  pl.BlockSpec(memory_space=pl.ANY), pl.BlockSpec(memory_space=pl.ANY)],
        out_specs=pl.BlockSpec((1,) + q4.shape[1:], lambda i, a, c: (i, 0, 0, 0)),
        scratch_shapes=[pltpu.VMEM((topk * KV_HEADS, ATT_HEAD_DIM), F32),
                        pltpu.VMEM((topk * KV_HEADS, ATT_HEAD_DIM), F32),
                        pltpu.SemaphoreType.DMA((2,))],
    )
    return pl.pallas_call(
        functools.partial(_attn_sample_kernel, topk, past_len, page_rows, n_pages),
        grid_spec=gs,
        out_shape=jax.ShapeDtypeStruct(q4.shape, BF16),
        compiler_params=_cparams(("arbitrary",)),
    )(idx, page_table, q4, nw3, ck2, cv2, kn2, vn2)


def _row(v, width=None):
    v = v.reshape(1, -1)
    return v if width is None else _pad_cols(v, width)


def _layer_params(p):
    d = p["w_in"].shape[0]
    wr = jnp.concatenate([p["w_router_e"], p["w_router_g"]], axis=1)
    br = jnp.concatenate([p["b_router_e"], p["b_router_g"]])
    return dict(
        w_perm=_perm_w_in(p["w_in"]),
        w_out_b=p["w_out"].astype(BF16),
        wr=_pad_cols(wr, LANES).astype(BF16),
        br=_row(br, LANES),
        nw1=_row(p["norm1_w"]), nw2=_row(p["norm2_w"]),
        lnw=_row(p["ln_kidx_w"], LANES), lnb=_row(p["ln_kidx_b"], LANES),
        dt_bias=_row(p["dt_bias"], LANES), a_log=_row(p["a_log"], LANES),
        dskip=_row(jnp.repeat(p["d_skip"], SSD_HEAD_DIM)),
        norm_ssd=_row(p["norm_ssd_w"]), norm_att=_row(p["norm_att_w"]),
        conv_w=p["conv_w"], conv_b=_row(p["conv_b"]),
        d_ssd=d // 2,
    )


def _route(x, ya, yb, mod, lp, tm):
    return _out_proj(ya, yb, lp["w_out_b"], x, mod[2], lp["nw2"], mod[4], mod[3], lp["wr"], lp["br"], tm)


def _moe_and_norm(routed_p, routed_s, g2_p, g2_s, p, nf):
    x1p, h2p, eidp, wtsp, cntp = routed_p
    x1s, h2s, eids, wtss, cnts = routed_s
    tp, ts = x1p.shape[0], x1s.shape[0]
    tt = tp + ts
    tpos = 256
    tt_pad = -(-tt // tpos) * tpos
    eid_all = jnp.concatenate([eidp, eids, jnp.full((tt_pad - tt, LANES), -1, I32)])
    n_tiles = -(-(2 * tt + N_EXPERTS * (MOE_TILE - 1)) // MOE_TILE)
    off_row, plan = _moe_plan(cntp, cnts, n_tiles)
    pos_flat = _moe_positions(eid_all, off_row, tpos)[:, :2].reshape(-1)
    ys = _moe_grouped(plan, pos_flat, h2p, h2s, p["w_exp_up"], p["w_exp_down"])
    out_p = _combine(pos_flat, x1p, wtsp, g2_p, nf, ys, tok0=0, tm=256)
    out_s = _combine(pos_flat, x1s, wtss, g2_s, nf, ys, tok0=tp, tm=ts)
    return out_p, out_s


def _prompt_layer(x, mod, lp, p):
    t, d = x.shape
    pr = _in_proj(x, lp["nw1"], mod[1], mod[0], lp["w_perm"], lp["lnw"], lp["lnb"], tm=256)
    y_ssd, st = _ssd_prompt(pr["xbc"], pr["dt"], pr["z"], lp["conv_w"], lp["conv_b"], lp["dt_bias"], lp["a_log"],
                            lp["dskip"], lp["norm_ssd"])
    ki = pr["ki"]
    zeros = jnp.zeros_like(ki)
    ki2 = jnp.stack([jnp.concatenate([ki, zeros], axis=1), jnp.concatenate([zeros, ki], axis=1)]).astype(BF16)
    topk = min(TOPK_MAX, t // 4)
    v3 = pr["vb"].reshape(t, KV_HEADS, ATT_HEAD_DIM)
    vx = jnp.concatenate([v3, jnp.ones_like(v3)], axis=-1).reshape(t, 2 * KV_HEADS * ATT_HEAD_DIM)
    y_att = _attn_prompt(pr["q"], pr["qi"], pr["wi"], ki2, pr["kb"], vx, lp["norm_att"], topk, tq=256)
    routed = _route(x, y_ssd, y_att, mod, lp, tm=256)
    conv_new = jnp.concatenate([jnp.zeros((CONV_W - 1, pr["xbc"].shape[1]), F32), pr["xbc"]])[-(CONV_W - 1):]
    return routed, (pr["k"], pr["v"], ki, conv_new, st)


def _sample_layer(x, mod, lp, p, cache_k, cache_v, cache_ki, conv_prev, ssm_prev, page_table):
    b, d = x.shape
    d_ssd = lp["d_ssd"]
    heads = d_ssd // SSD_HEAD_DIM
    gn = SSD_GROUPS * D_STATE
    pr = _in_proj(x, lp["nw1"], mod[1], mod[0], lp["w_perm"], lp["lnw"], lp["lnb"], tm=b)
    expand = (jnp.arange(LANES)[:, None] == (jnp.arange(d_ssd)[None, :] // SSD_HEAD_DIM)).astype(F32)
    xc, xdt, dec = _ssd_prep(pr["xbc"], conv_prev[:, 0], conv_prev[:, 1], conv_prev[:, 2], lp["conv_w"], lp["conv_b"],
                             pr["dt"], lp["dt_bias"], lp["a_log"], expand)
    xs, bm, cm = xc[:, :d_ssd], xc[:, d_ssd:d_ssd + gn], xc[:, d_ssd + gn:]
    st5 = ssm_prev.reshape((1, b, heads, SSD_HEAD_DIM, D_STATE))
    st_new, y = _ssd_step(xdt, dec, bm, cm, st5)
    y_ssd = _ssd_finish(y, xs, pr["z"], lp["dskip"], lp["norm_ssd"])
    conv_new = jnp.concatenate([conv_prev[:, 1:], pr["xbc"][:, None, :]], axis=1)
    n_pool, page_rows = cache_k.shape[0], cache_k.shape[1]
    n_pages = page_table.shape[1]
    past_len = n_pages * page_rows
    topk = min(TOPK_MAX, (past_len + 1) // 4)
    qi3 = pr["qi"].reshape(b, IDX_HEADS, IDX_DIM)
    wi3 = pr["wi"][:, :IDX_HEADS].reshape(b, IDX_HEADS, 1)
    eye = jnp.eye(PAGE_PACK, dtype=BF16)
    q8 = (eye[None, :, None, :, None] * qi3[:, None, :, None, :]).reshape(b, PAGE_PACK * IDX_HEADS,
                                                                         PAGE_PACK * IDX_DIM)
    w8 = jnp.tile(wi3, (1, PAGE_PACK, 1))
    tail_rows = -(n_pages + 1) % LANES + 1
    cache_kit = jnp.swapaxes(cache_ki, -1, -2)[None]
    s3 = _score_sample(page_table, q8, w8, qi3, wi3, pr["ki"].reshape(b, 1, IDX_DIM), cache_kit, tail_rows)
    idx = _select_sample(s3, topk).reshape(b, topk)
    n_heads = pr["q"].shape[1] // ATT_HEAD_DIM
    q_per_kv = n_heads // KV_HEADS
    q4 = jnp.pad(pr["q"].reshape(b, KV_HEADS, q_per_kv, ATT_HEAD_DIM), ((0, 0), (0, 0), (0, q_per_kv), (0, 0)))
    nw3 = jnp.pad(lp["norm_att"].reshape(KV_HEADS, q_per_kv, ATT_HEAD_DIM), ((0, 0), (0, q_per_kv), (0, 0)))
    ck2 = cache_k.reshape(n_pool * page_rows * KV_HEADS, ATT_HEAD_DIM)
    cv2 = cache_v.reshape(n_pool * page_rows * KV_HEADS, ATT_HEAD_DIM)
    kn2 = pr["k"].reshape(b * KV_HEADS, ATT_HEAD_DIM)
    vn2 = pr["v"].reshape(b * KV_HEADS, ATT_HEAD_DIM)
    o4 = _attn_sample(idx, page_table, q4, nw3, ck2, cv2, kn2, vn2, past_len, page_rows)
    y_att = o4[:, :, :q_per_kv].reshape(b, n_heads * ATT_HEAD_DIM)
    routed = _route(x, y_ssd, y_att, mod, lp, tm=b)
    return routed, (pr["k"], pr["v"], pr["ki"], conv_new, st_new.reshape(ssm_prev.shape))


def kernel(x_prompt, x_sample, cache_k, cache_v, cache_k_idx, state_conv, state_ssm, page_table, c_prompt, c_sample, w_ada, b_ada, norm1_w, norm2_w, w_in, conv_w, conv_b, dt_bias, a_log, d_skip, norm_ssd_w, ln_kidx_w, ln_kidx_b, norm_att_w, w_out, w_router_g, b_router_g, w_router_e, b_router_e, w_exp_up, w_exp_down, norm_f_w):
    batch, seq, d = x_prompt.shape
    dec_batch, dec_seq, _ = x_sample.shape
    assert batch == 1 and dec_seq == 1, "one prompt sequence and one new token per sample sequence"
    depth = w_ada.shape[0]
    heads = (d // 2) // SSD_HEAD_DIM
    xp = x_prompt.reshape(seq, d)
    xs = x_sample.reshape(dec_batch, d)
    n_c = batch + dec_batch
    c_all = jnp.pad(jnp.concatenate([c_prompt, c_sample]), ((0, -n_c % 8), (0, 0)))
    nf = _row(norm_f_w)
    outs_p, outs_s = [], []
    yp = ys = None
    for l in range(depth):
        p = dict(w_in=w_in[l], conv_w=conv_w[l], conv_b=conv_b[l], dt_bias=dt_bias[l], a_log=a_log[l],
                 d_skip=d_skip[l], norm_ssd_w=norm_ssd_w[l], ln_kidx_w=ln_kidx_w[l], ln_kidx_b=ln_kidx_b[l],
                 norm_att_w=norm_att_w[l], w_out=w_out[l], w_router_g=w_router_g[l], b_router_g=b_router_g[l],
                 w_router_e=w_router_e[l], b_router_e=b_router_e[l], w_exp_up=w_exp_up[l],
                 w_exp_down=w_exp_down[l], norm1_w=norm1_w[l], norm2_w=norm2_w[l])
        lp = _layer_params(p)
        mod = _ada_mod(c_all, w_ada[l], b_ada[l])
        mod_p = [mod[0:1, k * d:(k + 1) * d] for k in range(6)]
        mod_s = [mod[batch:n_c, k * d:(k + 1) * d] for k in range(6)]
        routed_p, st_p = _prompt_layer(xp, mod_p, lp, p)
        routed_s, st_s = _sample_layer(xs, mod_s, lp, p, cache_k[l], cache_v[l], cache_k_idx[l], state_conv[l],
                                       state_ssm[l], page_table)
        (xp, yp), (xs, ys) = _moe_and_norm(routed_p, routed_s, mod_p[5], mod_s[5], p, nf)
        outs_p.append(st_p)
        outs_s.append(st_s)

    def stack(outs, n_rows, lead):
        k = jnp.stack([o[0].reshape(lead + (n_rows, KV_HEADS, ATT_HEAD_DIM)) for o in outs])
        v = jnp.stack([o[1].reshape(lead + (n_rows, KV_HEADS, ATT_HEAD_DIM)) for o in outs])
        ki = jnp.stack([o[2].reshape(lead + (n_rows, IDX_DIM)) for o in outs])
        return k, v, ki

    k_p, v_p, ki_p = stack(outs_p, seq, (batch,))
    conv_p = jnp.stack([o[3][None] for o in outs_p])
    ssm_p = jnp.stack([o[4].reshape(batch, heads, SSD_HEAD_DIM, D_STATE) for o in outs_p])
    k_s = jnp.stack([o[0].reshape(dec_batch, dec_seq, KV_HEADS, ATT_HEAD_DIM) for o in outs_s])
    v_s = jnp.stack([o[1].reshape(dec_batch, dec_seq, KV_HEADS, ATT_HEAD_DIM) for o in outs_s])
    ki_s = jnp.stack([o[2].reshape(dec_batch, dec_seq, IDX_DIM) for o in outs_s])
    conv_s = jnp.stack([o[3] for o in outs_s])
    ssm_s = jnp.stack([o[4] for o in outs_s])
    return (yp.reshape(batch, seq, d), ys.reshape(dec_batch, dec_seq, d), k_p, v_p, ki_p, conv_p, ssm_p,
            k_s, v_s, ki_s, conv_s, ssm_s)
```

```python
import functools

import numpy as np
import jax
import jax.numpy as jnp
from jax import lax
from jax.experimental import pallas as pl
from jax.experimental.pallas import tpu as pltpu

F32 = jnp.float32
BF16 = jnp.bfloat16
I32 = jnp.int32

SSD_HEAD_DIM = 64
SSD_GROUPS = 2
D_STATE = 128
CONV_W = 4
SSD_CHUNK = 128
ATT_HEAD_DIM = 128
KV_HEADS = 2
IDX_HEADS = 16
IDX_DIM = 64
TOPK_MAX = 256
N_EGROUPS = 4
EXPERTS_PER_GROUP = 8
N_EXPERTS = N_EGROUPS * EXPERTS_PER_GROUP
D_EXPERT = 512
EPS = 1e-6

LANES = 128
INT_MIN = -2 ** 31
NEG_BIG = -1e30
VMEM_LIMIT = 56 * 1024 * 1024
HIGHEST = lax.Precision.HIGHEST
Q_SCALE = ATT_HEAD_DIM ** -0.5 * 1.4426950408889634


def _cparams(sem):
    return pltpu.CompilerParams(dimension_semantics=sem, vmem_limit_bytes=VMEM_LIMIT)


def _dot(a, b, precision=None):
    return jnp.dot(a, b, preferred_element_type=F32, precision=precision)


def _dot_nt(a, b, precision=None):
    return lax.dot_general(a, b, (((1,), (1,)), ((), ())), preferred_element_type=F32, precision=precision)


def _silu(x):
    return x * jax.nn.sigmoid(x)


def _softplus(x):
    return jnp.maximum(x, 0.0) + jnp.log(1.0 + jnp.exp(-jnp.abs(x)))


def _rms(x):
    return x * lax.rsqrt(jnp.mean(x * x, axis=-1, keepdims=True) + EPS)


def _pad_cols(a, width):
    return jnp.pad(a, ((0, 0), (0, width - a.shape[1])))


def _ada_kernel(c_ref, w_ref, b_ref, o_ref):
    s = _silu(c_ref[...]).astype(BF16)
    o_ref[...] = _dot(s, w_ref[...].astype(BF16)) + b_ref[...]


def _ada_mod(c_all, w_ada, b_ada):
    r, d = c_all.shape
    n = w_ada.shape[1]
    tn = 1024
    return pl.pallas_call(
        _ada_kernel,
        grid=(n // tn,),
        in_specs=[pl.BlockSpec((r, d), lambda j: (0, 0)),
                  pl.BlockSpec((d, tn), lambda j: (0, j)),
                  pl.BlockSpec((1, tn), lambda j: (0, j))],
        out_specs=pl.BlockSpec((r, tn), lambda j: (0, j)),
        out_shape=jax.ShapeDtypeStruct((r, n), F32),
        compiler_params=_cparams(("arbitrary",)),
    )(c_all, w_ada, b_ada.reshape(1, n))


def _in_layout(d_model):
    d_ssd = d_model // 2
    d_att = d_model - d_ssd
    conv_dim = d_ssd + 2 * SSD_GROUPS * D_STATE
    ssd_heads = d_ssd // SSD_HEAD_DIM
    sizes = dict(z=d_ssd, xbc=conv_dim, dt=ssd_heads, q=d_att, k=KV_HEADS * ATT_HEAD_DIM,
                 v=KV_HEADS * ATT_HEAD_DIM, qi=IDX_HEADS * IDX_DIM, ki=IDX_DIM, wi=IDX_HEADS)
    order = ("z", "xbc", "dt", "q", "k", "v", "qi", "ki", "wi")
    src, dst, off_s, off_d = {}, {}, 0, 0
    for name in order:
        w = sizes[name]
        wp = -(-w // LANES) * LANES
        src[name] = (off_s, w)
        dst[name] = (off_d, wp)
        off_s += w
        off_d += wp
    return order, src, dst, off_d


def _perm_w_in(w_in):
    order, src, dst, _ = _in_layout(w_in.shape[0])
    parts = [_pad_cols(w_in[:, src[n][0]:src[n][0] + src[n][1]], dst[n][1]) for n in order]
    return jnp.concatenate(parts, axis=1).astype(BF16)


def _inproj_kernel(seg, x_ref, nw_ref, sc_ref, sh_ref, w_ref, lnw_ref, lnb_ref,
                   z_ref, xbc_ref, dt_ref, q_ref, k_ref, v_ref, kb_ref, vb_ref, qi_ref, ki_ref, wi_ref):
    h = _rms(x_ref[...]) * nw_ref[...]
    h = h * (1.0 + sc_ref[...]) + sh_ref[...]
    hb = h.astype(BF16)

    def mm(name):
        a, w = seg[name]
        return _dot(hb, w_ref[:, a:a + w])

    z_ref[...] = mm("z")
    xbc_ref[...] = mm("xbc")
    dt_ref[...] = mm("dt")
    q_ref[...] = (mm("q") * Q_SCALE).astype(BF16)
    k = mm("k")
    k_ref[...] = k
    kb_ref[...] = k.astype(BF16)
    v = mm("v")
    v_ref[...] = v
    vb_ref[...] = v.astype(BF16)
    qi_ref[...] = mm("qi").astype(BF16)
    wi_ref[...] = mm("wi")
    ki = mm("ki")
    lane = lax.broadcasted_iota(I32, ki.shape, 1)
    ok = lane < IDX_DIM
    mu = jnp.sum(jnp.where(ok, ki, 0.0), axis=-1, keepdims=True) * (1.0 / IDX_DIM)
    cen = jnp.where(ok, ki - mu, 0.0)
    var = jnp.sum(cen * cen, axis=-1, keepdims=True) * (1.0 / IDX_DIM)
    y = cen * lax.rsqrt(var + EPS) * lnw_ref[...] + lnb_ref[...]
    ki_ref[...] = y[:, :IDX_DIM]


def _in_proj(x, nw, sc, sh, w_perm, lnw, lnb, tm):
    t, d = x.shape
    _, _, dst, npad = _in_layout(d)
    tmod = sc.shape[0]
    mod_map = (lambda i: (0, 0)) if tmod == 1 else (lambda i: (i, 0))
    mod_rows = 1 if tmod == 1 else tm
    row = lambda w: pl.BlockSpec((tm, w), lambda i: (i, 0))
    d_ssd, d_att = dst["z"][1], dst["q"][1]
    kvw = KV_HEADS * ATT_HEAD_DIM
    outs = [("z", d_ssd, F32), ("xbc", dst["xbc"][1], F32), ("dt", LANES, F32), ("q", d_att, BF16),
            ("k", kvw, F32), ("v", kvw, F32), ("kb", kvw, BF16), ("vb", kvw, BF16),
            ("qi", IDX_HEADS * IDX_DIM, BF16), ("ki", IDX_DIM, F32), ("wi", LANES, F32)]
    res = pl.pallas_call(
        functools.partial(_inproj_kernel, dst),
        grid=(t // tm,),
        in_specs=[row(d),
                  pl.BlockSpec((1, d), lambda i: (0, 0)),
                  pl.BlockSpec((mod_rows, d), mod_map),
                  pl.BlockSpec((mod_rows, d), mod_map),
                  pl.BlockSpec((d, npad), lambda i: (0, 0)),
                  pl.BlockSpec((1, LANES), lambda i: (0, 0)),
                  pl.BlockSpec((1, LANES), lambda i: (0, 0))],
        out_specs=[row(w) for _, w, _ in outs],
        out_shape=[jax.ShapeDtypeStruct((t, w), dt) for _, w, dt in outs],
        compiler_params=_cparams(("arbitrary",)),
    )(x, nw, sc, sh, w_perm, lnw, lnb)
    return dict(zip([n for n, _, _ in outs], res))


def _ssd_prompt_kernel(n_pairs, xbc_ref, dt_ref, z_ref, cw_ref, cb_ref, dtb_ref, alog_ref, dsk_ref, nw_ref,
                       y_ref, st_ref, xprev, ht, ybuf):
    c = pl.program_id(0)
    q = SSD_CHUNK
    d_ssd = n_pairs * LANES
    gn = SSD_GROUPS * D_STATE

    @pl.when(c == 0)
    def _():
        xprev[...] = jnp.zeros_like(xprev)
        ht[...] = jnp.zeros_like(ht)

    x = xbc_ref[...]
    xp = xprev[...]
    rowi = lax.broadcasted_iota(I32, (q, 1), 0)
    acc = cb_ref[...] + cw_ref[CONV_W - 1:CONV_W, :] * x
    for k in range(1, CONV_W):
        sh = jnp.where(rowi < k, pltpu.roll(xp, k, 0), pltpu.roll(x, k, 0))
        acc = acc + cw_ref[CONV_W - 1 - k:CONV_W - k, :] * sh
    xprev[...] = x
    xc = _silu(acc)

    dt = _softplus(dt_ref[...] + dtb_ref[...])
    a_neg = -jnp.exp(alog_ref[...])
    r2 = lax.broadcasted_iota(I32, (q, q), 0)
    c2 = lax.broadcasted_iota(I32, (q, q), 1)
    tril = c2 <= r2
    a = _dot(tril.astype(F32), dt * a_neg, precision=HIGHEST)
    a_t = a.T
    dt_t = dt.T
    a_last = a[q - 1:q, :]
    wmat = jnp.exp(a_last - a) * dt
    emat = jnp.exp(a)
    cd = jnp.exp(a_last)
    lane = lax.broadcasted_iota(I32, (q, LANES), 1)
    left = lane < SSD_HEAD_DIM
    pairs_per_group = n_pairs // SSD_GROUPS

    bts, cbs, cgs = [], [], []
    for g in range(SSD_GROUPS):
        bg = xc[:, d_ssd + g * D_STATE:d_ssd + (g + 1) * D_STATE]
        cg = xc[:, d_ssd + gn + g * D_STATE:d_ssd + gn + (g + 1) * D_STATE].astype(BF16)
        bt = bg.T.astype(BF16)
        bts.append(bt)
        cgs.append(cg)
        cbs.append(_dot(cg, bt))

    def colb(m, h):
        return jnp.broadcast_to(m[:, h:h + 1], (q, LANES))

    for p in range(n_pairs):
        g = p // pairs_per_group
        h0, h1 = 2 * p, 2 * p + 1
        xpair = xc[:, p * LANES:(p + 1) * LANES]
        xpb = xpair.astype(BF16)
        yd = []
        for h in (h0, h1):
            diff = colb(a, h) - a_t[h:h + 1, :]
            decay = jnp.exp(jnp.where(tril, diff, -jnp.inf))
            sc = cbs[g] * decay * dt_t[h:h + 1, :]
            yd.append(_dot(sc.astype(BF16), xpb))
        y_diag = jnp.where(left, yd[0], yd[1])
        w_pair = jnp.where(left, colb(wmat, h0), colb(wmat, h1))
        e_pair = jnp.where(left, colb(emat, h0), colb(emat, h1))
        cd_pair = jnp.where(left[0:1, :], jnp.broadcast_to(cd[:, h0:h0 + 1], (1, LANES)),
                            jnp.broadcast_to(cd[:, h1:h1 + 1], (1, LANES)))
        hprev = ht[p]
        y_off = _dot(cgs[g], hprev.astype(BF16)) * e_pair
        states = _dot(bts[g], (xpair * w_pair).astype(BF16))
        ht[p] = hprev * cd_pair + states
        ybuf[:, p * LANES:(p + 1) * LANES] = y_diag + y_off + xpair * dsk_ref[:, p * LANES:(p + 1) * LANES]

    y = ybuf[...] * _silu(z_ref[...])
    y_ref[...] = (_rms(y) * nw_ref[...]).astype(BF16)

    @pl.when(c == pl.num_programs(0) - 1)
    def _():
        for p in range(n_pairs):
            st_ref[p * LANES:(p + 1) * LANES, :] = ht[p].T


def _ssd_prompt(xbc, dt_raw, z, conv_w, conv_b, dt_bias_p, a_log_p, dskip_row, norm_w):
    t, conv_dim = xbc.shape
    d_ssd = z.shape[1]
    n_pairs = d_ssd // LANES
    q = SSD_CHUNK
    full = lambda a: pl.BlockSpec(a.shape, lambda c: (0, 0))
    return pl.pallas_call(
        functools.partial(_ssd_prompt_kernel, n_pairs),
        grid=(t // q,),
        in_specs=[pl.BlockSpec((q, conv_dim), lambda c: (c, 0)),
                  pl.BlockSpec((q, LANES), lambda c: (c, 0)),
                  pl.BlockSpec((q, d_ssd), lambda c: (c, 0)),
                  full(conv_w), full(conv_b), full(dt_bias_p), full(a_log_p), full(dskip_row), full(norm_w)],
        out_specs=[pl.BlockSpec((q, d_ssd), lambda c: (c, 0)),
                   pl.BlockSpec((d_ssd, D_STATE), lambda c: (0, 0))],
        out_shape=[jax.ShapeDtypeStruct((t, d_ssd), BF16),
                   jax.ShapeDtypeStruct((d_ssd, D_STATE), F32)],
        scratch_shapes=[pltpu.VMEM((q, conv_dim), F32),
                        pltpu.VMEM((n_pairs, D_STATE, LANES), F32),
                        pltpu.VMEM((q, d_ssd), F32)],
        compiler_params=_cparams(("arbitrary",)),
    )(xbc, dt_raw, z, conv_w, conv_b, dt_bias_p, a_log_p, dskip_row, norm_w)


ROW_SUB = 128
FOLD_CHUNKS = 4
BISECT_CAP = 320
SAFE_SHIFT = 40.0


def _attn_prompt_kernel(topk, tq, q_ref, qi_ref, wi_ref, ki2_ref, kb_ref, vx_ref, nw_ref, o_ref,
                        sc, wb, thr_b, mrun, acc_scr, kmax):
    i = pl.program_id(0)
    kc = tq
    n_chunks = i + 1
    n_heads = q_ref.shape[1] // ATT_HEAD_DIM
    q_per_kv = n_heads // KV_HEADS
    wscale = (IDX_DIM ** -0.5) * (IDX_HEADS ** -0.5)
    n_sub = tq // ROW_SUB

    @pl.when(i == 0)
    def _():
        def norms(j, best):
            kf32 = kb_ref[pl.ds(j * kc, kc), :].astype(F32)
            return tuple(jnp.maximum(best[g], jnp.max(jnp.sum(
                jnp.square(kf32[:, g * ATT_HEAD_DIM:(g + 1) * ATT_HEAD_DIM]), axis=-1, keepdims=True)))
                for g in range(KV_HEADS))

        best = lax.fori_loop(0, kb_ref.shape[0] // kc, norms, (jnp.float32(0.0),) * KV_HEADS)
        for g in range(KV_HEADS):
            kmax[g] = best[g]

    wi = wi_ref[...] * wscale
    for h in range(IDX_HEADS):
        wb[h] = jnp.broadcast_to(wi[:, h:h + 1], (tq, kc))

    row_g = i * tq + lax.broadcasted_iota(I32, (tq, kc), 0)
    col_l = lax.broadcasted_iota(I32, (tq, kc), 1)

    def score_chunk(j, carry):
        k0 = ki2_ref[0, pl.ds(j * kc, kc), :]
        k1 = ki2_ref[1, pl.ds(j * kc, kc), :]
        s = jnp.zeros((tq, kc), F32)
        for p in range(IDX_HEADS // 2):
            qp = qi_ref[:, p * LANES:(p + 1) * LANES]
            s = s + jnp.maximum(_dot_nt(qp, k0), 0.0) * wb[2 * p]
            s = s + jnp.maximum(_dot_nt(qp, k1), 0.0) * wb[2 * p + 1]
        sc[j] = jnp.where(j * kc + col_l <= row_g, s, -jnp.inf)
        return carry

    lax.fori_loop(0, n_chunks, score_chunk, 0)
    for extra in range(FOLD_CHUNKS - 1):
        sc[n_chunks + extra] = jnp.full((tq, kc), -jnp.inf, F32)
    n_steps = (n_chunks + FOLD_CHUNKS - 1) // FOLD_CHUNKS

    def fold(fn, init):
        outs = []
        for r in range(n_sub):
            rows = slice(r * ROW_SUB, (r + 1) * ROW_SUB)

            def body(j, acc, rows=rows, r=r):
                for c in range(FOLD_CHUNKS):
                    for part in range(kc // LANES):
                        c0 = (j * FOLD_CHUNKS + c) * kc + part * LANES
                        acc = fn(acc, sc[j * FOLD_CHUNKS + c, rows, part * LANES:(part + 1) * LANES], c0, r)
                return acc

            outs.append(lax.fori_loop(0, n_steps, body, jax.tree.map(
                lambda v: jnp.full((ROW_SUB, LANES), v, F32), init)))
        return outs

    def spread(row):
        return [jnp.broadcast_to(row[:, r * ROW_SUB:(r + 1) * ROW_SUB], (ROW_SUB, ROW_SUB)).T for r in range(n_sub)]

    def collect(parts, op):
        return jnp.concatenate([op(p.T, axis=0, keepdims=True) for p in parts], axis=1)

    def count(pred):
        return collect(fold(lambda acc, blk, c0, r: acc + jnp.where(pred(blk, c0, r), 1.0, 0.0), 0.0), jnp.sum)

    def count_ge(t):
        tb = spread(t)
        return count(lambda blk, c0, r: blk >= tb[r])

    top2 = fold(lambda acc, blk, c0, r: (jnp.maximum(acc[0], blk), jnp.maximum(acc[1], jnp.minimum(acc[0], blk))),
                (-jnp.inf, -jnp.inf))
    fmin = jnp.float32(jnp.finfo(F32).min)
    lo0 = jnp.maximum(collect([p[1] for p in top2], jnp.min), fmin)
    hi0 = collect([p[1 if topk > LANES else 0] for p in top2], jnp.max)

    kf = jnp.float32(topk)
    n_valid = (i * tq + lax.broadcasted_iota(I32, (1, tq), 1) + 1).astype(F32)
    done0 = jnp.where(n_valid <= kf, 1.0, 0.0)

    def cond(st):
        it, lo, hi, thr, done, stalled = st
        return (it < BISECT_CAP) & (jnp.min(done) == 0.0)

    def halve(st):
        it, lo, hi, thr, done, stalled = st
        mid = 0.5 * lo + 0.5 * hi
        n = count_ge(mid)
        live = done == 0.0
        exact = live & (n == kf)
        stall = live & jnp.logical_not(exact) & ((mid <= lo) | (mid >= hi))
        move = live & jnp.logical_not(exact) & jnp.logical_not(stall)
        up = n >= kf
        return (it + 1,
                jnp.where(move & up, mid, lo),
                jnp.where(move & jnp.logical_not(up), mid, hi),
                jnp.where(exact, mid, thr),
                jnp.where(exact | stall, 1.0, done),
                jnp.where(stall, 1.0, stalled))

    st = lax.while_loop(cond, halve, (jnp.int32(0), lo0, hi0, jnp.full((1, tq), fmin), done0,
                                      jnp.zeros((1, tq), F32)))
    _, lo, hi, thr, _, stalled = st
    n_hi = count_ge(hi)
    thr = jnp.where(stalled == 1.0, jnp.where(n_hi >= kf, hi, lo), thr)
    n_ge = count_ge(thr)
    tb = spread(thr)

    @pl.when(jnp.max(n_ge) > kf)
    def _():
        n_gt = count(lambda blk, c0, r: blk > tb[r])
        need = kf - n_gt
        lane_i = lax.broadcasted_iota(I32, (ROW_SUB, LANES), 1)
        n_bits = max(int(sc.shape[0] * kc).bit_length(), 1)

        def idx_step(t, jlo):
            trial = jlo + jnp.left_shift(jnp.int32(1), n_bits - 1 - t).astype(F32)
            trb = spread(trial)
            f = count(lambda blk, c0, r: (blk == tb[r]) & ((c0 + lane_i).astype(F32) < trb[r]))
            return jnp.where(f <= need - 1.0, trial, jlo)

        jlo = lax.fori_loop(0, n_bits, idx_step, jnp.zeros((1, tq), F32))
        cut = spread(jnp.where(n_ge > kf, jlo + 1.0, jnp.float32(2 ** 30)))

        def drop(j, carry):
            for r in range(n_sub):
                rows = slice(r * ROW_SUB, (r + 1) * ROW_SUB)
                for part in range(kc // LANES):
                    cols = slice(part * LANES, (part + 1) * LANES)
                    blk = sc[j, rows, cols]
                    gone = (blk == tb[r]) & ((j * kc + part * LANES + lane_i).astype(F32) >= cut[r])
                    sc[j, rows, cols] = jnp.where(gone, -jnp.inf, blk)
            return carry

        lax.fori_loop(0, n_chunks, drop, 0)

    for r in range(n_sub):
        thr_b[r * ROW_SUB:(r + 1) * ROW_SUB, :] = jnp.concatenate([tb[r]] * (kc // LANES), axis=1)

    acc_scr[...] = jnp.zeros(acc_scr.shape, F32)
    bounds = []
    for h in range(n_heads):
        qf = q_ref[:, h * ATT_HEAD_DIM:(h + 1) * ATT_HEAD_DIM].astype(F32)
        qn = jnp.sqrt(jnp.sum(qf * qf, axis=-1, keepdims=True))
        bounds.append(qn * (jnp.sqrt(kmax[h // q_per_kv]) * 1.01))
    safe = functools.reduce(jnp.maximum, [jnp.max(b) for b in bounds]) <= SAFE_SHIFT
    vw = 2 * ATT_HEAD_DIM

    def halves(x):
        return [x[:, k * LANES:(k + 1) * LANES] for k in range(kc // LANES)]

    def logits(j, h):
        g = h // q_per_kv
        kj = kb_ref[pl.ds(j * kc, kc), g * ATT_HEAD_DIM:(g + 1) * ATT_HEAD_DIM]
        qh = q_ref[:, h * ATT_HEAD_DIM:(h + 1) * ATT_HEAD_DIM]
        return _dot_nt(qh, kj)

    def max_chunk(j, carry):
        sel = sc[j] >= thr_b[...]
        for h in range(n_heads):
            lg = jnp.where(sel, logits(j, h), NEG_BIG)
            mrun[h] = functools.reduce(jnp.maximum, halves(lg), mrun[h])
        return carry

    @pl.when(safe)
    def _():
        for h in range(n_heads):
            mrun[h] = jnp.broadcast_to(bounds[h], (tq, LANES))

    @pl.when(jnp.logical_not(safe))
    def _():
        mrun[...] = jnp.full(mrun.shape, NEG_BIG, F32)
        lax.fori_loop(0, n_chunks, max_chunk, 0)
        for h in range(n_heads):
            mrun[h] = jnp.broadcast_to(jnp.max(mrun[h], axis=-1, keepdims=True), (tq, LANES))

    def sum_chunk(j, carry):
        sel = sc[j] >= thr_b[...]
        for h in range(n_heads):
            g = h // q_per_kv
            vj = vx_ref[pl.ds(j * kc, kc), g * vw:(g + 1) * vw]
            m = mrun[h]
            p = jnp.where(sel, jnp.exp2(logits(j, h) - jnp.concatenate([m] * (kc // LANES), axis=1)), 0.0)
            acc_scr[h] = acc_scr[h] + _dot(p.astype(BF16), vj)
        return carry

    lax.fori_loop(0, n_chunks, sum_chunk, 0)

    ss = jnp.zeros((tq, 1), F32)
    for h in range(n_heads):
        o = acc_scr[h, :, :ATT_HEAD_DIM] / acc_scr[h, :, ATT_HEAD_DIM:]
        acc_scr[h, :, :ATT_HEAD_DIM] = o
        ss = ss + jnp.sum(o * o, axis=-1, keepdims=True)
    inv = lax.rsqrt(ss * (1.0 / (n_heads * ATT_HEAD_DIM)) + EPS)
    for h in range(n_heads):
        sl = slice(h * ATT_HEAD_DIM, (h + 1) * ATT_HEAD_DIM)
        o_ref[:, sl] = (acc_scr[h, :, :ATT_HEAD_DIM] * inv * nw_ref[:, sl]).astype(BF16)


def _attn_prompt(q, qi, wi, ki2, kb, vx, norm_w, topk, tq):
    t, d_att = q.shape
    n_heads = d_att // ATT_HEAD_DIM
    full = lambda a: pl.BlockSpec(a.shape, lambda i: (0,) * a.ndim)
    return pl.pallas_call(
        functools.partial(_attn_prompt_kernel, topk, tq),
        grid=(t // tq,),
        in_specs=[pl.BlockSpec((tq, d_att), lambda i: (i, 0)),
                  pl.BlockSpec((tq, qi.shape[1]), lambda i: (i, 0)),
                  pl.BlockSpec((tq, LANES), lambda i: (i, 0)),
                  full(ki2), full(kb), full(vx), full(norm_w)],
        out_specs=pl.BlockSpec((tq, d_att), lambda i: (i, 0)),
        out_shape=jax.ShapeDtypeStruct((t, d_att), BF16),
        scratch_shapes=[pltpu.VMEM((t // tq + FOLD_CHUNKS, tq, tq), F32),
                        pltpu.VMEM((IDX_HEADS, tq, tq), F32),
                        pltpu.VMEM((tq, tq), F32),
                        pltpu.VMEM((n_heads, tq, LANES), F32),
                        pltpu.VMEM((n_heads, tq, 2 * ATT_HEAD_DIM), F32),
                        pltpu.SMEM((KV_HEADS,), F32)],
        compiler_params=_cparams(("arbitrary",)),
    )(q, qi, wi, ki2, kb, vx, norm_w)


def _outproj_kernel(ya_ref, yb_ref, w_ref, x_ref, g1_ref, nw_ref, sc_ref, sh_ref, wr_ref, br_ref,
                    x1_ref, h2_ref, eid_ref, wts_ref, cnt_ref):
    d_a = ya_ref.shape[1]
    m = _dot(ya_ref[...], w_ref[:d_a, :]) + _dot(yb_ref[...], w_ref[d_a:, :])
    x1 = x_ref[...] + g1_ref[...] * m
    x1_ref[...] = x1
    h2 = _rms(x1) * nw_ref[...]
    h2 = h2 * (1.0 + sc_ref[...]) + sh_ref[...]
    h2_ref[...] = h2
    lg = _dot(h2.astype(BF16), wr_ref[...]) + br_ref[...]
    lane = lax.broadcasted_iota(I32, lg.shape, 1)
    big = jnp.int32(4 * LANES)

    def rmax(v):
        return jnp.max(v, axis=-1, keepdims=True)

    def rmin(v):
        return jnp.min(v, axis=-1, keepdims=True)

    def rsum(v):
        return jnp.sum(v, axis=-1, keepdims=True)

    is_g = (lane >= N_EXPERTS) & (lane < N_EXPERTS + N_EGROUPS)
    mg = rmax(jnp.where(is_g, lg, -jnp.inf))
    sg = rsum(jnp.where(is_g, jnp.exp(lg - mg), 0.0))
    gsel = rmin(jnp.where(is_g & (lg == mg), lane - N_EXPERTS, big))
    pgsel = 1.0 / sg
    in_grp = (lane < N_EXPERTS) & (jnp.right_shift(lane, EXPERTS_PER_GROUP.bit_length() - 1) == gsel)
    me = rmax(jnp.where(in_grp, lg, -jnp.inf))
    ee = jnp.where(in_grp, jnp.exp(lg - me), 0.0)
    pe = ee / rsum(ee)
    p1 = rmax(jnp.where(in_grp, pe, -1.0))
    i1 = rmin(jnp.where(in_grp & (pe == p1), lane, big))
    rem = in_grp & (lane != i1)
    p2 = rmax(jnp.where(rem, pe, -1.0))
    i2 = rmin(jnp.where(rem & (pe == p2), lane, big))
    den = p1 + p2
    eid_ref[...] = jnp.where(lane == 0, i1, jnp.where(lane == 1, i2, 0))
    wts_ref[...] = jnp.where(lane == 0, pgsel * p1 / den, jnp.where(lane == 1, pgsel * p2 / den, 0.0))

    @pl.when(pl.program_id(0) == 0)
    def _():
        cnt_ref[...] = jnp.zeros_like(cnt_ref)

    chosen = jnp.where((lane == i1) | (lane == i2), 1.0, 0.0)
    cnt_ref[...] += jnp.sum(chosen, axis=0, keepdims=True)


def _out_proj(ya, yb, w_out_b, x, g1, nw2, sc2, sh2, wr, br, tm):
    t, d = x.shape
    d_a = ya.shape[1]
    tmod = g1.shape[0]
    mod_map = (lambda i: (0, 0)) if tmod == 1 else (lambda i: (i, 0))
    mod_rows = 1 if tmod == 1 else tm
    modspec = pl.BlockSpec((mod_rows, d), mod_map)
    row = lambda w: pl.BlockSpec((tm, w), lambda i: (i, 0))
    full = lambda a: pl.BlockSpec(a.shape, lambda i: (0, 0))
    return pl.pallas_call(
        _outproj_kernel,
        grid=(t // tm,),
        in_specs=[row(d_a), row(yb.shape[1]), full(w_out_b), row(d), modspec, full(nw2), modspec, modspec,
                  full(wr), full(br)],
        out_specs=[row(d), row(d), row(LANES), row(LANES), pl.BlockSpec((1, LANES), lambda i: (0, 0))],
        out_shape=[jax.ShapeDtypeStruct((t, d), F32), jax.ShapeDtypeStruct((t, d), F32),
                   jax.ShapeDtypeStruct((t, LANES), I32), jax.ShapeDtypeStruct((t, LANES), F32),
                   jax.ShapeDtypeStruct((1, LANES), F32)],
        compiler_params=_cparams(("arbitrary",)),
    )(ya, yb, w_out_b, x, g1, nw2, sc2, sh2, wr, br)


MOE_TILE = 256


def _moe_pos_kernel(eid_ref, off_ref, pos_ref, carry):
    @pl.when(pl.program_id(0) == 0)
    def _():
        carry[...] = jnp.zeros_like(carry)

    eid = eid_ref[...]
    tm = eid.shape[0]
    i1, i2 = eid[:, 0:1], eid[:, 1:2]
    lane = lax.broadcasted_iota(I32, eid.shape, 1)
    chosen = jnp.where((lane == i1) | (lane == i2), 1.0, 0.0)
    r = lax.broadcasted_iota(I32, (tm, tm), 0)
    c = lax.broadcasted_iota(I32, (tm, tm), 1)
    earlier = _dot(jnp.where(c < r, 1.0, 0.0).astype(BF16), chosen.astype(BF16))
    row = earlier + carry[...] + off_ref[...]
    p1 = jnp.sum(jnp.where(lane == i1, row, 0.0), axis=-1, keepdims=True)
    p2 = jnp.sum(jnp.where(lane == i2, row, 0.0), axis=-1, keepdims=True)
    out = jnp.where(lane == 0, p1, jnp.where(lane == 1, p2, 0.0))
    pos_ref[...] = jnp.where(i1 >= 0, out, -1.0).astype(I32)
    carry[...] += jnp.sum(chosen, axis=0, keepdims=True)


def _moe_positions(eid_all, off_row, tm):
    t = eid_all.shape[0]
    return pl.pallas_call(
        _moe_pos_kernel,
        grid=(t // tm,),
        in_specs=[pl.BlockSpec((tm, LANES), lambda i: (i, 0)), pl.BlockSpec((1, LANES), lambda i: (0, 0))],
        out_specs=pl.BlockSpec((tm, LANES), lambda i: (i, 0)),
        out_shape=jax.ShapeDtypeStruct((t, LANES), I32),
        scratch_shapes=[pltpu.VMEM((1, LANES), F32)],
        compiler_params=_cparams(("arbitrary",)),
    )(eid_all, off_row)


def _dyn_loop(lo, hi, fn, unroll=4):
    shift = unroll.bit_length() - 1
    n_blk = jnp.right_shift(hi - lo, shift)

    def blk(k, carry):
        for u in range(unroll):
            fn(lo + k * unroll + u)
        return carry

    def one(i, carry):
        fn(i)
        return carry

    lax.fori_loop(0, n_blk, blk, 0)
    lax.fori_loop(lo + n_blk * unroll, hi, one, 0)


def _moe_grouped_kernel(t_prompt, te_ref, nu_ref, np_ref, nv_ref, pos_ref, hp_ref, hs_ref, wu_ref, wd_ref, o_ref,
                        src, xbuf, wub, wdb, sem):
    g = pl.program_id(0)
    n_used = nu_ref[0]
    tmg = xbuf.shape[1]
    n_tok = pos_ref.shape[0] // 2
    n_rows = o_ref.shape[0] * pl.num_programs(0)
    slot = g % 2

    def row_copy(h_ref, tok, sl, r):
        return pltpu.make_async_copy(h_ref.at[pl.ds(tok, 1)], xbuf.at[sl, pl.ds(r, 1)], sem.at[sl])

    def gather_start(tile, sl):
        base = tile * tmg
        _dyn_loop(0, np_ref[tile], lambda r: row_copy(hp_ref, src[base + r], sl, r).start())
        _dyn_loop(np_ref[tile], nv_ref[tile], lambda r: row_copy(hs_ref, src[base + r] - t_prompt, sl, r).start())

    def gather_wait(tile, sl):
        _dyn_loop(0, nv_ref[tile], lambda r: row_copy(hp_ref, 0, sl, r).wait())

    @pl.when(g == 0)
    def _():
        def fill(t, carry):
            for k in range(2):
                p = pos_ref[2 * t + k]
                src[jnp.where(p < 0, n_rows, p)] = t
            return carry

        lax.fori_loop(0, n_tok, fill, 0, unroll=4)
        xbuf[...] = jnp.zeros_like(xbuf)
        gather_start(0, 0)

    @pl.when(g < n_used)
    def _():
        @pl.when(g + 1 < n_used)
        def _():
            gather_start(g + 1, 1 - slot)

        gather_wait(g, slot)
        fresh = (g == 0) | (te_ref[g] != te_ref[jnp.maximum(g - 1, 0)])

        @pl.when(fresh)
        def _():
            wub[...] = wu_ref[0].astype(BF16)
            wdb[...] = wd_ref[0].astype(BF16)

        gu = _dot(xbuf[slot].astype(BF16), wub[...])
        de = gu.shape[1] // 2
        act = _silu(gu[:, :de]) * gu[:, de:]
        o_ref[...] = _dot(act.astype(BF16), wdb[...])

    @pl.when(g >= n_used)
    def _():
        o_ref[...] = jnp.zeros_like(o_ref)


def _moe_grouped(plan, pos_flat, h2_p, h2_s, w_up, w_down):
    tile_expert, n_used, tile_np, tile_nv = plan
    n_tiles = tile_expert.shape[0]
    t_prompt, d = h2_p.shape
    _, _, two_de = w_up.shape
    tmg = MOE_TILE
    wmap = lambda g, te, nu, tp, tv, ps: (te[g], 0, 0)
    gs = pltpu.PrefetchScalarGridSpec(
        num_scalar_prefetch=5,
        grid=(n_tiles,),
        in_specs=[pl.BlockSpec(memory_space=pl.ANY), pl.BlockSpec(memory_space=pl.ANY),
                  pl.BlockSpec((1, d, two_de), wmap),
                  pl.BlockSpec((1, two_de // 2, d), wmap)],
        out_specs=pl.BlockSpec((tmg, d), lambda g, te, nu, tp, tv, ps: (g, 0)),
        scratch_shapes=[pltpu.SMEM((n_tiles * tmg + 8,), I32),
                        pltpu.VMEM((2, tmg, d), F32),
                        pltpu.VMEM((d, two_de), BF16),
                        pltpu.VMEM((two_de // 2, d), BF16),
                        pltpu.SemaphoreType.DMA((2,))],
    )
    return pl.pallas_call(
        functools.partial(_moe_grouped_kernel, t_prompt),
        grid_spec=gs,
        out_shape=jax.ShapeDtypeStruct((n_tiles * tmg, d), F32),
        compiler_params=_cparams(("arbitrary",)),
    )(tile_expert, n_used, tile_np, tile_nv, pos_flat, h2_p, h2_s, w_up, w_down)


def _combine_kernel(tok0, pos_ref, x1_ref, wts_ref, g2_ref, nf_ref, ys_ref, xo_ref, o_ref, ybuf, sem):
    i = pl.program_id(0)
    n = pl.num_programs(0)
    tm = x1_ref.shape[0]
    slot = i % 2

    def gather(tile, sl, wait):
        def body(r, carry):
            tok = tok0 + tile * tm + r
            for k in range(2):
                src_row = 0 if wait else pos_ref[2 * tok + k]
                cp = pltpu.make_async_copy(ys_ref.at[pl.ds(src_row, 1)], ybuf.at[sl, k, pl.ds(r, 1)], sem.at[sl])
                cp.wait() if wait else cp.start()
            return carry

        lax.fori_loop(0, tm, body, 0, unroll=8)

    @pl.when(i == 0)
    def _():
        gather(0, 0, False)

    @pl.when(i + 1 < n)
    def _():
        gather(i + 1, 1 - slot, False)

    gather(i, slot, True)
    w = wts_ref[...]
    y = w[:, 0:1] * ybuf[slot, 0] + w[:, 1:2] * ybuf[slot, 1]
    x2 = x1_ref[...] + g2_ref[...] * y
    xo_ref[...] = x2
    o_ref[...] = _rms(x2) * nf_ref[...]


def _combine(pos_flat, x1, wts, g2, nf, ys, tok0, tm):
    t, d = x1.shape
    tmod = g2.shape[0]
    mod_map = (lambda i, ps: (0, 0)) if tmod == 1 else (lambda i, ps: (i, 0))
    row = lambda w: pl.BlockSpec((tm, w), lambda i, ps: (i, 0))
    gs = pltpu.PrefetchScalarGridSpec(
        num_scalar_prefetch=1,
        grid=(t // tm,),
        in_specs=[row(d), row(LANES), pl.BlockSpec((1 if tmod == 1 else tm, d), mod_map),
                  pl.BlockSpec((1, d), lambda i, ps: (0, 0)), pl.BlockSpec(memory_space=pl.ANY)],
        out_specs=[row(d), row(d)],
        scratch_shapes=[pltpu.VMEM((2, 2, tm, d), F32), pltpu.SemaphoreType.DMA((2,))],
    )
    return pl.pallas_call(
        functools.partial(_combine_kernel, tok0),
        grid_spec=gs,
        out_shape=[jax.ShapeDtypeStruct((t, d), F32), jax.ShapeDtypeStruct((t, d), F32)],
        compiler_params=_cparams(("arbitrary",)),
    )(pos_flat, x1, wts, g2, nf, ys)


def _moe_plan(cnt_p, cnt_s, n_tiles):
    cp = cnt_p[0, :N_EXPERTS].astype(I32)
    cnt = cp + cnt_s[0, :N_EXPERTS].astype(I32)
    padded = (cnt + MOE_TILE - 1) // MOE_TILE * MOE_TILE
    ends = jnp.cumsum(padded)
    off = ends - padded
    off_row = _pad_cols(off.astype(F32).reshape(1, N_EXPERTS), LANES)
    starts = jnp.arange(n_tiles, dtype=I32) * MOE_TILE
    te = jnp.minimum(jnp.sum(starts[:, None] >= ends[None, :], axis=1), N_EXPERTS - 1).astype(I32)
    tile_np = jnp.clip(off[te] + cp[te] - starts, 0, MOE_TILE).astype(I32)
    tile_nv = jnp.clip(off[te] + cnt[te] - starts, 0, MOE_TILE).astype(I32)
    return off_row, (te, (ends[-1:] // MOE_TILE).astype(I32), tile_np, tile_nv)


def _ssd_prep_kernel(xbc_ref, p0_ref, p1_ref, p2_ref, cw_ref, cb_ref, dt_ref, dtb_ref, alog_ref, ex_ref,
                     xc_ref, xdt_ref, dec_ref):
    d_ssd = xdt_ref.shape[1]
    acc = (cb_ref[...] + cw_ref[0:1, :] * p0_ref[...] + cw_ref[1:2, :] * p1_ref[...]
           + cw_ref[2:3, :] * p2_ref[...] + cw_ref[3:4, :] * xbc_ref[...])
    xc = _silu(acc)
    xc_ref[...] = xc
    dt = _softplus(dt_ref[...] + dtb_ref[...])
    dec = jnp.exp(dt * (-jnp.exp(alog_ref[...])))
    xdt_ref[...] = _dot(dt, ex_ref[...], precision=HIGHEST) * xc[:, :d_ssd]
    dec_ref[...] = _dot(dec, ex_ref[...], precision=HIGHEST)


def _ssd_prep(xbc, p0, p1, p2, conv_w, conv_b, dt_raw, dt_bias_p, a_log_p, expand):
    b, conv_dim = xbc.shape
    d_ssd = expand.shape[1]
    args = (xbc, p0, p1, p2, conv_w, conv_b, dt_raw, dt_bias_p, a_log_p, expand)
    return pl.pallas_call(
        _ssd_prep_kernel,
        grid=(1,),
        in_specs=[pl.BlockSpec(a.shape, lambda i: (0, 0)) for a in args],
        out_specs=[pl.BlockSpec((b, conv_dim), lambda i: (0, 0)),
                   pl.BlockSpec((b, d_ssd), lambda i: (0, 0)),
                   pl.BlockSpec((b, d_ssd), lambda i: (0, 0))],
        out_shape=[jax.ShapeDtypeStruct((b, conv_dim), F32), jax.ShapeDtypeStruct((b, d_ssd), F32),
                   jax.ShapeDtypeStruct((b, d_ssd), F32)],
        compiler_params=_cparams(("arbitrary",)),
    )(*args)


def _ssd_step_kernel(n_pairs, xdt_ref, dec_ref, bm_ref, cm_ref, s_ref, so_ref, y_ref):
    r2 = lax.broadcasted_iota(I32, (LANES, LANES), 0)
    c2 = lax.broadcasted_iota(I32, (LANES, LANES), 1)
    eye = r2 == c2
    ones = jnp.ones((LANES, LANES), F32)
    pairs_per_group = n_pairs // SSD_GROUPS
    rows_per_pair = LANES // SSD_HEAD_DIM
    for p in range(n_pairs):
        g = p // pairs_per_group
        sl = slice(p * LANES, (p + 1) * LANES)
        hs = slice(p * rows_per_pair, (p + 1) * rows_per_pair)
        hb = s_ref[0, 0, hs].reshape(LANES, D_STATE)
        xd = jnp.where(eye, jnp.broadcast_to(xdt_ref[0, :, sl], (LANES, LANES)), 0.0)
        dd = jnp.where(eye, jnp.broadcast_to(dec_ref[0, :, sl], (LANES, LANES)), 0.0)
        bmat = jnp.broadcast_to(bm_ref[0, :, g * D_STATE:(g + 1) * D_STATE], (LANES, D_STATE))
        upd = _dot(xd, bmat, precision=HIGHEST)
        dcol = _dot(dd, ones, precision=HIGHEST)
        hn = hb * dcol + upd
        so_ref[0, 0, hs] = hn.reshape(rows_per_pair, SSD_HEAD_DIM, D_STATE)
        cmat = jnp.broadcast_to(cm_ref[0, :, g * D_STATE:(g + 1) * D_STATE], (8, D_STATE))
        y_ref[0, :, sl] = _dot_nt(cmat, hn, precision=HIGHEST)[0:1, :]


def _ssd_step(xdt, dec, bm, cm, state):
    b, d_ssd = xdt.shape
    n_pairs = d_ssd // LANES
    heads = d_ssd // SSD_HEAD_DIM
    r3 = lambda a: a.reshape(b, 1, a.shape[1])
    row = lambda w: pl.BlockSpec((1, 1, w), lambda i: (i, 0, 0))
    sspec = pl.BlockSpec((1, 1, heads, SSD_HEAD_DIM, D_STATE), lambda i: (0, i, 0, 0, 0))
    so, y = pl.pallas_call(
        functools.partial(_ssd_step_kernel, n_pairs),
        grid=(b,),
        in_specs=[row(d_ssd), row(d_ssd), row(bm.shape[1]), row(cm.shape[1]), sspec],
        out_specs=[sspec, row(d_ssd)],
        out_shape=[jax.ShapeDtypeStruct(state.shape, F32), jax.ShapeDtypeStruct((b, 1, d_ssd), F32)],
        compiler_params=_cparams(("arbitrary",)),
    )(r3(xdt), r3(dec), r3(bm), r3(cm), state)
    return so, y.reshape(b, d_ssd)


def _ssd_finish_kernel(y_ref, xs_ref, z_ref, dsk_ref, nw_ref, o_ref):
    y = (y_ref[...] + xs_ref[...] * dsk_ref[...]) * _silu(z_ref[...])
    o_ref[...] = (_rms(y) * nw_ref[...]).astype(BF16)


def _ssd_finish(y, xs, z, dskip_row, norm_w):
    args = (y, xs, z, dskip_row, norm_w)
    return pl.pallas_call(
        _ssd_finish_kernel,
        grid=(1,),
        in_specs=[pl.BlockSpec(a.shape, lambda i: (0, 0)) for a in args],
        out_specs=pl.BlockSpec(y.shape, lambda i: (0, 0)),
        out_shape=jax.ShapeDtypeStruct(y.shape, BF16),
        compiler_params=_cparams(("arbitrary",)),
    )(*args)


PAGE_PACK = 8


def _page_copy(cache_ref, buf, sem, pt_ref, b, p, slot):
    rows = cache_ref.shape[2]
    return pltpu.make_async_copy(cache_ref.at[0, pt_ref[b, p]], buf.at[slot, pl.ds(p * rows, rows)], sem.at[slot])


def _score_sample_kernel(n_pages, pt_ref, q8_ref, w8_ref, qi_ref, wi_ref, kin_ref, cache_ref, s_ref, buf, sem):
    b = pl.program_id(0)
    nb = pl.num_programs(0)
    slot = b % 2

    def start(bb, sl):
        def body(p, carry):
            _page_copy(cache_ref, buf, sem, pt_ref, bb, p, sl).start()
            return carry
        lax.fori_loop(0, n_pages, body, 0)

    @pl.when(b == 0)
    def _():
        start(0, 0)

    @pl.when(b + 1 < nb)
    def _():
        start(b + 1, 1 - slot)

    def wait(p, carry):
        _page_copy(cache_ref, buf, sem, pt_ref, b, p, slot).wait()
        return carry

    lax.fori_loop(0, n_pages, wait, 0)

    wscale = (IDX_DIM ** -0.5) * (IDX_HEADS ** -0.5)
    q8 = q8_ref[0]
    w8 = w8_ref[0] * wscale
    kdim = q8.shape[1]
    page_rows = buf.shape[2]

    def group(gi, carry):
        keys_t = buf[slot, pl.ds(gi * kdim, kdim), :].astype(BF16)
        r = jnp.maximum(_dot(q8, keys_t), 0.0) * w8
        s_ref[0, pl.ds(gi * PAGE_PACK, PAGE_PACK), :] = jnp.sum(
            r.reshape(PAGE_PACK, IDX_HEADS, page_rows), axis=1)
        return carry

    lax.fori_loop(0, n_pages // PAGE_PACK, group, 0)
    tail = s_ref.shape[1] - n_pages
    kn = jnp.broadcast_to(kin_ref[0], (page_rows, kin_ref.shape[2])).astype(BF16)
    dn = _dot_nt(qi_ref[0], kn)
    sn = jnp.sum(jnp.maximum(dn, 0.0) * (wi_ref[0] * wscale), axis=0, keepdims=True)
    r = lax.broadcasted_iota(I32, (tail, page_rows), 0)
    c = lax.broadcasted_iota(I32, (tail, page_rows), 1)
    s_ref[0, n_pages:, :] = jnp.where((r == 0) & (c == 0), jnp.broadcast_to(sn, (tail, page_rows)), -jnp.inf)


def _score_sample(page_table, q8, w8, qi3, wi3, ki_new3, cache_kit, tail_rows):
    b, n_pages = page_table.shape
    idx_dim, page_rows = cache_kit.shape[2], cache_kit.shape[3]
    blk = lambda a: pl.BlockSpec((1,) + a.shape[1:], lambda i, pt: (i, 0, 0))
    gs = pltpu.PrefetchScalarGridSpec(
        num_scalar_prefetch=1,
        grid=(b,),
        in_specs=[blk(q8), blk(w8), blk(qi3), blk(wi3), blk(ki_new3), pl.BlockSpec(memory_space=pl.ANY)],
        out_specs=pl.BlockSpec((1, n_pages + tail_rows, page_rows), lambda i, pt: (i, 0, 0)),
        scratch_shapes=[pltpu.VMEM((2, n_pages * idx_dim, page_rows), F32),
                        pltpu.SemaphoreType.DMA((2,))],
    )
    return pl.pallas_call(
        functools.partial(_score_sample_kernel, n_pages),
        grid_spec=gs,
        out_shape=jax.ShapeDtypeStruct((b, n_pages + tail_rows, page_rows), F32),
        compiler_params=_cparams(("arbitrary",)),
    )(page_table, q8, w8, qi3, wi3, ki_new3, cache_kit)


def _select_sample_kernel(topk, s_ref, idx_ref):
    s = s_ref[0]
    n_rows, width = s.shape
    pos = lax.broadcasted_iota(I32, s.shape, 0) * width + lax.broadcasted_iota(I32, s.shape, 1)

    def total(v):
        return jnp.sum(jnp.sum(v, axis=-1, keepdims=True), axis=0, keepdims=True)

    def count(mask):
        return total(jnp.where(mask, 1, 0))

    valid = s > -jnp.inf
    lo0 = jnp.min(jnp.min(jnp.where(valid, s, 3e38), axis=-1, keepdims=True), axis=0, keepdims=True)
    hi0 = jnp.max(jnp.max(s, axis=-1, keepdims=True), axis=0, keepdims=True)
    done0 = (count(valid) <= topk).astype(I32)

    def cond(st):
        it, lo, hi, thr, done, stalled = st
        return (it < BISECT_CAP) & (jnp.min(done) == 0)

    def halve(st):
        it, lo, hi, thr, done, stalled = st
        mid = 0.5 * lo + 0.5 * hi
        n = count(s >= mid)
        live = done == 0
        exact = live & (n == topk)
        stall = live & jnp.logical_not(exact) & ((mid <= lo) | (mid >= hi))
        move = live & jnp.logical_not(exact) & jnp.logical_not(stall)
        up = n >= topk
        return (it + 1,
                jnp.where(move & up, mid, lo),
                jnp.where(move & jnp.logical_not(up), mid, hi),
                jnp.where(exact, mid, thr),
                jnp.where(exact | stall, 1, done),
                jnp.where(stall, 1, stalled))

    st = lax.while_loop(cond, halve, (jnp.int32(0), lo0, hi0, lo0, done0, jnp.zeros((1, 1), I32)))
    _, lo, hi, thr, _, stalled = st
    thr = jnp.where(stalled == 1, jnp.where(count(s >= hi) >= topk, hi, lo), thr)
    tied = s == thr
    n_bits = max(int(n_rows * width).bit_length(), 1)

    def tie_cut():
        need = topk - count(s > thr)

        def idx_step(t, jlo):
            trial = jlo + jnp.left_shift(jnp.int32(1), n_bits - 1 - t)
            return jnp.where(count(tied & (pos < trial)) <= need - 1, trial, jlo)

        return lax.fori_loop(0, n_bits, idx_step, jnp.zeros((1, 1), I32)) + 1

    cut = lax.cond(count(s >= thr)[0, 0] > topk, tie_cut, lambda: jnp.full((1, 1), n_rows * width, I32))
    sel = (s > thr) | (tied & (pos < cut))
    self = jnp.where(sel, 1.0, 0.0).astype(BF16)
    ra = lax.broadcasted_iota(I32, (width, width), 0)
    ca = lax.broadcasted_iota(I32, (width, width), 1)
    local = jnp.where(sel, _dot(self, jnp.where(ra < ca, 1.0, 0.0).astype(BF16)), -1.0)
    cnt_row = _dot_nt(jnp.ones((8, width), BF16), self)
    rb = lax.broadcasted_iota(I32, (n_rows, n_rows), 0)
    cb = lax.broadcasted_iota(I32, (n_rows, n_rows), 1)
    end_row = _dot(cnt_row.astype(BF16), jnp.where(rb <= cb, 1.0, 0.0).astype(BF16))
    rank = lax.broadcasted_iota(I32, (topk, n_rows), 0).astype(F32)
    row_id = lax.broadcasted_iota(I32, (topk, n_rows), 1).astype(F32)
    passed = jnp.broadcast_to(end_row[0:1, :], (topk, n_rows)) <= rank
    row_of = jnp.sum(jnp.where(passed, 1.0, 0.0), axis=-1, keepdims=True)
    start = jnp.sum(jnp.where(passed, jnp.broadcast_to(cnt_row[0:1, :], (topk, n_rows)), 0.0), axis=-1,
                    keepdims=True)
    picked = _dot(jnp.where(row_id == row_of, 1.0, 0.0).astype(BF16), local.astype(BF16))
    lane = lax.broadcasted_iota(I32, (topk, width), 1).astype(F32)
    lane_of = jnp.sum(jnp.where(picked == rank[:, 0:1] - start, lane, 0.0), axis=-1, keepdims=True)
    idx_ref[0] = (row_of * width + lane_of).astype(I32)


def _select_sample(s3, topk):
    b, n_rows, width = s3.shape
    return pl.pallas_call(
        functools.partial(_select_sample_kernel, topk),
        grid=(b,),
        in_specs=[pl.BlockSpec((1, n_rows, width), lambda i: (i, 0, 0))],
        out_specs=pl.BlockSpec((1, topk, 1), lambda i: (i, 0, 0)),
        out_shape=jax.ShapeDtypeStruct((b, topk, 1), I32),
        compiler_params=_cparams(("arbitrary",)),
    )(s3)


def _row_copy(src, dst, sem, src_row, dst_row):
    return pltpu.make_async_copy(src.at[pl.ds(src_row, KV_HEADS)], dst.at[pl.ds(dst_row, KV_HEADS)], sem)


def _attn_sample_kernel(topk, past_len, page_rows, n_pages, idx_ref, pt_ref, q_ref, nw_ref, ck_ref, cv_ref,
                        kn_ref, vn_ref, o_ref, kbuf, vbuf, sem):
    b = pl.program_id(0)

    pow2 = page_rows & (page_rows - 1) == 0

    def start(r, carry):
        j = jnp.minimum(idx_ref[b, r], past_len - 1)
        if pow2:
            page, off = jnp.right_shift(j, page_rows.bit_length() - 1), j & (page_rows - 1)
        else:
            page, off = j // page_rows, j % page_rows
        row = (pt_ref[b, page] * page_rows + off) * KV_HEADS
        _row_copy(ck_ref, kbuf, sem.at[0], row, r * KV_HEADS).start()
        _row_copy(cv_ref, vbuf, sem.at[1], row, r * KV_HEADS).start()
        return carry

    lax.fori_loop(0, topk, start, 0, unroll=8)

    def wait(r, carry):
        _row_copy(ck_ref, kbuf, sem.at[0], 0, r * KV_HEADS).wait()
        _row_copy(cv_ref, vbuf, sem.at[1], 0, r * KV_HEADS).wait()
        return carry

    lax.fori_loop(0, topk, wait, 0, unroll=8)

    @pl.when(idx_ref[b, topk - 1] >= past_len)
    def _():
        last = (topk - 1) * KV_HEADS
        for src_ref, buf, s in ((kn_ref, kbuf, sem.at[0]), (vn_ref, vbuf, sem.at[1])):
            cp = _row_copy(src_ref, buf, s, b * KV_HEADS, last)
            cp.start()
            cp.wait()

    outs = []
    ss = jnp.zeros((1, 1), F32)
    for g in range(KV_HEADS):
        kg = kbuf[pl.ds(g, topk, stride=KV_HEADS), :].astype(BF16)
        vg = vbuf[pl.ds(g, topk, stride=KV_HEADS), :].astype(BF16)
        lg = _dot_nt(q_ref[0, g], kg)
        m = jnp.max(lg, axis=-1, keepdims=True)
        p = jnp.exp2(lg - m)
        p = p / jnp.sum(p, axis=-1, keepdims=True)
        o = _dot(p.astype(BF16), vg)
        rows = lax.broadcasted_iota(I32, o.shape, 0)
        o = jnp.where(rows < q_ref.shape[2] // 2, o, 0.0)
        outs.append(o)
        ss = ss + jnp.sum(jnp.sum(o * o, axis=-1, keepdims=True), axis=0, keepdims=True)
    n_feat = KV_HEADS * (q_ref.shape[2] // 2) * ATT_HEAD_DIM
    inv = lax.rsqrt(ss * (1.0 / n_feat) + EPS)
    for g in range(KV_HEADS):
        o_ref[0, g] = (outs[g] * inv * nw_ref[g]).astype(BF16)


def _attn_sample(idx, page_table, q4, nw3, ck2, cv2, kn2, vn2, past_len, page_rows):
    b, topk = idx.shape
    n_pages = page_table.shape[1]
    gs = pltpu.PrefetchScalarGridSpec(
        num_scalar_prefetch=2,
        grid=(b,),
        in_specs=[pl.BlockSpec((1,) + q4.shape[1:], lambda i, a, c: (i, 0, 0, 0)),
                  pl.BlockSpec(nw3.shape, lambda i, a, c: (0, 0, 0)),
                  pl.BlockSpec(memory_space=pl.ANY), pl.BlockSpec(memory_space=pl.ANY),
                  pl.BlockSpec(memory_space=pl.ANY), pl.BlockSpec(memory_space=pl.ANY)],
        out_specs=pl.BlockSpec((1,) + q4.shape[1:], lambda i, a, c: (i, 0, 0, 0)),
        scratch_shapes=[pltpu.VMEM((topk * KV_HEADS, ATT_HEAD_DIM), F32),
                        pltpu.VMEM((topk * KV_HEADS, ATT_HEAD_DIM), F32),
                        pltpu.SemaphoreType.DMA((2,))],
    )
    return pl.pallas_call(
        functools.partial(_attn_sample_kernel, topk, past_len, page_rows, n_pages),
        grid_spec=gs,
        out_shape=jax.ShapeDtypeStruct(q4.shape, BF16),
        compiler_params=_cparams(("arbitrary",)),
    )(idx, page_table, q4, nw3, ck2, cv2, kn2, vn2)


def _row(v, width=None):
    v = v.reshape(1, -1)
    return v if width is None else _pad_cols(v, width)


def _layer_params(p):
    d = p["w_in"].shape[0]
    wr = jnp.concatenate([p["w_router_e"], p["w_router_g"]], axis=1)
    br = jnp.concatenate([p["b_router_e"], p["b_router_g"]])
    return dict(
        w_perm=_perm_w_in(p["w_in"]),
        w_out_b=p["w_out"].astype(BF16),
        wr=_pad_cols(wr, LANES).astype(BF16),
        br=_row(br, LANES),
        nw1=_row(p["norm1_w"]), nw2=_row(p["norm2_w"]),
        lnw=_row(p["ln_kidx_w"], LANES), lnb=_row(p["ln_kidx_b"], LANES),
        dt_bias=_row(p["dt_bias"], LANES), a_log=_row(p["a_log"], LANES),
        dskip=_row(jnp.repeat(p["d_skip"], SSD_HEAD_DIM)),
        norm_ssd=_row(p["norm_ssd_w"]), norm_att=_row(p["norm_att_w"]),
        conv_w=p["conv_w"], conv_b=_row(p["conv_b"]),
        d_ssd=d // 2,
    )


def _route(x, ya, yb, mod, lp, tm):
    return _out_proj(ya, yb, lp["w_out_b"], x, mod[2], lp["nw2"], mod[4], mod[3], lp["wr"], lp["br"], tm)


def _moe_and_norm(routed_p, routed_s, g2_p, g2_s, p, nf):
    x1p, h2p, eidp, wtsp, cntp = routed_p
    x1s, h2s, eids, wtss, cnts = routed_s
    tp, ts = x1p.shape[0], x1s.shape[0]
    tt = tp + ts
    tpos = 256
    tt_pad = -(-tt // tpos) * tpos
    eid_all = jnp.concatenate([eidp, eids, jnp.full((tt_pad - tt, LANES), -1, I32)])
    n_tiles = -(-(2 * tt + N_EXPERTS * (MOE_TILE - 1)) // MOE_TILE)
    off_row, plan = _moe_plan(cntp, cnts, n_tiles)
    pos_flat = _moe_positions(eid_all, off_row, tpos)[:, :2].reshape(-1)
    ys = _moe_grouped(plan, pos_flat, h2p, h2s, p["w_exp_up"], p["w_exp_down"])
    out_p = _combine(pos_flat, x1p, wtsp, g2_p, nf, ys, tok0=0, tm=256)
    out_s = _combine(pos_flat, x1s, wtss, g2_s, nf, ys, tok0=tp, tm=ts)
    return out_p, out_s


def _prompt_layer(x, mod, lp, p):
    t, d = x.shape
    pr = _in_proj(x, lp["nw1"], mod[1], mod[0], lp["w_perm"], lp["lnw"], lp["lnb"], tm=256)
    y_ssd, st = _ssd_prompt(pr["xbc"], pr["dt"], pr["z"], lp["conv_w"], lp["conv_b"], lp["dt_bias"], lp["a_log"],
                            lp["dskip"], lp["norm_ssd"])
    ki = pr["ki"]
    zeros = jnp.zeros_like(ki)
    ki2 = jnp.stack([jnp.concatenate([ki, zeros], axis=1), jnp.concatenate([zeros, ki], axis=1)]).astype(BF16)
    topk = min(TOPK_MAX, t // 4)
    v3 = pr["vb"].reshape(t, KV_HEADS, ATT_HEAD_DIM)
    vx = jnp.concatenate([v3, jnp.ones_like(v3)], axis=-1).reshape(t, 2 * KV_HEADS * ATT_HEAD_DIM)
    y_att = _attn_prompt(pr["q"], pr["qi"], pr["wi"], ki2, pr["kb"], vx, lp["norm_att"], topk, tq=256)
    routed = _route(x, y_ssd, y_att, mod, lp, tm=256)
    conv_new = jnp.concatenate([jnp.zeros((CONV_W - 1, pr["xbc"].shape[1]), F32), pr["xbc"]])[-(CONV_W - 1):]
    return routed, (pr["k"], pr["v"], ki, conv_new, st)


def _sample_layer(x, mod, lp, p, cache_k, cache_v, cache_ki, conv_prev, ssm_prev, page_table):
    b, d = x.shape
    d_ssd = lp["d_ssd"]
    heads = d_ssd // SSD_HEAD_DIM
    gn = SSD_GROUPS * D_STATE
    pr = _in_proj(x, lp["nw1"], mod[1], mod[0], lp["w_perm"], lp["lnw"], lp["lnb"], tm=b)
    expand = (jnp.arange(LANES)[:, None] == (jnp.arange(d_ssd)[None, :] // SSD_HEAD_DIM)).astype(F32)
    xc, xdt, dec = _ssd_prep(pr["xbc"], conv_prev[:, 0], conv_prev[:, 1], conv_prev[:, 2], lp["conv_w"], lp["conv_b"],
                             pr["dt"], lp["dt_bias"], lp["a_log"], expand)
    xs, bm, cm = xc[:, :d_ssd], xc[:, d_ssd:d_ssd + gn], xc[:, d_ssd + gn:]
    st5 = ssm_prev.reshape((1, b, heads, SSD_HEAD_DIM, D_STATE))
    st_new, y = _ssd_step(xdt, dec, bm, cm, st5)
    y_ssd = _ssd_finish(y, xs, pr["z"], lp["dskip"], lp["norm_ssd"])
    conv_new = jnp.concatenate([conv_prev[:, 1:], pr["xbc"][:, None, :]], axis=1)
    n_pool, page_rows = cache_k.shape[0], cache_k.shape[1]
    n_pages = page_table.shape[1]
    past_len = n_pages * page_rows
    topk = min(TOPK_MAX, (past_len + 1) // 4)
    qi3 = pr["qi"].reshape(b, IDX_HEADS, IDX_DIM)
    wi3 = pr["wi"][:, :IDX_HEADS].reshape(b, IDX_HEADS, 1)
    eye = jnp.eye(PAGE_PACK, dtype=BF16)
    q8 = (eye[None, :, None, :, None] * qi3[:, None, :, None, :]).reshape(b, PAGE_PACK * IDX_HEADS,
                                                                         PAGE_PACK * IDX_DIM)
    w8 = jnp.tile(wi3, (1, PAGE_PACK, 1))
    tail_rows = -(n_pages + 1) % LANES + 1
    cache_kit = jnp.swapaxes(cache_ki, -1, -2)[None]
    s3 = _score_sample(page_table, q8, w8, qi3, wi3, pr["ki"].reshape(b, 1, IDX_DIM), cache_kit, tail_rows)
    idx = _select_sample(s3, topk).reshape(b, topk)
    n_heads = pr["q"].shape[1] // ATT_HEAD_DIM
    q_per_kv = n_heads // KV_HEADS
    q4 = jnp.pad(pr["q"].reshape(b, KV_HEADS, q_per_kv, ATT_HEAD_DIM), ((0, 0), (0, 0), (0, q_per_kv), (0, 0)))
    nw3 = jnp.pad(lp["norm_att"].reshape(KV_HEADS, q_per_kv, ATT_HEAD_DIM), ((0, 0), (0, q_per_kv), (0, 0)))
    ck2 = cache_k.reshape(n_pool * page_rows * KV_HEADS, ATT_HEAD_DIM)
    cv2 = cache_v.reshape(n_pool * page_rows * KV_HEADS, ATT_HEAD_DIM)
    kn2 = pr["k"].reshape(b * KV_HEADS, ATT_HEAD_DIM)
    vn2 = pr["v"].reshape(b * KV_HEADS, ATT_HEAD_DIM)
    o4 = _attn_sample(idx, page_table, q4, nw3, ck2, cv2, kn2, vn2, past_len, page_rows)
    y_att = o4[:, :, :q_per_kv].reshape(b, n_heads * ATT_HEAD_DIM)
    routed = _route(x, y_ssd, y_att, mod, lp, tm=b)
    return routed, (pr["k"], pr["v"], pr["ki"], conv_new, st_new.reshape(ssm_prev.shape))


def kernel(x_prompt, x_sample, cache_k, cache_v, cache_k_idx, state_conv, state_ssm, page_table, c_prompt, c_sample, w_ada, b_ada, norm1_w, norm2_w, w_in, conv_w, conv_b, dt_bias, a_log, d_skip, norm_ssd_w, ln_kidx_w, ln_kidx_b, norm_att_w, w_out, w_router_g, b_router_g, w_router_e, b_router_e, w_exp_up, w_exp_down, norm_f_w):
    batch, seq, d = x_prompt.shape
    dec_batch, dec_seq, _ = x_sample.shape
    assert batch == 1 and dec_seq == 1, "one prompt sequence and one new token per sample sequence"
    depth = w_ada.shape[0]
    heads = (d // 2) // SSD_HEAD_DIM
    xp = x_prompt.reshape(seq, d)
    xs = x_sample.reshape(dec_batch, d)
    n_c = batch + dec_batch
    c_all = jnp.pad(jnp.concatenate([c_prompt, c_sample]), ((0, -n_c % 8), (0, 0)))
    nf = _row(norm_f_w)
    outs_p, outs_s = [], []
    yp = ys = None
    for l in range(depth):
        p = dict(w_in=w_in[l], conv_w=conv_w[l], conv_b=conv_b[l], dt_bias=dt_bias[l], a_log=a_log[l],
                 d_skip=d_skip[l], norm_ssd_w=norm_ssd_w[l], ln_kidx_w=ln_kidx_w[l], ln_kidx_b=ln_kidx_b[l],
                 norm_att_w=norm_att_w[l], w_out=w_out[l], w_router_g=w_router_g[l], b_router_g=b_router_g[l],
                 w_router_e=w_router_e[l], b_router_e=b_router_e[l], w_exp_up=w_exp_up[l],
                 w_exp_down=w_exp_down[l], norm1_w=norm1_w[l], norm2_w=norm2_w[l])
        lp = _layer_params(p)
        mod = _ada_mod(c_all, w_ada[l], b_ada[l])
        mod_p = [mod[0:1, k * d:(k + 1) * d] for k in range(6)]
        mod_s = [mod[batch:n_c, k * d:(k + 1) * d] for k in range(6)]
        routed_p, st_p = _prompt_layer(xp, mod_p, lp, p)
        routed_s, st_s = _sample_layer(xs, mod_s, lp, p, cache_k[l], cache_v[l], cache_k_idx[l], state_conv[l],
                                       state_ssm[l], page_table)
        (xp, yp), (xs, ys) = _moe_and_norm(routed_p, routed_s, mod_p[5], mod_s[5], p, nf)
        outs_p.append(st_p)
        outs_s.append(st_s)

    def stack(outs, n_rows, lead):
        k = jnp.stack([o[0].reshape(lead + (n_rows, KV_HEADS, ATT_HEAD_DIM)) for o in outs])
        v = jnp.stack([o[1].reshape(lead + (n_rows, KV_HEADS, ATT_HEAD_DIM)) for o in outs])
        ki = jnp.stack([o[2].reshape(lead + (n_rows, IDX_DIM)) for o in outs])
        return k, v, ki

    k_p, v_p, ki_p = stack(outs_p, seq, (batch,))
    conv_p = jnp.stack([o[3][None] for o in outs_p])
    ssm_p = jnp.stack([o[4].reshape(batch, heads, SSD_HEAD_DIM, D_STATE) for o in outs_p])
    k_s = jnp.stack([o[0].reshape(dec_batch, dec_seq, KV_HEADS, ATT_HEAD_DIM) for o in outs_s])
    v_s = jnp.stack([o[1].reshape(dec_batch, dec_seq, KV_HEADS, ATT_HEAD_DIM) for o in outs_s])
    ki_s = jnp.stack([o[2].reshape(dec_batch, dec_seq, IDX_DIM) for o in outs_s])
    conv_s = jnp.stack([o[3] for o in outs_s])
    ssm_s = jnp.stack([o[4] for o in outs_s])
    return (yp.reshape(batch, seq, d), ys.reshape(dec_batch, dec_seq, d), k_p, v_p, ki_p, conv_p, ssm_p,
            k_s, v_s, ki_s, conv_s, ssm_s)
```

```python
import functools

import numpy as np
import jax
import jax.numpy as jnp
from jax import lax
from jax.experimental import pallas as pl
from jax.experimental.pallas import tpu as pltpu

F32 = jnp.float32
BF16 = jnp.bfloat16
I32 = jnp.int32

SSD_HEAD_DIM = 64
SSD_GROUPS = 2
D_STATE = 128
CONV_W = 4
SSD_CHUNK = 128
ATT_HEAD_DIM = 128
KV_HEADS = 2
IDX_HEADS = 16
IDX_DIM = 64
TOPK_MAX = 256
N_EGROUPS = 4
EXPERTS_PER_GROUP = 8
N_EXPERTS = N_EGROUPS * EXPERTS_PER_GROUP
D_EXPERT = 512
EPS = 1e-6

LANES = 128
INT_MIN = -2 ** 31
NEG_BIG = -1e30
VMEM_LIMIT = 56 * 1024 * 1024
HIGHEST = lax.Precision.HIGHEST
Q_SCALE = ATT_HEAD_DIM ** -0.5 * 1.4426950408889634


def _cparams(sem):
    return pltpu.CompilerParams(dimension_semantics=sem, vmem_limit_bytes=VMEM_LIMIT)


def _dot(a, b, precision=None):
    return jnp.dot(a, b, preferred_element_type=F32, precision=precision)


def _dot_nt(a, b, precision=None):
    return lax.dot_general(a, b, (((1,), (1,)), ((), ())), preferred_element_type=F32, precision=precision)


def _silu(x):
    return x * jax.nn.sigmoid(x)


def _softplus(x):
    return jnp.maximum(x, 0.0) + jnp.log(1.0 + jnp.exp(-jnp.abs(x)))


def _rms(x):
    return x * lax.rsqrt(jnp.mean(x * x, axis=-1, keepdims=True) + EPS)


def _pad_cols(a, width):
    return jnp.pad(a, ((0, 0), (0, width - a.shape[1])))


def _ada_kernel(c_ref, w_ref, b_ref, o_ref):
    s = _silu(c_ref[...]).astype(BF16)
    o_ref[...] = _dot(s, w_ref[...].astype(BF16)) + b_ref[...]


def _ada_mod(c_all, w_ada, b_ada):
    r, d = c_all.shape
    n = w_ada.shape[1]
    tn = 1024
    return pl.pallas_call(
        _ada_kernel,
        grid=(n // tn,),
        in_specs=[pl.BlockSpec((r, d), lambda j: (0, 0)),
                  pl.BlockSpec((d, tn), lambda j: (0, j)),
                  pl.BlockSpec((1, tn), lambda j: (0, j))],
        out_specs=pl.BlockSpec((r, tn), lambda j: (0, j)),
        out_shape=jax.ShapeDtypeStruct((r, n), F32),
        compiler_params=_cparams(("arbitrary",)),
    )(c_all, w_ada, b_ada.reshape(1, n))


def _in_layout(d_model):
    d_ssd = d_model // 2
    d_att = d_model - d_ssd
    conv_dim = d_ssd + 2 * SSD_GROUPS * D_STATE
    ssd_heads = d_ssd // SSD_HEAD_DIM
    sizes = dict(z=d_ssd, xbc=conv_dim, dt=ssd_heads, q=d_att, k=KV_HEADS * ATT_HEAD_DIM,
                 v=KV_HEADS * ATT_HEAD_DIM, qi=IDX_HEADS * IDX_DIM, ki=IDX_DIM, wi=IDX_HEADS)
    order = ("z", "xbc", "dt", "q", "k", "v", "qi", "ki", "wi")
    src, dst, off_s, off_d = {}, {}, 0, 0
    for name in order:
        w = sizes[name]
        wp = -(-w // LANES) * LANES
        src[name] = (off_s, w)
        dst[name] = (off_d, wp)
        off_s += w
        off_d += wp
    return order, src, dst, off_d


def _perm_w_in(w_in):
    order, src, dst, _ = _in_layout(w_in.shape[0])
    parts = [_pad_cols(w_in[:, src[n][0]:src[n][0] + src[n][1]], dst[n][1]) for n in order]
    return jnp.concatenate(parts, axis=1).astype(BF16)


def _inproj_kernel(seg, x_ref, nw_ref, sc_ref, sh_ref, w_ref, lnw_ref, lnb_ref,
                   z_ref, xbc_ref, dt_ref, q_ref, k_ref, v_ref, kb_ref, vb_ref, qi_ref, ki_ref, wi_ref):
    h = _rms(x_ref[...]) * nw_ref[...]
    h = h * (1.0 + sc_ref[...]) + sh_ref[...]
    hb = h.astype(BF16)

    def mm(name):
        a, w = seg[name]
        return _dot(hb, w_ref[:, a:a + w])

    z_ref[...] = mm("z")
    xbc_ref[...] = mm("xbc")
    dt_ref[...] = mm("dt")
    q_ref[...] = (mm("q") * Q_SCALE).astype(BF16)
    k = mm("k")
    k_ref[...] = k
    kb_ref[...] = k.astype(BF16)
    v = mm("v")
    v_ref[...] = v
    vb_ref[...] = v.astype(BF16)
    qi_ref[...] = mm("qi").astype(BF16)
    wi_ref[...] = mm("wi")
    ki = mm("ki")
    lane = lax.broadcasted_iota(I32, ki.shape, 1)
    ok = lane < IDX_DIM
    mu = jnp.sum(jnp.where(ok, ki, 0.0), axis=-1, keepdims=True) * (1.0 / IDX_DIM)
    cen = jnp.where(ok, ki - mu, 0.0)
    var = jnp.sum(cen * cen, axis=-1, keepdims=True) * (1.0 / IDX_DIM)
    y = cen * lax.rsqrt(var + EPS) * lnw_ref[...] + lnb_ref[...]
    ki_ref[...] = y[:, :IDX_DIM]


def _in_proj(x, nw, sc, sh, w_perm, lnw, lnb, tm):
    t, d = x.shape
    _, _, dst, npad = _in_layout(d)
    tmod = sc.shape[0]
    mod_map = (lambda i: (0, 0)) if tmod == 1 else (lambda i: (i, 0))
    mod_rows = 1 if tmod == 1 else tm
    row = lambda w: pl.BlockSpec((tm, w), lambda i: (i, 0))
    d_ssd, d_att = dst["z"][1], dst["q"][1]
    kvw = KV_HEADS * ATT_HEAD_DIM
    outs = [("z", d_ssd, F32), ("xbc", dst["xbc"][1], F32), ("dt", LANES, F32), ("q", d_att, BF16),
            ("k", kvw, F32), ("v", kvw, F32), ("kb", kvw, BF16), ("vb", kvw, BF16),
            ("qi", IDX_HEADS * IDX_DIM, BF16), ("ki", IDX_DIM, F32), ("wi", LANES, F32)]
    res = pl.pallas_call(
        functools.partial(_inproj_kernel, dst),
        grid=(t // tm,),
        in_specs=[row(d),
                  pl.BlockSpec((1, d), lambda i: (0, 0)),
                  pl.BlockSpec((mod_rows, d), mod_map),
                  pl.BlockSpec((mod_rows, d), mod_map),
                  pl.BlockSpec((d, npad), lambda i: (0, 0)),
                  pl.BlockSpec((1, LANES), lambda i: (0, 0)),
                  pl.BlockSpec((1, LANES), lambda i: (0, 0))],
        out_specs=[row(w) for _, w, _ in outs],
        out_shape=[jax.ShapeDtypeStruct((t, w), dt) for _, w, dt in outs],
        compiler_params=_cparams(("arbitrary",)),
    )(x, nw, sc, sh, w_perm, lnw, lnb)
    return dict(zip([n for n, _, _ in outs], res))


def _ssd_prompt_kernel(n_pairs, xbc_ref, dt_ref, z_ref, cw_ref, cb_ref, dtb_ref, alog_ref, dsk_ref, nw_ref,
                       y_ref, st_ref, xprev, ht, ybuf):
    c = pl.program_id(0)
    q = SSD_CHUNK
    d_ssd = n_pairs * LANES
    gn = SSD_GROUPS * D_STATE

    @pl.when(c == 0)
    def _():
        xprev[...] = jnp.zeros_like(xprev)
        ht[...] = jnp.zeros_like(ht)

    x = xbc_ref[...]
    xp = xprev[...]
    rowi = lax.broadcasted_iota(I32, (q, 1), 0)
    acc = cb_ref[...] + cw_ref[CONV_W - 1:CONV_W, :] * x
    for k in range(1, CONV_W):
        sh = jnp.where(rowi < k, pltpu.roll(xp, k, 0), pltpu.roll(x, k, 0))
        acc = acc + cw_ref[CONV_W - 1 - k:CONV_W - k, :] * sh
    xprev[...] = x
    xc = _silu(acc)

    dt = _softplus(dt_ref[...] + dtb_ref[...])
    a_neg = -jnp.exp(alog_ref[...])
    r2 = lax.broadcasted_iota(I32, (q, q), 0)
    c2 = lax.broadcasted_iota(I32, (q, q), 1)
    tril = c2 <= r2
    a = _dot(tril.astype(F32), dt * a_neg, precision=HIGHEST)
    a_t = a.T
    dt_t = dt.T
    a_last = a[q - 1:q, :]
    wmat = jnp.exp(a_last - a) * dt
    emat = jnp.exp(a)
    cd = jnp.exp(a_last)
    lane = lax.broadcasted_iota(I32, (q, LANES), 1)
    left = lane < SSD_HEAD_DIM
    pairs_per_group = n_pairs // SSD_GROUPS

    bts, cbs, cgs = [], [], []
    for g in range(SSD_GROUPS):
        bg = xc[:, d_ssd + g * D_STATE:d_ssd + (g + 1) * D_STATE]
        cg = xc[:, d_ssd + gn + g * D_STATE:d_ssd + gn + (g + 1) * D_STATE].astype(BF16)
        bt = bg.T.astype(BF16)
        bts.append(bt)
        cgs.append(cg)
        cbs.append(_dot(cg, bt))

    def colb(m, h):
        return jnp.broadcast_to(m[:, h:h + 1], (q, LANES))

    for p in range(n_pairs):
        g = p // pairs_per_group
        h0, h1 = 2 * p, 2 * p + 1
        xpair = xc[:, p * LANES:(p + 1) * LANES]
        xpb = xpair.astype(BF16)
        yd = []
        for h in (h0, h1):
            diff = colb(a, h) - a_t[h:h + 1, :]
            decay = jnp.exp(jnp.where(tril, diff, -jnp.inf))
            sc = cbs[g] * decay * dt_t[h:h + 1, :]
            yd.append(_dot(sc.astype(BF16), xpb))
        y_diag = jnp.where(left, yd[0], yd[1])
        w_pair = jnp.where(left, colb(wmat, h0), colb(wmat, h1))
        e_pair = jnp.where(left, colb(emat, h0), colb(emat, h1))
        cd_pair = jnp.where(left[0:1, :], jnp.broadcast_to(cd[:, h0:h0 + 1], (1, LANES)),
                            jnp.broadcast_to(cd[:, h1:h1 + 1], (1, LANES)))
        hprev = ht[p]
        y_off = _dot(cgs[g], hprev.astype(BF16)) * e_pair
        states = _dot(bts[g], (xpair * w_pair).astype(BF16))
        ht[p] = hprev * cd_pair + states
        ybuf[:, p * LANES:(p + 1) * LANES] = y_diag + y_off + xpair * dsk_ref[:, p * LANES:(p + 1) * LANES]

    y = ybuf[...] * _silu(z_ref[...])
    y_ref[...] = (_rms(y) * nw_ref[...]).astype(BF16)

    @pl.when(c == pl.num_programs(0) - 1)
    def _():
        for p in range(n_pairs):
            st_ref[p * LANES:(p + 1) * LANES, :] = ht[p].T


def _ssd_prompt(xbc, dt_raw, z, conv_w, conv_b, dt_bias_p, a_log_p, dskip_row, norm_w):
    t, conv_dim = xbc.shape
    d_ssd = z.shape[1]
    n_pairs = d_ssd // LANES
    q = SSD_CHUNK
    full = lambda a: pl.BlockSpec(a.shape, lambda c: (0, 0))
    return pl.pallas_call(
        functools.partial(_ssd_prompt_kernel, n_pairs),
        grid=(t // q,),
        in_specs=[pl.BlockSpec((q, conv_dim), lambda c: (c, 0)),
                  pl.BlockSpec((q, LANES), lambda c: (c, 0)),
                  pl.BlockSpec((q, d_ssd), lambda c: (c, 0)),
                  full(conv_w), full(conv_b), full(dt_bias_p), full(a_log_p), full(dskip_row), full(norm_w)],
        out_specs=[pl.BlockSpec((q, d_ssd), lambda c: (c, 0)),
                   pl.BlockSpec((d_ssd, D_STATE), lambda c: (0, 0))],
        out_shape=[jax.ShapeDtypeStruct((t, d_ssd), BF16),
                   jax.ShapeDtypeStruct((d_ssd, D_STATE), F32)],
        scratch_shapes=[pltpu.VMEM((q, conv_dim), F32),
                        pltpu.VMEM((n_pairs, D_STATE, LANES), F32),
                        pltpu.VMEM((q, d_ssd), F32)],
        compiler_params=_cparams(("arbitrary",)),
    )(xbc, dt_raw, z, conv_w, conv_b, dt_bias_p, a_log_p, dskip_row, norm_w)


ROW_SUB = 128
FOLD_CHUNKS = 4
BISECT_CAP = 320
SAFE_SHIFT = 40.0


def _attn_prompt_kernel(topk, tq, q_ref, qi_ref, wi_ref, ki2_ref, kb_ref, vx_ref, nw_ref, o_ref,
                        sc, wb, thr_b, mrun, acc_scr, kmax):
    i = pl.program_id(0)
    kc = tq
    n_chunks = i + 1
    n_heads = q_ref.shape[1] // ATT_HEAD_DIM
    q_per_kv = n_heads // KV_HEADS
    wscale = (IDX_DIM ** -0.5) * (IDX_HEADS ** -0.5)
    n_sub = tq // ROW_SUB

    @pl.when(i == 0)
    def _():
        def norms(j, best):
            kf32 = kb_ref[pl.ds(j * kc, kc), :].astype(F32)
            return tuple(jnp.maximum(best[g], jnp.max(jnp.sum(
                jnp.square(kf32[:, g * ATT_HEAD_DIM:(g + 1) * ATT_HEAD_DIM]), axis=-1, keepdims=True)))
                for g in range(KV_HEADS))

        best = lax.fori_loop(0, kb_ref.shape[0] // kc, norms, (jnp.float32(0.0),) * KV_HEADS)
        for g in range(KV_HEADS):
            kmax[g] = best[g]

    wi = wi_ref[...] * wscale
    for h in range(IDX_HEADS):
        wb[h] = jnp.broadcast_to(wi[:, h:h + 1], (tq, kc))

    row_g = i * tq + lax.broadcasted_iota(I32, (tq, kc), 0)
    col_l = lax.broadcasted_iota(I32, (tq, kc), 1)

    def score_chunk(j, carry):
        k0 = ki2_ref[0, pl.ds(j * kc, kc), :]
        k1 = ki2_ref[1, pl.ds(j * kc, kc), :]
        s = jnp.zeros((tq, kc), F32)
        for p in range(IDX_HEADS // 2):
            qp = qi_ref[:, p * LANES:(p + 1) * LANES]
            s = s + jnp.maximum(_dot_nt(qp, k0), 0.0) * wb[2 * p]
            s = s + jnp.maximum(_dot_nt(qp, k1), 0.0) * wb[2 * p + 1]
        sc[j] = jnp.where(j * kc + col_l <= row_g, s, -jnp.inf)
        return carry

    lax.fori_loop(0, n_chunks, score_chunk, 0)
    for extra in range(FOLD_CHUNKS - 1):
        sc[n_chunks + extra] = jnp.full((tq, kc), -jnp.inf, F32)
    n_steps = (n_chunks + FOLD_CHUNKS - 1) // FOLD_CHUNKS

    def fold(fn, init):
        outs = []
        for r in range(n_sub):
            rows = slice(r * ROW_SUB, (r + 1) * ROW_SUB)

            def body(j, acc, rows=rows, r=r):
                for c in range(FOLD_CHUNKS):
                    for part in range(kc // LANES):
                        c0 = (j * FOLD_CHUNKS + c) * kc + part * LANES
                        acc = fn(acc, sc[j * FOLD_CHUNKS + c, rows, part * LANES:(part + 1) * LANES], c0, r)
                return acc

            outs.append(lax.fori_loop(0, n_steps, body, jax.tree.map(
                lambda v: jnp.full((ROW_SUB, LANES), v, F32), init)))
        return outs

    def spread(row):
        return [jnp.broadcast_to(row[:, r * ROW_SUB:(r + 1) * ROW_SUB], (ROW_SUB, ROW_SUB)).T for r in range(n_sub)]

    def collect(parts, op):
        return jnp.concatenate([op(p.T, axis=0, keepdims=True) for p in parts], axis=1)

    def count(pred):
        return collect(fold(lambda acc, blk, c0, r: acc + jnp.where(pred(blk, c0, r), 1.0, 0.0), 0.0), jnp.sum)

    def count_ge(t):
        tb = spread(t)
        return count(lambda blk, c0, r: blk >= tb[r])

    top2 = fold(lambda acc, blk, c0, r: (jnp.maximum(acc[0], blk), jnp.maximum(acc[1], jnp.minimum(acc[0], blk))),
                (-jnp.inf, -jnp.inf))
    fmin = jnp.float32(jnp.finfo(F32).min)
    lo0 = jnp.maximum(collect([p[1] for p in top2], jnp.min), fmin)
    hi0 = collect([p[1 if topk > LANES else 0] for p in top2], jnp.max)

    kf = jnp.float32(topk)
    n_valid = (i * tq + lax.broadcasted_iota(I32, (1, tq), 1) + 1).astype(F32)
    done0 = jnp.where(n_valid <= kf, 1.0, 0.0)

    def cond(st):
        it, lo, hi, thr, done, stalled = st
        return (it < BISECT_CAP) & (jnp.min(done) == 0.0)

    def halve(st):
        it, lo, hi, thr, done, stalled = st
        mid = 0.5 * lo + 0.5 * hi
        n = count_ge(mid)
        live = done == 0.0
        exact = live & (n == kf)
        stall = live & jnp.logical_not(exact) & ((mid <= lo) | (mid >= hi))
        move = live & jnp.logical_not(exact) & jnp.logical_not(stall)
        up = n >= kf
        return (it + 1,
                jnp.where(move & up, mid, lo),
                jnp.where(move & jnp.logical_not(up), mid, hi),
                jnp.where(exact, mid, thr),
                jnp.where(exact | stall, 1.0, done),
                jnp.where(stall, 1.0, stalled))

    st = lax.while_loop(cond, halve, (jnp.int32(0), lo0, hi0, jnp.full((1, tq), fmin), done0,
                                      jnp.zeros((1, tq), F32)))
    _, lo, hi, thr, _, stalled = st
    n_hi = count_ge(hi)
    thr = jnp.where(stalled == 1.0, jnp.where(n_hi >= kf, hi, lo), thr)
    n_ge = count_ge(thr)
    tb = spread(thr)

    @pl.when(jnp.max(n_ge) > kf)
    def _():
        n_gt = count(lambda blk, c0, r: blk > tb[r])
        need = kf - n_gt
        lane_i = lax.broadcasted_iota(I32, (ROW_SUB, LANES), 1)
        n_bits = max(int(sc.shape[0] * kc).bit_length(), 1)

        def idx_step(t, jlo):
            trial = jlo + jnp.left_shift(jnp.int32(1), n_bits - 1 - t).astype(F32)
            trb = spread(trial)
            f = count(lambda blk, c0, r: (blk == tb[r]) & ((c0 + lane_i).astype(F32) < trb[r]))
            return jnp.where(f <= need - 1.0, trial, jlo)

        jlo = lax.fori_loop(0, n_bits, idx_step, jnp.zeros((1, tq), F32))
        cut = spread(jnp.where(n_ge > kf, jlo + 1.0, jnp.float32(2 ** 30)))

        def drop(j, carry):
            for r in range(n_sub):
                rows = slice(r * ROW_SUB, (r + 1) * ROW_SUB)
                for part in range(kc // LANES):
                    cols = slice(part * LANES, (part + 1) * LANES)
                    blk = sc[j, rows, cols]
                    gone = (blk == tb[r]) & ((j * kc + part * LANES + lane_i).astype(F32) >= cut[r])
                    sc[j, rows, cols] = jnp.where(gone, -jnp.inf, blk)
            return carry

        lax.fori_loop(0, n_chunks, drop, 0)

    for r in range(n_sub):
        thr_b[r * ROW_SUB:(r + 1) * ROW_SUB, :] = jnp.concatenate([tb[r]] * (kc // LANES), axis=1)

    acc_scr[...] = jnp.zeros(acc_scr.shape, F32)
    bounds = []
    for h in range(n_heads):
        qf = q_ref[:, h * ATT_HEAD_DIM:(h + 1) * ATT_HEAD_DIM].astype(F32)
        qn = jnp.sqrt(jnp.sum(qf * qf, axis=-1, keepdims=True))
        bounds.append(qn * (jnp.sqrt(kmax[h // q_per_kv]) * 1.01))
    safe = functools.reduce(jnp.maximum, [jnp.max(b) for b in bounds]) <= SAFE_SHIFT
    vw = 2 * ATT_HEAD_DIM

    def halves(x):
        return [x[:, k * LANES:(k + 1) * LANES] for k in range(kc // LANES)]

    def logits(j, h):
        g = h // q_per_kv
        kj = kb_ref[pl.ds(j * kc, kc), g * ATT_HEAD_DIM:(g + 1) * ATT_HEAD_DIM]
        qh = q_ref[:, h * ATT_HEAD_DIM:(h + 1) * ATT_HEAD_DIM]
        return _dot_nt(qh, kj)

    def max_chunk(j, carry):
        sel = sc[j] >= thr_b[...]
        for h in range(n_heads):
            lg = jnp.where(sel, logits(j, h), NEG_BIG)
            mrun[h] = functools.reduce(jnp.maximum, halves(lg), mrun[h])
        return carry

    @pl.when(safe)
    def _():
        for h in range(n_heads):
            mrun[h] = jnp.broadcast_to(bounds[h], (tq, LANES))

    @pl.when(jnp.logical_not(safe))
    def _():
        mrun[...] = jnp.full(mrun.shape, NEG_BIG, F32)
        lax.fori_loop(0, n_chunks, max_chunk, 0)
        for h in range(n_heads):
            mrun[h] = jnp.broadcast_to(jnp.max(mrun[h], axis=-1, keepdims=True), (tq, LANES))

    def sum_chunk(j, carry):
        sel = sc[j] >= thr_b[...]
        for h in range(n_heads):
            g = h // q_per_kv
            vj = vx_ref[pl.ds(j * kc, kc), g * vw:(g + 1) * vw]
            m = mrun[h]
            p = jnp.where(sel, jnp.exp2(logits(j, h) - jnp.concatenate([m] * (kc // LANES), axis=1)), 0.0)
            acc_scr[h] = acc_scr[h] + _dot(p.astype(BF16), vj)
        return carry

    lax.fori_loop(0, n_chunks, sum_chunk, 0)

    ss = jnp.zeros((tq, 1), F32)
    for h in range(n_heads):
        o = acc_scr[h, :, :ATT_HEAD_DIM] / acc_scr[h, :, ATT_HEAD_DIM:]
        acc_scr[h, :, :ATT_HEAD_DIM] = o
        ss = ss + jnp.sum(o * o, axis=-1, keepdims=True)
    inv = lax.rsqrt(ss * (1.0 / (n_heads * ATT_HEAD_DIM)) + EPS)
    for h in range(n_heads):
        sl = slice(h * ATT_HEAD_DIM, (h + 1) * ATT_HEAD_DIM)
        o_ref[:, sl] = (acc_scr[h, :, :ATT_HEAD_DIM] * inv * nw_ref[:, sl]).astype(BF16)


def _attn_prompt(q, qi, wi, ki2, kb, vx, norm_w, topk, tq):
    t, d_att = q.shape
    n_heads = d_att // ATT_HEAD_DIM
    full = lambda a: pl.BlockSpec(a.shape, lambda i: (0,) * a.ndim)
    return pl.pallas_call(
        functools.partial(_attn_prompt_kernel, topk, tq),
        grid=(t // tq,),
        in_specs=[pl.BlockSpec((tq, d_att), lambda i: (i, 0)),
                  pl.BlockSpec((tq, qi.shape[1]), lambda i: (i, 0)),
                  pl.BlockSpec((tq, LANES), lambda i: (i, 0)),
                  full(ki2), full(kb), full(vx), full(norm_w)],
        out_specs=pl.BlockSpec((tq, d_att), lambda i: (i, 0)),
        out_shape=jax.ShapeDtypeStruct((t, d_att), BF16),
        scratch_shapes=[pltpu.VMEM((t // tq + FOLD_CHUNKS, tq, tq), F32),
                        pltpu.VMEM((IDX_HEADS, tq, tq), F32),
                        pltpu.VMEM((tq, tq), F32),
                        pltpu.VMEM((n_heads, tq, LANES), F32),
                        pltpu.VMEM((n_heads, tq, 2 * ATT_HEAD_DIM), F32),
                        pltpu.SMEM((KV_HEADS,), F32)],
        compiler_params=_cparams(("arbitrary",)),
    )(q, qi, wi, ki2, kb, vx, norm_w)


def _outproj_kernel(ya_ref, yb_ref, w_ref, x_ref, g1_ref, nw_ref, sc_ref, sh_ref, wr_ref, br_ref,
                    x1_ref, h2_ref, eid_ref, wts_ref, cnt_ref):
    d_a = ya_ref.shape[1]
    m = _dot(ya_ref[...], w_ref[:d_a, :]) + _dot(yb_ref[...], w_ref[d_a:, :])
    x1 = x_ref[...] + g1_ref[...] * m
    x1_ref[...] = x1
    h2 = _rms(x1) * nw_ref[...]
    h2 = h2 * (1.0 + sc_ref[...]) + sh_ref[...]
    h2_ref[...] = h2
    lg = _dot(h2.astype(BF16), wr_ref[...]) + br_ref[...]
    lane = lax.broadcasted_iota(I32, lg.shape, 1)
    big = jnp.int32(4 * LANES)

    def rmax(v):
        return jnp.max(v, axis=-1, keepdims=True)

    def rmin(v):
        return jnp.min(v, axis=-1, keepdims=True)

    def rsum(v):
        return jnp.sum(v, axis=-1, keepdims=True)

    is_g = (lane >= N_EXPERTS) & (lane < N_EXPERTS + N_EGROUPS)
    mg = rmax(jnp.where(is_g, lg, -jnp.inf))
    sg = rsum(jnp.where(is_g, jnp.exp(lg - mg), 0.0))
    gsel = rmin(jnp.where(is_g & (lg == mg), lane - N_EXPERTS, big))
    pgsel = 1.0 / sg
    in_grp = (lane < N_EXPERTS) & (jnp.right_shift(lane, EXPERTS_PER_GROUP.bit_length() - 1) == gsel)
    me = rmax(jnp.where(in_grp, lg, -jnp.inf))
    ee = jnp.where(in_grp, jnp.exp(lg - me), 0.0)
    pe = ee / rsum(ee)
    p1 = rmax(jnp.where(in_grp, pe, -1.0))
    i1 = rmin(jnp.where(in_grp & (pe == p1), lane, big))
    rem = in_grp & (lane != i1)
    p2 = rmax(jnp.where(rem, pe, -1.0))
    i2 = rmin(jnp.where(rem & (pe == p2), lane, big))
    den = p1 + p2
    eid_ref[...] = jnp.where(lane == 0, i1, jnp.where(lane == 1, i2, 0))
    wts_ref[...] = jnp.where(lane == 0, pgsel * p1 / den, jnp.where(lane == 1, pgsel * p2 / den, 0.0))

    @pl.when(pl.program_id(0) == 0)
    def _():
        cnt_ref[...] = jnp.zeros_like(cnt_ref)

    chosen = jnp.where((lane == i1) | (lane == i2), 1.0, 0.0)
    cnt_ref[...] += jnp.sum(chosen, axis=0, keepdims=True)


def _out_proj(ya, yb, w_out_b, x, g1, nw2, sc2, sh2, wr, br, tm):
    t, d = x.shape
    d_a = ya.shape[1]
    tmod = g1.shape[0]
    mod_map = (lambda i: (0, 0)) if tmod == 1 else (lambda i: (i, 0))
    mod_rows = 1 if tmod == 1 else tm
    modspec = pl.BlockSpec((mod_rows, d), mod_map)
    row = lambda w: pl.BlockSpec((tm, w), lambda i: (i, 0))
    full = lambda a: pl.BlockSpec(a.shape, lambda i: (0, 0))
    return pl.pallas_call(
        _outproj_kernel,
        grid=(t // tm,),
        in_specs=[row(d_a), row(yb.shape[1]), full(w_out_b), row(d), modspec, full(nw2), modspec, modspec,
                  full(wr), full(br)],
        out_specs=[row(d), row(d), row(LANES), row(LANES), pl.BlockSpec((1, LANES), lambda i: (0, 0))],
        out_shape=[jax.ShapeDtypeStruct((t, d), F32), jax.ShapeDtypeStruct((t, d), F32),
                   jax.ShapeDtypeStruct((t, LANES), I32), jax.ShapeDtypeStruct((t, LANES), F32),
                   jax.ShapeDtypeStruct((1, LANES), F32)],
        compiler_params=_cparams(("arbitrary",)),
    )(ya, yb, w_out_b, x, g1, nw2, sc2, sh2, wr, br)


MOE_TILE = 256


def _moe_pos_kernel(eid_ref, off_ref, pos_ref, carry):
    @pl.when(pl.program_id(0) == 0)
    def _():
        carry[...] = jnp.zeros_like(carry)

    eid = eid_ref[...]
    tm = eid.shape[0]
    i1, i2 = eid[:, 0:1], eid[:, 1:2]
    lane = lax.broadcasted_iota(I32, eid.shape, 1)
    chosen = jnp.where((lane == i1) | (lane == i2), 1.0, 0.0)
    r = lax.broadcasted_iota(I32, (tm, tm), 0)
    c = lax.broadcasted_iota(I32, (tm, tm), 1)
    earlier = _dot(jnp.where(c < r, 1.0, 0.0).astype(BF16), chosen.astype(BF16))
    row = earlier + carry[...] + off_ref[...]
    p1 = jnp.sum(jnp.where(lane == i1, row, 0.0), axis=-1, keepdims=True)
    p2 = jnp.sum(jnp.where(lane == i2, row, 0.0), axis=-1, keepdims=True)
    out = jnp.where(lane == 0, p1, jnp.where(lane == 1, p2, 0.0))
    pos_ref[...] = jnp.where(i1 >= 0, out, -1.0).astype(I32)
    carry[...] += jnp.sum(chosen, axis=0, keepdims=True)


def _moe_positions(eid_all, off_row, tm):
    t = eid_all.shape[0]
    return pl.pallas_call(
        _moe_pos_kernel,
        grid=(t // tm,),
        in_specs=[pl.BlockSpec((tm, LANES), lambda i: (i, 0)), pl.BlockSpec((1, LANES), lambda i: (0, 0))],
        out_specs=pl.BlockSpec((tm, LANES), lambda i: (i, 0)),
        out_shape=jax.ShapeDtypeStruct((t, LANES), I32),
        scratch_shapes=[pltpu.VMEM((1, LANES), F32)],
        compiler_params=_cparams(("arbitrary",)),
    )(eid_all, off_row)


def _dyn_loop(lo, hi, fn, unroll=4):
    shift = unroll.bit_length() - 1
    n_blk = jnp.right_shift(hi - lo, shift)

    def blk(k, carry):
        for u in range(unroll):
            fn(lo + k * unroll + u)
        return carry

    def one(i, carry):
        fn(i)
        return carry

    lax.fori_loop(0, n_blk, blk, 0)
    lax.fori_loop(lo + n_blk * unroll, hi, one, 0)


def _moe_grouped_kernel(t_prompt, te_ref, nu_ref, np_ref, nv_ref, pos_ref, hp_ref, hs_ref, wu_ref, wd_ref, o_ref,
                        src, xbuf, wub, wdb, sem):
    g = pl.program_id(0)
    n_used = nu_ref[0]
    tmg = xbuf.shape[1]
    n_tok = pos_ref.shape[0] // 2
    n_rows = o_ref.shape[0] * pl.num_programs(0)
    slot = g % 2

    def row_copy(h_ref, tok, sl, r):
        return pltpu.make_async_copy(h_ref.at[pl.ds(tok, 1)], xbuf.at[sl, pl.ds(r, 1)], sem.at[sl])

    def gather_start(tile, sl):
        base = tile * tmg
        _dyn_loop(0, np_ref[tile], lambda r: row_copy(hp_ref, src[base + r], sl, r).start())
        _dyn_loop(np_ref[tile], nv_ref[tile], lambda r: row_copy(hs_ref, src[base + r] - t_prompt, sl, r).start())

    def gather_wait(tile, sl):
        _dyn_loop(0, nv_ref[tile], lambda r: row_copy(hp_ref, 0, sl, r).wait())

    @pl.when(g == 0)
    def _():
        def fill(t, carry):
            for k in range(2):
                p = pos_ref[2 * t + k]
                src[jnp.where(p < 0, n_rows, p)] = t
            return carry

        lax.fori_loop(0, n_tok, fill, 0, unroll=4)
        xbuf[...] = jnp.zeros_like(xbuf)
        gather_start(0, 0)

    @pl.when(g < n_used)
    def _():
        @pl.when(g + 1 < n_used)
        def _():
            gather_start(g + 1, 1 - slot)

        gather_wait(g, slot)
        fresh = (g == 0) | (te_ref[g] != te_ref[jnp.maximum(g - 1, 0)])

        @pl.when(fresh)
        def _():
            wub[...] = wu_ref[0].astype(BF16)
            wdb[...] = wd_ref[0].astype(BF16)

        gu = _dot(xbuf[slot].astype(BF16), wub[...])
        de = gu.shape[1] // 2
        act = _silu(gu[:, :de]) * gu[:, de:]
        o_ref[...] = _dot(act.astype(BF16), wdb[...])

    @pl.when(g >= n_used)
    def _():
        o_ref[...] = jnp.zeros_like(o_ref)


def _moe_grouped(plan, pos_flat, h2_p, h2_s, w_up, w_down):
    tile_expert, n_used, tile_np, tile_nv = plan
    n_tiles = tile_expert.shape[0]
    t_prompt, d = h2_p.shape
    _, _, two_de = w_up.shape
    tmg = MOE_TILE
    wmap = lambda g, te, nu, tp, tv, ps: (te[g], 0, 0)
    gs = pltpu.PrefetchScalarGridSpec(
        num_scalar_prefetch=5,
        grid=(n_tiles,),
        in_specs=[pl.BlockSpec(memory_space=pl.ANY), pl.BlockSpec(memory_space=pl.ANY),
                  pl.BlockSpec((1, d, two_de), wmap),
                  pl.BlockSpec((1, two_de // 2, d), wmap)],
        out_specs=pl.BlockSpec((tmg, d), lambda g, te, nu, tp, tv, ps: (g, 0)),
        scratch_shapes=[pltpu.SMEM((n_tiles * tmg + 8,), I32),
                        pltpu.VMEM((2, tmg, d), F32),
                        pltpu.VMEM((d, two_de), BF16),
                        pltpu.VMEM((two_de // 2, d), BF16),
                        pltpu.SemaphoreType.DMA((2,))],
    )
    return pl.pallas_call(
        functools.partial(_moe_grouped_kernel, t_prompt),
        grid_spec=gs,
        out_shape=jax.ShapeDtypeStruct((n_tiles * tmg, d), F32),
        compiler_params=_cparams(("arbitrary",)),
    )(tile_expert, n_used, tile_np, tile_nv, pos_flat, h2_p, h2_s, w_up, w_down)


def _combine_kernel(tok0, pos_ref, x1_ref, wts_ref, g2_ref, nf_ref, ys_ref, xo_ref, o_ref, ybuf, sem):
    i = pl.program_id(0)
    n = pl.num_programs(0)
    tm = x1_ref.shape[0]
    slot = i % 2

    def gather(tile, sl, wait):
        def body(r, carry):
            tok = tok0 + tile * tm + r
            for k in range(2):
                src_row = 0 if wait else pos_ref[2 * tok + k]
                cp = pltpu.make_async_copy(ys_ref.at[pl.ds(src_row, 1)], ybuf.at[sl, k, pl.ds(r, 1)], sem.at[sl])
                cp.wait() if wait else cp.start()
            return carry

        lax.fori_loop(0, tm, body, 0, unroll=8)

    @pl.when(i == 0)
    def _():
        gather(0, 0, False)

    @pl.when(i + 1 < n)
    def _():
        gather(i + 1, 1 - slot, False)

    gather(i, slot, True)
    w = wts_ref[...]
    y = w[:, 0:1] * ybuf[slot, 0] + w[:, 1:2] * ybuf[slot, 1]
    x2 = x1_ref[...] + g2_ref[...] * y
    xo_ref[...] = x2
    o_ref[...] = _rms(x2) * nf_ref[...]


def _combine(pos_flat, x1, wts, g2, nf, ys, tok0, tm):
    t, d = x1.shape
    tmod = g2.shape[0]
    mod_map = (lambda i, ps: (0, 0)) if tmod == 1 else (lambda i, ps: (i, 0))
    row = lambda w: pl.BlockSpec((tm, w), lambda i, ps: (i, 0))
    gs = pltpu.PrefetchScalarGridSpec(
        num_scalar_prefetch=1,
        grid=(t // tm,),
        in_specs=[row(d), row(LANES), pl.BlockSpec((1 if tmod == 1 else tm, d), mod_map),
                  pl.BlockSpec((1, d), lambda i, ps: (0, 0)), pl.BlockSpec(memory_space=pl.ANY)],
        out_specs=[row(d), row(d)],
        scratch_shapes=[pltpu.VMEM((2, 2, tm, d), F32), pltpu.SemaphoreType.DMA((2,))],
    )
    return pl.pallas_call(
        functools.partial(_combine_kernel, tok0),
        grid_spec=gs,
        out_shape=[jax.ShapeDtypeStruct((t, d), F32), jax.ShapeDtypeStruct((t, d), F32)],
        compiler_params=_cparams(("arbitrary",)),
    )(pos_flat, x1, wts, g2, nf, ys)


def _moe_plan(cnt_p, cnt_s, n_tiles):
    cp = cnt_p[0, :N_EXPERTS].astype(I32)
    cnt = cp + cnt_s[0, :N_EXPERTS].astype(I32)
    padded = (cnt + MOE_TILE - 1) // MOE_TILE * MOE_TILE
    ends = jnp.cumsum(padded)
    off = ends - padded
    off_row = _pad_cols(off.astype(F32).reshape(1, N_EXPERTS), LANES)
    starts = jnp.arange(n_tiles, dtype=I32) * MOE_TILE
    te = jnp.minimum(jnp.sum(starts[:, None] >= ends[None, :], axis=1), N_EXPERTS - 1).astype(I32)
    tile_np = jnp.clip(off[te] + cp[te] - starts, 0, MOE_TILE).astype(I32)
    tile_nv = jnp.clip(off[te] + cnt[te] - starts, 0, MOE_TILE).astype(I32)
    return off_row, (te, (ends[-1:] // MOE_TILE).astype(I32), tile_np, tile_nv)


def _ssd_prep_kernel(xbc_ref, p0_ref, p1_ref, p2_ref, cw_ref, cb_ref, dt_ref, dtb_ref, alog_ref, ex_ref,
                     xc_ref, xdt_ref, dec_ref):
    d_ssd = xdt_ref.shape[1]
    acc = (cb_ref[...] + cw_ref[0:1, :] * p0_ref[...] + cw_ref[1:2, :] * p1_ref[...]
           + cw_ref[2:3, :] * p2_ref[...] + cw_ref[3:4, :] * xbc_ref[...])
    xc = _silu(acc)
    xc_ref[...] = xc
    dt = _softplus(dt_ref[...] + dtb_ref[...])
    dec = jnp.exp(dt * (-jnp.exp(alog_ref[...])))
    xdt_ref[...] = _dot(dt, ex_ref[...], precision=HIGHEST) * xc[:, :d_ssd]
    dec_ref[...] = _dot(dec, ex_ref[...], precision=HIGHEST)


def _ssd_prep(xbc, p0, p1, p2, conv_w, conv_b, dt_raw, dt_bias_p, a_log_p, expand):
    b, conv_dim = xbc.shape
    d_ssd = expand.shape[1]
    args = (xbc, p0, p1, p2, conv_w, conv_b, dt_raw, dt_bias_p, a_log_p, expand)
    return pl.pallas_call(
        _ssd_prep_kernel,
        grid=(1,),
        in_specs=[pl.BlockSpec(a.shape, lambda i: (0, 0)) for a in args],
        out_specs=[pl.BlockSpec((b, conv_dim), lambda i: (0, 0)),
                   pl.BlockSpec((b, d_ssd), lambda i: (0, 0)),
                   pl.BlockSpec((b, d_ssd), lambda i: (0, 0))],
        out_shape=[jax.ShapeDtypeStruct((b, conv_dim), F32), jax.ShapeDtypeStruct((b, d_ssd), F32),
                   jax.ShapeDtypeStruct((b, d_ssd), F32)],
        compiler_params=_cparams(("arbitrary",)),
    )(*args)


def _ssd_step_kernel(n_pairs, xdt_ref, dec_ref, bm_ref, cm_ref, s_ref, so_ref, y_ref):
    r2 = lax.broadcasted_iota(I32, (LANES, LANES), 0)
    c2 = lax.broadcasted_iota(I32, (LANES, LANES), 1)
    eye = r2 == c2
    ones = jnp.ones((LANES, LANES), F32)
    pairs_per_group = n_pairs // SSD_GROUPS
    rows_per_pair = LANES // SSD_HEAD_DIM
    for p in range(n_pairs):
        g = p // pairs_per_group
        sl = slice(p * LANES, (p + 1) * LANES)
        hs = slice(p * rows_per_pair, (p + 1) * rows_per_pair)
        hb = s_ref[0, 0, hs].reshape(LANES, D_STATE)
        xd = jnp.where(eye, jnp.broadcast_to(xdt_ref[0, :, sl], (LANES, LANES)), 0.0)
        dd = jnp.where(eye, jnp.broadcast_to(dec_ref[0, :, sl], (LANES, LANES)), 0.0)
        bmat = jnp.broadcast_to(bm_ref[0, :, g * D_STATE:(g + 1) * D_STATE], (LANES, D_STATE))
        upd = _dot(xd, bmat, precision=HIGHEST)
        dcol = _dot(dd, ones, precision=HIGHEST)
        hn = hb * dcol + upd
        so_ref[0, 0, hs] = hn.reshape(rows_per_pair, SSD_HEAD_DIM, D_STATE)
        cmat = jnp.broadcast_to(cm_ref[0, :, g * D_STATE:(g + 1) * D_STATE], (8, D_STATE))
        y_ref[0, :, sl] = _dot_nt(cmat, hn, precision=HIGHEST)[0:1, :]


def _ssd_step(xdt, dec, bm, cm, state):
    b, d_ssd = xdt.shape
    n_pairs = d_ssd // LANES
    heads = d_ssd // SSD_HEAD_DIM
    r3 = lambda a: a.reshape(b, 1, a.shape[1])
    row = lambda w: pl.BlockSpec((1, 1, w), lambda i: (i, 0, 0))
    sspec = pl.BlockSpec((1, 1, heads, SSD_HEAD_DIM, D_STATE), lambda i: (0, i, 0, 0, 0))
    so, y = pl.pallas_call(
        functools.partial(_ssd_step_kernel, n_pairs),
        grid=(b,),
        in_specs=[row(d_ssd), row(d_ssd), row(bm.shape[1]), row(cm.shape[1]), sspec],
        out_specs=[sspec, row(d_ssd)],
        out_shape=[jax.ShapeDtypeStruct(state.shape, F32), jax.ShapeDtypeStruct((b, 1, d_ssd), F32)],
        compiler_params=_cparams(("arbitrary",)),
    )(r3(xdt), r3(dec), r3(bm), r3(cm), state)
    return so, y.reshape(b, d_ssd)


def _ssd_finish_kernel(y_ref, xs_ref, z_ref, dsk_ref, nw_ref, o_ref):
    y = (y_ref[...] + xs_ref[...] * dsk_ref[...]) * _silu(z_ref[...])
    o_ref[...] = (_rms(y) * nw_ref[...]).astype(BF16)


def _ssd_finish(y, xs, z, dskip_row, norm_w):
    args = (y, xs, z, dskip_row, norm_w)
    return pl.pallas_call(
        _ssd_finish_kernel,
        grid=(1,),
        in_specs=[pl.BlockSpec(a.shape, lambda i: (0, 0)) for a in args],
        out_specs=pl.BlockSpec(y.shape, lambda i: (0, 0)),
        out_shape=jax.ShapeDtypeStruct(y.shape, BF16),
        compiler_params=_cparams(("arbitrary",)),
    )(*args)


PAGE_PACK = 8


def _page_copy(cache_ref, buf, sem, pt_ref, b, p, slot):
    rows = cache_ref.shape[2]
    return pltpu.make_async_copy(cache_ref.at[0, pt_ref[b, p]], buf.at[slot, pl.ds(p * rows, rows)], sem.at[slot])


def _score_sample_kernel(n_pages, pt_ref, q8_ref, w8_ref, qi_ref, wi_ref, kin_ref, cache_ref, s_ref, buf, sem):
    b = pl.program_id(0)
    nb = pl.num_programs(0)
    slot = b % 2

    def start(bb, sl):
        def body(p, carry):
            _page_copy(cache_ref, buf, sem, pt_ref, bb, p, sl).start()
            return carry
        lax.fori_loop(0, n_pages, body, 0)

    @pl.when(b == 0)
    def _():
        start(0, 0)

    @pl.when(b + 1 < nb)
    def _():
        start(b + 1, 1 - slot)

    def wait(p, carry):
        _page_copy(cache_ref, buf, sem, pt_ref, b, p, slot).wait()
        return carry

    lax.fori_loop(0, n_pages, wait, 0)

    wscale = (IDX_DIM ** -0.5) * (IDX_HEADS ** -0.5)
    q8 = q8_ref[0]
    w8 = w8_ref[0] * wscale
    kdim = q8.shape[1]
    page_rows = buf.shape[2]

    def group(gi, carry):
        keys_t = buf[slot, pl.ds(gi * kdim, kdim), :].astype(BF16)
        r = jnp.maximum(_dot(q8, keys_t), 0.0) * w8
        s_ref[0, pl.ds(gi * PAGE_PACK, PAGE_PACK), :] = jnp.sum(
            r.reshape(PAGE_PACK, IDX_HEADS, page_rows), axis=1)
        return carry

    lax.fori_loop(0, n_pages // PAGE_PACK, group, 0, unroll=2)
    tail = s_ref.shape[1] - n_pages
    kn = jnp.broadcast_to(kin_ref[0], (page_rows, kin_ref.shape[2])).astype(BF16)
    dn = _dot_nt(qi_ref[0], kn)
    sn = jnp.sum(jnp.maximum(dn, 0.0) * (wi_ref[0] * wscale), axis=0, keepdims=True)
    r = lax.broadcasted_iota(I32, (tail, page_rows), 0)
    c = lax.broadcasted_iota(I32, (tail, page_rows), 1)
    s_ref[0, n_pages:, :] = jnp.where((r == 0) & (c == 0), jnp.broadcast_to(sn, (tail, page_rows)), -jnp.inf)


def _score_sample(page_table, q8, w8, qi3, wi3, ki_new3, cache_kit, tail_rows):
    b, n_pages = page_table.shape
    idx_dim, page_rows = cache_kit.shape[2], cache_kit.shape[3]
    blk = lambda a: pl.BlockSpec((1,) + a.shape[1:], lambda i, pt: (i, 0, 0))
    gs = pltpu.PrefetchScalarGridSpec(
        num_scalar_prefetch=1,
        grid=(b,),
        in_specs=[blk(q8), blk(w8), blk(qi3), blk(wi3), blk(ki_new3), pl.BlockSpec(memory_space=pl.ANY)],
        out_specs=pl.BlockSpec((1, n_pages + tail_rows, page_rows), lambda i, pt: (i, 0, 0)),
        scratch_shapes=[pltpu.VMEM((2, n_pages * idx_dim, page_rows), F32),
                        pltpu.SemaphoreType.DMA((2,))],
    )
    return pl.pallas_call(
        functools.partial(_score_sample_kernel, n_pages),
        grid_spec=gs,
        out_shape=jax.ShapeDtypeStruct((b, n_pages + tail_rows, page_rows), F32),
        compiler_params=_cparams(("arbitrary",)),
    )(page_table, q8, w8, qi3, wi3, ki_new3, cache_kit)


def _threshold_sample_kernel(topk, s_ref, thr_ref, cut_ref):
    nb, n = s_ref.shape
    n_blk = n // LANES
    kf = jnp.float32(topk)

    def fold(fn, init):
        def body(j, acc):
            return fn(acc, s_ref[:, pl.ds(pl.multiple_of(j * LANES, LANES), LANES)], j * LANES)
        return lax.fori_loop(0, n_blk, body, jnp.full((nb, LANES), init, F32), unroll=8)

    def count(pred):
        return jnp.sum(fold(lambda acc, blk, c0: acc + jnp.where(pred(blk, c0), 1.0, 0.0), 0.0), axis=-1, keepdims=True)

    def count_ge(t):
        tb = jnp.broadcast_to(t, (nb, LANES))
        return count(lambda blk, c0: blk >= tb)

    fmin = jnp.float32(jnp.finfo(F32).min)
    lo0 = jnp.min(fold(lambda acc, blk, c0: jnp.minimum(acc, jnp.where(blk == -jnp.inf, 3e38, blk)), 3e38),
                  axis=-1, keepdims=True)
    hi0 = jnp.max(fold(lambda acc, blk, c0: jnp.maximum(acc, blk), -jnp.inf), axis=-1, keepdims=True)
    n_valid = count(lambda blk, c0: blk > -jnp.inf)
    done0 = jnp.where(n_valid <= kf, 1.0, 0.0)

    def cond(st):
        it, lo, hi, thr, done, stalled = st
        return (it < BISECT_CAP) & (jnp.min(done) == 0.0)

    def halve(st):
        it, lo, hi, thr, done, stalled = st
        mid = 0.5 * lo + 0.5 * hi
        n = count_ge(mid)
        live = done == 0.0
        exact = live & (n == kf)
        stall = live & jnp.logical_not(exact) & ((mid <= lo) | (mid >= hi))
        move = live & jnp.logical_not(exact) & jnp.logical_not(stall)
        up = n >= kf
        return (it + 1,
                jnp.where(move & up, mid, lo),
                jnp.where(move & jnp.logical_not(up), mid, hi),
                jnp.where(exact, mid, thr),
                jnp.where(exact | stall, 1.0, done),
                jnp.where(stall, 1.0, stalled))

    st = lax.while_loop(cond, halve, (jnp.int32(0), lo0, hi0, jnp.minimum(lo0, fmin), done0, jnp.zeros((nb, 1), F32)))
    _, lo, hi, thr, _, stalled = st
    thr = jnp.where(stalled == 1.0, jnp.where(count_ge(hi) >= kf, hi, lo), thr)
    n_ge = count_ge(thr)
    thr_ref[...] = jnp.broadcast_to(thr, (nb, LANES))
    n_bits = max(int(n).bit_length(), 1)
    tb = jnp.broadcast_to(thr, (nb, LANES))
    lane = lax.broadcasted_iota(I32, (nb, LANES), 1)

    def tie_cut():
        need = kf - count(lambda blk, c0: blk > tb)

        def idx_step(t, jlo):
            trial = jlo + jnp.left_shift(jnp.int32(1), n_bits - 1 - t).astype(F32)
            trb = jnp.broadcast_to(trial, (nb, LANES))
            f = count(lambda blk, c0: (blk == tb) & ((c0 + lane).astype(F32) < trb))
            return jnp.where(f <= need - 1.0, trial, jlo)

        jlo = lax.fori_loop(0, n_bits, idx_step, jnp.zeros((nb, 1), F32))
        return jnp.where(n_ge > kf, jlo + 1.0, jnp.float32(2 ** 30))

    cut = lax.cond(jnp.max(n_ge) > kf, tie_cut, lambda: jnp.full((nb, 1), 2 ** 30, F32))
    cut_ref[...] = jnp.broadcast_to(cut, (nb, LANES))


def _threshold_sample(s2, topk):
    nb, n = s2.shape
    return pl.pallas_call(
        functools.partial(_threshold_sample_kernel, topk),
        grid=(1,),
        in_specs=[pl.BlockSpec((nb, n), lambda i: (0, 0))],
        out_specs=[pl.BlockSpec((nb, LANES), lambda i: (0, 0))] * 2,
        out_shape=[jax.ShapeDtypeStruct((nb, LANES), F32)] * 2,
        compiler_params=_cparams(("arbitrary",)),
    )(s2)


def _select_sample_kernel(topk, s_ref, thr_ref, cut_ref, idx_ref):
    s = s_ref[0]
    n_rows, width = s.shape
    pos = (lax.broadcasted_iota(I32, s.shape, 0) * width + lax.broadcasted_iota(I32, s.shape, 1)).astype(F32)
    thr = thr_ref[0]
    sel = (s > thr) | ((s == thr) & (pos < cut_ref[0]))
    self = jnp.where(sel, 1.0, 0.0).astype(BF16)
    ra = lax.broadcasted_iota(I32, (width, width), 0)
    ca = lax.broadcasted_iota(I32, (width, width), 1)
    local = jnp.where(sel, _dot(self, jnp.where(ra < ca, 1.0, 0.0).astype(BF16)), -1.0)
    cnt_row = _dot_nt(jnp.ones((8, width), BF16), self)
    rb = lax.broadcasted_iota(I32, (n_rows, n_rows), 0)
    cb = lax.broadcasted_iota(I32, (n_rows, n_rows), 1)
    end_row = _dot(cnt_row.astype(BF16), jnp.where(rb <= cb, 1.0, 0.0).astype(BF16))
    rank = lax.broadcasted_iota(I32, (topk, n_rows), 0).astype(F32)
    row_id = lax.broadcasted_iota(I32, (topk, n_rows), 1).astype(F32)
    passed = jnp.broadcast_to(end_row[0:1, :], (topk, n_rows)) <= rank
    row_of = jnp.sum(jnp.where(passed, 1.0, 0.0), axis=-1, keepdims=True)
    start = jnp.sum(jnp.where(passed, jnp.broadcast_to(cnt_row[0:1, :], (topk, n_rows)), 0.0), axis=-1,
                    keepdims=True)
    picked = _dot(jnp.where(row_id == row_of, 1.0, 0.0).astype(BF16), local.astype(BF16))
    lane = lax.broadcasted_iota(I32, (topk, width), 1).astype(F32)
    lane_of = jnp.sum(jnp.where(picked == rank[:, 0:1] - start, lane, 0.0), axis=-1, keepdims=True)
    idx_ref[0] = (row_of * width + lane_of).astype(I32)


def _select_sample(s3, thr, cut, topk):
    b, n_rows, width = s3.shape
    per_b = lambda a: pl.BlockSpec((1, 1, LANES), lambda i: (i, 0, 0))
    return pl.pallas_call(
        functools.partial(_select_sample_kernel, topk),
        grid=(b,),
        in_specs=[pl.BlockSpec((1, n_rows, width), lambda i: (i, 0, 0)), per_b(thr), per_b(cut)],
        out_specs=pl.BlockSpec((1, topk, 1), lambda i: (i, 0, 0)),
        out_shape=jax.ShapeDtypeStruct((b, topk, 1), I32),
        compiler_params=_cparams(("arbitrary",)),
    )(s3, thr.reshape(b, 1, LANES), cut.reshape(b, 1, LANES))


def _row_copy(src, dst, sem, src_row, dst_row):
    return pltpu.make_async_copy(src.at[pl.ds(src_row, KV_HEADS)], dst.at[pl.ds(dst_row, KV_HEADS)], sem)


def _attn_sample_kernel(topk, past_len, page_rows, n_pages, idx_ref, pt_ref, q_ref, nw_ref, ck_ref, cv_ref,
                        kn_ref, vn_ref, o_ref, kbuf, vbuf, sem):
    b = pl.program_id(0)

    pow2 = page_rows & (page_rows - 1) == 0

    def start(r, carry):
        j = jnp.minimum(idx_ref[b, r], past_len - 1)
        if pow2:
            page, off = jnp.right_shift(j, page_rows.bit_length() - 1), j & (page_rows - 1)
        else:
            page, off = j // page_rows, j % page_rows
        row = (pt_ref[b, page] * page_rows + off) * KV_HEADS
        _row_copy(ck_ref, kbuf, sem.at[0], row, r * KV_HEADS).start()
        _row_copy(cv_ref, vbuf, sem.at[1], row, r * KV_HEADS).start()
        return carry

    lax.fori_loop(0, topk, start, 0, unroll=8)

    def wait(r, carry):
        _row_copy(ck_ref, kbuf, sem.at[0], 0, r * KV_HEADS).wait()
        _row_copy(cv_ref, vbuf, sem.at[1], 0, r * KV_HEADS).wait()
        return carry

    lax.fori_loop(0, topk, wait, 0, unroll=8)

    @pl.when(idx_ref[b, topk - 1] >= past_len)
    def _():
        last = (topk - 1) * KV_HEADS
        for src_ref, buf, s in ((kn_ref, kbuf, sem.at[0]), (vn_ref, vbuf, sem.at[1])):
            cp = _row_copy(src_ref, buf, s, b * KV_HEADS, last)
            cp.start()
            cp.wait()

    outs = []
    ss = jnp.zeros((1, 1), F32)
    for g in range(KV_HEADS):
        kg = kbuf[pl.ds(g, topk, stride=KV_HEADS), :].astype(BF16)
        vg = vbuf[pl.ds(g, topk, stride=KV_HEADS), :].astype(BF16)
        lg = _dot_nt(q_ref[0, g], kg)
        m = jnp.max(lg, axis=-1, keepdims=True)
        p = jnp.exp2(lg - m)
        p = p / jnp.sum(p, axis=-1, keepdims=True)
        o = _dot(p.astype(BF16), vg)
        rows = lax.broadcasted_iota(I32, o.shape, 0)
        o = jnp.where(rows < q_ref.shape[2] // 2, o, 0.0)
        outs.append(o)
        ss = ss + jnp.sum(jnp.sum(o * o, axis=-1, keepdims=True), axis=0, keepdims=True)
    n_feat = KV_HEADS * (q_ref.shape[2] // 2) * ATT_HEAD_DIM
    inv = lax.rsqrt(ss * (1.0 / n_feat) + EPS)
    for g in range(KV_HEADS):
        o_ref[0, g] = (outs[g] * inv * nw_ref[g]).astype(BF16)


def _attn_sample(idx, page_table, q4, nw3, ck2, cv2, kn2, vn2, past_len, page_rows):
    b, topk = idx.shape
    n_pages = page_table.shape[1]
    gs = pltpu.PrefetchScalarGridSpec(
        num_scalar_prefetch=2,
        grid=(b,),
        in_specs=[pl.BlockSpec((1,) + q4.shape[1:], lambda i, a, c: (i, 0, 0, 0)),
                  pl.BlockSpec(nw3.shape, lambda i, a, c: (0, 0, 0)),
                  pl.BlockSpec(memory_space=pl.ANY), pl.BlockSpec(memory_space=pl.ANY),
                  pl.BlockSpec(memory_space=pl.ANY), pl.BlockSpec(memory_space=pl.ANY)],
        out_specs=pl.BlockSpec((1,) + q4.shape[1:], lambda i, a, c: (i, 0, 0, 0)),
        scratch_shapes=[pltpu.VMEM((topk * KV_HEADS, ATT_HEAD_DIM), F32),
                        pltpu.VMEM((topk * KV_HEADS, ATT_HEAD_DIM), F32),
                        pltpu.SemaphoreType.DMA((2,))],
    )
    return pl.pallas_call(
        functools.partial(_attn_sample_kernel, topk, past_len, page_rows, n_pages),
        grid_spec=gs,
        out_shape=jax.ShapeDtypeStruct(q4.shape, BF16),
        compiler_params=_cparams(("arbitrary",)),
    )(idx, page_table, q4, nw3, ck2, cv2, kn2, vn2)


def _row(v, width=None):
    v = v.reshape(1, -1)
    return v if width is None else _pad_cols(v, width)


def _layer_params(p):
    d = p["w_in"].shape[0]
    wr = jnp.concatenate([p["w_router_e"], p["w_router_g"]], axis=1)
    br = jnp.concatenate([p["b_router_e"], p["b_router_g"]])
    return dict(
        w_perm=_perm_w_in(p["w_in"]),
        w_out_b=p["w_out"].astype(BF16),
        wr=_pad_cols(wr, LANES).astype(BF16),
        br=_row(br, LANES),
        nw1=_row(p["norm1_w"]), nw2=_row(p["norm2_w"]),
        lnw=_row(p["ln_kidx_w"], LANES), lnb=_row(p["ln_kidx_b"], LANES),
        dt_bias=_row(p["dt_bias"], LANES), a_log=_row(p["a_log"], LANES),
        dskip=_row(jnp.repeat(p["d_skip"], SSD_HEAD_DIM)),
        norm_ssd=_row(p["norm_ssd_w"]), norm_att=_row(p["norm_att_w"]),
        conv_w=p["conv_w"], conv_b=_row(p["conv_b"]),
        d_ssd=d // 2,
    )


def _row_tile(t):
    return 512 if t % 512 == 0 else 256


def _route(x, ya, yb, mod, lp, tm):
    return _out_proj(ya, yb, lp["w_out_b"], x, mod[2], lp["nw2"], mod[4], mod[3], lp["wr"], lp["br"], tm)


def _moe_and_norm(routed_p, routed_s, g2_p, g2_s, p, nf):
    x1p, h2p, eidp, wtsp, cntp = routed_p
    x1s, h2s, eids, wtss, cnts = routed_s
    tp, ts = x1p.shape[0], x1s.shape[0]
    tt = tp + ts
    tpos = 256
    tt_pad = -(-tt // tpos) * tpos
    eid_all = jnp.concatenate([eidp, eids, jnp.full((tt_pad - tt, LANES), -1, I32)])
    n_tiles = -(-(2 * tt + N_EXPERTS * (MOE_TILE - 1)) // MOE_TILE)
    off_row, plan = _moe_plan(cntp, cnts, n_tiles)
    pos_flat = _moe_positions(eid_all, off_row, tpos)[:, :2].reshape(-1)
    ys = _moe_grouped(plan, pos_flat, h2p, h2s, p["w_exp_up"], p["w_exp_down"])
    out_p = _combine(pos_flat, x1p, wtsp, g2_p, nf, ys, tok0=0, tm=256)
    out_s = _combine(pos_flat, x1s, wtss, g2_s, nf, ys, tok0=tp, tm=ts)
    return out_p, out_s


def _prompt_layer(x, mod, lp, p):
    t, d = x.shape
    pr = _in_proj(x, lp["nw1"], mod[1], mod[0], lp["w_perm"], lp["lnw"], lp["lnb"], tm=_row_tile(t))
    y_ssd, st = _ssd_prompt(pr["xbc"], pr["dt"], pr["z"], lp["conv_w"], lp["conv_b"], lp["dt_bias"], lp["a_log"],
                            lp["dskip"], lp["norm_ssd"])
    ki = pr["ki"]
    zeros = jnp.zeros_like(ki)
    ki2 = jnp.stack([jnp.concatenate([ki, zeros], axis=1), jnp.concatenate([zeros, ki], axis=1)]).astype(BF16)
    topk = min(TOPK_MAX, t // 4)
    v3 = pr["vb"].reshape(t, KV_HEADS, ATT_HEAD_DIM)
    vx = jnp.concatenate([v3, jnp.ones_like(v3)], axis=-1).reshape(t, 2 * KV_HEADS * ATT_HEAD_DIM)
    y_att = _attn_prompt(pr["q"], pr["qi"], pr["wi"], ki2, pr["kb"], vx, lp["norm_att"], topk, tq=256)
    routed = _route(x, y_ssd, y_att, mod, lp, tm=_row_tile(t))
    conv_new = jnp.concatenate([jnp.zeros((CONV_W - 1, pr["xbc"].shape[1]), F32), pr["xbc"]])[-(CONV_W - 1):]
    return routed, (pr["k"], pr["v"], ki, conv_new, st)


def _sample_layer(x, mod, lp, p, cache_k, cache_v, cache_ki, conv_prev, ssm_prev, page_table):
    b, d = x.shape
    d_ssd = lp["d_ssd"]
    heads = d_ssd // SSD_HEAD_DIM
    gn = SSD_GROUPS * D_STATE
    pr = _in_proj(x, lp["nw1"], mod[1], mod[0], lp["w_perm"], lp["lnw"], lp["lnb"], tm=b)
    expand = (jnp.arange(LANES)[:, None] == (jnp.arange(d_ssd)[None, :] // SSD_HEAD_DIM)).astype(F32)
    xc, xdt, dec = _ssd_prep(pr["xbc"], conv_prev[:, 0], conv_prev[:, 1], conv_prev[:, 2], lp["conv_w"], lp["conv_b"],
                             pr["dt"], lp["dt_bias"], lp["a_log"], expand)
    xs, bm, cm = xc[:, :d_ssd], xc[:, d_ssd:d_ssd + gn], xc[:, d_ssd + gn:]
    st5 = ssm_prev.reshape((1, b, heads, SSD_HEAD_DIM, D_STATE))
    st_new, y = _ssd_step(xdt, dec, bm, cm, st5)
    y_ssd = _ssd_finish(y, xs, pr["z"], lp["dskip"], lp["norm_ssd"])
    conv_new = jnp.concatenate([conv_prev[:, 1:], pr["xbc"][:, None, :]], axis=1)
    n_pool, page_rows = cache_k.shape[0], cache_k.shape[1]
    n_pages = page_table.shape[1]
    past_len = n_pages * page_rows
    topk = min(TOPK_MAX, (past_len + 1) // 4)
    qi3 = pr["qi"].reshape(b, IDX_HEADS, IDX_DIM)
    wi3 = pr["wi"][:, :IDX_HEADS].reshape(b, IDX_HEADS, 1)
    eye = jnp.eye(PAGE_PACK, dtype=BF16)
    q8 = (eye[None, :, None, :, None] * qi3[:, None, :, None, :]).reshape(b, PAGE_PACK * IDX_HEADS,
                                                                         PAGE_PACK * IDX_DIM)
    w8 = jnp.tile(wi3, (1, PAGE_PACK, 1))
    tail_rows = -(n_pages + 1) % LANES + 1
    cache_kit = jnp.swapaxes(cache_ki, -1, -2)[None]
    s3 = _score_sample(page_table, q8, w8, qi3, wi3, pr["ki"].reshape(b, 1, IDX_DIM), cache_kit, tail_rows)
    thr, cut = _threshold_sample(s3.reshape(b, -1), topk)
    idx = _select_sample(s3, thr, cut, topk).reshape(b, topk)
    n_heads = pr["q"].shape[1] // ATT_HEAD_DIM
    q_per_kv = n_heads // KV_HEADS
    q4 = jnp.pad(pr["q"].reshape(b, KV_HEADS, q_per_kv, ATT_HEAD_DIM), ((0, 0), (0, 0), (0, q_per_kv), (0, 0)))
    nw3 = jnp.pad(lp["norm_att"].reshape(KV_HEADS, q_per_kv, ATT_HEAD_DIM), ((0, 0), (0, q_per_kv), (0, 0)))
    ck2 = cache_k.reshape(n_pool * page_rows * KV_HEADS, ATT_HEAD_DIM)
    cv2 = cache_v.reshape(n_pool * page_rows * KV_HEADS, ATT_HEAD_DIM)
    kn2 = pr["k"].reshape(b * KV_HEADS, ATT_HEAD_DIM)
    vn2 = pr["v"].reshape(b * KV_HEADS, ATT_HEAD_DIM)
    o4 = _attn_sample(idx, page_table, q4, nw3, ck2, cv2, kn2, vn2, past_len, page_rows)
    y_att = o4[:, :, :q_per_kv].reshape(b, n_heads * ATT_HEAD_DIM)
    routed = _route(x, y_ssd, y_att, mod, lp, tm=b)
    return routed, (pr["k"], pr["v"], pr["ki"], conv_new, st_new.reshape(ssm_prev.shape))


def kernel(x_prompt, x_sample, cache_k, cache_v, cache_k_idx, state_conv, state_ssm, page_table, c_prompt, c_sample, w_ada, b_ada, norm1_w, norm2_w, w_in, conv_w, conv_b, dt_bias, a_log, d_skip, norm_ssd_w, ln_kidx_w, ln_kidx_b, norm_att_w, w_out, w_router_g, b_router_g, w_router_e, b_router_e, w_exp_up, w_exp_down, norm_f_w):
    batch, seq, d = x_prompt.shape
    dec_batch, dec_seq, _ = x_sample.shape
    assert batch == 1 and dec_seq == 1, "one prompt sequence and one new token per sample sequence"
    depth = w_ada.shape[0]
    heads = (d // 2) // SSD_HEAD_DIM
    xp = x_prompt.reshape(seq, d)
    xs = x_sample.reshape(dec_batch, d)
    n_c = batch + dec_batch
    c_all = jnp.pad(jnp.concatenate([c_prompt, c_sample]), ((0, -n_c % 8), (0, 0)))
    nf = _row(norm_f_w)
    outs_p, outs_s = [], []
    yp = ys = None
    for l in range(depth):
        p = dict(w_in=w_in[l], conv_w=conv_w[l], conv_b=conv_b[l], dt_bias=dt_bias[l], a_log=a_log[l],
                 d_skip=d_skip[l], norm_ssd_w=norm_ssd_w[l], ln_kidx_w=ln_kidx_w[l], ln_kidx_b=ln_kidx_b[l],
                 norm_att_w=norm_att_w[l], w_out=w_out[l], w_router_g=w_router_g[l], b_router_g=b_router_g[l],
                 w_router_e=w_router_e[l], b_router_e=b_router_e[l], w_exp_up=w_exp_up[l],
                 w_exp_down=w_exp_down[l], norm1_w=norm1_w[l], norm2_w=norm2_w[l])
        lp = _layer_params(p)
        mod = _ada_mod(c_all, w_ada[l], b_ada[l])
        mod_p = [mod[0:1, k * d:(k + 1) * d] for k in range(6)]
        mod_s = [mod[batch:n_c, k * d:(k + 1) * d] for k in range(6)]
        routed_p, st_p = _prompt_layer(xp, mod_p, lp, p)
        routed_s, st_s = _sample_layer(xs, mod_s, lp, p, cache_k[l], cache_v[l], cache_k_idx[l], state_conv[l],
                                       state_ssm[l], page_table)
        (xp, yp), (xs, ys) = _moe_and_norm(routed_p, routed_s, mod_p[5], mod_s[5], p, nf)
        outs_p.append(st_p)
        outs_s.append(st_s)

    def stack(outs, n_rows, lead):
        k = jnp.stack([o[0].reshape(lead + (n_rows, KV_HEADS, ATT_HEAD_DIM)) for o in outs])
        v = jnp.stack([o[1].reshape(lead + (n_rows, KV_HEADS, ATT_HEAD_DIM)) for o in outs])
        ki = jnp.stack([o[2].reshape(lead + (n_rows, IDX_DIM)) for o in outs])
        return k, v, ki

    k_p, v_p, ki_p = stack(outs_p, seq, (batch,))
    conv_p = jnp.stack([o[3][None] for o in outs_p])
    ssm_p = jnp.stack([o[4].reshape(batch, heads, SSD_HEAD_DIM, D_STATE) for o in outs_p])
    k_s = jnp.stack([o[0].reshape(dec_batch, dec_seq, KV_HEADS, ATT_HEAD_DIM) for o in outs_s])
    v_s = jnp.stack([o[1].reshape(dec_batch, dec_seq, KV_HEADS, ATT_HEAD_DIM) for o in outs_s])
    ki_s = jnp.stack([o[2].reshape(dec_batch, dec_seq, IDX_DIM) for o in outs_s])
    conv_s = jnp.stack([o[3] for o in outs_s])
    ssm_s = jnp.stack([o[4] for o in outs_s])
    return (yp.reshape(batch, seq, d), ys.reshape(dec_batch, dec_seq, d), k_p, v_p, ki_p, conv_p, ssm_p,
            k_s, v_s, ki_s, conv_s, ssm_s)
```

```python
import functools

import jax
import jax.numpy as jnp
from jax import lax
from jax.experimental import pallas as pl
from jax.experimental.pallas import tpu as pltpu

F32 = jnp.float32
BF16 = jnp.bfloat16
I32 = jnp.int32

SSD_HEAD_DIM = 64
SSD_GROUPS = 2
D_STATE = 128
CONV_W = 4
SSD_CHUNK = 128
ATT_HEAD_DIM = 128
KV_HEADS = 2
IDX_HEADS = 16
IDX_DIM = 64
TOPK_MAX = 256
N_EGROUPS = 4
EXPERTS_PER_GROUP = 8
N_EXPERTS = N_EGROUPS * EXPERTS_PER_GROUP
EPS = 1e-6

LANES = 128
NEG_BIG = -1e30
VMEM_LIMIT = 56 * 1024 * 1024
HIGHEST = lax.Precision.HIGHEST
ATTN_TILE = 256
TOKEN_TILE = 256
Q_SCALE = ATT_HEAD_DIM ** -0.5 * 1.4426950408889634


def _cparams(sem):
    return pltpu.CompilerParams(dimension_semantics=sem, vmem_limit_bytes=VMEM_LIMIT)


def _dot(a, b, precision=None):
    return jnp.dot(a, b, preferred_element_type=F32, precision=precision)


def _dot_nt(a, b, precision=None):
    return lax.dot_general(a, b, (((1,), (1,)), ((), ())), preferred_element_type=F32, precision=precision)


def _silu(x):
    return x * jax.nn.sigmoid(x)


def _softplus(x):
    return jnp.maximum(x, 0.0) + jnp.log(1.0 + jnp.exp(-jnp.abs(x)))


def _rms(x):
    return x * lax.rsqrt(jnp.mean(x * x, axis=-1, keepdims=True) + EPS)


def _pad_cols(a, width):
    return jnp.pad(a, ((0, 0), (0, width - a.shape[1])))


def _ada_kernel(c_ref, w_ref, b_ref, o_ref):
    s = _silu(c_ref[...]).astype(BF16)
    o_ref[...] = _dot(s, w_ref[...].astype(BF16)) + b_ref[...]


def _ada_mod(c_all, w_ada, b_ada):
    r, d = c_all.shape
    n = w_ada.shape[1]
    tn = 1024
    return pl.pallas_call(
        _ada_kernel,
        grid=(n // tn,),
        in_specs=[pl.BlockSpec((r, d), lambda j: (0, 0)),
                  pl.BlockSpec((d, tn), lambda j: (0, j)),
                  pl.BlockSpec((1, tn), lambda j: (0, j))],
        out_specs=pl.BlockSpec((r, tn), lambda j: (0, j)),
        out_shape=jax.ShapeDtypeStruct((r, n), F32),
        compiler_params=_cparams(("arbitrary",)),
    )(c_all, w_ada, b_ada.reshape(1, n))


def _in_layout(d_model):
    d_ssd = d_model // 2
    d_att = d_model - d_ssd
    conv_dim = d_ssd + 2 * SSD_GROUPS * D_STATE
    ssd_heads = d_ssd // SSD_HEAD_DIM
    sizes = dict(z=d_ssd, xbc=conv_dim, dt=ssd_heads, q=d_att, k=KV_HEADS * ATT_HEAD_DIM,
                 v=KV_HEADS * ATT_HEAD_DIM, qi=IDX_HEADS * IDX_DIM, ki=IDX_DIM, wi=IDX_HEADS)
    order = ("z", "xbc", "dt", "q", "k", "v", "qi", "ki", "wi")
    src, dst, off_s, off_d = {}, {}, 0, 0
    for name in order:
        w = sizes[name]
        wp = -(-w // LANES) * LANES
        src[name] = (off_s, w)
        dst[name] = (off_d, wp)
        off_s += w
        off_d += wp
    return order, src, dst, off_d


def _perm_w_in(w_in):
    order, src, dst, _ = _in_layout(w_in.shape[0])
    parts = [_pad_cols(w_in[:, src[n][0]:src[n][0] + src[n][1]], dst[n][1]) for n in order]
    return jnp.concatenate(parts, axis=1).astype(BF16)


def _inproj_kernel(seg, x_ref, nw_ref, sc_ref, sh_ref, w_ref, lnw_ref, lnb_ref,
                   z_ref, xbc_ref, dt_ref, q_ref, k_ref, v_ref, kb_ref, vb_ref, qi_ref, ki_ref, wi_ref):
    h = _rms(x_ref[...]) * nw_ref[...]
    h = h * (1.0 + sc_ref[...]) + sh_ref[...]
    hb = h.astype(BF16)

    def mm(name):
        a, w = seg[name]
        return _dot(hb, w_ref[:, a:a + w])

    z_ref[...] = mm("z")
    xbc_ref[...] = mm("xbc")
    dt_ref[...] = mm("dt")
    q_ref[...] = (mm("q") * Q_SCALE).astype(BF16)
    k = mm("k")
    k_ref[...] = k
    kb_ref[...] = k.astype(BF16)
    v = mm("v")
    v_ref[...] = v
    vb_ref[...] = v.astype(BF16)
    qi_ref[...] = mm("qi").astype(BF16)
    wi_ref[...] = mm("wi")
    ki = mm("ki")
    lane = lax.broadcasted_iota(I32, ki.shape, 1)
    ok = lane < IDX_DIM
    mu = jnp.sum(jnp.where(ok, ki, 0.0), axis=-1, keepdims=True) * (1.0 / IDX_DIM)
    cen = jnp.where(ok, ki - mu, 0.0)
    var = jnp.sum(cen * cen, axis=-1, keepdims=True) * (1.0 / IDX_DIM)
    y = cen * lax.rsqrt(var + EPS) * lnw_ref[...] + lnb_ref[...]
    ki_ref[...] = y[:, :IDX_DIM]


def _in_proj(x, nw, sc, sh, w_perm, lnw, lnb, tm):
    t, d = x.shape
    _, _, dst, npad = _in_layout(d)
    tmod = sc.shape[0]
    mod_map = (lambda i: (0, 0)) if tmod == 1 else (lambda i: (i, 0))
    mod_rows = 1 if tmod == 1 else tm
    row = lambda w: pl.BlockSpec((tm, w), lambda i: (i, 0))
    d_ssd, d_att = dst["z"][1], dst["q"][1]
    kvw = KV_HEADS * ATT_HEAD_DIM
    outs = [("z", d_ssd, F32), ("xbc", dst["xbc"][1], F32), ("dt", LANES, F32), ("q", d_att, BF16),
            ("k", kvw, F32), ("v", kvw, F32), ("kb", kvw, BF16), ("vb", kvw, BF16),
            ("qi", IDX_HEADS * IDX_DIM, BF16), ("ki", IDX_DIM, F32), ("wi", LANES, F32)]
    res = pl.pallas_call(
        functools.partial(_inproj_kernel, dst),
        grid=(t // tm,),
        in_specs=[row(d),
                  pl.BlockSpec((1, d), lambda i: (0, 0)),
                  pl.BlockSpec((mod_rows, d), mod_map),
                  pl.BlockSpec((mod_rows, d), mod_map),
                  pl.BlockSpec((d, npad), lambda i: (0, 0)),
                  pl.BlockSpec((1, LANES), lambda i: (0, 0)),
                  pl.BlockSpec((1, LANES), lambda i: (0, 0))],
        out_specs=[row(w) for _, w, _ in outs],
        out_shape=[jax.ShapeDtypeStruct((t, w), dt) for _, w, dt in outs],
        compiler_params=_cparams(("arbitrary",)),
    )(x, nw, sc, sh, w_perm, lnw, lnb)
    return dict(zip([n for n, _, _ in outs], res))


def _ssd_prompt_kernel(n_pairs, xbc_ref, dt_ref, z_ref, cw_ref, cb_ref, dtb_ref, alog_ref, dsk_ref, nw_ref,
                       y_ref, st_ref, xprev, ht, ybuf):
    c = pl.program_id(0)
    q = SSD_CHUNK
    d_ssd = n_pairs * LANES
    gn = SSD_GROUPS * D_STATE

    @pl.when(c == 0)
    def _():
        xprev[...] = jnp.zeros_like(xprev)
        ht[...] = jnp.zeros_like(ht)

    x = xbc_ref[...]
    xp = xprev[...]
    rowi = lax.broadcasted_iota(I32, (q, 1), 0)
    acc = cb_ref[...] + cw_ref[CONV_W - 1:CONV_W, :] * x
    for k in range(1, CONV_W):
        sh = jnp.where(rowi < k, pltpu.roll(xp, k, 0), pltpu.roll(x, k, 0))
        acc = acc + cw_ref[CONV_W - 1 - k:CONV_W - k, :] * sh
    xprev[...] = x
    xc = _silu(acc)

    dt = _softplus(dt_ref[...] + dtb_ref[...])
    a_neg = -jnp.exp(alog_ref[...])
    r2 = lax.broadcasted_iota(I32, (q, q), 0)
    c2 = lax.broadcasted_iota(I32, (q, q), 1)
    tril = c2 <= r2
    a = _dot(tril.astype(F32), dt * a_neg, precision=HIGHEST)
    a_t = a.T
    dt_t = dt.T
    a_last = a[q - 1:q, :]
    wmat = jnp.exp(a_last - a) * dt
    emat = jnp.exp(a)
    cd = jnp.exp(a_last)
    lane = lax.broadcasted_iota(I32, (q, LANES), 1)
    left = lane < SSD_HEAD_DIM
    pairs_per_group = n_pairs // SSD_GROUPS

    bts, cbs, cgs = [], [], []
    for g in range(SSD_GROUPS):
        bg = xc[:, d_ssd + g * D_STATE:d_ssd + (g + 1) * D_STATE]
        cg = xc[:, d_ssd + gn + g * D_STATE:d_ssd + gn + (g + 1) * D_STATE].astype(BF16)
        bt = bg.T.astype(BF16)
        bts.append(bt)
        cgs.append(cg)
        cbs.append(_dot(cg, bt))

    def colb(m, h):
        return jnp.broadcast_to(m[:, h:h + 1], (q, LANES))

    for p in range(n_pairs):
        g = p // pairs_per_group
        h0, h1 = 2 * p, 2 * p + 1
        xpair = xc[:, p * LANES:(p + 1) * LANES]
        xpb = xpair.astype(BF16)
        yd = []
        for h in (h0, h1):
            diff = colb(a, h) - a_t[h:h + 1, :]
            decay = jnp.exp(jnp.where(tril, diff, -jnp.inf))
            sc = cbs[g] * decay * dt_t[h:h + 1, :]
            yd.append(_dot(sc.astype(BF16), xpb))
        y_diag = jnp.where(left, yd[0], yd[1])
        w_pair = jnp.where(left, colb(wmat, h0), colb(wmat, h1))
        e_pair = jnp.where(left, colb(emat, h0), colb(emat, h1))
        cd_pair = jnp.where(left[0:1, :], jnp.broadcast_to(cd[:, h0:h0 + 1], (1, LANES)),
                            jnp.broadcast_to(cd[:, h1:h1 + 1], (1, LANES)))
        hprev = ht[p]
        y_off = _dot(cgs[g], hprev.astype(BF16)) * e_pair
        states = _dot(bts[g], (xpair * w_pair).astype(BF16))
        ht[p] = hprev * cd_pair + states
        ybuf[:, p * LANES:(p + 1) * LANES] = y_diag + y_off + xpair * dsk_ref[:, p * LANES:(p + 1) * LANES]

    y = ybuf[...] * _silu(z_ref[...])
    y_ref[...] = (_rms(y) * nw_ref[...]).astype(BF16)

    @pl.when(c == pl.num_programs(0) - 1)
    def _():
        for p in range(n_pairs):
            st_ref[p * LANES:(p + 1) * LANES, :] = ht[p].T


def _ssd_prompt(xbc, dt_raw, z, conv_w, conv_b, dt_bias_p, a_log_p, dskip_row, norm_w):
    t, conv_dim = xbc.shape
    d_ssd = z.shape[1]
    n_pairs = d_ssd // LANES
    q = SSD_CHUNK
    full = lambda a: pl.BlockSpec(a.shape, lambda c: (0, 0))
    return pl.pallas_call(
        functools.partial(_ssd_prompt_kernel, n_pairs),
        grid=(t // q,),
        in_specs=[pl.BlockSpec((q, conv_dim), lambda c: (c, 0)),
                  pl.BlockSpec((q, LANES), lambda c: (c, 0)),
                  pl.BlockSpec((q, d_ssd), lambda c: (c, 0)),
                  full(conv_w), full(conv_b), full(dt_bias_p), full(a_log_p), full(dskip_row), full(norm_w)],
        out_specs=[pl.BlockSpec((q, d_ssd), lambda c: (c, 0)),
                   pl.BlockSpec((d_ssd, D_STATE), lambda c: (0, 0))],
        out_shape=[jax.ShapeDtypeStruct((t, d_ssd), BF16),
                   jax.ShapeDtypeStruct((d_ssd, D_STATE), F32)],
        scratch_shapes=[pltpu.VMEM((q, conv_dim), F32),
                        pltpu.VMEM((n_pairs, D_STATE, LANES), F32),
                        pltpu.VMEM((q, d_ssd), F32)],
        compiler_params=_cparams(("arbitrary",)),
    )(xbc, dt_raw, z, conv_w, conv_b, dt_bias_p, a_log_p, dskip_row, norm_w)


ROW_SUB = 128
FOLD_CHUNKS = 4
BISECT_CAP = 320
SAFE_SHIFT = 40.0


def _attn_prompt_kernel(topk, tq, q_ref, qi_ref, wi_ref, ki2t_ref, kbt_ref, vx_ref, nw_ref, o_ref,
                        sc, wb, thr_b, mrun, acc_scr, kmax):
    i = pl.program_id(0)
    kc = tq
    n_chunks = i + 1
    n_heads = q_ref.shape[1] // ATT_HEAD_DIM
    q_per_kv = n_heads // KV_HEADS
    wscale = (IDX_DIM ** -0.5) * (IDX_HEADS ** -0.5)
    n_sub = tq // ROW_SUB

    @pl.when(i == 0)
    def _():
        def norms(j, best):
            kf32 = kbt_ref[:, pl.ds(j * kc, kc)].astype(F32)
            return tuple(jnp.maximum(best[g], jnp.max(jnp.sum(
                jnp.square(kf32[g * ATT_HEAD_DIM:(g + 1) * ATT_HEAD_DIM]), axis=0, keepdims=True)))
                for g in range(KV_HEADS))

        best = lax.fori_loop(0, kbt_ref.shape[1] // kc, norms, (jnp.float32(0.0),) * KV_HEADS)
        for g in range(KV_HEADS):
            kmax[g] = best[g]

    wi = wi_ref[...] * wscale
    for h in range(IDX_HEADS):
        wb[h] = jnp.broadcast_to(wi[:, h:h + 1], (tq, LANES))

    def wide(x):
        return jnp.concatenate([x] * (kc // LANES), axis=1)

    row_g = i * tq + lax.broadcasted_iota(I32, (tq, kc), 0)
    col_l = lax.broadcasted_iota(I32, (tq, kc), 1)

    def score_chunk(j, carry):
        k0 = ki2t_ref[0, :, pl.ds(j * kc, kc)]
        k1 = ki2t_ref[1, :, pl.ds(j * kc, kc)]
        s = jnp.zeros((tq, kc), F32)
        for p in range(IDX_HEADS // 2):
            qp = qi_ref[:, p * LANES:(p + 1) * LANES]
            s = s + jnp.maximum(_dot(qp, k0), 0.0) * wide(wb[2 * p])
            s = s + jnp.maximum(_dot(qp, k1), 0.0) * wide(wb[2 * p + 1])
        sc[j] = jnp.where(j * kc + col_l <= row_g, s, -jnp.inf)
        return carry

    lax.fori_loop(0, n_chunks, score_chunk, 0)
    for extra in range(FOLD_CHUNKS - 1):
        sc[n_chunks + extra] = jnp.full((tq, kc), -jnp.inf, F32)
    n_steps = (n_chunks + FOLD_CHUNKS - 1) // FOLD_CHUNKS

    def fold(fn, init):
        outs = []
        for r in range(n_sub):
            rows = slice(r * ROW_SUB, (r + 1) * ROW_SUB)

            def body(j, acc, rows=rows, r=r):
                for c in range(FOLD_CHUNKS):
                    for part in range(kc // LANES):
                        c0 = (j * FOLD_CHUNKS + c) * kc + part * LANES
                        acc = fn(acc, sc[j * FOLD_CHUNKS + c, rows, part * LANES:(part + 1) * LANES], c0, r)
                return acc

            outs.append(lax.fori_loop(0, n_steps, body, jax.tree.map(
                lambda v: jnp.full((ROW_SUB, LANES), v, F32), init)))
        return outs

    def spread(row):
        return [jnp.broadcast_to(row[:, r * ROW_SUB:(r + 1) * ROW_SUB], (ROW_SUB, ROW_SUB)).T for r in range(n_sub)]

    def collect(parts, op):
        return jnp.concatenate([op(p.T, axis=0, keepdims=True) for p in parts], axis=1)

    def count(pred):
        return collect(fold(lambda acc, blk, c0, r: acc + jnp.where(pred(blk, c0, r), 1.0, 0.0), 0.0), jnp.sum)

    def count_ge(t):
        tb = spread(t)
        return count(lambda blk, c0, r: blk >= tb[r])

    top2 = fold(lambda acc, blk, c0, r: (jnp.maximum(acc[0], blk), jnp.maximum(acc[1], jnp.minimum(acc[0], blk))),
                (-jnp.inf, -jnp.inf))
    fmin = jnp.float32(jnp.finfo(F32).min)
    lo0 = jnp.maximum(collect([p[1] for p in top2], jnp.min), fmin)
    hi0 = collect([p[1 if topk > LANES else 0] for p in top2], jnp.max)

    kf = jnp.float32(topk)
    n_valid = (i * tq + lax.broadcasted_iota(I32, (1, tq), 1) + 1).astype(F32)
    done0 = jnp.where(n_valid <= kf, 1.0, 0.0)

    def cond(st):
        it, lo, hi, thr, done, stalled = st
        return (it < BISECT_CAP) & (jnp.min(done) == 0.0)

    def halve(st):
        it, lo, hi, thr, done, stalled = st
        mid = 0.5 * lo + 0.5 * hi
        n = count_ge(mid)
        live = done == 0.0
        exact = live & (n == kf)
        stall = live & jnp.logical_not(exact) & ((mid <= lo) | (mid >= hi))
        move = live & jnp.logical_not(exact) & jnp.logical_not(stall)
        up = n >= kf
        return (it + 1,
                jnp.where(move & up, mid, lo),
                jnp.where(move & jnp.logical_not(up), mid, hi),
                jnp.where(exact, mid, thr),
                jnp.where(exact | stall, 1.0, done),
                jnp.where(stall, 1.0, stalled))

    st = lax.while_loop(cond, halve, (jnp.int32(0), lo0, hi0, jnp.full((1, tq), fmin), done0,
                                      jnp.zeros((1, tq), F32)))
    _, lo, hi, thr, _, stalled = st
    n_hi = count_ge(hi)
    thr = jnp.where(stalled == 1.0, jnp.where(n_hi >= kf, hi, lo), thr)
    n_ge = count_ge(thr)
    tb = spread(thr)

    @pl.when(jnp.max(n_ge) > kf)
    def _():
        n_gt = count(lambda blk, c0, r: blk > tb[r])
        need = kf - n_gt
        lane_i = lax.broadcasted_iota(I32, (ROW_SUB, LANES), 1)
        n_bits = max(int(sc.shape[0] * kc).bit_length(), 1)

        def idx_step(t, jlo):
            trial = jlo + jnp.left_shift(jnp.int32(1), n_bits - 1 - t).astype(F32)
            trb = spread(trial)
            f = count(lambda blk, c0, r: (blk == tb[r]) & ((c0 + lane_i).astype(F32) < trb[r]))
            return jnp.where(f <= need - 1.0, trial, jlo)

        jlo = lax.fori_loop(0, n_bits, idx_step, jnp.zeros((1, tq), F32))
        cut = spread(jnp.where(n_ge > kf, jlo + 1.0, jnp.float32(2 ** 30)))

        def drop(j, carry):
            for r in range(n_sub):
                rows = slice(r * ROW_SUB, (r + 1) * ROW_SUB)
                for part in range(kc // LANES):
                    cols = slice(part * LANES, (part + 1) * LANES)
                    blk = sc[j, rows, cols]
                    gone = (blk == tb[r]) & ((j * kc + part * LANES + lane_i).astype(F32) >= cut[r])
                    sc[j, rows, cols] = jnp.where(gone, -jnp.inf, blk)
            return carry

        lax.fori_loop(0, n_chunks, drop, 0)

    for r in range(n_sub):
        thr_b[r * ROW_SUB:(r + 1) * ROW_SUB, :] = jnp.concatenate([tb[r]] * (kc // LANES), axis=1)

    acc_scr[...] = jnp.zeros(acc_scr.shape, F32)
    bounds = []
    for h in range(n_heads):
        qf = q_ref[:, h * ATT_HEAD_DIM:(h + 1) * ATT_HEAD_DIM].astype(F32)
        qn = jnp.sqrt(jnp.sum(qf * qf, axis=-1, keepdims=True))
        bounds.append(qn * (jnp.sqrt(kmax[h // q_per_kv]) * 1.01))
    safe = functools.reduce(jnp.maximum, [jnp.max(b) for b in bounds]) <= SAFE_SHIFT
    vw = 2 * ATT_HEAD_DIM

    def halves(x):
        return [x[:, k * LANES:(k + 1) * LANES] for k in range(kc // LANES)]

    def logits(j, h):
        g = h // q_per_kv
        kj = kbt_ref[g * ATT_HEAD_DIM:(g + 1) * ATT_HEAD_DIM, pl.ds(j * kc, kc)]
        qh = q_ref[:, h * ATT_HEAD_DIM:(h + 1) * ATT_HEAD_DIM]
        return _dot(qh, kj)

    def max_chunk(j, carry):
        sel = sc[j] >= thr_b[...]
        for h in range(n_heads):
            lg = jnp.where(sel, logits(j, h), NEG_BIG)
            mrun[h] = functools.reduce(jnp.maximum, halves(lg), mrun[h])
        return carry

    @pl.when(safe)
    def _():
        for h in range(n_heads):
            mrun[h] = jnp.broadcast_to(bounds[h], (tq, LANES))

    @pl.when(jnp.logical_not(safe))
    def _():
        mrun[...] = jnp.full(mrun.shape, NEG_BIG, F32)
        lax.fori_loop(0, n_chunks, max_chunk, 0)
        for h in range(n_heads):
            mrun[h] = jnp.broadcast_to(jnp.max(mrun[h], axis=-1, keepdims=True), (tq, LANES))

    def sum_chunk(j, carry):
        sel = sc[j] >= thr_b[...]
        for h in range(n_heads):
            g = h // q_per_kv
            vj = vx_ref[pl.ds(j * kc, kc), g * vw:(g + 1) * vw]
            m = mrun[h]
            p = jnp.where(sel, jnp.exp2(logits(j, h) - jnp.concatenate([m] * (kc // LANES), axis=1)), 0.0)
            acc_scr[h] = acc_scr[h] + _dot(p.astype(BF16), vj)
        return carry

    lax.fori_loop(0, n_chunks, sum_chunk, 0)

    ss = jnp.zeros((tq, 1), F32)
    for h in range(n_heads):
        o = acc_scr[h, :, :ATT_HEAD_DIM] / acc_scr[h, :, ATT_HEAD_DIM:]
        acc_scr[h, :, :ATT_HEAD_DIM] = o
        ss = ss + jnp.sum(o * o, axis=-1, keepdims=True)
    inv = lax.rsqrt(ss * (1.0 / (n_heads * ATT_HEAD_DIM)) + EPS)
    for h in range(n_heads):
        sl = slice(h * ATT_HEAD_DIM, (h + 1) * ATT_HEAD_DIM)
        o_ref[:, sl] = (acc_scr[h, :, :ATT_HEAD_DIM] * inv * nw_ref[:, sl]).astype(BF16)


def _attn_prompt(q, qi, wi, ki2t, kbt, vx, norm_w, topk, tq):
    t, d_att = q.shape
    n_heads = d_att // ATT_HEAD_DIM
    full = lambda a: pl.BlockSpec(a.shape, lambda i: (0,) * a.ndim)
    return pl.pallas_call(
        functools.partial(_attn_prompt_kernel, topk, tq),
        grid=(t // tq,),
        in_specs=[pl.BlockSpec((tq, d_att), lambda i: (i, 0)),
                  pl.BlockSpec((tq, qi.shape[1]), lambda i: (i, 0)),
                  pl.BlockSpec((tq, LANES), lambda i: (i, 0)),
                  full(ki2t), full(kbt), full(vx), full(norm_w)],
        out_specs=pl.BlockSpec((tq, d_att), lambda i: (i, 0)),
        out_shape=jax.ShapeDtypeStruct((t, d_att), BF16),
        scratch_shapes=[pltpu.VMEM((t // tq + FOLD_CHUNKS, tq, tq), F32),
                        pltpu.VMEM((IDX_HEADS, tq, LANES), F32),
                        pltpu.VMEM((tq, tq), F32),
                        pltpu.VMEM((n_heads, tq, LANES), F32),
                        pltpu.VMEM((n_heads, tq, 2 * ATT_HEAD_DIM), F32),
                        pltpu.SMEM((KV_HEADS,), F32)],
        compiler_params=_cparams(("arbitrary",)),
    )(q, qi, wi, ki2t, kbt, vx, norm_w)


def _outproj_kernel(ya_ref, yb_ref, w_ref, x_ref, g1_ref, nw_ref, sc_ref, sh_ref, wr_ref, br_ref,
                    x1_ref, h2_ref, eid_ref, wts_ref, cnt_ref):
    d_a = ya_ref.shape[1]
    m = _dot(ya_ref[...], w_ref[:d_a, :]) + _dot(yb_ref[...], w_ref[d_a:, :])
    x1 = x_ref[...] + g1_ref[...] * m
    x1_ref[...] = x1
    h2 = _rms(x1) * nw_ref[...]
    h2 = h2 * (1.0 + sc_ref[...]) + sh_ref[...]
    h2_ref[...] = h2
    lg = _dot(h2.astype(BF16), wr_ref[...]) + br_ref[...]
    lane = lax.broadcasted_iota(I32, lg.shape, 1)
    big = jnp.int32(4 * LANES)

    def rmax(v):
        return jnp.max(v, axis=-1, keepdims=True)

    def rmin(v):
        return jnp.min(v, axis=-1, keepdims=True)

    def rsum(v):
        return jnp.sum(v, axis=-1, keepdims=True)

    is_g = (lane >= N_EXPERTS) & (lane < N_EXPERTS + N_EGROUPS)
    mg = rmax(jnp.where(is_g, lg, -jnp.inf))
    sg = rsum(jnp.where(is_g, jnp.exp(lg - mg), 0.0))
    gsel = rmin(jnp.where(is_g & (lg == mg), lane - N_EXPERTS, big))
    pgsel = 1.0 / sg
    in_grp = (lane < N_EXPERTS) & (jnp.right_shift(lane, EXPERTS_PER_GROUP.bit_length() - 1) == gsel)
    me = rmax(jnp.where(in_grp, lg, -jnp.inf))
    ee = jnp.where(in_grp, jnp.exp(lg - me), 0.0)
    pe = ee / rsum(ee)
    p1 = rmax(jnp.where(in_grp, pe, -1.0))
    i1 = rmin(jnp.where(in_grp & (pe == p1), lane, big))
    rem = in_grp & (lane != i1)
    p2 = rmax(jnp.where(rem, pe, -1.0))
    i2 = rmin(jnp.where(rem & (pe == p2), lane, big))
    den = p1 + p2
    eid_ref[...] = jnp.where(lane == 0, i1, jnp.where(lane == 1, i2, 0))
    wts_ref[...] = jnp.where(lane == 0, pgsel * p1 / den, jnp.where(lane == 1, pgsel * p2 / den, 0.0))

    @pl.when(pl.program_id(0) == 0)
    def _():
        cnt_ref[...] = jnp.zeros_like(cnt_ref)

    chosen = jnp.where((lane == i1) | (lane == i2), 1.0, 0.0)
    cnt_ref[...] += jnp.sum(chosen, axis=0, keepdims=True)


def _out_proj(ya, yb, w_out_b, x, g1, nw2, sc2, sh2, wr, br, tm):
    t, d = x.shape
    d_a = ya.shape[1]
    tmod = g1.shape[0]
    mod_map = (lambda i: (0, 0)) if tmod == 1 else (lambda i: (i, 0))
    mod_rows = 1 if tmod == 1 else tm
    modspec = pl.BlockSpec((mod_rows, d), mod_map)
    row = lambda w: pl.BlockSpec((tm, w), lambda i: (i, 0))
    full = lambda a: pl.BlockSpec(a.shape, lambda i: (0, 0))
    return pl.pallas_call(
        _outproj_kernel,
        grid=(t // tm,),
        in_specs=[row(d_a), row(yb.shape[1]), full(w_out_b), row(d), modspec, full(nw2), modspec, modspec,
                  full(wr), full(br)],
        out_specs=[row(d), row(d), row(LANES), row(LANES), pl.BlockSpec((1, LANES), lambda i: (0, 0))],
        out_shape=[jax.ShapeDtypeStruct((t, d), F32), jax.ShapeDtypeStruct((t, d), F32),
                   jax.ShapeDtypeStruct((t, LANES), I32), jax.ShapeDtypeStruct((t, LANES), F32),
                   jax.ShapeDtypeStruct((1, LANES), F32)],
        compiler_params=_cparams(("arbitrary",)),
    )(ya, yb, w_out_b, x, g1, nw2, sc2, sh2, wr, br)


MOE_TILE = 256


def _moe_pos_kernel(eid_ref, off_ref, pos_ref, carry):
    @pl.when(pl.program_id(0) == 0)
    def _():
        carry[...] = jnp.zeros_like(carry)

    eid = eid_ref[...]
    tm = eid.shape[0]
    i1, i2 = eid[:, 0:1], eid[:, 1:2]
    lane = lax.broadcasted_iota(I32, eid.shape, 1)
    chosen = jnp.where((lane == i1) | (lane == i2), 1.0, 0.0)
    r = lax.broadcasted_iota(I32, (tm, tm), 0)
    c = lax.broadcasted_iota(I32, (tm, tm), 1)
    earlier = _dot(jnp.where(c < r, 1.0, 0.0).astype(BF16), chosen.astype(BF16))
    row = earlier + carry[...] + off_ref[...]
    p1 = jnp.sum(jnp.where(lane == i1, row, 0.0), axis=-1, keepdims=True)
    p2 = jnp.sum(jnp.where(lane == i2, row, 0.0), axis=-1, keepdims=True)
    out = jnp.where(lane == 0, p1, jnp.where(lane == 1, p2, 0.0))
    pos_ref[...] = jnp.where(i1 >= 0, out, -1.0).astype(I32)
    carry[...] += jnp.sum(chosen, axis=0, keepdims=True)


def _moe_positions(eid_all, off_row, tm):
    t = eid_all.shape[0]
    return pl.pallas_call(
        _moe_pos_kernel,
        grid=(t // tm,),
        in_specs=[pl.BlockSpec((tm, LANES), lambda i: (i, 0)), pl.BlockSpec((1, LANES), lambda i: (0, 0))],
        out_specs=pl.BlockSpec((tm, LANES), lambda i: (i, 0)),
        out_shape=jax.ShapeDtypeStruct((t, LANES), I32),
        scratch_shapes=[pltpu.VMEM((1, LANES), F32)],
        compiler_params=_cparams(("arbitrary",)),
    )(eid_all, off_row)


def _dyn_loop(lo, hi, fn, unroll=4):
    shift = unroll.bit_length() - 1
    n_blk = jnp.right_shift(hi - lo, shift)

    def blk(k, carry):
        for u in range(unroll):
            fn(lo + k * unroll + u)
        return carry

    def one(i, carry):
        fn(i)
        return carry

    lax.fori_loop(0, n_blk, blk, 0)
    lax.fori_loop(lo + n_blk * unroll, hi, one, 0)


def _moe_grouped_kernel(t_prompt, te_ref, nu_ref, np_ref, nv_ref, pos_ref, hp_ref, hs_ref, wu_ref, wd_ref, o_ref,
                        src, xbuf, wub, wdb, sem):
    g = pl.program_id(0)
    n_used = nu_ref[0]
    tmg = xbuf.shape[1]
    n_tok = pos_ref.shape[0] // 2
    n_rows = o_ref.shape[0] * pl.num_programs(0)
    slot = g % 2

    def row_copy(h_ref, tok, sl, r):
        return pltpu.make_async_copy(h_ref.at[pl.ds(tok, 1)], xbuf.at[sl, pl.ds(r, 1)], sem.at[sl])

    def gather_start(tile, sl):
        base = tile * tmg
        _dyn_loop(0, np_ref[tile], lambda r: row_copy(hp_ref, src[base + r], sl, r).start())
        _dyn_loop(np_ref[tile], nv_ref[tile], lambda r: row_copy(hs_ref, src[base + r] - t_prompt, sl, r).start())

    def gather_wait(tile, sl):
        _dyn_loop(0, nv_ref[tile], lambda r: row_copy(hp_ref, 0, sl, r).wait())

    @pl.when(g == 0)
    def _():
        def fill(t, carry):
            for k in range(2):
                p = pos_ref[2 * t + k]
                src[jnp.where(p < 0, n_rows, p)] = t
            return carry

        lax.fori_loop(0, n_tok, fill, 0, unroll=4)
        xbuf[...] = jnp.zeros_like(xbuf)
        gather_start(0, 0)

    @pl.when(g < n_used)
    def _():
        @pl.when(g + 1 < n_used)
        def _():
            gather_start(g + 1, 1 - slot)

        gather_wait(g, slot)
        fresh = (g == 0) | (te_ref[g] != te_ref[jnp.maximum(g - 1, 0)])

        @pl.when(fresh)
        def _():
            wub[...] = wu_ref[0].astype(BF16)
            wdb[...] = wd_ref[0].astype(BF16)

        gu = _dot(xbuf[slot].astype(BF16), wub[...])
        de = gu.shape[1] // 2
        act = _silu(gu[:, :de]) * gu[:, de:]
        o_ref[...] = _dot(act.astype(BF16), wdb[...])

    @pl.when(g >= n_used)
    def _():
        o_ref[...] = jnp.zeros_like(o_ref)


def _moe_grouped(plan, pos_flat, h2_p, h2_s, w_up, w_down):
    tile_expert, n_used, tile_np, tile_nv = plan
    n_tiles = tile_expert.shape[0]
    t_prompt, d = h2_p.shape
    _, _, two_de = w_up.shape
    tmg = MOE_TILE
    wmap = lambda g, te, nu, tp, tv, ps: (te[g], 0, 0)
    gs = pltpu.PrefetchScalarGridSpec(
        num_scalar_prefetch=5,
        grid=(n_tiles,),
        in_specs=[pl.BlockSpec(memory_space=pl.ANY), pl.BlockSpec(memory_space=pl.ANY),
                  pl.BlockSpec((1, d, two_de), wmap),
                  pl.BlockSpec((1, two_de // 2, d), wmap)],
        out_specs=pl.BlockSpec((tmg, d), lambda g, te, nu, tp, tv, ps: (g, 0)),
        scratch_shapes=[pltpu.SMEM((n_tiles * tmg + 8,), I32),
                        pltpu.VMEM((2, tmg, d), F32),
                        pltpu.VMEM((d, two_de), BF16),
                        pltpu.VMEM((two_de // 2, d), BF16),
                        pltpu.SemaphoreType.DMA((2,))],
    )
    return pl.pallas_call(
        functools.partial(_moe_grouped_kernel, t_prompt),
        grid_spec=gs,
        out_shape=jax.ShapeDtypeStruct((n_tiles * tmg, d), F32),
        compiler_params=_cparams(("arbitrary",)),
    )(tile_expert, n_used, tile_np, tile_nv, pos_flat, h2_p, h2_s, w_up, w_down)


def _combine_kernel(tok0, pos_ref, x1_ref, wts_ref, g2_ref, nf_ref, ys_ref, xo_ref, o_ref, ybuf, sem):
    i = pl.program_id(0)
    n = pl.num_programs(0)
    tm = x1_ref.shape[0]
    slot = i % 2

    def gather(tile, sl, wait):
        def body(r, carry):
            tok = tok0 + tile * tm + r
            for k in range(2):
                src_row = 0 if wait else pos_ref[2 * tok + k]
                cp = pltpu.make_async_copy(ys_ref.at[pl.ds(src_row, 1)], ybuf.at[sl, k, pl.ds(r, 1)], sem.at[sl])
                cp.wait() if wait else cp.start()
            return carry

        lax.fori_loop(0, tm, body, 0, unroll=8)

    @pl.when(i == 0)
    def _():
        gather(0, 0, False)

    @pl.when(i + 1 < n)
    def _():
        gather(i + 1, 1 - slot, False)

    gather(i, slot, True)
    w = wts_ref[...]
    y = w[:, 0:1] * ybuf[slot, 0] + w[:, 1:2] * ybuf[slot, 1]
    x2 = x1_ref[...] + g2_ref[...] * y
    xo_ref[...] = x2
    o_ref[...] = _rms(x2) * nf_ref[...]


def _combine(pos_flat, x1, wts, g2, nf, ys, tok0, tm):
    t, d = x1.shape
    tmod = g2.shape[0]
    mod_map = (lambda i, ps: (0, 0)) if tmod == 1 else (lambda i, ps: (i, 0))
    row = lambda w: pl.BlockSpec((tm, w), lambda i, ps: (i, 0))
    gs = pltpu.PrefetchScalarGridSpec(
        num_scalar_prefetch=1,
        grid=(t // tm,),
        in_specs=[row(d), row(LANES), pl.BlockSpec((1 if tmod == 1 else tm, d), mod_map),
                  pl.BlockSpec((1, d), lambda i, ps: (0, 0)), pl.BlockSpec(memory_space=pl.ANY)],
        out_specs=[row(d), row(d)],
        scratch_shapes=[pltpu.VMEM((2, 2, tm, d), F32), pltpu.SemaphoreType.DMA((2,))],
    )
    return pl.pallas_call(
        functools.partial(_combine_kernel, tok0),
        grid_spec=gs,
        out_shape=[jax.ShapeDtypeStruct((t, d), F32), jax.ShapeDtypeStruct((t, d), F32)],
        compiler_params=_cparams(("arbitrary",)),
    )(pos_flat, x1, wts, g2, nf, ys)


def _moe_plan(cnt_p, cnt_s, n_tiles):
    cp = cnt_p[0, :N_EXPERTS].astype(I32)
    cnt = cp + cnt_s[0, :N_EXPERTS].astype(I32)
    padded = (cnt + MOE_TILE - 1) // MOE_TILE * MOE_TILE
    ends = jnp.cumsum(padded)
    off = ends - padded
    off_row = _pad_cols(off.astype(F32).reshape(1, N_EXPERTS), LANES)
    starts = jnp.arange(n_tiles, dtype=I32) * MOE_TILE
    te = jnp.minimum(jnp.sum(starts[:, None] >= ends[None, :], axis=1), N_EXPERTS - 1).astype(I32)
    tile_np = jnp.clip(off[te] + cp[te] - starts, 0, MOE_TILE).astype(I32)
    tile_nv = jnp.clip(off[te] + cnt[te] - starts, 0, MOE_TILE).astype(I32)
    return off_row, (te, (ends[-1:] // MOE_TILE).astype(I32), tile_np, tile_nv)


def _ssd_prep_kernel(xbc_ref, p0_ref, p1_ref, p2_ref, cw_ref, cb_ref, dt_ref, dtb_ref, alog_ref, ex_ref,
                     xc_ref, xdt_ref, dec_ref):
    d_ssd = xdt_ref.shape[1]
    acc = (cb_ref[...] + cw_ref[0:1, :] * p0_ref[...] + cw_ref[1:2, :] * p1_ref[...]
           + cw_ref[2:3, :] * p2_ref[...] + cw_ref[3:4, :] * xbc_ref[...])
    xc = _silu(acc)
    xc_ref[...] = xc
    dt = _softplus(dt_ref[...] + dtb_ref[...])
    dec = jnp.exp(dt * (-jnp.exp(alog_ref[...])))
    xdt_ref[...] = _dot(dt, ex_ref[...], precision=HIGHEST) * xc[:, :d_ssd]
    dec_ref[...] = _dot(dec, ex_ref[...], precision=HIGHEST)


def _ssd_prep(xbc, p0, p1, p2, conv_w, conv_b, dt_raw, dt_bias_p, a_log_p, expand):
    b, conv_dim = xbc.shape
    d_ssd = expand.shape[1]
    args = (xbc, p0, p1, p2, conv_w, conv_b, dt_raw, dt_bias_p, a_log_p, expand)
    return pl.pallas_call(
        _ssd_prep_kernel,
        grid=(1,),
        in_specs=[pl.BlockSpec(a.shape, lambda i: (0, 0)) for a in args],
        out_specs=[pl.BlockSpec((b, conv_dim), lambda i: (0, 0)),
                   pl.BlockSpec((b, d_ssd), lambda i: (0, 0)),
                   pl.BlockSpec((b, d_ssd), lambda i: (0, 0))],
        out_shape=[jax.ShapeDtypeStruct((b, conv_dim), F32), jax.ShapeDtypeStruct((b, d_ssd), F32),
                   jax.ShapeDtypeStruct((b, d_ssd), F32)],
        compiler_params=_cparams(("arbitrary",)),
    )(*args)


def _ssd_step_kernel(n_pairs, xdt_ref, dec_ref, bm_ref, cm_ref, s_ref, so_ref, y_ref):
    r2 = lax.broadcasted_iota(I32, (LANES, LANES), 0)
    c2 = lax.broadcasted_iota(I32, (LANES, LANES), 1)
    eye = r2 == c2
    ones = jnp.ones((LANES, LANES), F32)
    pairs_per_group = n_pairs // SSD_GROUPS
    rows_per_pair = LANES // SSD_HEAD_DIM
    for p in range(n_pairs):
        g = p // pairs_per_group
        sl = slice(p * LANES, (p + 1) * LANES)
        hs = slice(p * rows_per_pair, (p + 1) * rows_per_pair)
        hb = s_ref[0, 0, hs].reshape(LANES, D_STATE)
        xd = jnp.where(eye, jnp.broadcast_to(xdt_ref[0, :, sl], (LANES, LANES)), 0.0)
        dd = jnp.where(eye, jnp.broadcast_to(dec_ref[0, :, sl], (LANES, LANES)), 0.0)
        bmat = jnp.broadcast_to(bm_ref[0, :, g * D_STATE:(g + 1) * D_STATE], (LANES, D_STATE))
        upd = _dot(xd, bmat, precision=HIGHEST)
        dcol = _dot(dd, ones, precision=HIGHEST)
        hn = hb * dcol + upd
        so_ref[0, 0, hs] = hn.reshape(rows_per_pair, SSD_HEAD_DIM, D_STATE)
        cmat = jnp.broadcast_to(cm_ref[0, :, g * D_STATE:(g + 1) * D_STATE], (8, D_STATE))
        y_ref[0, :, sl] = _dot_nt(cmat, hn, precision=HIGHEST)[0:1, :]


def _ssd_step(xdt, dec, bm, cm, state):
    b, d_ssd = xdt.shape
    n_pairs = d_ssd // LANES
    heads = d_ssd // SSD_HEAD_DIM
    r3 = lambda a: a.reshape(b, 1, a.shape[1])
    row = lambda w: pl.BlockSpec((1, 1, w), lambda i: (i, 0, 0))
    sspec = pl.BlockSpec((1, 1, heads, SSD_HEAD_DIM, D_STATE), lambda i: (0, i, 0, 0, 0))
    so, y = pl.pallas_call(
        functools.partial(_ssd_step_kernel, n_pairs),
        grid=(b,),
        in_specs=[row(d_ssd), row(d_ssd), row(bm.shape[1]), row(cm.shape[1]), sspec],
        out_specs=[sspec, row(d_ssd)],
        out_shape=[jax.ShapeDtypeStruct(state.shape, F32), jax.ShapeDtypeStruct((b, 1, d_ssd), F32)],
        compiler_params=_cparams(("arbitrary",)),
    )(r3(xdt), r3(dec), r3(bm), r3(cm), state)
    return so, y.reshape(b, d_ssd)


def _ssd_finish_kernel(y_ref, xs_ref, z_ref, dsk_ref, nw_ref, o_ref):
    y = (y_ref[...] + xs_ref[...] * dsk_ref[...]) * _silu(z_ref[...])
    o_ref[...] = (_rms(y) * nw_ref[...]).astype(BF16)


def _ssd_finish(y, xs, z, dskip_row, norm_w):
    args = (y, xs, z, dskip_row, norm_w)
    return pl.pallas_call(
        _ssd_finish_kernel,
        grid=(1,),
        in_specs=[pl.BlockSpec(a.shape, lambda i: (0, 0)) for a in args],
        out_specs=pl.BlockSpec(y.shape, lambda i: (0, 0)),
        out_shape=jax.ShapeDtypeStruct(y.shape, BF16),
        compiler_params=_cparams(("arbitrary",)),
    )(*args)


PAGE_PACK = 8


def _page_copy(cache_ref, buf, sem, pt_ref, b, p, slot):
    rows = cache_ref.shape[2]
    return pltpu.make_async_copy(cache_ref.at[0, pt_ref[b, p]], buf.at[slot, pl.ds(p * rows, rows)], sem.at[slot])


def _score_sample_kernel(n_pages, pt_ref, q8_ref, w8_ref, qi_ref, wi_ref, kin_ref, cache_ref, s_ref, buf, sem):
    b = pl.program_id(0)
    nb = pl.num_programs(0)
    slot = b % 2

    def start(bb, sl):
        def body(p, carry):
            _page_copy(cache_ref, buf, sem, pt_ref, bb, p, sl).start()
            return carry
        lax.fori_loop(0, n_pages, body, 0)

    @pl.when(b == 0)
    def _():
        start(0, 0)

    @pl.when(b + 1 < nb)
    def _():
        start(b + 1, 1 - slot)

    def wait(p, carry):
        _page_copy(cache_ref, buf, sem, pt_ref, b, p, slot).wait()
        return carry

    lax.fori_loop(0, n_pages, wait, 0)

    wscale = (IDX_DIM ** -0.5) * (IDX_HEADS ** -0.5)
    q8 = q8_ref[0]
    w8 = w8_ref[0] * wscale
    kdim = q8.shape[1]
    page_rows = buf.shape[2]

    def group(gi, carry):
        keys_t = buf[slot, pl.ds(gi * kdim, kdim), :].astype(BF16)
        r = jnp.maximum(_dot(q8, keys_t), 0.0) * w8
        s_ref[0, pl.ds(gi * PAGE_PACK, PAGE_PACK), :] = jnp.sum(
            r.reshape(PAGE_PACK, IDX_HEADS, page_rows), axis=1)
        return carry

    lax.fori_loop(0, n_pages // PAGE_PACK, group, 0, unroll=2)
    tail = s_ref.shape[1] - n_pages
    kn = jnp.broadcast_to(kin_ref[0], (page_rows, kin_ref.shape[2])).astype(BF16)
    dn = _dot_nt(qi_ref[0], kn)
    sn = jnp.sum(jnp.maximum(dn, 0.0) * (wi_ref[0] * wscale), axis=0, keepdims=True)
    r = lax.broadcasted_iota(I32, (tail, page_rows), 0)
    c = lax.broadcasted_iota(I32, (tail, page_rows), 1)
    s_ref[0, n_pages:, :] = jnp.where((r == 0) & (c == 0), jnp.broadcast_to(sn, (tail, page_rows)), -jnp.inf)


def _score_sample(page_table, q8, w8, qi3, wi3, ki_new3, cache_kit, tail_rows):
    b, n_pages = page_table.shape
    idx_dim, page_rows = cache_kit.shape[2], cache_kit.shape[3]
    blk = lambda a: pl.BlockSpec((1,) + a.shape[1:], lambda i, pt: (i, 0, 0))
    gs = pltpu.PrefetchScalarGridSpec(
        num_scalar_prefetch=1,
        grid=(b,),
        in_specs=[blk(q8), blk(w8), blk(qi3), blk(wi3), blk(ki_new3), pl.BlockSpec(memory_space=pl.ANY)],
        out_specs=pl.BlockSpec((1, n_pages + tail_rows, page_rows), lambda i, pt: (i, 0, 0)),
        scratch_shapes=[pltpu.VMEM((2, n_pages * idx_dim, page_rows), F32),
                        pltpu.SemaphoreType.DMA((2,))],
    )
    return pl.pallas_call(
        functools.partial(_score_sample_kernel, n_pages),
        grid_spec=gs,
        out_shape=jax.ShapeDtypeStruct((b, n_pages + tail_rows, page_rows), F32),
        compiler_params=_cparams(("arbitrary",)),
    )(page_table, q8, w8, qi3, wi3, ki_new3, cache_kit)


def _threshold_sample_kernel(topk, s_ref, thr_ref, cut_ref):
    nb, n = s_ref.shape
    n_blk = n // LANES
    kf = jnp.float32(topk)

    def fold(fn, init):
        def body(j, acc):
            return fn(acc, s_ref[:, pl.ds(pl.multiple_of(j * LANES, LANES), LANES)], j * LANES)
        return lax.fori_loop(0, n_blk, body, jnp.full((nb, LANES), init, F32), unroll=8)

    def count(pred):
        return jnp.sum(fold(lambda acc, blk, c0: acc + jnp.where(pred(blk, c0), 1.0, 0.0), 0.0), axis=-1, keepdims=True)

    def count_ge(t):
        tb = jnp.broadcast_to(t, (nb, LANES))
        return count(lambda blk, c0: blk >= tb)

    fmin = jnp.float32(jnp.finfo(F32).min)
    lo0 = jnp.min(fold(lambda acc, blk, c0: jnp.minimum(acc, jnp.where(blk == -jnp.inf, 3e38, blk)), 3e38),
                  axis=-1, keepdims=True)
    hi0 = jnp.max(fold(lambda acc, blk, c0: jnp.maximum(acc, blk), -jnp.inf), axis=-1, keepdims=True)
    n_valid = count(lambda blk, c0: blk > -jnp.inf)
    done0 = jnp.where(n_valid <= kf, 1.0, 0.0)

    def cond(st):
        it, lo, hi, thr, done, stalled = st
        return (it < BISECT_CAP) & (jnp.min(done) == 0.0)

    def halve(st):
        it, lo, hi, thr, done, stalled = st
        mid = 0.5 * lo + 0.5 * hi
        n = count_ge(mid)
        live = done == 0.0
        exact = live & (n == kf)
        stall = live & jnp.logical_not(exact) & ((mid <= lo) | (mid >= hi))
        move = live & jnp.logical_not(exact) & jnp.logical_not(stall)
        up = n >= kf
        return (it + 1,
                jnp.where(move & up, mid, lo),
                jnp.where(move & jnp.logical_not(up), mid, hi),
                jnp.where(exact, mid, thr),
                jnp.where(exact | stall, 1.0, done),
                jnp.where(stall, 1.0, stalled))

    st = lax.while_loop(cond, halve, (jnp.int32(0), lo0, hi0, jnp.minimum(lo0, fmin), done0, jnp.zeros((nb, 1), F32)))
    _, lo, hi, thr, _, stalled = st
    thr = jnp.where(stalled == 1.0, jnp.where(count_ge(hi) >= kf, hi, lo), thr)
    n_ge = count_ge(thr)
    thr_ref[...] = jnp.broadcast_to(thr, (nb, LANES))
    n_bits = max(int(n).bit_length(), 1)
    tb = jnp.broadcast_to(thr, (nb, LANES))
    lane = lax.broadcasted_iota(I32, (nb, LANES), 1)

    def tie_cut():
        need = kf - count(lambda blk, c0: blk > tb)

        def idx_step(t, jlo):
            trial = jlo + jnp.left_shift(jnp.int32(1), n_bits - 1 - t).astype(F32)
            trb = jnp.broadcast_to(trial, (nb, LANES))
            f = count(lambda blk, c0: (blk == tb) & ((c0 + lane).astype(F32) < trb))
            return jnp.where(f <= need - 1.0, trial, jlo)

        jlo = lax.fori_loop(0, n_bits, idx_step, jnp.zeros((nb, 1), F32))
        return jnp.where(n_ge > kf, jlo + 1.0, jnp.float32(2 ** 30))

    cut = lax.cond(jnp.max(n_ge) > kf, tie_cut, lambda: jnp.full((nb, 1), 2 ** 30, F32))
    cut_ref[...] = jnp.broadcast_to(cut, (nb, LANES))


def _threshold_sample(s2, topk):
    nb, n = s2.shape
    return pl.pallas_call(
        functools.partial(_threshold_sample_kernel, topk),
        grid=(1,),
        in_specs=[pl.BlockSpec((nb, n), lambda i: (0, 0))],
        out_specs=[pl.BlockSpec((nb, LANES), lambda i: (0, 0))] * 2,
        out_shape=[jax.ShapeDtypeStruct((nb, LANES), F32)] * 2,
        compiler_params=_cparams(("arbitrary",)),
    )(s2)


def _select_sample_kernel(topk, s_ref, thr_ref, cut_ref, idx_ref):
    s = s_ref[0]
    n_rows, width = s.shape
    pos = (lax.broadcasted_iota(I32, s.shape, 0) * width + lax.broadcasted_iota(I32, s.shape, 1)).astype(F32)
    thr = thr_ref[0]
    sel = (s > thr) | ((s == thr) & (pos < cut_ref[0]))
    self = jnp.where(sel, 1.0, 0.0).astype(BF16)
    ra = lax.broadcasted_iota(I32, (width, width), 0)
    ca = lax.broadcasted_iota(I32, (width, width), 1)
    local = jnp.where(sel, _dot(self, jnp.where(ra < ca, 1.0, 0.0).astype(BF16)), -1.0)
    cnt_row = _dot_nt(jnp.ones((8, width), BF16), self)
    rb = lax.broadcasted_iota(I32, (n_rows, n_rows), 0)
    cb = lax.broadcasted_iota(I32, (n_rows, n_rows), 1)
    end_row = _dot(cnt_row.astype(BF16), jnp.where(rb <= cb, 1.0, 0.0).astype(BF16))
    rank = lax.broadcasted_iota(I32, (topk, n_rows), 0).astype(F32)
    row_id = lax.broadcasted_iota(I32, (topk, n_rows), 1).astype(F32)
    passed = jnp.broadcast_to(end_row[0:1, :], (topk, n_rows)) <= rank
    row_of = jnp.sum(jnp.where(passed, 1.0, 0.0), axis=-1, keepdims=True)
    start = jnp.sum(jnp.where(passed, jnp.broadcast_to(cnt_row[0:1, :], (topk, n_rows)), 0.0), axis=-1,
                    keepdims=True)
    picked = _dot(jnp.where(row_id == row_of, 1.0, 0.0).astype(BF16), local.astype(BF16))
    lane = lax.broadcasted_iota(I32, (topk, width), 1).astype(F32)
    lane_of = jnp.sum(jnp.where(picked == rank[:, 0:1] - start, lane, 0.0), axis=-1, keepdims=True)
    idx_ref[0] = (row_of * width + lane_of).astype(I32)


def _select_sample(s3, thr, cut, topk):
    b, n_rows, width = s3.shape
    per_b = lambda a: pl.BlockSpec((1, 1, LANES), lambda i: (i, 0, 0))
    return pl.pallas_call(
        functools.partial(_select_sample_kernel, topk),
        grid=(b,),
        in_specs=[pl.BlockSpec((1, n_rows, width), lambda i: (i, 0, 0)), per_b(thr), per_b(cut)],
        out_specs=pl.BlockSpec((1, topk, 1), lambda i: (i, 0, 0)),
        out_shape=jax.ShapeDtypeStruct((b, topk, 1), I32),
        compiler_params=_cparams(("arbitrary",)),
    )(s3, thr.reshape(b, 1, LANES), cut.reshape(b, 1, LANES))


def _row_copy(src, dst, sem, src_row, dst_row):
    return pltpu.make_async_copy(src.at[pl.ds(src_row, KV_HEADS)], dst.at[pl.ds(dst_row, KV_HEADS)], sem)


def _attn_sample_kernel(topk, past_len, page_rows, n_pages, idx_ref, pt_ref, q_ref, nw_ref, ck_ref, cv_ref,
                        kn_ref, vn_ref, o_ref, kbuf, vbuf, sem):
    b = pl.program_id(0)

    pow2 = page_rows & (page_rows - 1) == 0

    def start(r, carry):
        j = jnp.minimum(idx_ref[b, r], past_len - 1)
        if pow2:
            page, off = jnp.right_shift(j, page_rows.bit_length() - 1), j & (page_rows - 1)
        else:
            page, off = j // page_rows, j % page_rows
        row = (pt_ref[b, page] * page_rows + off) * KV_HEADS
        _row_copy(ck_ref, kbuf, sem.at[0], row, r * KV_HEADS).start()
        _row_copy(cv_ref, vbuf, sem.at[1], row, r * KV_HEADS).start()
        return carry

    lax.fori_loop(0, topk, start, 0, unroll=8)

    def wait(r, carry):
        _row_copy(ck_ref, kbuf, sem.at[0], 0, r * KV_HEADS).wait()
        _row_copy(cv_ref, vbuf, sem.at[1], 0, r * KV_HEADS).wait()
        return carry

    lax.fori_loop(0, topk, wait, 0, unroll=8)

    @pl.when(idx_ref[b, topk - 1] >= past_len)
    def _():
        last = (topk - 1) * KV_HEADS
        for src_ref, buf, s in ((kn_ref, kbuf, sem.at[0]), (vn_ref, vbuf, sem.at[1])):
            cp = _row_copy(src_ref, buf, s, b * KV_HEADS, last)
            cp.start()
            cp.wait()

    outs = []
    ss = jnp.zeros((1, 1), F32)
    for g in range(KV_HEADS):
        kg = kbuf[pl.ds(g, topk, stride=KV_HEADS), :].astype(BF16)
        vg = vbuf[pl.ds(g, topk, stride=KV_HEADS), :].astype(BF16)
        lg = _dot_nt(q_ref[0, g], kg)
        m = jnp.max(lg, axis=-1, keepdims=True)
        p = jnp.exp2(lg - m)
        p = p / jnp.sum(p, axis=-1, keepdims=True)
        o = _dot(p.astype(BF16), vg)
        rows = lax.broadcasted_iota(I32, o.shape, 0)
        o = jnp.where(rows < q_ref.shape[2] // 2, o, 0.0)
        outs.append(o)
        ss = ss + jnp.sum(jnp.sum(o * o, axis=-1, keepdims=True), axis=0, keepdims=True)
    n_feat = KV_HEADS * (q_ref.shape[2] // 2) * ATT_HEAD_DIM
    inv = lax.rsqrt(ss * (1.0 / n_feat) + EPS)
    for g in range(KV_HEADS):
        o_ref[0, g] = (outs[g] * inv * nw_ref[g]).astype(BF16)


def _attn_sample(idx, page_table, q4, nw3, ck2, cv2, kn2, vn2, past_len, page_rows):
    b, topk = idx.shape
    n_pages = page_table.shape[1]
    gs = pltpu.PrefetchScalarGridSpec(
        num_scalar_prefetch=2,
        grid=(b,),
        in_specs=[pl.BlockSpec((1,) + q4.shape[1:], lambda i, a, c: (i, 0, 0, 0)),
                  pl.BlockSpec(nw3.shape, lambda i, a, c: (0, 0, 0)),
                  pl.BlockSpec(memory_space=pl.ANY), pl.BlockSpec(memory_space=pl.ANY),
                  pl.BlockSpec(memory_space=pl.ANY), pl.BlockSpec(memory_space=pl.ANY)],
        out_specs=pl.BlockSpec((1,) + q4.shape[1:], lambda i, a, c: (i, 0, 0, 0)),
        scratch_shapes=[pltpu.VMEM((topk * KV_HEADS, ATT_HEAD_DIM), F32),
                        pltpu.VMEM((topk * KV_HEADS, ATT_HEAD_DIM), F32),
                        pltpu.SemaphoreType.DMA((2,))],
    )
    return pl.pallas_call(
        functools.partial(_attn_sample_kernel, topk, past_len, page_rows, n_pages),
        grid_spec=gs,
        out_shape=jax.ShapeDtypeStruct(q4.shape, BF16),
        compiler_params=_cparams(("arbitrary",)),
    )(idx, page_table, q4, nw3, ck2, cv2, kn2, vn2)


def _row(v, width=None):
    v = v.reshape(1, -1)
    return v if width is None else _pad_cols(v, width)


def _layer_params(p):
    d = p["w_in"].shape[0]
    wr = jnp.concatenate([p["w_router_e"], p["w_router_g"]], axis=1)
    br = jnp.concatenate([p["b_router_e"], p["b_router_g"]])
    return dict(
        w_perm=_perm_w_in(p["w_in"]),
        w_out_b=p["w_out"].astype(BF16),
        wr=_pad_cols(wr, LANES).astype(BF16),
        br=_row(br, LANES),
        nw1=_row(p["norm1_w"]), nw2=_row(p["norm2_w"]),
        lnw=_row(p["ln_kidx_w"], LANES), lnb=_row(p["ln_kidx_b"], LANES),
        dt_bias=_row(p["dt_bias"], LANES), a_log=_row(p["a_log"], LANES),
        dskip=_row(jnp.repeat(p["d_skip"], SSD_HEAD_DIM)),
        norm_ssd=_row(p["norm_ssd_w"]), norm_att=_row(p["norm_att_w"]),
        conv_w=p["conv_w"], conv_b=_row(p["conv_b"]),
        d_ssd=d // 2,
    )


def _row_tile(t):
    return 512 if t % 512 == 0 else 256


def _route(x, ya, yb, mod, lp, tm):
    return _out_proj(ya, yb, lp["w_out_b"], x, mod[2], lp["nw2"], mod[4], mod[3], lp["wr"], lp["br"], tm)


def _moe_and_norm(routed_p, routed_s, g2_p, g2_s, p, nf):
    x1p, h2p, eidp, wtsp, cntp = routed_p
    x1s, h2s, eids, wtss, cnts = routed_s
    tp, ts = x1p.shape[0], x1s.shape[0]
    tt = tp + ts
    tpos = TOKEN_TILE
    tt_pad = -(-tt // tpos) * tpos
    eid_all = jnp.concatenate([eidp, eids, jnp.full((tt_pad - tt, LANES), -1, I32)])
    n_tiles = -(-(2 * tt + N_EXPERTS * (MOE_TILE - 1)) // MOE_TILE)
    off_row, plan = _moe_plan(cntp, cnts, n_tiles)
    pos_flat = _moe_positions(eid_all, off_row, tpos)[:, :2].reshape(-1)
    ys = _moe_grouped(plan, pos_flat, h2p, h2s, p["w_exp_up"], p["w_exp_down"])
    out_p = _combine(pos_flat, x1p, wtsp, g2_p, nf, ys, tok0=0, tm=TOKEN_TILE)
    out_s = _combine(pos_flat, x1s, wtss, g2_s, nf, ys, tok0=tp, tm=ts)
    return out_p, out_s


def _prompt_layer(x, mod, lp, p):
    t, d = x.shape
    pr = _in_proj(x, lp["nw1"], mod[1], mod[0], lp["w_perm"], lp["lnw"], lp["lnb"], tm=_row_tile(t))
    y_ssd, st = _ssd_prompt(pr["xbc"], pr["dt"], pr["z"], lp["conv_w"], lp["conv_b"], lp["dt_bias"], lp["a_log"],
                            lp["dskip"], lp["norm_ssd"])
    ki = pr["ki"]
    kit = ki.T.astype(BF16)
    zeros = jnp.zeros_like(kit)
    ki2t = jnp.stack([jnp.concatenate([kit, zeros], axis=0), jnp.concatenate([zeros, kit], axis=0)])
    topk = min(TOPK_MAX, t // 4)
    v3 = pr["vb"].reshape(t, KV_HEADS, ATT_HEAD_DIM)
    vx = jnp.concatenate([v3, jnp.ones_like(v3)], axis=-1).reshape(t, 2 * KV_HEADS * ATT_HEAD_DIM)
    y_att = _attn_prompt(pr["q"], pr["qi"], pr["wi"], ki2t, pr["kb"].T, vx, lp["norm_att"], topk, tq=ATTN_TILE)
    routed = _route(x, y_ssd, y_att, mod, lp, tm=_row_tile(t))
    conv_new = jnp.concatenate([jnp.zeros((CONV_W - 1, pr["xbc"].shape[1]), F32), pr["xbc"]])[-(CONV_W - 1):]
    return routed, (pr["k"], pr["v"], ki, conv_new, st)


def _sample_layer(x, mod, lp, p, cache_k, cache_v, cache_ki, conv_prev, ssm_prev, page_table):
    b, d = x.shape
    d_ssd = lp["d_ssd"]
    heads = d_ssd // SSD_HEAD_DIM
    gn = SSD_GROUPS * D_STATE
    pr = _in_proj(x, lp["nw1"], mod[1], mod[0], lp["w_perm"], lp["lnw"], lp["lnb"], tm=b)
    expand = (jnp.arange(LANES)[:, None] == (jnp.arange(d_ssd)[None, :] // SSD_HEAD_DIM)).astype(F32)
    xc, xdt, dec = _ssd_prep(pr["xbc"], conv_prev[:, 0], conv_prev[:, 1], conv_prev[:, 2], lp["conv_w"], lp["conv_b"],
                             pr["dt"], lp["dt_bias"], lp["a_log"], expand)
    xs, bm, cm = xc[:, :d_ssd], xc[:, d_ssd:d_ssd + gn], xc[:, d_ssd + gn:]
    st5 = ssm_prev.reshape((1, b, heads, SSD_HEAD_DIM, D_STATE))
    st_new, y = _ssd_step(xdt, dec, bm, cm, st5)
    y_ssd = _ssd_finish(y, xs, pr["z"], lp["dskip"], lp["norm_ssd"])
    conv_new = jnp.concatenate([conv_prev[:, 1:], pr["xbc"][:, None, :]], axis=1)
    n_pool, page_rows = cache_k.shape[0], cache_k.shape[1]
    n_pages = page_table.shape[1]
    past_len = n_pages * page_rows
    topk = min(TOPK_MAX, (past_len + 1) // 4)
    qi3 = pr["qi"].reshape(b, IDX_HEADS, IDX_DIM)
    wi3 = pr["wi"][:, :IDX_HEADS].reshape(b, IDX_HEADS, 1)
    eye = jnp.eye(PAGE_PACK, dtype=BF16)
    q8 = (eye[None, :, None, :, None] * qi3[:, None, :, None, :]).reshape(b, PAGE_PACK * IDX_HEADS,
                                                                         PAGE_PACK * IDX_DIM)
    w8 = jnp.tile(wi3, (1, PAGE_PACK, 1))
    tail_rows = -(n_pages + 1) % LANES + 1
    cache_kit = jnp.swapaxes(cache_ki, -1, -2)[None]
    s3 = _score_sample(page_table, q8, w8, qi3, wi3, pr["ki"].reshape(b, 1, IDX_DIM), cache_kit, tail_rows)
    thr, cut = _threshold_sample(s3.reshape(b, -1), topk)
    idx = _select_sample(s3, thr, cut, topk).reshape(b, topk)
    n_heads = pr["q"].shape[1] // ATT_HEAD_DIM
    q_per_kv = n_heads // KV_HEADS
    q4 = jnp.pad(pr["q"].reshape(b, KV_HEADS, q_per_kv, ATT_HEAD_DIM), ((0, 0), (0, 0), (0, q_per_kv), (0, 0)))
    nw3 = jnp.pad(lp["norm_att"].reshape(KV_HEADS, q_per_kv, ATT_HEAD_DIM), ((0, 0), (0, q_per_kv), (0, 0)))
    ck2 = cache_k.reshape(n_pool * page_rows * KV_HEADS, ATT_HEAD_DIM)
    cv2 = cache_v.reshape(n_pool * page_rows * KV_HEADS, ATT_HEAD_DIM)
    kn2 = pr["k"].reshape(b * KV_HEADS, ATT_HEAD_DIM)
    vn2 = pr["v"].reshape(b * KV_HEADS, ATT_HEAD_DIM)
    o4 = _attn_sample(idx, page_table, q4, nw3, ck2, cv2, kn2, vn2, past_len, page_rows)
    y_att = o4[:, :, :q_per_kv].reshape(b, n_heads * ATT_HEAD_DIM)
    routed = _route(x, y_ssd, y_att, mod, lp, tm=b)
    return routed, (pr["k"], pr["v"], pr["ki"], conv_new, st_new.reshape(ssm_prev.shape))


def kernel(x_prompt, x_sample, cache_k, cache_v, cache_k_idx, state_conv, state_ssm, page_table, c_prompt, c_sample, w_ada, b_ada, norm1_w, norm2_w, w_in, conv_w, conv_b, dt_bias, a_log, d_skip, norm_ssd_w, ln_kidx_w, ln_kidx_b, norm_att_w, w_out, w_router_g, b_router_g, w_router_e, b_router_e, w_exp_up, w_exp_down, norm_f_w):
    batch, seq, d = x_prompt.shape
    dec_batch, dec_seq, _ = x_sample.shape
    assert batch == 1 and dec_seq == 1, "one prompt sequence and one new token per sample sequence"
    depth = w_ada.shape[0]
    heads = (d // 2) // SSD_HEAD_DIM
    xp = x_prompt.reshape(seq, d)
    xs = x_sample.reshape(dec_batch, d)
    n_c = batch + dec_batch
    c_all = jnp.pad(jnp.concatenate([c_prompt, c_sample]), ((0, -n_c % 8), (0, 0)))
    nf = _row(norm_f_w)
    outs_p, outs_s = [], []
    yp = ys = None
    for l in range(depth):
        p = dict(w_in=w_in[l], conv_w=conv_w[l], conv_b=conv_b[l], dt_bias=dt_bias[l], a_log=a_log[l],
                 d_skip=d_skip[l], norm_ssd_w=norm_ssd_w[l], ln_kidx_w=ln_kidx_w[l], ln_kidx_b=ln_kidx_b[l],
                 norm_att_w=norm_att_w[l], w_out=w_out[l], w_router_g=w_router_g[l], b_router_g=b_router_g[l],
                 w_router_e=w_router_e[l], b_router_e=b_router_e[l], w_exp_up=w_exp_up[l],
                 w_exp_down=w_exp_down[l], norm1_w=norm1_w[l], norm2_w=norm2_w[l])
        lp = _layer_params(p)
        mod = _ada_mod(c_all, w_ada[l], b_ada[l])
        mod_p = [mod[0:1, k * d:(k + 1) * d] for k in range(6)]
        mod_s = [mod[batch:n_c, k * d:(k + 1) * d] for k in range(6)]
        routed_p, st_p = _prompt_layer(xp, mod_p, lp, p)
        routed_s, st_s = _sample_layer(xs, mod_s, lp, p, cache_k[l], cache_v[l], cache_k_idx[l], state_conv[l],
                                       state_ssm[l], page_table)
        (xp, yp), (xs, ys) = _moe_and_norm(routed_p, routed_s, mod_p[5], mod_s[5], p, nf)
        outs_p.append(st_p)
        outs_s.append(st_s)

    def stack(outs, n_rows, lead):
        k = jnp.stack([o[0].reshape(lead + (n_rows, KV_HEADS, ATT_HEAD_DIM)) for o in outs])
        v = jnp.stack([o[1].reshape(lead + (n_rows, KV_HEADS, ATT_HEAD_DIM)) for o in outs])
        ki = jnp.stack([o[2].reshape(lead + (n_rows, IDX_DIM)) for o in outs])
        return k, v, ki

    k_p, v_p, ki_p = stack(outs_p, seq, (batch,))
    conv_p = jnp.stack([o[3][None] for o in outs_p])
    ssm_p = jnp.stack([o[4].reshape(batch, heads, SSD_HEAD_DIM, D_STATE) for o in outs_p])
    k_s = jnp.stack([o[0].reshape(dec_batch, dec_seq, KV_HEADS, ATT_HEAD_DIM) for o in outs_s])
    v_s = jnp.stack([o[1].reshape(dec_batch, dec_seq, KV_HEADS, ATT_HEAD_DIM) for o in outs_s])
    ki_s = jnp.stack([o[2].reshape(dec_batch, dec_seq, IDX_DIM) for o in outs_s])
    conv_s = jnp.stack([o[3] for o in outs_s])
    ssm_s = jnp.stack([o[4] for o in outs_s])
    return (yp.reshape(batch, seq, d), ys.reshape(dec_batch, dec_seq, d), k_p, v_p, ki_p, conv_p, ssm_p,
            k_s, v_s, ki_s, conv_s, ssm_s)
```

```python
import functools

import jax
import jax.numpy as jnp
from jax import lax
from jax.experimental import pallas as pl
from jax.experimental.pallas import tpu as pltpu

F32 = jnp.float32
BF16 = jnp.bfloat16
I32 = jnp.int32

SSD_HEAD_DIM = 64
SSD_GROUPS = 2
D_STATE = 128
CONV_W = 4
SSD_CHUNK = 128
ATT_HEAD_DIM = 128
KV_HEADS = 2
IDX_HEADS = 16
IDX_DIM = 64
TOPK_MAX = 256
N_EGROUPS = 4
EXPERTS_PER_GROUP = 8
N_EXPERTS = N_EGROUPS * EXPERTS_PER_GROUP
EPS = 1e-6

LANES = 128
NEG_BIG = -1e30
VMEM_LIMIT = 56 * 1024 * 1024
HIGHEST = lax.Precision.HIGHEST
ATTN_TILE = 256
TOKEN_TILE = 256
Q_SCALE = ATT_HEAD_DIM ** -0.5 * 1.4426950408889634


def _cparams(sem):
    return pltpu.CompilerParams(dimension_semantics=sem, vmem_limit_bytes=VMEM_LIMIT)


def _dot(a, b, precision=None):
    return jnp.dot(a, b, preferred_element_type=F32, precision=precision)


def _dot_nt(a, b, precision=None):
    return lax.dot_general(a, b, (((1,), (1,)), ((), ())), preferred_element_type=F32, precision=precision)


def _silu(x):
    return x * jax.nn.sigmoid(x)


def _softplus(x):
    return jnp.maximum(x, 0.0) + jnp.log(1.0 + jnp.exp(-jnp.abs(x)))


def _rms(x):
    return x * lax.rsqrt(jnp.mean(x * x, axis=-1, keepdims=True) + EPS)


def _pad_cols(a, width):
    return jnp.pad(a, ((0, 0), (0, width - a.shape[1])))


def _ada_kernel(c_ref, w_ref, b_ref, o_ref):
    s = _silu(c_ref[...]).astype(BF16)
    o_ref[...] = _dot(s, w_ref[...].astype(BF16)) + b_ref[...]


def _ada_mod(c_all, w_ada, b_ada):
    r, d = c_all.shape
    n = w_ada.shape[1]
    tn = 1024
    return pl.pallas_call(
        _ada_kernel,
        grid=(n // tn,),
        in_specs=[pl.BlockSpec((r, d), lambda j: (0, 0)),
                  pl.BlockSpec((d, tn), lambda j: (0, j)),
                  pl.BlockSpec((1, tn), lambda j: (0, j))],
        out_specs=pl.BlockSpec((r, tn), lambda j: (0, j)),
        out_shape=jax.ShapeDtypeStruct((r, n), F32),
        compiler_params=_cparams(("arbitrary",)),
    )(c_all, w_ada, b_ada.reshape(1, n))


def _in_layout(d_model):
    d_ssd = d_model // 2
    d_att = d_model - d_ssd
    conv_dim = d_ssd + 2 * SSD_GROUPS * D_STATE
    ssd_heads = d_ssd // SSD_HEAD_DIM
    sizes = dict(z=d_ssd, xbc=conv_dim, dt=ssd_heads, q=d_att, k=KV_HEADS * ATT_HEAD_DIM,
                 v=KV_HEADS * ATT_HEAD_DIM, qi=IDX_HEADS * IDX_DIM, ki=IDX_DIM, wi=IDX_HEADS)
    order = ("z", "xbc", "dt", "q", "k", "v", "qi", "ki", "wi")
    src, dst, off_s, off_d = {}, {}, 0, 0
    for name in order:
        w = sizes[name]
        wp = -(-w // LANES) * LANES
        src[name] = (off_s, w)
        dst[name] = (off_d, wp)
        off_s += w
        off_d += wp
    return order, src, dst, off_d


def _perm_w_in(w_in):
    order, src, dst, _ = _in_layout(w_in.shape[0])
    parts = [_pad_cols(w_in[:, src[n][0]:src[n][0] + src[n][1]], dst[n][1]) for n in order]
    return jnp.concatenate(parts, axis=1).astype(BF16)


def _inproj_kernel(seg, x_ref, nw_ref, sc_ref, sh_ref, w_ref, lnw_ref, lnb_ref,
                   z_ref, xbc_ref, dt_ref, q_ref, k_ref, v_ref, kb_ref, vb_ref, qi_ref, ki_ref, wi_ref):
    h = _rms(x_ref[...]) * nw_ref[...]
    h = h * (1.0 + sc_ref[...]) + sh_ref[...]
    hb = h.astype(BF16)

    def mm(name):
        a, w = seg[name]
        return _dot(hb, w_ref[:, a:a + w])

    z_ref[...] = mm("z")
    xbc_ref[...] = mm("xbc")
    dt_ref[...] = mm("dt")
    q_ref[...] = (mm("q") * Q_SCALE).astype(BF16)
    k = mm("k")
    k_ref[...] = k
    kb_ref[...] = k.astype(BF16)
    v = mm("v")
    v_ref[...] = v
    vb_ref[...] = v.astype(BF16)
    qi_ref[...] = mm("qi").astype(BF16)
    wi_ref[...] = mm("wi")
    ki = mm("ki")
    lane = lax.broadcasted_iota(I32, ki.shape, 1)
    ok = lane < IDX_DIM
    mu = jnp.sum(jnp.where(ok, ki, 0.0), axis=-1, keepdims=True) * (1.0 / IDX_DIM)
    cen = jnp.where(ok, ki - mu, 0.0)
    var = jnp.sum(cen * cen, axis=-1, keepdims=True) * (1.0 / IDX_DIM)
    y = cen * lax.rsqrt(var + EPS) * lnw_ref[...] + lnb_ref[...]
    ki_ref[...] = y[:, :IDX_DIM]


def _in_proj(x, nw, sc, sh, w_perm, lnw, lnb, tm):
    t, d = x.shape
    _, _, dst, npad = _in_layout(d)
    tmod = sc.shape[0]
    mod_map = (lambda i: (0, 0)) if tmod == 1 else (lambda i: (i, 0))
    mod_rows = 1 if tmod == 1 else tm
    row = lambda w: pl.BlockSpec((tm, w), lambda i: (i, 0))
    d_ssd, d_att = dst["z"][1], dst["q"][1]
    kvw = KV_HEADS * ATT_HEAD_DIM
    outs = [("z", d_ssd, F32), ("xbc", dst["xbc"][1], F32), ("dt", LANES, F32), ("q", d_att, BF16),
            ("k", kvw, F32), ("v", kvw, F32), ("kb", kvw, BF16), ("vb", kvw, BF16),
            ("qi", IDX_HEADS * IDX_DIM, BF16), ("ki", IDX_DIM, F32), ("wi", LANES, F32)]
    res = pl.pallas_call(
        functools.partial(_inproj_kernel, dst),
        grid=(t // tm,),
        in_specs=[row(d),
                  pl.BlockSpec((1, d), lambda i: (0, 0)),
                  pl.BlockSpec((mod_rows, d), mod_map),
                  pl.BlockSpec((mod_rows, d), mod_map),
                  pl.BlockSpec((d, npad), lambda i: (0, 0)),
                  pl.BlockSpec((1, LANES), lambda i: (0, 0)),
                  pl.BlockSpec((1, LANES), lambda i: (0, 0))],
        out_specs=[row(w) for _, w, _ in outs],
        out_shape=[jax.ShapeDtypeStruct((t, w), dt) for _, w, dt in outs],
        compiler_params=_cparams(("arbitrary",)),
    )(x, nw, sc, sh, w_perm, lnw, lnb)
    return dict(zip([n for n, _, _ in outs], res))


def _ssd_prompt_kernel(n_pairs, xbc_ref, dt_ref, z_ref, cw_ref, cb_ref, dtb_ref, alog_ref, dsk_ref, nw_ref,
                       y_ref, st_ref, xprev, ht, ybuf):
    c = pl.program_id(0)
    q = SSD_CHUNK
    d_ssd = n_pairs * LANES
    gn = SSD_GROUPS * D_STATE

    @pl.when(c == 0)
    def _():
        xprev[...] = jnp.zeros_like(xprev)
        ht[...] = jnp.zeros_like(ht)

    x = xbc_ref[...]
    xp = xprev[...]
    rowi = lax.broadcasted_iota(I32, (q, 1), 0)
    acc = cb_ref[...] + cw_ref[CONV_W - 1:CONV_W, :] * x
    for k in range(1, CONV_W):
        sh = jnp.where(rowi < k, pltpu.roll(xp, k, 0), pltpu.roll(x, k, 0))
        acc = acc + cw_ref[CONV_W - 1 - k:CONV_W - k, :] * sh
    xprev[...] = x
    xc = _silu(acc)

    dt = _softplus(dt_ref[...] + dtb_ref[...])
    a_neg = -jnp.exp(alog_ref[...])
    r2 = lax.broadcasted_iota(I32, (q, q), 0)
    c2 = lax.broadcasted_iota(I32, (q, q), 1)
    tril = c2 <= r2
    a = _dot(tril.astype(F32), dt * a_neg, precision=HIGHEST)
    a_t = a.T
    dt_t = dt.T
    a_last = a[q - 1:q, :]
    wmat = jnp.exp(a_last - a) * dt
    emat = jnp.exp(a)
    cd = jnp.exp(a_last)
    lane = lax.broadcasted_iota(I32, (q, LANES), 1)
    left = lane < SSD_HEAD_DIM
    pairs_per_group = n_pairs // SSD_GROUPS

    bts, cbs, cgs = [], [], []
    for g in range(SSD_GROUPS):
        bg = xc[:, d_ssd + g * D_STATE:d_ssd + (g + 1) * D_STATE]
        cg = xc[:, d_ssd + gn + g * D_STATE:d_ssd + gn + (g + 1) * D_STATE].astype(BF16)
        bt = bg.T.astype(BF16)
        bts.append(bt)
        cgs.append(cg)
        cbs.append(_dot(cg, bt))

    def colb(m, h):
        return jnp.broadcast_to(m[:, h:h + 1], (q, LANES))

    for p in range(n_pairs):
        g = p // pairs_per_group
        h0, h1 = 2 * p, 2 * p + 1
        xpair = xc[:, p * LANES:(p + 1) * LANES]
        xpb = xpair.astype(BF16)
        yd = []
        for h in (h0, h1):
            diff = colb(a, h) - a_t[h:h + 1, :]
            decay = jnp.exp(jnp.where(tril, diff, -jnp.inf))
            sc = cbs[g] * decay * dt_t[h:h + 1, :]
            yd.append(_dot(sc.astype(BF16), xpb))
        y_diag = jnp.where(left, yd[0], yd[1])
        w_pair = jnp.where(left, colb(wmat, h0), colb(wmat, h1))
        e_pair = jnp.where(left, colb(emat, h0), colb(emat, h1))
        cd_pair = jnp.where(left[0:1, :], jnp.broadcast_to(cd[:, h0:h0 + 1], (1, LANES)),
                            jnp.broadcast_to(cd[:, h1:h1 + 1], (1, LANES)))
        hprev = ht[p]
        y_off = _dot(cgs[g], hprev.astype(BF16)) * e_pair
        states = _dot(bts[g], (xpair * w_pair).astype(BF16))
        ht[p] = hprev * cd_pair + states
        ybuf[:, p * LANES:(p + 1) * LANES] = y_diag + y_off + xpair * dsk_ref[:, p * LANES:(p + 1) * LANES]

    y = ybuf[...] * _silu(z_ref[...])
    y_ref[...] = (_rms(y) * nw_ref[...]).astype(BF16)

    @pl.when(c == pl.num_programs(0) - 1)
    def _():
        for p in range(n_pairs):
            st_ref[p * LANES:(p + 1) * LANES, :] = ht[p].T


def _ssd_prompt(xbc, dt_raw, z, conv_w, conv_b, dt_bias_p, a_log_p, dskip_row, norm_w):
    t, conv_dim = xbc.shape
    d_ssd = z.shape[1]
    n_pairs = d_ssd // LANES
    q = SSD_CHUNK
    full = lambda a: pl.BlockSpec(a.shape, lambda c: (0, 0))
    return pl.pallas_call(
        functools.partial(_ssd_prompt_kernel, n_pairs),
        grid=(t // q,),
        in_specs=[pl.BlockSpec((q, conv_dim), lambda c: (c, 0)),
                  pl.BlockSpec((q, LANES), lambda c: (c, 0)),
                  pl.BlockSpec((q, d_ssd), lambda c: (c, 0)),
                  full(conv_w), full(conv_b), full(dt_bias_p), full(a_log_p), full(dskip_row), full(norm_w)],
        out_specs=[pl.BlockSpec((q, d_ssd), lambda c: (c, 0)),
                   pl.BlockSpec((d_ssd, D_STATE), lambda c: (0, 0))],
        out_shape=[jax.ShapeDtypeStruct((t, d_ssd), BF16),
                   jax.ShapeDtypeStruct((d_ssd, D_STATE), F32)],
        scratch_shapes=[pltpu.VMEM((q, conv_dim), F32),
                        pltpu.VMEM((n_pairs, D_STATE, LANES), F32),
                        pltpu.VMEM((q, d_ssd), F32)],
        compiler_params=_cparams(("arbitrary",)),
    )(xbc, dt_raw, z, conv_w, conv_b, dt_bias_p, a_log_p, dskip_row, norm_w)


ROW_SUB = 128
FOLD_CHUNKS = 4
BISECT_CAP = 320
SAFE_SHIFT = 40.0


def _attn_prompt_kernel(topk, tq, q_ref, qi_ref, wi_ref, ki2t_ref, kbt_ref, vx_ref, nw_ref, o_ref,
                        sc, wb, thr_b, mrun, acc_scr, kmax):
    i = pl.program_id(0)
    kc = tq
    n_chunks = i + 1
    n_heads = q_ref.shape[1] // ATT_HEAD_DIM
    q_per_kv = n_heads // KV_HEADS
    wscale = (IDX_DIM ** -0.5) * (IDX_HEADS ** -0.5)
    n_sub = tq // ROW_SUB

    @pl.when(i == 0)
    def _():
        def norms(j, best):
            kf32 = kbt_ref[:, pl.ds(j * kc, kc)].astype(F32)
            return tuple(jnp.maximum(best[g], jnp.max(jnp.sum(
                jnp.square(kf32[g * ATT_HEAD_DIM:(g + 1) * ATT_HEAD_DIM]), axis=0, keepdims=True)))
                for g in range(KV_HEADS))

        best = lax.fori_loop(0, kbt_ref.shape[1] // kc, norms, (jnp.float32(0.0),) * KV_HEADS)
        for g in range(KV_HEADS):
            kmax[g] = best[g]

    wi = wi_ref[...] * wscale
    for h in range(IDX_HEADS):
        wb[h] = jnp.broadcast_to(wi[:, h:h + 1], (tq, LANES))

    def wide(x):
        return jnp.concatenate([x] * (kc // LANES), axis=1)

    row_g = i * tq + lax.broadcasted_iota(I32, (tq, kc), 0)
    col_l = lax.broadcasted_iota(I32, (tq, kc), 1)

    def score_chunk(j, carry):
        k0 = ki2t_ref[0, :, pl.ds(j * kc, kc)]
        k1 = ki2t_ref[1, :, pl.ds(j * kc, kc)]
        s = jnp.zeros((tq, kc), F32)
        for p in range(IDX_HEADS // 2):
            qp = qi_ref[:, p * LANES:(p + 1) * LANES]
            s = s + jnp.maximum(_dot(qp, k0), 0.0) * wide(wb[2 * p])
            s = s + jnp.maximum(_dot(qp, k1), 0.0) * wide(wb[2 * p + 1])
        sc[j] = jnp.where(j * kc + col_l <= row_g, s, -jnp.inf)
        return carry

    lax.fori_loop(0, n_chunks, score_chunk, 0)
    for extra in range(FOLD_CHUNKS - 1):
        sc[n_chunks + extra] = jnp.full((tq, kc), -jnp.inf, F32)
    n_steps = (n_chunks + FOLD_CHUNKS - 1) // FOLD_CHUNKS

    def fold(fn, init):
        outs = []
        for r in range(n_sub):
            rows = slice(r * ROW_SUB, (r + 1) * ROW_SUB)

            def body(j, acc, rows=rows, r=r):
                for c in range(FOLD_CHUNKS):
                    for part in range(kc // LANES):
                        c0 = (j * FOLD_CHUNKS + c) * kc + part * LANES
                        acc = fn(acc, sc[j * FOLD_CHUNKS + c, rows, part * LANES:(part + 1) * LANES], c0, r)
                return acc

            outs.append(lax.fori_loop(0, n_steps, body, jax.tree.map(
                lambda v: jnp.full((ROW_SUB, LANES), v, F32), init)))
        return outs

    def spread(row):
        return [jnp.broadcast_to(row[:, r * ROW_SUB:(r + 1) * ROW_SUB], (ROW_SUB, ROW_SUB)).T for r in range(n_sub)]

    def collect(parts, op):
        return jnp.concatenate([op(p.T, axis=0, keepdims=True) for p in parts], axis=1)

    def count(pred):
        return collect(fold(lambda acc, blk, c0, r: acc + jnp.where(pred(blk, c0, r), 1.0, 0.0), 0.0), jnp.sum)

    def count_ge(t):
        tb = spread(t)
        return count(lambda blk, c0, r: blk >= tb[r])

    top2 = fold(lambda acc, blk, c0, r: (jnp.maximum(acc[0], blk), jnp.maximum(acc[1], jnp.minimum(acc[0], blk))),
                (-jnp.inf, -jnp.inf))
    fmin = jnp.float32(jnp.finfo(F32).min)
    lo0 = jnp.maximum(collect([p[1] for p in top2], jnp.min), fmin)
    hi0 = collect([p[1 if topk > LANES else 0] for p in top2], jnp.max)

    kf = jnp.float32(topk)
    n_valid = (i * tq + lax.broadcasted_iota(I32, (1, tq), 1) + 1).astype(F32)
    done0 = jnp.where(n_valid <= kf, 1.0, 0.0)

    def cond(st):
        it, lo, hi, thr, done, stalled = st
        return (it < BISECT_CAP) & (jnp.min(done) == 0.0)

    def halve(st):
        it, lo, hi, thr, done, stalled = st
        mid = 0.5 * lo + 0.5 * hi
        n = count_ge(mid)
        live = done == 0.0
        exact = live & (n == kf)
        stall = live & jnp.logical_not(exact) & ((mid <= lo) | (mid >= hi))
        move = live & jnp.logical_not(exact) & jnp.logical_not(stall)
        up = n >= kf
        return (it + 1,
                jnp.where(move & up, mid, lo),
                jnp.where(move & jnp.logical_not(up), mid, hi),
                jnp.where(exact, mid, thr),
                jnp.where(exact | stall, 1.0, done),
                jnp.where(stall, 1.0, stalled))

    st = lax.while_loop(cond, halve, (jnp.int32(0), lo0, hi0, jnp.full((1, tq), fmin), done0,
                                      jnp.zeros((1, tq), F32)))
    _, lo, hi, thr, _, stalled = st
    n_hi = count_ge(hi)
    thr = jnp.where(stalled == 1.0, jnp.where(n_hi >= kf, hi, lo), thr)
    n_ge = count_ge(thr)
    tb = spread(thr)

    @pl.when(jnp.max(n_ge) > kf)
    def _():
        n_gt = count(lambda blk, c0, r: blk > tb[r])
        need = kf - n_gt
        lane_i = lax.broadcasted_iota(I32, (ROW_SUB, LANES), 1)
        n_bits = max(int(sc.shape[0] * kc).bit_length(), 1)

        def idx_step(t, jlo):
            trial = jlo + jnp.left_shift(jnp.int32(1), n_bits - 1 - t).astype(F32)
            trb = spread(trial)
            f = count(lambda blk, c0, r: (blk == tb[r]) & ((c0 + lane_i).astype(F32) < trb[r]))
            return jnp.where(f <= need - 1.0, trial, jlo)

        jlo = lax.fori_loop(0, n_bits, idx_step, jnp.zeros((1, tq), F32))
        cut = spread(jnp.where(n_ge > kf, jlo + 1.0, jnp.float32(2 ** 30)))

        def drop(j, carry):
            for r in range(n_sub):
                rows = slice(r * ROW_SUB, (r + 1) * ROW_SUB)
                for part in range(kc // LANES):
                    cols = slice(part * LANES, (part + 1) * LANES)
                    blk = sc[j, rows, cols]
                    gone = (blk == tb[r]) & ((j * kc + part * LANES + lane_i).astype(F32) >= cut[r])
                    sc[j, rows, cols] = jnp.where(gone, -jnp.inf, blk)
            return carry

        lax.fori_loop(0, n_chunks, drop, 0)

    for r in range(n_sub):
        thr_b[r * ROW_SUB:(r + 1) * ROW_SUB, :] = jnp.concatenate([tb[r]] * (kc // LANES), axis=1)

    acc_scr[...] = jnp.zeros(acc_scr.shape, F32)
    bounds = []
    for h in range(n_heads):
        qf = q_ref[:, h * ATT_HEAD_DIM:(h + 1) * ATT_HEAD_DIM].astype(F32)
        qn = jnp.sqrt(jnp.sum(qf * qf, axis=-1, keepdims=True))
        bounds.append(qn * (jnp.sqrt(kmax[h // q_per_kv]) * 1.01))
    safe = functools.reduce(jnp.maximum, [jnp.max(b) for b in bounds]) <= SAFE_SHIFT
    vw = 2 * ATT_HEAD_DIM

    def halves(x):
        return [x[:, k * LANES:(k + 1) * LANES] for k in range(kc // LANES)]

    def logits(j, h):
        g = h // q_per_kv
        kj = kbt_ref[g * ATT_HEAD_DIM:(g + 1) * ATT_HEAD_DIM, pl.ds(j * kc, kc)]
        qh = q_ref[:, h * ATT_HEAD_DIM:(h + 1) * ATT_HEAD_DIM]
        return _dot(qh, kj)

    def max_chunk(j, carry):
        sel = sc[j] >= thr_b[...]
        for h in range(n_heads):
            lg = jnp.where(sel, logits(j, h), NEG_BIG)
            mrun[h] = functools.reduce(jnp.maximum, halves(lg), mrun[h])
        return carry

    @pl.when(safe)
    def _():
        for h in range(n_heads):
            mrun[h] = jnp.broadcast_to(bounds[h], (tq, LANES))

    @pl.when(jnp.logical_not(safe))
    def _():
        mrun[...] = jnp.full(mrun.shape, NEG_BIG, F32)
        lax.fori_loop(0, n_chunks, max_chunk, 0)
        for h in range(n_heads):
            mrun[h] = jnp.broadcast_to(jnp.max(mrun[h], axis=-1, keepdims=True), (tq, LANES))

    def sum_chunk(j, carry):
        sel = sc[j] >= thr_b[...]
        for h in range(n_heads):
            g = h // q_per_kv
            vj = vx_ref[pl.ds(j * kc, kc), g * vw:(g + 1) * vw]
            m = mrun[h]
            p = jnp.where(sel, jnp.exp2(logits(j, h) - jnp.concatenate([m] * (kc // LANES), axis=1)), 0.0)
            acc_scr[h] = acc_scr[h] + _dot(p.astype(BF16), vj)
        return carry

    lax.fori_loop(0, n_chunks, sum_chunk, 0)

    ss = jnp.zeros((tq, 1), F32)
    for h in range(n_heads):
        o = acc_scr[h, :, :ATT_HEAD_DIM] / acc_scr[h, :, ATT_HEAD_DIM:]
        acc_scr[h, :, :ATT_HEAD_DIM] = o
        ss = ss + jnp.sum(o * o, axis=-1, keepdims=True)
    inv = lax.rsqrt(ss * (1.0 / (n_heads * ATT_HEAD_DIM)) + EPS)
    for h in range(n_heads):
        sl = slice(h * ATT_HEAD_DIM, (h + 1) * ATT_HEAD_DIM)
        o_ref[:, sl] = (acc_scr[h, :, :ATT_HEAD_DIM] * inv * nw_ref[:, sl]).astype(BF16)


def _attn_prompt(q, qi, wi, ki2t, kbt, vx, norm_w, topk, tq):
    t, d_att = q.shape
    n_heads = d_att // ATT_HEAD_DIM
    full = lambda a: pl.BlockSpec(a.shape, lambda i: (0,) * a.ndim)
    return pl.pallas_call(
        functools.partial(_attn_prompt_kernel, topk, tq),
        grid=(t // tq,),
        in_specs=[pl.BlockSpec((tq, d_att), lambda i: (i, 0)),
                  pl.BlockSpec((tq, qi.shape[1]), lambda i: (i, 0)),
                  pl.BlockSpec((tq, LANES), lambda i: (i, 0)),
                  full(ki2t), full(kbt), full(vx), full(norm_w)],
        out_specs=pl.BlockSpec((tq, d_att), lambda i: (i, 0)),
        out_shape=jax.ShapeDtypeStruct((t, d_att), BF16),
        scratch_shapes=[pltpu.VMEM((t // tq + FOLD_CHUNKS, tq, tq), F32),
                        pltpu.VMEM((IDX_HEADS, tq, LANES), F32),
                        pltpu.VMEM((tq, tq), F32),
                        pltpu.VMEM((n_heads, tq, LANES), F32),
                        pltpu.VMEM((n_heads, tq, 2 * ATT_HEAD_DIM), F32),
                        pltpu.SMEM((KV_HEADS,), F32)],
        compiler_params=_cparams(("arbitrary",)),
    )(q, qi, wi, ki2t, kbt, vx, norm_w)


def _outproj_kernel(ya_ref, yb_ref, w_ref, x_ref, g1_ref, nw_ref, sc_ref, sh_ref, wr_ref, br_ref,
                    x1_ref, h2_ref, eid_ref, wts_ref, cnt_ref):
    d_a = ya_ref.shape[1]
    m = _dot(ya_ref[...], w_ref[:d_a, :]) + _dot(yb_ref[...], w_ref[d_a:, :])
    x1 = x_ref[...] + g1_ref[...] * m
    x1_ref[...] = x1
    h2 = _rms(x1) * nw_ref[...]
    h2 = h2 * (1.0 + sc_ref[...]) + sh_ref[...]
    h2_ref[...] = h2
    lg = _dot(h2.astype(BF16), wr_ref[...]) + br_ref[...]
    lane = lax.broadcasted_iota(I32, lg.shape, 1)
    big = jnp.int32(4 * LANES)

    def rmax(v):
        return jnp.max(v, axis=-1, keepdims=True)

    def rmin(v):
        return jnp.min(v, axis=-1, keepdims=True)

    def rsum(v):
        return jnp.sum(v, axis=-1, keepdims=True)

    is_g = (lane >= N_EXPERTS) & (lane < N_EXPERTS + N_EGROUPS)
    mg = rmax(jnp.where(is_g, lg, -jnp.inf))
    sg = rsum(jnp.where(is_g, jnp.exp(lg - mg), 0.0))
    gsel = rmin(jnp.where(is_g & (lg == mg), lane - N_EXPERTS, big))
    pgsel = 1.0 / sg
    in_grp = (lane < N_EXPERTS) & (jnp.right_shift(lane, EXPERTS_PER_GROUP.bit_length() - 1) == gsel)
    me = rmax(jnp.where(in_grp, lg, -jnp.inf))
    ee = jnp.where(in_grp, jnp.exp(lg - me), 0.0)
    pe = ee / rsum(ee)
    p1 = rmax(jnp.where(in_grp, pe, -1.0))
    i1 = rmin(jnp.where(in_grp & (pe == p1), lane, big))
    rem = in_grp & (lane != i1)
    p2 = rmax(jnp.where(rem, pe, -1.0))
    i2 = rmin(jnp.where(rem & (pe == p2), lane, big))
    den = p1 + p2
    eid_ref[...] = jnp.where(lane == 0, i1, jnp.where(lane == 1, i2, 0))
    wts_ref[...] = jnp.where(lane == 0, pgsel * p1 / den, jnp.where(lane == 1, pgsel * p2 / den, 0.0))

    @pl.when(pl.program_id(0) == 0)
    def _():
        cnt_ref[...] = jnp.zeros_like(cnt_ref)

    chosen = jnp.where((lane == i1) | (lane == i2), 1.0, 0.0)
    cnt_ref[...] += jnp.sum(chosen, axis=0, keepdims=True)


def _out_proj(ya, yb, w_out_b, x, g1, nw2, sc2, sh2, wr, br, tm):
    t, d = x.shape
    d_a = ya.shape[1]
    tmod = g1.shape[0]
    mod_map = (lambda i: (0, 0)) if tmod == 1 else (lambda i: (i, 0))
    mod_rows = 1 if tmod == 1 else tm
    modspec = pl.BlockSpec((mod_rows, d), mod_map)
    row = lambda w: pl.BlockSpec((tm, w), lambda i: (i, 0))
    full = lambda a: pl.BlockSpec(a.shape, lambda i: (0, 0))
    return pl.pallas_call(
        _outproj_kernel,
        grid=(t // tm,),
        in_specs=[row(d_a), row(yb.shape[1]), full(w_out_b), row(d), modspec, full(nw2), modspec, modspec,
                  full(wr), full(br)],
        out_specs=[row(d), row(d), row(LANES), row(LANES), pl.BlockSpec((1, LANES), lambda i: (0, 0))],
        out_shape=[jax.ShapeDtypeStruct((t, d), F32), jax.ShapeDtypeStruct((t, d), F32),
                   jax.ShapeDtypeStruct((t, LANES), I32), jax.ShapeDtypeStruct((t, LANES), F32),
                   jax.ShapeDtypeStruct((1, LANES), F32)],
        compiler_params=_cparams(("arbitrary",)),
    )(ya, yb, w_out_b, x, g1, nw2, sc2, sh2, wr, br)


MOE_TILE = 256


def _moe_pos_kernel(eid_ref, off_ref, pos_ref, carry):
    @pl.when(pl.program_id(0) == 0)
    def _():
        carry[...] = jnp.zeros_like(carry)

    eid = eid_ref[...]
    tm = eid.shape[0]
    i1, i2 = eid[:, 0:1], eid[:, 1:2]
    lane = lax.broadcasted_iota(I32, eid.shape, 1)
    chosen = jnp.where((lane == i1) | (lane == i2), 1.0, 0.0)
    r = lax.broadcasted_iota(I32, (tm, tm), 0)
    c = lax.broadcasted_iota(I32, (tm, tm), 1)
    earlier = _dot(jnp.where(c < r, 1.0, 0.0).astype(BF16), chosen.astype(BF16))
    row = earlier + carry[...] + off_ref[...]
    p1 = jnp.sum(jnp.where(lane == i1, row, 0.0), axis=-1, keepdims=True)
    p2 = jnp.sum(jnp.where(lane == i2, row, 0.0), axis=-1, keepdims=True)
    out = jnp.where(lane == 0, p1, jnp.where(lane == 1, p2, 0.0))
    pos_ref[...] = jnp.where(i1 >= 0, out, -1.0).astype(I32)
    carry[...] += jnp.sum(chosen, axis=0, keepdims=True)


def _moe_positions(eid_all, off_row, tm):
    t = eid_all.shape[0]
    return pl.pallas_call(
        _moe_pos_kernel,
        grid=(t // tm,),
        in_specs=[pl.BlockSpec((tm, LANES), lambda i: (i, 0)), pl.BlockSpec((1, LANES), lambda i: (0, 0))],
        out_specs=pl.BlockSpec((tm, LANES), lambda i: (i, 0)),
        out_shape=jax.ShapeDtypeStruct((t, LANES), I32),
        scratch_shapes=[pltpu.VMEM((1, LANES), F32)],
        compiler_params=_cparams(("arbitrary",)),
    )(eid_all, off_row)


def _dyn_loop(lo, hi, fn, unroll=4):
    shift = unroll.bit_length() - 1
    n_blk = jnp.right_shift(hi - lo, shift)

    def blk(k, carry):
        for u in range(unroll):
            fn(lo + k * unroll + u)
        return carry

    def one(i, carry):
        fn(i)
        return carry

    lax.fori_loop(0, n_blk, blk, 0)
    lax.fori_loop(lo + n_blk * unroll, hi, one, 0)


def _moe_grouped_kernel(t_prompt, te_ref, nu_ref, np_ref, nv_ref, pos_ref, hp_ref, hs_ref, wu_ref, wd_ref, o_ref,
                        src, xbuf, wub, wdb, sem):
    g = pl.program_id(0)
    n_used = nu_ref[0]
    tmg = xbuf.shape[1]
    n_tok = pos_ref.shape[0] // 2
    n_rows = o_ref.shape[0] * pl.num_programs(0)
    slot = g % 2

    def row_copy(h_ref, tok, sl, r):
        return pltpu.make_async_copy(h_ref.at[pl.ds(tok, 1)], xbuf.at[sl, pl.ds(r, 1)], sem.at[sl])

    def gather_start(tile, sl):
        base = tile * tmg
        _dyn_loop(0, np_ref[tile], lambda r: row_copy(hp_ref, src[base + r], sl, r).start())
        _dyn_loop(np_ref[tile], nv_ref[tile], lambda r: row_copy(hs_ref, src[base + r] - t_prompt, sl, r).start())

    def gather_wait(tile, sl):
        _dyn_loop(0, nv_ref[tile], lambda r: row_copy(hp_ref, 0, sl, r).wait())

    @pl.when(g == 0)
    def _():
        def fill(t, carry):
            for k in range(2):
                p = pos_ref[2 * t + k]
                src[jnp.where(p < 0, n_rows, p)] = t
            return carry

        lax.fori_loop(0, n_tok, fill, 0, unroll=4)
        xbuf[...] = jnp.zeros_like(xbuf)
        gather_start(0, 0)

    @pl.when(g < n_used)
    def _():
        @pl.when(g + 1 < n_used)
        def _():
            gather_start(g + 1, 1 - slot)

        gather_wait(g, slot)
        fresh = (g == 0) | (te_ref[g] != te_ref[jnp.maximum(g - 1, 0)])

        @pl.when(fresh)
        def _():
            wub[...] = wu_ref[0].astype(BF16)
            wdb[...] = wd_ref[0].astype(BF16)

        gu = _dot(xbuf[slot].astype(BF16), wub[...])
        de = gu.shape[1] // 2
        act = _silu(gu[:, :de]) * gu[:, de:]
        o_ref[...] = _dot(act.astype(BF16), wdb[...])

    @pl.when(g >= n_used)
    def _():
        o_ref[...] = jnp.zeros_like(o_ref)


def _moe_grouped(plan, pos_flat, h2_p, h2_s, w_up, w_down):
    tile_expert, n_used, tile_np, tile_nv = plan
    n_tiles = tile_expert.shape[0]
    t_prompt, d = h2_p.shape
    _, _, two_de = w_up.shape
    tmg = MOE_TILE
    wmap = lambda g, te, nu, tp, tv, ps: (te[g], 0, 0)
    gs = pltpu.PrefetchScalarGridSpec(
        num_scalar_prefetch=5,
        grid=(n_tiles,),
        in_specs=[pl.BlockSpec(memory_space=pl.ANY), pl.BlockSpec(memory_space=pl.ANY),
                  pl.BlockSpec((1, d, two_de), wmap),
                  pl.BlockSpec((1, two_de // 2, d), wmap)],
        out_specs=pl.BlockSpec((tmg, d), lambda g, te, nu, tp, tv, ps: (g, 0)),
        scratch_shapes=[pltpu.SMEM((n_tiles * tmg + 8,), I32),
                        pltpu.VMEM((2, tmg, d), F32),
                        pltpu.VMEM((d, two_de), BF16),
                        pltpu.VMEM((two_de // 2, d), BF16),
                        pltpu.SemaphoreType.DMA((2,))],
    )
    return pl.pallas_call(
        functools.partial(_moe_grouped_kernel, t_prompt),
        grid_spec=gs,
        out_shape=jax.ShapeDtypeStruct((n_tiles * tmg, d), F32),
        compiler_params=_cparams(("arbitrary",)),
    )(tile_expert, n_used, tile_np, tile_nv, pos_flat, h2_p, h2_s, w_up, w_down)


def _combine_kernel(tok0, pos_ref, x1_ref, wts_ref, g2_ref, nf_ref, ys_ref, xo_ref, o_ref, ybuf, sem):
    i = pl.program_id(0)
    n = pl.num_programs(0)
    tm = x1_ref.shape[0]
    slot = i % 2

    def gather(tile, sl, wait):
        def body(r, carry):
            tok = tok0 + tile * tm + r
            for k in range(2):
                src_row = 0 if wait else pos_ref[2 * tok + k]
                cp = pltpu.make_async_copy(ys_ref.at[pl.ds(src_row, 1)], ybuf.at[sl, k, pl.ds(r, 1)], sem.at[sl])
                cp.wait() if wait else cp.start()
            return carry

        lax.fori_loop(0, tm, body, 0, unroll=8)

    @pl.when(i == 0)
    def _():
        gather(0, 0, False)

    @pl.when(i + 1 < n)
    def _():
        gather(i + 1, 1 - slot, False)

    gather(i, slot, True)
    w = wts_ref[...]
    y = w[:, 0:1] * ybuf[slot, 0] + w[:, 1:2] * ybuf[slot, 1]
    x2 = x1_ref[...] + g2_ref[...] * y
    xo_ref[...] = x2
    o_ref[...] = _rms(x2) * nf_ref[...]


def _combine(pos_flat, x1, wts, g2, nf, ys, tok0, tm):
    t, d = x1.shape
    tmod = g2.shape[0]
    mod_map = (lambda i, ps: (0, 0)) if tmod == 1 else (lambda i, ps: (i, 0))
    row = lambda w: pl.BlockSpec((tm, w), lambda i, ps: (i, 0))
    gs = pltpu.PrefetchScalarGridSpec(
        num_scalar_prefetch=1,
        grid=(t // tm,),
        in_specs=[row(d), row(LANES), pl.BlockSpec((1 if tmod == 1 else tm, d), mod_map),
                  pl.BlockSpec((1, d), lambda i, ps: (0, 0)), pl.BlockSpec(memory_space=pl.ANY)],
        out_specs=[row(d), row(d)],
        scratch_shapes=[pltpu.VMEM((2, 2, tm, d), F32), pltpu.SemaphoreType.DMA((2,))],
    )
    return pl.pallas_call(
        functools.partial(_combine_kernel, tok0),
        grid_spec=gs,
        out_shape=[jax.ShapeDtypeStruct((t, d), F32), jax.ShapeDtypeStruct((t, d), F32)],
        compiler_params=_cparams(("arbitrary",)),
    )(pos_flat, x1, wts, g2, nf, ys)


def _moe_plan(cnt_p, cnt_s, n_tiles):
    cp = cnt_p[0, :N_EXPERTS].astype(I32)
    cnt = cp + cnt_s[0, :N_EXPERTS].astype(I32)
    padded = (cnt + MOE_TILE - 1) // MOE_TILE * MOE_TILE
    ends = jnp.cumsum(padded)
    off = ends - padded
    off_row = _pad_cols(off.astype(F32).reshape(1, N_EXPERTS), LANES)
    starts = jnp.arange(n_tiles, dtype=I32) * MOE_TILE
    te = jnp.minimum(jnp.sum(starts[:, None] >= ends[None, :], axis=1), N_EXPERTS - 1).astype(I32)
    tile_np = jnp.clip(off[te] + cp[te] - starts, 0, MOE_TILE).astype(I32)
    tile_nv = jnp.clip(off[te] + cnt[te] - starts, 0, MOE_TILE).astype(I32)
    return off_row, (te, (ends[-1:] // MOE_TILE).astype(I32), tile_np, tile_nv)


def _ssd_prep_kernel(xbc_ref, p0_ref, p1_ref, p2_ref, cw_ref, cb_ref, dt_ref, dtb_ref, alog_ref, ex_ref,
                     xc_ref, xdt_ref, dec_ref):
    d_ssd = xdt_ref.shape[1]
    acc = (cb_ref[...] + cw_ref[0:1, :] * p0_ref[...] + cw_ref[1:2, :] * p1_ref[...]
           + cw_ref[2:3, :] * p2_ref[...] + cw_ref[3:4, :] * xbc_ref[...])
    xc = _silu(acc)
    xc_ref[...] = xc
    dt = _softplus(dt_ref[...] + dtb_ref[...])
    dec = jnp.exp(dt * (-jnp.exp(alog_ref[...])))
    xdt_ref[...] = _dot(dt, ex_ref[...], precision=HIGHEST) * xc[:, :d_ssd]
    dec_ref[...] = _dot(dec, ex_ref[...], precision=HIGHEST)


def _ssd_prep(xbc, p0, p1, p2, conv_w, conv_b, dt_raw, dt_bias_p, a_log_p, expand):
    b, conv_dim = xbc.shape
    d_ssd = expand.shape[1]
    args = (xbc, p0, p1, p2, conv_w, conv_b, dt_raw, dt_bias_p, a_log_p, expand)
    return pl.pallas_call(
        _ssd_prep_kernel,
        grid=(1,),
        in_specs=[pl.BlockSpec(a.shape, lambda i: (0, 0)) for a in args],
        out_specs=[pl.BlockSpec((b, conv_dim), lambda i: (0, 0)),
                   pl.BlockSpec((b, d_ssd), lambda i: (0, 0)),
                   pl.BlockSpec((b, d_ssd), lambda i: (0, 0))],
        out_shape=[jax.ShapeDtypeStruct((b, conv_dim), F32), jax.ShapeDtypeStruct((b, d_ssd), F32),
                   jax.ShapeDtypeStruct((b, d_ssd), F32)],
        compiler_params=_cparams(("arbitrary",)),
    )(*args)


def _ssd_step_kernel(n_pairs, xdt_ref, dec_ref, bm_ref, cm_ref, s_ref, so_ref, y_ref):
    pairs_per_group = n_pairs // SSD_GROUPS
    rows_per_pair = LANES // SSD_HEAD_DIM

    def on_rows(row):
        return jnp.broadcast_to(row, (LANES, LANES)).T

    for p in range(n_pairs):
        g = p // pairs_per_group
        sl = slice(p * LANES, (p + 1) * LANES)
        hs = slice(p * rows_per_pair, (p + 1) * rows_per_pair)
        hb = s_ref[0, 0, hs].reshape(LANES, D_STATE)
        bmat = jnp.broadcast_to(bm_ref[0, :, g * D_STATE:(g + 1) * D_STATE], (LANES, D_STATE))
        cmat = jnp.broadcast_to(cm_ref[0, :, g * D_STATE:(g + 1) * D_STATE], (LANES, D_STATE))
        hn = hb * on_rows(dec_ref[0, :, sl]) + on_rows(xdt_ref[0, :, sl]) * bmat
        so_ref[0, 0, hs] = hn.reshape(rows_per_pair, SSD_HEAD_DIM, D_STATE)
        y_ref[0, :, sl] = jnp.sum((hn * cmat).T, axis=0, keepdims=True)


def _ssd_step(xdt, dec, bm, cm, state):
    b, d_ssd = xdt.shape
    n_pairs = d_ssd // LANES
    heads = d_ssd // SSD_HEAD_DIM
    r3 = lambda a: a.reshape(b, 1, a.shape[1])
    row = lambda w: pl.BlockSpec((1, 1, w), lambda i: (i, 0, 0))
    sspec = pl.BlockSpec((1, 1, heads, SSD_HEAD_DIM, D_STATE), lambda i: (0, i, 0, 0, 0))
    so, y = pl.pallas_call(
        functools.partial(_ssd_step_kernel, n_pairs),
        grid=(b,),
        in_specs=[row(d_ssd), row(d_ssd), row(bm.shape[1]), row(cm.shape[1]), sspec],
        out_specs=[sspec, row(d_ssd)],
        out_shape=[jax.ShapeDtypeStruct(state.shape, F32), jax.ShapeDtypeStruct((b, 1, d_ssd), F32)],
        compiler_params=_cparams(("arbitrary",)),
    )(r3(xdt), r3(dec), r3(bm), r3(cm), state)
    return so, y.reshape(b, d_ssd)


def _ssd_finish_kernel(y_ref, xs_ref, z_ref, dsk_ref, nw_ref, o_ref):
    y = (y_ref[...] + xs_ref[...] * dsk_ref[...]) * _silu(z_ref[...])
    o_ref[...] = (_rms(y) * nw_ref[...]).astype(BF16)


def _ssd_finish(y, xs, z, dskip_row, norm_w):
    args = (y, xs, z, dskip_row, norm_w)
    return pl.pallas_call(
        _ssd_finish_kernel,
        grid=(1,),
        in_specs=[pl.BlockSpec(a.shape, lambda i: (0, 0)) for a in args],
        out_specs=pl.BlockSpec(y.shape, lambda i: (0, 0)),
        out_shape=jax.ShapeDtypeStruct(y.shape, BF16),
        compiler_params=_cparams(("arbitrary",)),
    )(*args)


PAGE_PACK = 8


def _page_copy(cache_ref, buf, sem, pt_ref, b, p, slot):
    rows = cache_ref.shape[2]
    return pltpu.make_async_copy(cache_ref.at[0, pt_ref[b, p]], buf.at[slot, pl.ds(p * rows, rows)], sem.at[slot])


def _score_sample_kernel(n_pages, pt_ref, q8_ref, w8_ref, qi_ref, wi_ref, kin_ref, cache_ref, s_ref, buf, sem):
    b = pl.program_id(0)
    nb = pl.num_programs(0)
    slot = b % 2

    def start(bb, sl):
        def body(p, carry):
            _page_copy(cache_ref, buf, sem, pt_ref, bb, p, sl).start()
            return carry
        lax.fori_loop(0, n_pages, body, 0)

    @pl.when(b == 0)
    def _():
        start(0, 0)

    @pl.when(b + 1 < nb)
    def _():
        start(b + 1, 1 - slot)

    def wait(p, carry):
        _page_copy(cache_ref, buf, sem, pt_ref, b, p, slot).wait()
        return carry

    lax.fori_loop(0, n_pages, wait, 0)

    wscale = (IDX_DIM ** -0.5) * (IDX_HEADS ** -0.5)
    q8 = q8_ref[0]
    w8 = w8_ref[0] * wscale
    kdim = q8.shape[1]
    page_rows = buf.shape[2]

    def group(gi, carry):
        keys_t = buf[slot, pl.ds(gi * kdim, kdim), :].astype(BF16)
        r = jnp.maximum(_dot(q8, keys_t), 0.0) * w8
        s_ref[0, pl.ds(gi * PAGE_PACK, PAGE_PACK), :] = jnp.sum(
            r.reshape(PAGE_PACK, IDX_HEADS, page_rows), axis=1)
        return carry

    lax.fori_loop(0, n_pages // PAGE_PACK, group, 0, unroll=4 if n_pages % (4 * PAGE_PACK) == 0 else 1)
    tail = s_ref.shape[1] - n_pages
    kn = jnp.broadcast_to(kin_ref[0], (page_rows, kin_ref.shape[2])).astype(BF16)
    dn = _dot_nt(qi_ref[0], kn)
    sn = jnp.sum(jnp.maximum(dn, 0.0) * (wi_ref[0] * wscale), axis=0, keepdims=True)
    r = lax.broadcasted_iota(I32, (tail, page_rows), 0)
    c = lax.broadcasted_iota(I32, (tail, page_rows), 1)
    s_ref[0, n_pages:, :] = jnp.where((r == 0) & (c == 0), jnp.broadcast_to(sn, (tail, page_rows)), -jnp.inf)


def _score_sample(page_table, q8, w8, qi3, wi3, ki_new3, cache_kit, tail_rows):
    b, n_pages = page_table.shape
    idx_dim, page_rows = cache_kit.shape[2], cache_kit.shape[3]
    blk = lambda a: pl.BlockSpec((1,) + a.shape[1:], lambda i, pt: (i, 0, 0))
    gs = pltpu.PrefetchScalarGridSpec(
        num_scalar_prefetch=1,
        grid=(b,),
        in_specs=[blk(q8), blk(w8), blk(qi3), blk(wi3), blk(ki_new3), pl.BlockSpec(memory_space=pl.ANY)],
        out_specs=pl.BlockSpec((1, n_pages + tail_rows, page_rows), lambda i, pt: (i, 0, 0)),
        scratch_shapes=[pltpu.VMEM((2, n_pages * idx_dim, page_rows), F32),
                        pltpu.SemaphoreType.DMA((2,))],
    )
    return pl.pallas_call(
        functools.partial(_score_sample_kernel, n_pages),
        grid_spec=gs,
        out_shape=jax.ShapeDtypeStruct((b, n_pages + tail_rows, page_rows), F32),
        compiler_params=_cparams(("arbitrary",)),
    )(page_table, q8, w8, qi3, wi3, ki_new3, cache_kit)


def _threshold_sample_kernel(topk, s_ref, thr_ref, cut_ref):
    nb, n = s_ref.shape
    n_blk = n // LANES
    kf = jnp.float32(topk)

    def fold(fn, init):
        def body(j, acc):
            return fn(acc, s_ref[:, pl.ds(pl.multiple_of(j * LANES, LANES), LANES)], j * LANES)
        return lax.fori_loop(0, n_blk, body, jnp.full((nb, LANES), init, F32), unroll=8)

    def count(pred):
        return jnp.sum(fold(lambda acc, blk, c0: acc + jnp.where(pred(blk, c0), 1.0, 0.0), 0.0), axis=-1, keepdims=True)

    def count_ge(t):
        tb = jnp.broadcast_to(t, (nb, LANES))
        return count(lambda blk, c0: blk >= tb)

    fmin = jnp.float32(jnp.finfo(F32).min)
    lo0 = jnp.min(fold(lambda acc, blk, c0: jnp.minimum(acc, jnp.where(blk == -jnp.inf, 3e38, blk)), 3e38),
                  axis=-1, keepdims=True)
    hi0 = jnp.max(fold(lambda acc, blk, c0: jnp.maximum(acc, blk), -jnp.inf), axis=-1, keepdims=True)
    n_valid = count(lambda blk, c0: blk > -jnp.inf)
    done0 = jnp.where(n_valid <= kf, 1.0, 0.0)

    def cond(st):
        it, lo, hi, thr, done, stalled = st
        return (it < BISECT_CAP) & (jnp.min(done) == 0.0)

    def halve(st):
        it, lo, hi, thr, done, stalled = st
        mid = 0.5 * lo + 0.5 * hi
        n = count_ge(mid)
        live = done == 0.0
        exact = live & (n == kf)
        stall = live & jnp.logical_not(exact) & ((mid <= lo) | (mid >= hi))
        move = live & jnp.logical_not(exact) & jnp.logical_not(stall)
        up = n >= kf
        return (it + 1,
                jnp.where(move & up, mid, lo),
                jnp.where(move & jnp.logical_not(up), mid, hi),
                jnp.where(exact, mid, thr),
                jnp.where(exact | stall, 1.0, done),
                jnp.where(stall, 1.0, stalled))

    st = lax.while_loop(cond, halve, (jnp.int32(0), lo0, hi0, jnp.minimum(lo0, fmin), done0, jnp.zeros((nb, 1), F32)))
    _, lo, hi, thr, _, stalled = st
    thr = jnp.where(stalled == 1.0, jnp.where(count_ge(hi) >= kf, hi, lo), thr)
    n_ge = count_ge(thr)
    thr_ref[...] = jnp.broadcast_to(thr, (nb, LANES))
    n_bits = max(int(n).bit_length(), 1)
    tb = jnp.broadcast_to(thr, (nb, LANES))
    lane = lax.broadcasted_iota(I32, (nb, LANES), 1)

    def tie_cut():
        need = kf - count(lambda blk, c0: blk > tb)

        def idx_step(t, jlo):
            trial = jlo + jnp.left_shift(jnp.int32(1), n_bits - 1 - t).astype(F32)
            trb = jnp.broadcast_to(trial, (nb, LANES))
            f = count(lambda blk, c0: (blk == tb) & ((c0 + lane).astype(F32) < trb))
            return jnp.where(f <= need - 1.0, trial, jlo)

        jlo = lax.fori_loop(0, n_bits, idx_step, jnp.zeros((nb, 1), F32))
        return jnp.where(n_ge > kf, jlo + 1.0, jnp.float32(2 ** 30))

    cut = lax.cond(jnp.max(n_ge) > kf, tie_cut, lambda: jnp.full((nb, 1), 2 ** 30, F32))
    cut_ref[...] = jnp.broadcast_to(cut, (nb, LANES))


def _threshold_sample(s2, topk):
    nb, n = s2.shape
    return pl.pallas_call(
        functools.partial(_threshold_sample_kernel, topk),
        grid=(1,),
        in_specs=[pl.BlockSpec((nb, n), lambda i: (0, 0))],
        out_specs=[pl.BlockSpec((nb, LANES), lambda i: (0, 0))] * 2,
        out_shape=[jax.ShapeDtypeStruct((nb, LANES), F32)] * 2,
        compiler_params=_cparams(("arbitrary",)),
    )(s2)


def _select_sample_kernel(topk, s_ref, thr_ref, cut_ref, idx_ref):
    s = s_ref[0]
    n_rows, width = s.shape
    pos = (lax.broadcasted_iota(I32, s.shape, 0) * width + lax.broadcasted_iota(I32, s.shape, 1)).astype(F32)
    thr = thr_ref[0]
    sel = (s > thr) | ((s == thr) & (pos < cut_ref[0]))
    self = jnp.where(sel, 1.0, 0.0).astype(BF16)
    ra = lax.broadcasted_iota(I32, (width, width), 0)
    ca = lax.broadcasted_iota(I32, (width, width), 1)
    local = jnp.where(sel, _dot(self, jnp.where(ra < ca, 1.0, 0.0).astype(BF16)), -1.0)
    cnt_row = _dot_nt(jnp.ones((8, width), BF16), self)
    rb = lax.broadcasted_iota(I32, (n_rows, n_rows), 0)
    cb = lax.broadcasted_iota(I32, (n_rows, n_rows), 1)
    end_row = _dot(cnt_row.astype(BF16), jnp.where(rb <= cb, 1.0, 0.0).astype(BF16))
    rank = lax.broadcasted_iota(I32, (topk, n_rows), 0).astype(F32)
    row_id = lax.broadcasted_iota(I32, (topk, n_rows), 1).astype(F32)
    passed = jnp.broadcast_to(end_row[0:1, :], (topk, n_rows)) <= rank
    row_of = jnp.sum(jnp.where(passed, 1.0, 0.0), axis=-1, keepdims=True)
    start = jnp.sum(jnp.where(passed, jnp.broadcast_to(cnt_row[0:1, :], (topk, n_rows)), 0.0), axis=-1,
                    keepdims=True)
    picked = _dot(jnp.where(row_id == row_of, 1.0, 0.0).astype(BF16), local.astype(BF16))
    lane = lax.broadcasted_iota(I32, (topk, width), 1).astype(F32)
    lane_of = jnp.sum(jnp.where(picked == rank[:, 0:1] - start, lane, 0.0), axis=-1, keepdims=True)
    idx_ref[0] = (row_of * width + lane_of).astype(I32)


def _select_sample(s3, thr, cut, topk):
    b, n_rows, width = s3.shape
    per_b = lambda a: pl.BlockSpec((1, 1, LANES), lambda i: (i, 0, 0))
    return pl.pallas_call(
        functools.partial(_select_sample_kernel, topk),
        grid=(b,),
        in_specs=[pl.BlockSpec((1, n_rows, width), lambda i: (i, 0, 0)), per_b(thr), per_b(cut)],
        out_specs=pl.BlockSpec((1, topk, 1), lambda i: (i, 0, 0)),
        out_shape=jax.ShapeDtypeStruct((b, topk, 1), I32),
        compiler_params=_cparams(("arbitrary",)),
    )(s3, thr.reshape(b, 1, LANES), cut.reshape(b, 1, LANES))


def _row_copy(src, dst, sem, src_row, dst_row):
    return pltpu.make_async_copy(src.at[pl.ds(src_row, KV_HEADS)], dst.at[pl.ds(dst_row, KV_HEADS)], sem)


def _attn_sample_kernel(topk, past_len, page_rows, n_pages, idx_ref, pt_ref, q_ref, nw_ref, ck_ref, cv_ref,
                        kn_ref, vn_ref, o_ref, kbuf, vbuf, sem):
    b = pl.program_id(0)

    pow2 = page_rows & (page_rows - 1) == 0

    def start(r, carry):
        j = jnp.minimum(idx_ref[b, r], past_len - 1)
        if pow2:
            page, off = jnp.right_shift(j, page_rows.bit_length() - 1), j & (page_rows - 1)
        else:
            page, off = j // page_rows, j % page_rows
        row = (pt_ref[b, page] * page_rows + off) * KV_HEADS
        _row_copy(ck_ref, kbuf, sem.at[0], row, r * KV_HEADS).start()
        _row_copy(cv_ref, vbuf, sem.at[1], row, r * KV_HEADS).start()
        return carry

    lax.fori_loop(0, topk, start, 0, unroll=8)

    def wait(r, carry):
        _row_copy(ck_ref, kbuf, sem.at[0], 0, r * KV_HEADS).wait()
        _row_copy(cv_ref, vbuf, sem.at[1], 0, r * KV_HEADS).wait()
        return carry

    lax.fori_loop(0, topk, wait, 0, unroll=8)

    @pl.when(idx_ref[b, topk - 1] >= past_len)
    def _():
        last = (topk - 1) * KV_HEADS
        for src_ref, buf, s in ((kn_ref, kbuf, sem.at[0]), (vn_ref, vbuf, sem.at[1])):
            cp = _row_copy(src_ref, buf, s, b * KV_HEADS, last)
            cp.start()
            cp.wait()

    outs = []
    ss = jnp.zeros((1, 1), F32)
    for g in range(KV_HEADS):
        kg = kbuf[pl.ds(g, topk, stride=KV_HEADS), :].astype(BF16)
        vg = vbuf[pl.ds(g, topk, stride=KV_HEADS), :].astype(BF16)
        lg = _dot_nt(q_ref[0, g], kg)
        m = jnp.max(lg, axis=-1, keepdims=True)
        p = jnp.exp2(lg - m)
        p = p / jnp.sum(p, axis=-1, keepdims=True)
        o = _dot(p.astype(BF16), vg)
        rows = lax.broadcasted_iota(I32, o.shape, 0)
        o = jnp.where(rows < q_ref.shape[2] // 2, o, 0.0)
        outs.append(o)
        ss = ss + jnp.sum(jnp.sum(o * o, axis=-1, keepdims=True), axis=0, keepdims=True)
    n_feat = KV_HEADS * (q_ref.shape[2] // 2) * ATT_HEAD_DIM
    inv = lax.rsqrt(ss * (1.0 / n_feat) + EPS)
    for g in range(KV_HEADS):
        o_ref[0, g] = (outs[g] * inv * nw_ref[g]).astype(BF16)


def _attn_sample(idx, page_table, q4, nw3, ck2, cv2, kn2, vn2, past_len, page_rows):
    b, topk = idx.shape
    n_pages = page_table.shape[1]
    gs = pltpu.PrefetchScalarGridSpec(
        num_scalar_prefetch=2,
        grid=(b,),
        in_specs=[pl.BlockSpec((1,) + q4.shape[1:], lambda i, a, c: (i, 0, 0, 0)),
                  pl.BlockSpec(nw3.shape, lambda i, a, c: (0, 0, 0)),
                  pl.BlockSpec(memory_space=pl.ANY), pl.BlockSpec(memory_space=pl.ANY),
                  pl.BlockSpec(memory_space=pl.ANY), pl.BlockSpec(memory_space=pl.ANY)],
        out_specs=pl.BlockSpec((1,) + q4.shape[1:], lambda i, a, c: (i, 0, 0, 0)),
        scratch_shapes=[pltpu.VMEM((topk * KV_HEADS, ATT_HEAD_DIM), F32),
                        pltpu.VMEM((topk * KV_HEADS, ATT_HEAD_DIM), F32),
                        pltpu.SemaphoreType.DMA((2,))],
    )
    return pl.pallas_call(
        functools.partial(_attn_sample_kernel, topk, past_len, page_rows, n_pages),
        grid_spec=gs,
        out_shape=jax.ShapeDtypeStruct(q4.shape, BF16),
        compiler_params=_cparams(("arbitrary",)),
    )(idx, page_table, q4, nw3, ck2, cv2, kn2, vn2)


def _row(v, width=None):
    v = v.reshape(1, -1)
    return v if width is None else _pad_cols(v, width)


def _layer_params(p):
    d = p["w_in"].shape[0]
    wr = jnp.concatenate([p["w_router_e"], p["w_router_g"]], axis=1)
    br = jnp.concatenate([p["b_router_e"], p["b_router_g"]])
    return dict(
        w_perm=_perm_w_in(p["w_in"]),
        w_out_b=p["w_out"].astype(BF16),
        wr=_pad_cols(wr, LANES).astype(BF16),
        br=_row(br, LANES),
        nw1=_row(p["norm1_w"]), nw2=_row(p["norm2_w"]),
        lnw=_row(p["ln_kidx_w"], LANES), lnb=_row(p["ln_kidx_b"], LANES),
        dt_bias=_row(p["dt_bias"], LANES), a_log=_row(p["a_log"], LANES),
        dskip=_row(jnp.repeat(p["d_skip"], SSD_HEAD_DIM)),
        norm_ssd=_row(p["norm_ssd_w"]), norm_att=_row(p["norm_att_w"]),
        conv_w=p["conv_w"], conv_b=_row(p["conv_b"]),
        d_ssd=d // 2,
    )


def _row_tile(t):
    return 512 if t % 512 == 0 else 256


def _route(x, ya, yb, mod, lp, tm):
    return _out_proj(ya, yb, lp["w_out_b"], x, mod[2], lp["nw2"], mod[4], mod[3], lp["wr"], lp["br"], tm)


def _moe_and_norm(routed_p, routed_s, g2_p, g2_s, p, nf):
    x1p, h2p, eidp, wtsp, cntp = routed_p
    x1s, h2s, eids, wtss, cnts = routed_s
    tp, ts = x1p.shape[0], x1s.shape[0]
    tt = tp + ts
    tpos = TOKEN_TILE
    tt_pad = -(-tt // tpos) * tpos
    eid_all = jnp.concatenate([eidp, eids, jnp.full((tt_pad - tt, LANES), -1, I32)])
    n_tiles = -(-(2 * tt + N_EXPERTS * (MOE_TILE - 1)) // MOE_TILE)
    off_row, plan = _moe_plan(cntp, cnts, n_tiles)
    pos_flat = _moe_positions(eid_all, off_row, tpos)[:, :2].reshape(-1)
    ys = _moe_grouped(plan, pos_flat, h2p, h2s, p["w_exp_up"], p["w_exp_down"])
    out_p = _combine(pos_flat, x1p, wtsp, g2_p, nf, ys, tok0=0, tm=TOKEN_TILE)
    out_s = _combine(pos_flat, x1s, wtss, g2_s, nf, ys, tok0=tp, tm=ts)
    return out_p, out_s


def _prompt_layer(x, mod, lp, p):
    t, d = x.shape
    pr = _in_proj(x, lp["nw1"], mod[1], mod[0], lp["w_perm"], lp["lnw"], lp["lnb"], tm=_row_tile(t))
    y_ssd, st = _ssd_prompt(pr["xbc"], pr["dt"], pr["z"], lp["conv_w"], lp["conv_b"], lp["dt_bias"], lp["a_log"],
                            lp["dskip"], lp["norm_ssd"])
    ki = pr["ki"]
    kit = ki.T.astype(BF16)
    zeros = jnp.zeros_like(kit)
    ki2t = jnp.stack([jnp.concatenate([kit, zeros], axis=0), jnp.concatenate([zeros, kit], axis=0)])
    topk = min(TOPK_MAX, t // 4)
    v3 = pr["vb"].reshape(t, KV_HEADS, ATT_HEAD_DIM)
    vx = jnp.concatenate([v3, jnp.ones_like(v3)], axis=-1).reshape(t, 2 * KV_HEADS * ATT_HEAD_DIM)
    y_att = _attn_prompt(pr["q"], pr["qi"], pr["wi"], ki2t, pr["kb"].T, vx, lp["norm_att"], topk, tq=ATTN_TILE)
    routed = _route(x, y_ssd, y_att, mod, lp, tm=_row_tile(t))
    conv_new = jnp.concatenate([jnp.zeros((CONV_W - 1, pr["xbc"].shape[1]), F32), pr["xbc"]])[-(CONV_W - 1):]
    return routed, (pr["k"], pr["v"], ki, conv_new, st)


def _sample_layer(x, mod, lp, p, cache_k, cache_v, cache_ki, conv_prev, ssm_prev, page_table):
    b, d = x.shape
    d_ssd = lp["d_ssd"]
    heads = d_ssd // SSD_HEAD_DIM
    gn = SSD_GROUPS * D_STATE
    pr = _in_proj(x, lp["nw1"], mod[1], mod[0], lp["w_perm"], lp["lnw"], lp["lnb"], tm=b)
    expand = (jnp.arange(LANES)[:, None] == (jnp.arange(d_ssd)[None, :] // SSD_HEAD_DIM)).astype(F32)
    xc, xdt, dec = _ssd_prep(pr["xbc"], conv_prev[:, 0], conv_prev[:, 1], conv_prev[:, 2], lp["conv_w"], lp["conv_b"],
                             pr["dt"], lp["dt_bias"], lp["a_log"], expand)
    xs, bm, cm = xc[:, :d_ssd], xc[:, d_ssd:d_ssd + gn], xc[:, d_ssd + gn:]
    st5 = ssm_prev.reshape((1, b, heads, SSD_HEAD_DIM, D_STATE))
    st_new, y = _ssd_step(xdt, dec, bm, cm, st5)
    y_ssd = _ssd_finish(y, xs, pr["z"], lp["dskip"], lp["norm_ssd"])
    conv_new = jnp.concatenate([conv_prev[:, 1:], pr["xbc"][:, None, :]], axis=1)
    n_pool, page_rows = cache_k.shape[0], cache_k.shape[1]
    n_pages = page_table.shape[1]
    past_len = n_pages * page_rows
    topk = min(TOPK_MAX, (past_len + 1) // 4)
    qi3 = pr["qi"].reshape(b, IDX_HEADS, IDX_DIM)
    wi3 = pr["wi"][:, :IDX_HEADS].reshape(b, IDX_HEADS, 1)
    eye = jnp.eye(PAGE_PACK, dtype=BF16)
    q8 = (eye[None, :, None, :, None] * qi3[:, None, :, None, :]).reshape(b, PAGE_PACK * IDX_HEADS,
                                                                         PAGE_PACK * IDX_DIM)
    w8 = jnp.tile(wi3, (1, PAGE_PACK, 1))
    tail_rows = -(n_pages + 1) % LANES + 1
    cache_kit = jnp.swapaxes(cache_ki, -1, -2)[None]
    s3 = _score_sample(page_table, q8, w8, qi3, wi3, pr["ki"].reshape(b, 1, IDX_DIM), cache_kit, tail_rows)
    thr, cut = _threshold_sample(s3.reshape(b, -1), topk)
    idx = _select_sample(s3, thr, cut, topk).reshape(b, topk)
    n_heads = pr["q"].shape[1] // ATT_HEAD_DIM
    q_per_kv = n_heads // KV_HEADS
    q4 = jnp.pad(pr["q"].reshape(b, KV_HEADS, q_per_kv, ATT_HEAD_DIM), ((0, 0), (0, 0), (0, q_per_kv), (0, 0)))
    nw3 = jnp.pad(lp["norm_att"].reshape(KV_HEADS, q_per_kv, ATT_HEAD_DIM), ((0, 0), (0, q_per_kv), (0, 0)))
    ck2 = cache_k.reshape(n_pool * page_rows * KV_HEADS, ATT_HEAD_DIM)
    cv2 = cache_v.reshape(n_pool * page_rows * KV_HEADS, ATT_HEAD_DIM)
    kn2 = pr["k"].reshape(b * KV_HEADS, ATT_HEAD_DIM)
    vn2 = pr["v"].reshape(b * KV_HEADS, ATT_HEAD_DIM)
    o4 = _attn_sample(idx, page_table, q4, nw3, ck2, cv2, kn2, vn2, past_len, page_rows)
    y_att = o4[:, :, :q_per_kv].reshape(b, n_heads * ATT_HEAD_DIM)
    routed = _route(x, y_ssd, y_att, mod, lp, tm=b)
    return routed, (pr["k"], pr["v"], pr["ki"], conv_new, st_new.reshape(ssm_prev.shape))


def kernel(x_prompt, x_sample, cache_k, cache_v, cache_k_idx, state_conv, state_ssm, page_table, c_prompt, c_sample, w_ada, b_ada, norm1_w, norm2_w, w_in, conv_w, conv_b, dt_bias, a_log, d_skip, norm_ssd_w, ln_kidx_w, ln_kidx_b, norm_att_w, w_out, w_router_g, b_router_g, w_router_e, b_router_e, w_exp_up, w_exp_down, norm_f_w):
    batch, seq, d = x_prompt.shape
    dec_batch, dec_seq, _ = x_sample.shape
    assert batch == 1 and dec_seq == 1, "one prompt sequence and one new token per sample sequence"
    depth = w_ada.shape[0]
    heads = (d // 2) // SSD_HEAD_DIM
    xp = x_prompt.reshape(seq, d)
    xs = x_sample.reshape(dec_batch, d)
    n_c = batch + dec_batch
    c_all = jnp.pad(jnp.concatenate([c_prompt, c_sample]), ((0, -n_c % 8), (0, 0)))
    nf = _row(norm_f_w)
    outs_p, outs_s = [], []
    yp = ys = None
    for l in range(depth):
        p = dict(w_in=w_in[l], conv_w=conv_w[l], conv_b=conv_b[l], dt_bias=dt_bias[l], a_log=a_log[l],
                 d_skip=d_skip[l], norm_ssd_w=norm_ssd_w[l], ln_kidx_w=ln_kidx_w[l], ln_kidx_b=ln_kidx_b[l],
                 norm_att_w=norm_att_w[l], w_out=w_out[l], w_router_g=w_router_g[l], b_router_g=b_router_g[l],
                 w_router_e=w_router_e[l], b_router_e=b_router_e[l], w_exp_up=w_exp_up[l],
                 w_exp_down=w_exp_down[l], norm1_w=norm1_w[l], norm2_w=norm2_w[l])
        lp = _layer_params(p)
        mod = _ada_mod(c_all, w_ada[l], b_ada[l])
        mod_p = [mod[0:1, k * d:(k + 1) * d] for k in range(6)]
        mod_s = [mod[batch:n_c, k * d:(k + 1) * d] for k in range(6)]
        routed_p, st_p = _prompt_layer(xp, mod_p, lp, p)
        routed_s, st_s = _sample_layer(xs, mod_s, lp, p, cache_k[l], cache_v[l], cache_k_idx[l], state_conv[l],
                                       state_ssm[l], page_table)
        (xp, yp), (xs, ys) = _moe_and_norm(routed_p, routed_s, mod_p[5], mod_s[5], p, nf)
        outs_p.append(st_p)
        outs_s.append(st_s)

    def stack(outs, n_rows, lead):
        k = jnp.stack([o[0].reshape(lead + (n_rows, KV_HEADS, ATT_HEAD_DIM)) for o in outs])
        v = jnp.stack([o[1].reshape(lead + (n_rows, KV_HEADS, ATT_HEAD_DIM)) for o in outs])
        ki = jnp.stack([o[2].reshape(lead + (n_rows, IDX_DIM)) for o in outs])
        return k, v, ki

    k_p, v_p, ki_p = stack(outs_p, seq, (batch,))
    conv_p = jnp.stack([o[3][None] for o in outs_p])
    ssm_p = jnp.stack([o[4].reshape(batch, heads, SSD_HEAD_DIM, D_STATE) for o in outs_p])
    k_s = jnp.stack([o[0].reshape(dec_batch, dec_seq, KV_HEADS, ATT_HEAD_DIM) for o in outs_s])
    v_s = jnp.stack([o[1].reshape(dec_batch, dec_seq, KV_HEADS, ATT_HEAD_DIM) for o in outs_s])
    ki_s = jnp.stack([o[2].reshape(dec_batch, dec_seq, IDX_DIM) for o in outs_s])
    conv_s = jnp.stack([o[3] for o in outs_s])
    ssm_s = jnp.stack([o[4] for o in outs_s])
    return (yp.reshape(batch, seq, d), ys.reshape(dec_batch, dec_seq, d), k_p, v_p, ki_p, conv_p, ssm_p,
            k_s, v_s, ki_s, conv_s, ssm_s)
```

```python
import functools

import jax
import jax.numpy as jnp
from jax import lax
from jax.experimental import pallas as pl
from jax.experimental.pallas import tpu as pltpu

F32 = jnp.float32
BF16 = jnp.bfloat16
I32 = jnp.int32

SSD_HEAD_DIM = 64
SSD_GROUPS = 2
D_STATE = 128
CONV_W = 4
SSD_CHUNK = 128
ATT_HEAD_DIM = 128
KV_HEADS = 2
IDX_HEADS = 16
IDX_DIM = 64
TOPK_MAX = 256
N_EGROUPS = 4
EXPERTS_PER_GROUP = 8
N_EXPERTS = N_EGROUPS * EXPERTS_PER_GROUP
EPS = 1e-6

LANES = 128
NEG_BIG = -1e30
VMEM_LIMIT = 56 * 1024 * 1024
HIGHEST = lax.Precision.HIGHEST
ATTN_TILE = 256
TOKEN_TILE = 256
Q_SCALE = ATT_HEAD_DIM ** -0.5 * 1.4426950408889634


def _cparams(sem):
    return pltpu.CompilerParams(dimension_semantics=sem, vmem_limit_bytes=VMEM_LIMIT)


def _dot(a, b, precision=None):
    return jnp.dot(a, b, preferred_element_type=F32, precision=precision)


def _dot_nt(a, b, precision=None):
    return lax.dot_general(a, b, (((1,), (1,)), ((), ())), preferred_element_type=F32, precision=precision)


def _silu(x):
    return x * jax.nn.sigmoid(x)


def _softplus(x):
    return jnp.maximum(x, 0.0) + jnp.log(1.0 + jnp.exp(-jnp.abs(x)))


def _rms(x):
    return x * lax.rsqrt(jnp.mean(x * x, axis=-1, keepdims=True) + EPS)


def _pad_cols(a, width):
    return jnp.pad(a, ((0, 0), (0, width - a.shape[1])))


def _ada_kernel(c_ref, w_ref, b_ref, o_ref):
    s = _silu(c_ref[...]).astype(BF16)
    o_ref[...] = _dot(s, w_ref[...].astype(BF16)) + b_ref[...]


def _ada_mod(c_all, w_ada, b_ada):
    r, d = c_all.shape
    n = w_ada.shape[1]
    tn = 1024
    return pl.pallas_call(
        _ada_kernel,
        grid=(n // tn,),
        in_specs=[pl.BlockSpec((r, d), lambda j: (0, 0)),
                  pl.BlockSpec((d, tn), lambda j: (0, j)),
                  pl.BlockSpec((1, tn), lambda j: (0, j))],
        out_specs=pl.BlockSpec((r, tn), lambda j: (0, j)),
        out_shape=jax.ShapeDtypeStruct((r, n), F32),
        compiler_params=_cparams(("arbitrary",)),
    )(c_all, w_ada, b_ada.reshape(1, n))


def _in_layout(d_model):
    d_ssd = d_model // 2
    d_att = d_model - d_ssd
    conv_dim = d_ssd + 2 * SSD_GROUPS * D_STATE
    ssd_heads = d_ssd // SSD_HEAD_DIM
    sizes = dict(z=d_ssd, xbc=conv_dim, dt=ssd_heads, q=d_att, k=KV_HEADS * ATT_HEAD_DIM,
                 v=KV_HEADS * ATT_HEAD_DIM, qi=IDX_HEADS * IDX_DIM, ki=IDX_DIM, wi=IDX_HEADS)
    order = ("z", "xbc", "dt", "q", "k", "v", "qi", "ki", "wi")
    src, dst, off_s, off_d = {}, {}, 0, 0
    for name in order:
        w = sizes[name]
        wp = -(-w // LANES) * LANES
        src[name] = (off_s, w)
        dst[name] = (off_d, wp)
        off_s += w
        off_d += wp
    return order, src, dst, off_d


def _perm_w_in(w_in):
    order, src, dst, _ = _in_layout(w_in.shape[0])
    parts = [_pad_cols(w_in[:, src[n][0]:src[n][0] + src[n][1]], dst[n][1]) for n in order]
    return jnp.concatenate(parts, axis=1).astype(BF16)


def _inproj_kernel(seg, x_ref, nw_ref, sc_ref, sh_ref, w_ref, lnw_ref, lnb_ref,
                   z_ref, xbc_ref, dt_ref, q_ref, k_ref, v_ref, kb_ref, vb_ref, qi_ref, ki_ref, wi_ref):
    h = _rms(x_ref[...]) * nw_ref[...]
    h = h * (1.0 + sc_ref[...]) + sh_ref[...]
    hb = h.astype(BF16)

    def mm(name):
        a, w = seg[name]
        return _dot(hb, w_ref[:, a:a + w])

    z_ref[...] = mm("z")
    xbc_ref[...] = mm("xbc")
    dt_ref[...] = mm("dt")
    q_ref[...] = (mm("q") * Q_SCALE).astype(BF16)
    k = mm("k")
    k_ref[...] = k
    kb_ref[...] = k.astype(BF16)
    v = mm("v")
    v_ref[...] = v
    vb_ref[...] = v.astype(BF16)
    qi_ref[...] = mm("qi").astype(BF16)
    wi_ref[...] = mm("wi")
    ki = mm("ki")
    lane = lax.broadcasted_iota(I32, ki.shape, 1)
    ok = lane < IDX_DIM
    mu = jnp.sum(jnp.where(ok, ki, 0.0), axis=-1, keepdims=True) * (1.0 / IDX_DIM)
    cen = jnp.where(ok, ki - mu, 0.0)
    var = jnp.sum(cen * cen, axis=-1, keepdims=True) * (1.0 / IDX_DIM)
    y = cen * lax.rsqrt(var + EPS) * lnw_ref[...] + lnb_ref[...]
    ki_ref[...] = y[:, :IDX_DIM]


def _in_proj(x, nw, sc, sh, w_perm, lnw, lnb, tm):
    t, d = x.shape
    _, _, dst, npad = _in_layout(d)
    tmod = sc.shape[0]
    mod_map = (lambda i: (0, 0)) if tmod == 1 else (lambda i: (i, 0))
    mod_rows = 1 if tmod == 1 else tm
    row = lambda w: pl.BlockSpec((tm, w), lambda i: (i, 0))
    d_ssd, d_att = dst["z"][1], dst["q"][1]
    kvw = KV_HEADS * ATT_HEAD_DIM
    outs = [("z", d_ssd, F32), ("xbc", dst["xbc"][1], F32), ("dt", LANES, F32), ("q", d_att, BF16),
            ("k", kvw, F32), ("v", kvw, F32), ("kb", kvw, BF16), ("vb", kvw, BF16),
            ("qi", IDX_HEADS * IDX_DIM, BF16), ("ki", IDX_DIM, F32), ("wi", LANES, F32)]
    res = pl.pallas_call(
        functools.partial(_inproj_kernel, dst),
        grid=(t // tm,),
        in_specs=[row(d),
                  pl.BlockSpec((1, d), lambda i: (0, 0)),
                  pl.BlockSpec((mod_rows, d), mod_map),
                  pl.BlockSpec((mod_rows, d), mod_map),
                  pl.BlockSpec((d, npad), lambda i: (0, 0)),
                  pl.BlockSpec((1, LANES), lambda i: (0, 0)),
                  pl.BlockSpec((1, LANES), lambda i: (0, 0))],
        out_specs=[row(w) for _, w, _ in outs],
        out_shape=[jax.ShapeDtypeStruct((t, w), dt) for _, w, dt in outs],
        compiler_params=_cparams(("arbitrary",)),
    )(x, nw, sc, sh, w_perm, lnw, lnb)
    return dict(zip([n for n, _, _ in outs], res))


def _ssd_prompt_kernel(n_pairs, xbc_ref, dt_ref, z_ref, cw_ref, cb_ref, dtb_ref, alog_ref, dsk_ref, nw_ref,
                       y_ref, st_ref, xprev, ht, ybuf):
    c = pl.program_id(0)
    q = SSD_CHUNK
    d_ssd = n_pairs * LANES
    gn = SSD_GROUPS * D_STATE

    @pl.when(c == 0)
    def _():
        xprev[...] = jnp.zeros_like(xprev)
        ht[...] = jnp.zeros_like(ht)

    x = xbc_ref[...]
    xp = xprev[...]
    rowi = lax.broadcasted_iota(I32, (q, 1), 0)
    acc = cb_ref[...] + cw_ref[CONV_W - 1:CONV_W, :] * x
    for k in range(1, CONV_W):
        sh = jnp.where(rowi < k, pltpu.roll(xp, k, 0), pltpu.roll(x, k, 0))
        acc = acc + cw_ref[CONV_W - 1 - k:CONV_W - k, :] * sh
    xprev[...] = x
    xc = _silu(acc)

    dt = _softplus(dt_ref[...] + dtb_ref[...])
    a_neg = -jnp.exp(alog_ref[...])
    r2 = lax.broadcasted_iota(I32, (q, q), 0)
    c2 = lax.broadcasted_iota(I32, (q, q), 1)
    tril = c2 <= r2
    a = _dot(tril.astype(F32), dt * a_neg, precision=HIGHEST)
    a_t = a.T
    dt_t = dt.T
    a_last = a[q - 1:q, :]
    wmat = jnp.exp(a_last - a) * dt
    emat = jnp.exp(a)
    cd = jnp.exp(a_last)
    lane = lax.broadcasted_iota(I32, (q, LANES), 1)
    left = lane < SSD_HEAD_DIM
    pairs_per_group = n_pairs // SSD_GROUPS

    bts, cbs, cgs = [], [], []
    for g in range(SSD_GROUPS):
        bg = xc[:, d_ssd + g * D_STATE:d_ssd + (g + 1) * D_STATE]
        cg = xc[:, d_ssd + gn + g * D_STATE:d_ssd + gn + (g + 1) * D_STATE].astype(BF16)
        bt = bg.T.astype(BF16)
        bts.append(bt)
        cgs.append(cg)
        cbs.append(_dot(cg, bt))

    def colb(m, h):
        return jnp.broadcast_to(m[:, h:h + 1], (q, LANES))

    for p in range(n_pairs):
        g = p // pairs_per_group
        h0, h1 = 2 * p, 2 * p + 1
        xpair = xc[:, p * LANES:(p + 1) * LANES]
        xpb = xpair.astype(BF16)
        yd = []
        for h in (h0, h1):
            diff = colb(a, h) - a_t[h:h + 1, :]
            decay = jnp.exp(jnp.where(tril, diff, -jnp.inf))
            sc = cbs[g] * decay * dt_t[h:h + 1, :]
            yd.append(_dot(sc.astype(BF16), xpb))
        y_diag = jnp.where(left, yd[0], yd[1])
        w_pair = jnp.where(left, colb(wmat, h0), colb(wmat, h1))
        e_pair = jnp.where(left, colb(emat, h0), colb(emat, h1))
        cd_pair = jnp.where(left[0:1, :], jnp.broadcast_to(cd[:, h0:h0 + 1], (1, LANES)),
                            jnp.broadcast_to(cd[:, h1:h1 + 1], (1, LANES)))
        hprev = ht[p]
        y_off = _dot(cgs[g], hprev.astype(BF16)) * e_pair
        states = _dot(bts[g], (xpair * w_pair).astype(BF16))
        ht[p] = hprev * cd_pair + states
        ybuf[:, p * LANES:(p + 1) * LANES] = y_diag + y_off + xpair * dsk_ref[:, p * LANES:(p + 1) * LANES]

    y = ybuf[...] * _silu(z_ref[...])
    y_ref[...] = (_rms(y) * nw_ref[...]).astype(BF16)

    @pl.when(c == pl.num_programs(0) - 1)
    def _():
        for p in range(n_pairs):
            st_ref[p * LANES:(p + 1) * LANES, :] = ht[p].T


def _ssd_prompt(xbc, dt_raw, z, conv_w, conv_b, dt_bias_p, a_log_p, dskip_row, norm_w):
    t, conv_dim = xbc.shape
    d_ssd = z.shape[1]
    n_pairs = d_ssd // LANES
    q = SSD_CHUNK
    full = lambda a: pl.BlockSpec(a.shape, lambda c: (0, 0))
    return pl.pallas_call(
        functools.partial(_ssd_prompt_kernel, n_pairs),
        grid=(t // q,),
        in_specs=[pl.BlockSpec((q, conv_dim), lambda c: (c, 0)),
                  pl.BlockSpec((q, LANES), lambda c: (c, 0)),
                  pl.BlockSpec((q, d_ssd), lambda c: (c, 0)),
                  full(conv_w), full(conv_b), full(dt_bias_p), full(a_log_p), full(dskip_row), full(norm_w)],
        out_specs=[pl.BlockSpec((q, d_ssd), lambda c: (c, 0)),
                   pl.BlockSpec((d_ssd, D_STATE), lambda c: (0, 0))],
        out_shape=[jax.ShapeDtypeStruct((t, d_ssd), BF16),
                   jax.ShapeDtypeStruct((d_ssd, D_STATE), F32)],
        scratch_shapes=[pltpu.VMEM((q, conv_dim), F32),
                        pltpu.VMEM((n_pairs, D_STATE, LANES), F32),
                        pltpu.VMEM((q, d_ssd), F32)],
        compiler_params=_cparams(("arbitrary",)),
    )(xbc, dt_raw, z, conv_w, conv_b, dt_bias_p, a_log_p, dskip_row, norm_w)


ROW_SUB = 128
FOLD_CHUNKS = 4
BISECT_CAP = 320
SAFE_SHIFT = 40.0


def _attn_prompt_kernel(topk, tq, q_ref, qi_ref, wi_ref, ki2t_ref, kbt_ref, vx_ref, nw_ref, o_ref,
                        sc, wb, thr_b, mrun, acc_scr, kmax):
    i = pl.program_id(0)
    kc = tq
    n_chunks = i + 1
    n_heads = q_ref.shape[1] // ATT_HEAD_DIM
    q_per_kv = n_heads // KV_HEADS
    wscale = (IDX_DIM ** -0.5) * (IDX_HEADS ** -0.5)
    n_sub = tq // ROW_SUB

    @pl.when(i == 0)
    def _():
        def norms(j, best):
            kf32 = kbt_ref[:, pl.ds(j * kc, kc)].astype(F32)
            return tuple(jnp.maximum(best[g], jnp.max(jnp.sum(
                jnp.square(kf32[g * ATT_HEAD_DIM:(g + 1) * ATT_HEAD_DIM]), axis=0, keepdims=True)))
                for g in range(KV_HEADS))

        best = lax.fori_loop(0, kbt_ref.shape[1] // kc, norms, (jnp.float32(0.0),) * KV_HEADS)
        for g in range(KV_HEADS):
            kmax[g] = best[g]

    wi = wi_ref[...] * wscale
    for h in range(IDX_HEADS):
        wb[h] = jnp.broadcast_to(wi[:, h:h + 1], (tq, LANES))

    def wide(x):
        return jnp.concatenate([x] * (kc // LANES), axis=1)

    row_g = i * tq + lax.broadcasted_iota(I32, (tq, kc), 0)
    col_l = lax.broadcasted_iota(I32, (tq, kc), 1)

    def score_chunk(j, carry):
        k0 = ki2t_ref[0, :, pl.ds(j * kc, kc)]
        k1 = ki2t_ref[1, :, pl.ds(j * kc, kc)]
        s = jnp.zeros((tq, kc), F32)
        for p in range(IDX_HEADS // 2):
            qp = qi_ref[:, p * LANES:(p + 1) * LANES]
            s = s + jnp.maximum(_dot(qp, k0), 0.0) * wide(wb[2 * p])
            s = s + jnp.maximum(_dot(qp, k1), 0.0) * wide(wb[2 * p + 1])
        sc[j] = jnp.where(j * kc + col_l <= row_g, s, -jnp.inf)
        return carry

    lax.fori_loop(0, n_chunks, score_chunk, 0)
    for extra in range(FOLD_CHUNKS - 1):
        sc[n_chunks + extra] = jnp.full((tq, kc), -jnp.inf, F32)
    n_steps = (n_chunks + FOLD_CHUNKS - 1) // FOLD_CHUNKS

    def fold(fn, init):
        outs = []
        for r in range(n_sub):
            rows = slice(r * ROW_SUB, (r + 1) * ROW_SUB)

            def body(j, acc, rows=rows, r=r):
                for c in range(FOLD_CHUNKS):
                    for part in range(kc // LANES):
                        c0 = (j * FOLD_CHUNKS + c) * kc + part * LANES
                        acc = fn(acc, sc[j * FOLD_CHUNKS + c, rows, part * LANES:(part + 1) * LANES], c0, r)
                return acc

            outs.append(lax.fori_loop(0, n_steps, body, jax.tree.map(
                lambda v: jnp.full((ROW_SUB, LANES), v, F32), init)))
        return outs

    def spread(row):
        return [jnp.broadcast_to(row[:, r * ROW_SUB:(r + 1) * ROW_SUB], (ROW_SUB, ROW_SUB)).T for r in range(n_sub)]

    def collect(parts, op):
        return jnp.concatenate([op(p.T, axis=0, keepdims=True) for p in parts], axis=1)

    def count(pred):
        return collect(fold(lambda acc, blk, c0, r: acc + jnp.where(pred(blk, c0, r), 1.0, 0.0), 0.0), jnp.sum)

    def count_ge(t):
        tb = spread(t)
        return count(lambda blk, c0, r: blk >= tb[r])

    top2 = fold(lambda acc, blk, c0, r: (jnp.maximum(acc[0], blk), jnp.maximum(acc[1], jnp.minimum(acc[0], blk))),
                (-jnp.inf, -jnp.inf))
    fmin = jnp.float32(jnp.finfo(F32).min)
    lo0 = jnp.maximum(collect([p[1] for p in top2], jnp.min), fmin)
    hi0 = collect([p[1 if topk > LANES else 0] for p in top2], jnp.max)

    kf = jnp.float32(topk)
    n_valid = (i * tq + lax.broadcasted_iota(I32, (1, tq), 1) + 1).astype(F32)
    done0 = jnp.where(n_valid <= kf, 1.0, 0.0)

    def cond(st):
        it, lo, hi, thr, done, stalled = st
        return (it < BISECT_CAP) & (jnp.min(done) == 0.0)

    def halve(st):
        it, lo, hi, thr, done, stalled = st
        mid = 0.5 * lo + 0.5 * hi
        n = count_ge(mid)
        live = done == 0.0
        exact = live & (n == kf)
        stall = live & jnp.logical_not(exact) & ((mid <= lo) | (mid >= hi))
        move = live & jnp.logical_not(exact) & jnp.logical_not(stall)
        up = n >= kf
        return (it + 1,
                jnp.where(move & up, mid, lo),
                jnp.where(move & jnp.logical_not(up), mid, hi),
                jnp.where(exact, mid, thr),
                jnp.where(exact | stall, 1.0, done),
                jnp.where(stall, 1.0, stalled))

    st = lax.while_loop(cond, halve, (jnp.int32(0), lo0, hi0, jnp.full((1, tq), fmin), done0,
                                      jnp.zeros((1, tq), F32)))
    _, lo, hi, thr, _, stalled = st
    def resolve():
        t = jnp.where(stalled == 1.0, jnp.where(count_ge(hi) >= kf, hi, lo), thr)
        return t, count_ge(t)

    thr, n_ge = lax.cond(jnp.max(stalled) > 0.0, resolve, lambda: (thr, jnp.zeros((1, tq), F32)))
    tb = spread(thr)

    @pl.when(jnp.max(n_ge) > kf)
    def _():
        n_gt = count(lambda blk, c0, r: blk > tb[r])
        need = kf - n_gt
        lane_i = lax.broadcasted_iota(I32, (ROW_SUB, LANES), 1)
        n_bits = max(int(sc.shape[0] * kc).bit_length(), 1)

        def idx_step(t, jlo):
            trial = jlo + jnp.left_shift(jnp.int32(1), n_bits - 1 - t).astype(F32)
            trb = spread(trial)
            f = count(lambda blk, c0, r: (blk == tb[r]) & ((c0 + lane_i).astype(F32) < trb[r]))
            return jnp.where(f <= need - 1.0, trial, jlo)

        jlo = lax.fori_loop(0, n_bits, idx_step, jnp.zeros((1, tq), F32))
        cut = spread(jnp.where(n_ge > kf, jlo + 1.0, jnp.float32(2 ** 30)))

        def drop(j, carry):
            for r in range(n_sub):
                rows = slice(r * ROW_SUB, (r + 1) * ROW_SUB)
                for part in range(kc // LANES):
                    cols = slice(part * LANES, (part + 1) * LANES)
                    blk = sc[j, rows, cols]
                    gone = (blk == tb[r]) & ((j * kc + part * LANES + lane_i).astype(F32) >= cut[r])
                    sc[j, rows, cols] = jnp.where(gone, -jnp.inf, blk)
            return carry

        lax.fori_loop(0, n_chunks, drop, 0)

    for r in range(n_sub):
        thr_b[r * ROW_SUB:(r + 1) * ROW_SUB, :] = jnp.concatenate([tb[r]] * (kc // LANES), axis=1)

    acc_scr[...] = jnp.zeros(acc_scr.shape, F32)
    bounds = []
    for h in range(n_heads):
        qf = q_ref[:, h * ATT_HEAD_DIM:(h + 1) * ATT_HEAD_DIM].astype(F32)
        qn = jnp.sqrt(jnp.sum(qf * qf, axis=-1, keepdims=True))
        bounds.append(qn * (jnp.sqrt(kmax[h // q_per_kv]) * 1.01))
    safe = functools.reduce(jnp.maximum, [jnp.max(b) for b in bounds]) <= SAFE_SHIFT
    vw = 2 * ATT_HEAD_DIM

    def halves(x):
        return [x[:, k * LANES:(k + 1) * LANES] for k in range(kc // LANES)]

    def logits(j, h):
        g = h // q_per_kv
        kj = kbt_ref[g * ATT_HEAD_DIM:(g + 1) * ATT_HEAD_DIM, pl.ds(j * kc, kc)]
        qh = q_ref[:, h * ATT_HEAD_DIM:(h + 1) * ATT_HEAD_DIM]
        return _dot(qh, kj)

    def max_chunk(j, carry):
        sel = sc[j] >= thr_b[...]
        for h in range(n_heads):
            lg = jnp.where(sel, logits(j, h), NEG_BIG)
            mrun[h] = functools.reduce(jnp.maximum, halves(lg), mrun[h])
        return carry

    @pl.when(safe)
    def _():
        for h in range(n_heads):
            mrun[h] = jnp.broadcast_to(bounds[h], (tq, LANES))

    @pl.when(jnp.logical_not(safe))
    def _():
        mrun[...] = jnp.full(mrun.shape, NEG_BIG, F32)
        lax.fori_loop(0, n_chunks, max_chunk, 0)
        for h in range(n_heads):
            mrun[h] = jnp.broadcast_to(jnp.max(mrun[h], axis=-1, keepdims=True), (tq, LANES))

    def sum_chunk(j, carry):
        sel = sc[j] >= thr_b[...]
        for h in range(n_heads):
            g = h // q_per_kv
            vj = vx_ref[pl.ds(j * kc, kc), g * vw:(g + 1) * vw]
            m = mrun[h]
            p = jnp.where(sel, jnp.exp2(logits(j, h) - jnp.concatenate([m] * (kc // LANES), axis=1)), 0.0)
            acc_scr[h] = acc_scr[h] + _dot(p.astype(BF16), vj)
        return carry

    lax.fori_loop(0, n_chunks, sum_chunk, 0)

    ss = jnp.zeros((tq, 1), F32)
    for h in range(n_heads):
        o = acc_scr[h, :, :ATT_HEAD_DIM] / acc_scr[h, :, ATT_HEAD_DIM:]
        acc_scr[h, :, :ATT_HEAD_DIM] = o
        ss = ss + jnp.sum(o * o, axis=-1, keepdims=True)
    inv = lax.rsqrt(ss * (1.0 / (n_heads * ATT_HEAD_DIM)) + EPS)
    for h in range(n_heads):
        sl = slice(h * ATT_HEAD_DIM, (h + 1) * ATT_HEAD_DIM)
        o_ref[:, sl] = (acc_scr[h, :, :ATT_HEAD_DIM] * inv * nw_ref[:, sl]).astype(BF16)


def _attn_prompt(q, qi, wi, ki2t, kbt, vx, norm_w, topk, tq):
    t, d_att = q.shape
    n_heads = d_att // ATT_HEAD_DIM
    full = lambda a: pl.BlockSpec(a.shape, lambda i: (0,) * a.ndim)
    return pl.pallas_call(
        functools.partial(_attn_prompt_kernel, topk, tq),
        grid=(t // tq,),
        in_specs=[pl.BlockSpec((tq, d_att), lambda i: (i, 0)),
                  pl.BlockSpec((tq, qi.shape[1]), lambda i: (i, 0)),
                  pl.BlockSpec((tq, LANES), lambda i: (i, 0)),
                  full(ki2t), full(kbt), full(vx), full(norm_w)],
        out_specs=pl.BlockSpec((tq, d_att), lambda i: (i, 0)),
        out_shape=jax.ShapeDtypeStruct((t, d_att), BF16),
        scratch_shapes=[pltpu.VMEM((t // tq + FOLD_CHUNKS, tq, tq), F32),
                        pltpu.VMEM((IDX_HEADS, tq, LANES), F32),
                        pltpu.VMEM((tq, tq), F32),
                        pltpu.VMEM((n_heads, tq, LANES), F32),
                        pltpu.VMEM((n_heads, tq, 2 * ATT_HEAD_DIM), F32),
                        pltpu.SMEM((KV_HEADS,), F32)],
        compiler_params=_cparams(("arbitrary",)),
    )(q, qi, wi, ki2t, kbt, vx, norm_w)


def _outproj_kernel(ya_ref, yb_ref, w_ref, x_ref, g1_ref, nw_ref, sc_ref, sh_ref, wr_ref, br_ref,
                    x1_ref, h2_ref, eid_ref, wts_ref, cnt_ref):
    d_a = ya_ref.shape[1]
    m = _dot(ya_ref[...], w_ref[:d_a, :]) + _dot(yb_ref[...], w_ref[d_a:, :])
    x1 = x_ref[...] + g1_ref[...] * m
    x1_ref[...] = x1
    h2 = _rms(x1) * nw_ref[...]
    h2 = h2 * (1.0 + sc_ref[...]) + sh_ref[...]
    h2_ref[...] = h2
    lg = _dot(h2.astype(BF16), wr_ref[...]) + br_ref[...]
    lane = lax.broadcasted_iota(I32, lg.shape, 1)
    big = jnp.int32(4 * LANES)

    def rmax(v):
        return jnp.max(v, axis=-1, keepdims=True)

    def rmin(v):
        return jnp.min(v, axis=-1, keepdims=True)

    def rsum(v):
        return jnp.sum(v, axis=-1, keepdims=True)

    is_g = (lane >= N_EXPERTS) & (lane < N_EXPERTS + N_EGROUPS)
    mg = rmax(jnp.where(is_g, lg, -jnp.inf))
    sg = rsum(jnp.where(is_g, jnp.exp(lg - mg), 0.0))
    gsel = rmin(jnp.where(is_g & (lg == mg), lane - N_EXPERTS, big))
    pgsel = 1.0 / sg
    in_grp = (lane < N_EXPERTS) & (jnp.right_shift(lane, EXPERTS_PER_GROUP.bit_length() - 1) == gsel)
    me = rmax(jnp.where(in_grp, lg, -jnp.inf))
    ee = jnp.where(in_grp, jnp.exp(lg - me), 0.0)
    pe = ee / rsum(ee)
    p1 = rmax(jnp.where(in_grp, pe, -1.0))
    i1 = rmin(jnp.where(in_grp & (pe == p1), lane, big))
    rem = in_grp & (lane != i1)
    p2 = rmax(jnp.where(rem, pe, -1.0))
    i2 = rmin(jnp.where(rem & (pe == p2), lane, big))
    den = p1 + p2
    eid_ref[...] = jnp.where(lane == 0, i1, jnp.where(lane == 1, i2, 0))
    wts_ref[...] = jnp.where(lane == 0, pgsel * p1 / den, jnp.where(lane == 1, pgsel * p2 / den, 0.0))

    @pl.when(pl.program_id(0) == 0)
    def _():
        cnt_ref[...] = jnp.zeros_like(cnt_ref)

    chosen = jnp.where((lane == i1) | (lane == i2), 1.0, 0.0)
    cnt_ref[...] += jnp.sum(chosen, axis=0, keepdims=True)


def _out_proj(ya, yb, w_out_b, x, g1, nw2, sc2, sh2, wr, br, tm):
    t, d = x.shape
    d_a = ya.shape[1]
    tmod = g1.shape[0]
    mod_map = (lambda i: (0, 0)) if tmod == 1 else (lambda i: (i, 0))
    mod_rows = 1 if tmod == 1 else tm
    modspec = pl.BlockSpec((mod_rows, d), mod_map)
    row = lambda w: pl.BlockSpec((tm, w), lambda i: (i, 0))
    full = lambda a: pl.BlockSpec(a.shape, lambda i: (0, 0))
    return pl.pallas_call(
        _outproj_kernel,
        grid=(t // tm,),
        in_specs=[row(d_a), row(yb.shape[1]), full(w_out_b), row(d), modspec, full(nw2), modspec, modspec,
                  full(wr), full(br)],
        out_specs=[row(d), row(d), row(LANES), row(LANES), pl.BlockSpec((1, LANES), lambda i: (0, 0))],
        out_shape=[jax.ShapeDtypeStruct((t, d), F32), jax.ShapeDtypeStruct((t, d), F32),
                   jax.ShapeDtypeStruct((t, LANES), I32), jax.ShapeDtypeStruct((t, LANES), F32),
                   jax.ShapeDtypeStruct((1, LANES), F32)],
        compiler_params=_cparams(("arbitrary",)),
    )(ya, yb, w_out_b, x, g1, nw2, sc2, sh2, wr, br)


MOE_TILE = 256


def _moe_pos_kernel(eid_ref, off_ref, pos_ref, carry):
    @pl.when(pl.program_id(0) == 0)
    def _():
        carry[...] = jnp.zeros_like(carry)

    eid = eid_ref[...]
    tm = eid.shape[0]
    i1, i2 = eid[:, 0:1], eid[:, 1:2]
    lane = lax.broadcasted_iota(I32, eid.shape, 1)
    chosen = jnp.where((lane == i1) | (lane == i2), 1.0, 0.0)
    r = lax.broadcasted_iota(I32, (tm, tm), 0)
    c = lax.broadcasted_iota(I32, (tm, tm), 1)
    earlier = _dot(jnp.where(c < r, 1.0, 0.0).astype(BF16), chosen.astype(BF16))
    row = earlier + carry[...] + off_ref[...]
    p1 = jnp.sum(jnp.where(lane == i1, row, 0.0), axis=-1, keepdims=True)
    p2 = jnp.sum(jnp.where(lane == i2, row, 0.0), axis=-1, keepdims=True)
    out = jnp.where(lane == 0, p1, jnp.where(lane == 1, p2, 0.0))
    pos_ref[...] = jnp.where(i1 >= 0, out, -1.0).astype(I32)
    carry[...] += jnp.sum(chosen, axis=0, keepdims=True)


def _moe_positions(eid_all, off_row, tm):
    t = eid_all.shape[0]
    return pl.pallas_call(
        _moe_pos_kernel,
        grid=(t // tm,),
        in_specs=[pl.BlockSpec((tm, LANES), lambda i: (i, 0)), pl.BlockSpec((1, LANES), lambda i: (0, 0))],
        out_specs=pl.BlockSpec((tm, LANES), lambda i: (i, 0)),
        out_shape=jax.ShapeDtypeStruct((t, LANES), I32),
        scratch_shapes=[pltpu.VMEM((1, LANES), F32)],
        compiler_params=_cparams(("arbitrary",)),
    )(eid_all, off_row)


def _dyn_loop(lo, hi, fn, unroll=4):
    shift = unroll.bit_length() - 1
    n_blk = jnp.right_shift(hi - lo, shift)

    def blk(k, carry):
        for u in range(unroll):
            fn(lo + k * unroll + u)
        return carry

    def one(i, carry):
        fn(i)
        return carry

    lax.fori_loop(0, n_blk, blk, 0)
    lax.fori_loop(lo + n_blk * unroll, hi, one, 0)


def _moe_grouped_kernel(t_prompt, te_ref, nu_ref, np_ref, nv_ref, pos_ref, hp_ref, hs_ref, wu_ref, wd_ref, o_ref,
                        src, xbuf, wub, wdb, sem):
    g = pl.program_id(0)
    n_used = nu_ref[0]
    tmg = xbuf.shape[1]
    n_tok = pos_ref.shape[0] // 2
    n_rows = o_ref.shape[0] * pl.num_programs(0)
    slot = g % 2

    def row_copy(h_ref, tok, sl, r):
        return pltpu.make_async_copy(h_ref.at[pl.ds(tok, 1)], xbuf.at[sl, pl.ds(r, 1)], sem.at[sl])

    def gather_start(tile, sl):
        base = tile * tmg
        _dyn_loop(0, np_ref[tile], lambda r: row_copy(hp_ref, src[base + r], sl, r).start())
        _dyn_loop(np_ref[tile], nv_ref[tile], lambda r: row_copy(hs_ref, src[base + r] - t_prompt, sl, r).start())

    def gather_wait(tile, sl):
        _dyn_loop(0, nv_ref[tile], lambda r: row_copy(hp_ref, 0, sl, r).wait())

    @pl.when(g == 0)
    def _():
        def fill(t, carry):
            for k in range(2):
                p = pos_ref[2 * t + k]
                src[jnp.where(p < 0, n_rows, p)] = t
            return carry

        lax.fori_loop(0, n_tok, fill, 0, unroll=4)
        xbuf[...] = jnp.zeros_like(xbuf)
        gather_start(0, 0)

    @pl.when(g < n_used)
    def _():
        @pl.when(g + 1 < n_used)
        def _():
            gather_start(g + 1, 1 - slot)

        gather_wait(g, slot)
        fresh = (g == 0) | (te_ref[g] != te_ref[jnp.maximum(g - 1, 0)])

        @pl.when(fresh)
        def _():
            wub[...] = wu_ref[0].astype(BF16)
            wdb[...] = wd_ref[0].astype(BF16)

        gu = _dot(xbuf[slot].astype(BF16), wub[...])
        de = gu.shape[1] // 2
        act = _silu(gu[:, :de]) * gu[:, de:]
        o_ref[...] = _dot(act.astype(BF16), wdb[...])

    @pl.when(g >= n_used)
    def _():
        o_ref[...] = jnp.zeros_like(o_ref)


def _moe_grouped(plan, pos_flat, h2_p, h2_s, w_up, w_down):
    tile_expert, n_used, tile_np, tile_nv = plan
    n_tiles = tile_expert.shape[0]
    t_prompt, d = h2_p.shape
    _, _, two_de = w_up.shape
    tmg = MOE_TILE
    wmap = lambda g, te, nu, tp, tv, ps: (te[g], 0, 0)
    gs = pltpu.PrefetchScalarGridSpec(
        num_scalar_prefetch=5,
        grid=(n_tiles,),
        in_specs=[pl.BlockSpec(memory_space=pl.ANY), pl.BlockSpec(memory_space=pl.ANY),
                  pl.BlockSpec((1, d, two_de), wmap),
                  pl.BlockSpec((1, two_de // 2, d), wmap)],
        out_specs=pl.BlockSpec((tmg, d), lambda g, te, nu, tp, tv, ps: (g, 0)),
        scratch_shapes=[pltpu.SMEM((n_tiles * tmg + 8,), I32),
                        pltpu.VMEM((2, tmg, d), F32),
                        pltpu.VMEM((d, two_de), BF16),
                        pltpu.VMEM((two_de // 2, d), BF16),
                        pltpu.SemaphoreType.DMA((2,))],
    )
    return pl.pallas_call(
        functools.partial(_moe_grouped_kernel, t_prompt),
        grid_spec=gs,
        out_shape=jax.ShapeDtypeStruct((n_tiles * tmg, d), F32),
        compiler_params=_cparams(("arbitrary",)),
    )(tile_expert, n_used, tile_np, tile_nv, pos_flat, h2_p, h2_s, w_up, w_down)


def _combine_kernel(tok0, pos_ref, x1_ref, wts_ref, g2_ref, nf_ref, ys_ref, xo_ref, o_ref, ybuf, sem):
    i = pl.program_id(0)
    n = pl.num_programs(0)
    tm = x1_ref.shape[0]
    slot = i % 2

    def gather(tile, sl, wait):
        def body(r, carry):
            tok = tok0 + tile * tm + r
            for k in range(2):
                src_row = 0 if wait else pos_ref[2 * tok + k]
                cp = pltpu.make_async_copy(ys_ref.at[pl.ds(src_row, 1)], ybuf.at[sl, k, pl.ds(r, 1)], sem.at[sl])
                cp.wait() if wait else cp.start()
            return carry

        lax.fori_loop(0, tm, body, 0, unroll=8)

    @pl.when(i == 0)
    def _():
        gather(0, 0, False)

    @pl.when(i + 1 < n)
    def _():
        gather(i + 1, 1 - slot, False)

    gather(i, slot, True)
    w = wts_ref[...]
    y = w[:, 0:1] * ybuf[slot, 0] + w[:, 1:2] * ybuf[slot, 1]
    x2 = x1_ref[...] + g2_ref[...] * y
    xo_ref[...] = x2
    o_ref[...] = _rms(x2) * nf_ref[...]


def _combine(pos_flat, x1, wts, g2, nf, ys, tok0, tm):
    t, d = x1.shape
    tmod = g2.shape[0]
    mod_map = (lambda i, ps: (0, 0)) if tmod == 1 else (lambda i, ps: (i, 0))
    row = lambda w: pl.BlockSpec((tm, w), lambda i, ps: (i, 0))
    gs = pltpu.PrefetchScalarGridSpec(
        num_scalar_prefetch=1,
        grid=(t // tm,),
        in_specs=[row(d), row(LANES), pl.BlockSpec((1 if tmod == 1 else tm, d), mod_map),
                  pl.BlockSpec((1, d), lambda i, ps: (0, 0)), pl.BlockSpec(memory_space=pl.ANY)],
        out_specs=[row(d), row(d)],
        scratch_shapes=[pltpu.VMEM((2, 2, tm, d), F32), pltpu.SemaphoreType.DMA((2,))],
    )
    return pl.pallas_call(
        functools.partial(_combine_kernel, tok0),
        grid_spec=gs,
        out_shape=[jax.ShapeDtypeStruct((t, d), F32), jax.ShapeDtypeStruct((t, d), F32)],
        compiler_params=_cparams(("arbitrary",)),
    )(pos_flat, x1, wts, g2, nf, ys)


def _moe_plan(cnt_p, cnt_s, n_tiles):
    cp = cnt_p[0, :N_EXPERTS].astype(I32)
    cnt = cp + cnt_s[0, :N_EXPERTS].astype(I32)
    padded = (cnt + MOE_TILE - 1) // MOE_TILE * MOE_TILE
    ends = jnp.cumsum(padded)
    off = ends - padded
    off_row = _pad_cols(off.astype(F32).reshape(1, N_EXPERTS), LANES)
    starts = jnp.arange(n_tiles, dtype=I32) * MOE_TILE
    te = jnp.minimum(jnp.sum(starts[:, None] >= ends[None, :], axis=1), N_EXPERTS - 1).astype(I32)
    tile_np = jnp.clip(off[te] + cp[te] - starts, 0, MOE_TILE).astype(I32)
    tile_nv = jnp.clip(off[te] + cnt[te] - starts, 0, MOE_TILE).astype(I32)
    return off_row, (te, (ends[-1:] // MOE_TILE).astype(I32), tile_np, tile_nv)


def _ssd_prep_kernel(xbc_ref, p0_ref, p1_ref, p2_ref, cw_ref, cb_ref, dt_ref, dtb_ref, alog_ref, ex_ref,
                     xc_ref, xdt_ref, dec_ref):
    d_ssd = xdt_ref.shape[1]
    acc = (cb_ref[...] + cw_ref[0:1, :] * p0_ref[...] + cw_ref[1:2, :] * p1_ref[...]
           + cw_ref[2:3, :] * p2_ref[...] + cw_ref[3:4, :] * xbc_ref[...])
    xc = _silu(acc)
    xc_ref[...] = xc
    dt = _softplus(dt_ref[...] + dtb_ref[...])
    dec = jnp.exp(dt * (-jnp.exp(alog_ref[...])))
    xdt_ref[...] = _dot(dt, ex_ref[...], precision=HIGHEST) * xc[:, :d_ssd]
    dec_ref[...] = _dot(dec, ex_ref[...], precision=HIGHEST)


def _ssd_prep(xbc, p0, p1, p2, conv_w, conv_b, dt_raw, dt_bias_p, a_log_p, expand):
    b, conv_dim = xbc.shape
    d_ssd = expand.shape[1]
    args = (xbc, p0, p1, p2, conv_w, conv_b, dt_raw, dt_bias_p, a_log_p, expand)
    return pl.pallas_call(
        _ssd_prep_kernel,
        grid=(1,),
        in_specs=[pl.BlockSpec(a.shape, lambda i: (0, 0)) for a in args],
        out_specs=[pl.BlockSpec((b, conv_dim), lambda i: (0, 0)),
                   pl.BlockSpec((b, d_ssd), lambda i: (0, 0)),
                   pl.BlockSpec((b, d_ssd), lambda i: (0, 0))],
        out_shape=[jax.ShapeDtypeStruct((b, conv_dim), F32), jax.ShapeDtypeStruct((b, d_ssd), F32),
                   jax.ShapeDtypeStruct((b, d_ssd), F32)],
        compiler_params=_cparams(("arbitrary",)),
    )(*args)


def _ssd_step_kernel(n_pairs, xdt_ref, dec_ref, bm_ref, cm_ref, s_ref, so_ref, y_ref):
    pairs_per_group = n_pairs // SSD_GROUPS
    rows_per_pair = LANES // SSD_HEAD_DIM

    def on_rows(row):
        return jnp.broadcast_to(row, (LANES, LANES)).T

    for p in range(n_pairs):
        g = p // pairs_per_group
        sl = slice(p * LANES, (p + 1) * LANES)
        hs = slice(p * rows_per_pair, (p + 1) * rows_per_pair)
        hb = s_ref[0, 0, hs].reshape(LANES, D_STATE)
        bmat = jnp.broadcast_to(bm_ref[0, :, g * D_STATE:(g + 1) * D_STATE], (LANES, D_STATE))
        cmat = jnp.broadcast_to(cm_ref[0, :, g * D_STATE:(g + 1) * D_STATE], (LANES, D_STATE))
        hn = hb * on_rows(dec_ref[0, :, sl]) + on_rows(xdt_ref[0, :, sl]) * bmat
        so_ref[0, 0, hs] = hn.reshape(rows_per_pair, SSD_HEAD_DIM, D_STATE)
        y_ref[0, :, sl] = jnp.sum((hn * cmat).T, axis=0, keepdims=True)


def _ssd_step(xdt, dec, bm, cm, state):
    b, d_ssd = xdt.shape
    n_pairs = d_ssd // LANES
    heads = d_ssd // SSD_HEAD_DIM
    r3 = lambda a: a.reshape(b, 1, a.shape[1])
    row = lambda w: pl.BlockSpec((1, 1, w), lambda i: (i, 0, 0))
    sspec = pl.BlockSpec((1, 1, heads, SSD_HEAD_DIM, D_STATE), lambda i: (0, i, 0, 0, 0))
    so, y = pl.pallas_call(
        functools.partial(_ssd_step_kernel, n_pairs),
        grid=(b,),
        in_specs=[row(d_ssd), row(d_ssd), row(bm.shape[1]), row(cm.shape[1]), sspec],
        out_specs=[sspec, row(d_ssd)],
        out_shape=[jax.ShapeDtypeStruct(state.shape, F32), jax.ShapeDtypeStruct((b, 1, d_ssd), F32)],
        compiler_params=_cparams(("arbitrary",)),
    )(r3(xdt), r3(dec), r3(bm), r3(cm), state)
    return so, y.reshape(b, d_ssd)


def _ssd_finish_kernel(y_ref, xs_ref, z_ref, dsk_ref, nw_ref, o_ref):
    y = (y_ref[...] + xs_ref[...] * dsk_ref[...]) * _silu(z_ref[...])
    o_ref[...] = (_rms(y) * nw_ref[...]).astype(BF16)


def _ssd_finish(y, xs, z, dskip_row, norm_w):
    args = (y, xs, z, dskip_row, norm_w)
    return pl.pallas_call(
        _ssd_finish_kernel,
        grid=(1,),
        in_specs=[pl.BlockSpec(a.shape, lambda i: (0, 0)) for a in args],
        out_specs=pl.BlockSpec(y.shape, lambda i: (0, 0)),
        out_shape=jax.ShapeDtypeStruct(y.shape, BF16),
        compiler_params=_cparams(("arbitrary",)),
    )(*args)


PAGE_PACK = 8


def _page_copy(cache_ref, buf, sem, pt_ref, b, p, slot):
    rows = cache_ref.shape[2]
    return pltpu.make_async_copy(cache_ref.at[0, pt_ref[b, p]], buf.at[slot, pl.ds(p * rows, rows)], sem.at[slot])


def _score_sample_kernel(n_pages, pt_ref, q8_ref, w8_ref, qi_ref, wi_ref, kin_ref, cache_ref, s_ref, buf, sem):
    b = pl.program_id(0)
    nb = pl.num_programs(0)
    slot = b % 2

    def start(bb, sl):
        def body(p, carry):
            _page_copy(cache_ref, buf, sem, pt_ref, bb, p, sl).start()
            return carry
        lax.fori_loop(0, n_pages, body, 0)

    @pl.when(b == 0)
    def _():
        start(0, 0)

    @pl.when(b + 1 < nb)
    def _():
        start(b + 1, 1 - slot)

    def wait(p, carry):
        _page_copy(cache_ref, buf, sem, pt_ref, b, p, slot).wait()
        return carry

    lax.fori_loop(0, n_pages, wait, 0)

    wscale = (IDX_DIM ** -0.5) * (IDX_HEADS ** -0.5)
    q8 = q8_ref[0]
    w8 = w8_ref[0] * wscale
    kdim = q8.shape[1]
    page_rows = buf.shape[2]

    def group(gi, carry):
        keys_t = buf[slot, pl.ds(gi * kdim, kdim), :].astype(BF16)
        r = jnp.maximum(_dot(q8, keys_t), 0.0) * w8
        s_ref[0, pl.ds(gi * PAGE_PACK, PAGE_PACK), :] = jnp.sum(
            r.reshape(PAGE_PACK, IDX_HEADS, page_rows), axis=1)
        return carry

    lax.fori_loop(0, n_pages // PAGE_PACK, group, 0, unroll=4 if n_pages % (4 * PAGE_PACK) == 0 else 1)
    tail = s_ref.shape[1] - n_pages
    kn = jnp.broadcast_to(kin_ref[0], (page_rows, kin_ref.shape[2])).astype(BF16)
    dn = _dot_nt(qi_ref[0], kn)
    sn = jnp.sum(jnp.maximum(dn, 0.0) * (wi_ref[0] * wscale), axis=0, keepdims=True)
    r = lax.broadcasted_iota(I32, (tail, page_rows), 0)
    c = lax.broadcasted_iota(I32, (tail, page_rows), 1)
    s_ref[0, n_pages:, :] = jnp.where((r == 0) & (c == 0), jnp.broadcast_to(sn, (tail, page_rows)), -jnp.inf)


def _score_sample(page_table, q8, w8, qi3, wi3, ki_new3, cache_kit, tail_rows):
    b, n_pages = page_table.shape
    idx_dim, page_rows = cache_kit.shape[2], cache_kit.shape[3]
    blk = lambda a: pl.BlockSpec((1,) + a.shape[1:], lambda i, pt: (i, 0, 0))
    gs = pltpu.PrefetchScalarGridSpec(
        num_scalar_prefetch=1,
        grid=(b,),
        in_specs=[blk(q8), blk(w8), blk(qi3), blk(wi3), blk(ki_new3), pl.BlockSpec(memory_space=pl.ANY)],
        out_specs=pl.BlockSpec((1, n_pages + tail_rows, page_rows), lambda i, pt: (i, 0, 0)),
        scratch_shapes=[pltpu.VMEM((2, n_pages * idx_dim, page_rows), F32),
                        pltpu.SemaphoreType.DMA((2,))],
    )
    return pl.pallas_call(
        functools.partial(_score_sample_kernel, n_pages),
        grid_spec=gs,
        out_shape=jax.ShapeDtypeStruct((b, n_pages + tail_rows, page_rows), F32),
        compiler_params=_cparams(("arbitrary",)),
    )(page_table, q8, w8, qi3, wi3, ki_new3, cache_kit)


def _threshold_sample_kernel(topk, s_ref, thr_ref, cut_ref):
    nb, n = s_ref.shape
    n_blk = n // LANES
    kf = jnp.float32(topk)

    def fold(fn, init):
        def body(j, acc):
            return fn(acc, s_ref[:, pl.ds(pl.multiple_of(j * LANES, LANES), LANES)], j * LANES)
        return lax.fori_loop(0, n_blk, body, jnp.full((nb, LANES), init, F32), unroll=8)

    def count(pred):
        return jnp.sum(fold(lambda acc, blk, c0: acc + jnp.where(pred(blk, c0), 1.0, 0.0), 0.0), axis=-1, keepdims=True)

    def count_ge(t):
        tb = jnp.broadcast_to(t, (nb, LANES))
        return count(lambda blk, c0: blk >= tb)

    fmin = jnp.float32(jnp.finfo(F32).min)
    lo0 = jnp.min(fold(lambda acc, blk, c0: jnp.minimum(acc, jnp.where(blk == -jnp.inf, 3e38, blk)), 3e38),
                  axis=-1, keepdims=True)
    hi0 = jnp.max(fold(lambda acc, blk, c0: jnp.maximum(acc, blk), -jnp.inf), axis=-1, keepdims=True)
    n_valid = count(lambda blk, c0: blk > -jnp.inf)
    done0 = jnp.where(n_valid <= kf, 1.0, 0.0)

    def cond(st):
        it, lo, hi, thr, done, stalled = st
        return (it < BISECT_CAP) & (jnp.min(done) == 0.0)

    def halve(st):
        it, lo, hi, thr, done, stalled = st
        mid = 0.5 * lo + 0.5 * hi
        n = count_ge(mid)
        live = done == 0.0
        exact = live & (n == kf)
        stall = live & jnp.logical_not(exact) & ((mid <= lo) | (mid >= hi))
        move = live & jnp.logical_not(exact) & jnp.logical_not(stall)
        up = n >= kf
        return (it + 1,
                jnp.where(move & up, mid, lo),
                jnp.where(move & jnp.logical_not(up), mid, hi),
                jnp.where(exact, mid, thr),
                jnp.where(exact | stall, 1.0, done),
                jnp.where(stall, 1.0, stalled))

    st = lax.while_loop(cond, halve, (jnp.int32(0), lo0, hi0, jnp.minimum(lo0, fmin), done0, jnp.zeros((nb, 1), F32)))
    _, lo, hi, thr, _, stalled = st
    thr = jnp.where(stalled == 1.0, jnp.where(count_ge(hi) >= kf, hi, lo), thr)
    n_ge = count_ge(thr)
    thr_ref[...] = jnp.broadcast_to(thr, (nb, LANES))
    n_bits = max(int(n).bit_length(), 1)
    tb = jnp.broadcast_to(thr, (nb, LANES))
    lane = lax.broadcasted_iota(I32, (nb, LANES), 1)

    def tie_cut():
        need = kf - count(lambda blk, c0: blk > tb)

        def idx_step(t, jlo):
            trial = jlo + jnp.left_shift(jnp.int32(1), n_bits - 1 - t).astype(F32)
            trb = jnp.broadcast_to(trial, (nb, LANES))
            f = count(lambda blk, c0: (blk == tb) & ((c0 + lane).astype(F32) < trb))
            return jnp.where(f <= need - 1.0, trial, jlo)

        jlo = lax.fori_loop(0, n_bits, idx_step, jnp.zeros((nb, 1), F32))
        return jnp.where(n_ge > kf, jlo + 1.0, jnp.float32(2 ** 30))

    cut = lax.cond(jnp.max(n_ge) > kf, tie_cut, lambda: jnp.full((nb, 1), 2 ** 30, F32))
    cut_ref[...] = jnp.broadcast_to(cut, (nb, LANES))


def _threshold_sample(s2, topk):
    nb, n = s2.shape
    return pl.pallas_call(
        functools.partial(_threshold_sample_kernel, topk),
        grid=(1,),
        in_specs=[pl.BlockSpec((nb, n), lambda i: (0, 0))],
        out_specs=[pl.BlockSpec((nb, LANES), lambda i: (0, 0))] * 2,
        out_shape=[jax.ShapeDtypeStruct((nb, LANES), F32)] * 2,
        compiler_params=_cparams(("arbitrary",)),
    )(s2)


def _select_sample_kernel(topk, s_ref, thr_ref, cut_ref, idx_ref):
    s = s_ref[0]
    n_rows, width = s.shape
    pos = (lax.broadcasted_iota(I32, s.shape, 0) * width + lax.broadcasted_iota(I32, s.shape, 1)).astype(F32)
    thr = thr_ref[0]
    sel = (s > thr) | ((s == thr) & (pos < cut_ref[0]))
    self = jnp.where(sel, 1.0, 0.0).astype(BF16)
    ra = lax.broadcasted_iota(I32, (width, width), 0)
    ca = lax.broadcasted_iota(I32, (width, width), 1)
    local = jnp.where(sel, _dot(self, jnp.where(ra < ca, 1.0, 0.0).astype(BF16)), -1.0)
    cnt_row = _dot_nt(jnp.ones((8, width), BF16), self)
    rb = lax.broadcasted_iota(I32, (n_rows, n_rows), 0)
    cb = lax.broadcasted_iota(I32, (n_rows, n_rows), 1)
    end_row = _dot(cnt_row.astype(BF16), jnp.where(rb <= cb, 1.0, 0.0).astype(BF16))
    rank = lax.broadcasted_iota(I32, (topk, n_rows), 0).astype(F32)
    row_id = lax.broadcasted_iota(I32, (topk, n_rows), 1).astype(F32)
    passed = jnp.broadcast_to(end_row[0:1, :], (topk, n_rows)) <= rank
    row_of = jnp.sum(jnp.where(passed, 1.0, 0.0), axis=-1, keepdims=True)
    start = jnp.sum(jnp.where(passed, jnp.broadcast_to(cnt_row[0:1, :], (topk, n_rows)), 0.0), axis=-1,
                    keepdims=True)
    picked = _dot(jnp.where(row_id == row_of, 1.0, 0.0).astype(BF16), local.astype(BF16))
    lane = lax.broadcasted_iota(I32, (topk, width), 1).astype(F32)
    lane_of = jnp.sum(jnp.where(picked == rank[:, 0:1] - start, lane, 0.0), axis=-1, keepdims=True)
    idx_ref[0] = (row_of * width + lane_of).astype(I32)


def _select_sample(s3, thr, cut, topk):
    b, n_rows, width = s3.shape
    per_b = lambda a: pl.BlockSpec((1, 1, LANES), lambda i: (i, 0, 0))
    return pl.pallas_call(
        functools.partial(_select_sample_kernel, topk),
        grid=(b,),
        in_specs=[pl.BlockSpec((1, n_rows, width), lambda i: (i, 0, 0)), per_b(thr), per_b(cut)],
        out_specs=pl.BlockSpec((1, topk, 1), lambda i: (i, 0, 0)),
        out_shape=jax.ShapeDtypeStruct((b, topk, 1), I32),
        compiler_params=_cparams(("arbitrary",)),
    )(s3, thr.reshape(b, 1, LANES), cut.reshape(b, 1, LANES))


def _row_copy(src, dst, sem, src_row, dst_row):
    return pltpu.make_async_copy(src.at[pl.ds(src_row, KV_HEADS)], dst.at[pl.ds(dst_row, KV_HEADS)], sem)


def _attn_sample_kernel(topk, past_len, page_rows, n_pages, idx_ref, pt_ref, q_ref, nw_ref, ck_ref, cv_ref,
                        kn_ref, vn_ref, o_ref, kbuf, vbuf, sem):
    b = pl.program_id(0)

    pow2 = page_rows & (page_rows - 1) == 0

    def start(r, carry):
        j = jnp.minimum(idx_ref[b, r], past_len - 1)
        if pow2:
            page, off = jnp.right_shift(j, page_rows.bit_length() - 1), j & (page_rows - 1)
        else:
            page, off = j // page_rows, j % page_rows
        row = (pt_ref[b, page] * page_rows + off) * KV_HEADS
        _row_copy(ck_ref, kbuf, sem.at[0], row, r * KV_HEADS).start()
        _row_copy(cv_ref, vbuf, sem.at[1], row, r * KV_HEADS).start()
        return carry

    lax.fori_loop(0, topk, start, 0, unroll=8)

    def wait(r, carry):
        _row_copy(ck_ref, kbuf, sem.at[0], 0, r * KV_HEADS).wait()
        _row_copy(cv_ref, vbuf, sem.at[1], 0, r * KV_HEADS).wait()
        return carry

    lax.fori_loop(0, topk, wait, 0, unroll=8)

    @pl.when(idx_ref[b, topk - 1] >= past_len)
    def _():
        last = (topk - 1) * KV_HEADS
        for src_ref, buf, s in ((kn_ref, kbuf, sem.at[0]), (vn_ref, vbuf, sem.at[1])):
            cp = _row_copy(src_ref, buf, s, b * KV_HEADS, last)
            cp.start()
            cp.wait()

    outs = []
    ss = jnp.zeros((1, 1), F32)
    for g in range(KV_HEADS):
        kg = kbuf[pl.ds(g, topk, stride=KV_HEADS), :].astype(BF16)
        vg = vbuf[pl.ds(g, topk, stride=KV_HEADS), :].astype(BF16)
        lg = _dot_nt(q_ref[0, g], kg)
        m = jnp.max(lg, axis=-1, keepdims=True)
        p = jnp.exp2(lg - m)
        p = p / jnp.sum(p, axis=-1, keepdims=True)
        o = _dot(p.astype(BF16), vg)
        rows = lax.broadcasted_iota(I32, o.shape, 0)
        o = jnp.where(rows < q_ref.shape[2] // 2, o, 0.0)
        outs.append(o)
        ss = ss + jnp.sum(jnp.sum(o * o, axis=-1, keepdims=True), axis=0, keepdims=True)
    n_feat = KV_HEADS * (q_ref.shape[2] // 2) * ATT_HEAD_DIM
    inv = lax.rsqrt(ss * (1.0 / n_feat) + EPS)
    for g in range(KV_HEADS):
        o_ref[0, g] = (outs[g] * inv * nw_ref[g]).astype(BF16)


def _attn_sample(idx, page_table, q4, nw3, ck2, cv2, kn2, vn2, past_len, page_rows):
    b, topk = idx.shape
    n_pages = page_table.shape[1]
    gs = pltpu.PrefetchScalarGridSpec(
        num_scalar_prefetch=2,
        grid=(b,),
        in_specs=[pl.BlockSpec((1,) + q4.shape[1:], lambda i, a, c: (i, 0, 0, 0)),
                  pl.BlockSpec(nw3.shape, lambda i, a, c: (0, 0, 0)),
                  pl.BlockSpec(memory_space=pl.ANY), pl.BlockSpec(memory_space=pl.ANY),
                  pl.BlockSpec(memory_space=pl.ANY), pl.BlockSpec(memory_space=pl.ANY)],
        out_specs=pl.BlockSpec((1,) + q4.shape[1:], lambda i, a, c: (i, 0, 0, 0)),
        scratch_shapes=[pltpu.VMEM((topk * KV_HEADS, ATT_HEAD_DIM), F32),
                        pltpu.VMEM((topk * KV_HEADS, ATT_HEAD_DIM), F32),
                        pltpu.SemaphoreType.DMA((2,))],
    )
    return pl.pallas_call(
        functools.partial(_attn_sample_kernel, topk, past_len, page_rows, n_pages),
        grid_spec=gs,
        out_shape=jax.ShapeDtypeStruct(q4.shape, BF16),
        compiler_params=_cparams(("arbitrary",)),
    )(idx, page_table, q4, nw3, ck2, cv2, kn2, vn2)


def _row(v, width=None):
    v = v.reshape(1, -1)
    return v if width is None else _pad_cols(v, width)


def _layer_params(p):
    d = p["w_in"].shape[0]
    wr = jnp.concatenate([p["w_router_e"], p["w_router_g"]], axis=1)
    br = jnp.concatenate([p["b_router_e"], p["b_router_g"]])
    return dict(
        w_perm=_perm_w_in(p["w_in"]),
        w_out_b=p["w_out"].astype(BF16),
        wr=_pad_cols(wr, LANES).astype(BF16),
        br=_row(br, LANES),
        nw1=_row(p["norm1_w"]), nw2=_row(p["norm2_w"]),
        lnw=_row(p["ln_kidx_w"], LANES), lnb=_row(p["ln_kidx_b"], LANES),
        dt_bias=_row(p["dt_bias"], LANES), a_log=_row(p["a_log"], LANES),
        dskip=_row(jnp.repeat(p["d_skip"], SSD_HEAD_DIM)),
        norm_ssd=_row(p["norm_ssd_w"]), norm_att=_row(p["norm_att_w"]),
        conv_w=p["conv_w"], conv_b=_row(p["conv_b"]),
        d_ssd=d // 2,
    )


def _row_tile(t):
    return 512 if t % 512 == 0 else 256


def _route(x, ya, yb, mod, lp, tm):
    return _out_proj(ya, yb, lp["w_out_b"], x, mod[2], lp["nw2"], mod[4], mod[3], lp["wr"], lp["br"], tm)


def _moe_and_norm(routed_p, routed_s, g2_p, g2_s, p, nf):
    x1p, h2p, eidp, wtsp, cntp = routed_p
    x1s, h2s, eids, wtss, cnts = routed_s
    tp, ts = x1p.shape[0], x1s.shape[0]
    tt = tp + ts
    tpos = TOKEN_TILE
    tt_pad = -(-tt // tpos) * tpos
    eid_all = jnp.concatenate([eidp, eids, jnp.full((tt_pad - tt, LANES), -1, I32)])
    n_tiles = -(-(2 * tt + N_EXPERTS * (MOE_TILE - 1)) // MOE_TILE)
    off_row, plan = _moe_plan(cntp, cnts, n_tiles)
    pos_flat = _moe_positions(eid_all, off_row, tpos)[:, :2].reshape(-1)
    ys = _moe_grouped(plan, pos_flat, h2p, h2s, p["w_exp_up"], p["w_exp_down"])
    out_p = _combine(pos_flat, x1p, wtsp, g2_p, nf, ys, tok0=0, tm=TOKEN_TILE)
    out_s = _combine(pos_flat, x1s, wtss, g2_s, nf, ys, tok0=tp, tm=ts)
    return out_p, out_s


def _prompt_layer(x, mod, lp, p):
    t, d = x.shape
    pr = _in_proj(x, lp["nw1"], mod[1], mod[0], lp["w_perm"], lp["lnw"], lp["lnb"], tm=_row_tile(t))
    y_ssd, st = _ssd_prompt(pr["xbc"], pr["dt"], pr["z"], lp["conv_w"], lp["conv_b"], lp["dt_bias"], lp["a_log"],
                            lp["dskip"], lp["norm_ssd"])
    ki = pr["ki"]
    kit = ki.T.astype(BF16)
    zeros = jnp.zeros_like(kit)
    ki2t = jnp.stack([jnp.concatenate([kit, zeros], axis=0), jnp.concatenate([zeros, kit], axis=0)])
    topk = min(TOPK_MAX, t // 4)
    v3 = pr["vb"].reshape(t, KV_HEADS, ATT_HEAD_DIM)
    vx = jnp.concatenate([v3, jnp.ones_like(v3)], axis=-1).reshape(t, 2 * KV_HEADS * ATT_HEAD_DIM)
    y_att = _attn_prompt(pr["q"], pr["qi"], pr["wi"], ki2t, pr["kb"].T, vx, lp["norm_att"], topk, tq=ATTN_TILE)
    routed = _route(x, y_ssd, y_att, mod, lp, tm=_row_tile(t))
    conv_new = jnp.concatenate([jnp.zeros((CONV_W - 1, pr["xbc"].shape[1]), F32), pr["xbc"]])[-(CONV_W - 1):]
    return routed, (pr["k"], pr["v"], ki, conv_new, st)


def _sample_layer(x, mod, lp, p, cache_k, cache_v, cache_ki, conv_prev, ssm_prev, page_table):
    b, d = x.shape
    d_ssd = lp["d_ssd"]
    heads = d_ssd // SSD_HEAD_DIM
    gn = SSD_GROUPS * D_STATE
    pr = _in_proj(x, lp["nw1"], mod[1], mod[0], lp["w_perm"], lp["lnw"], lp["lnb"], tm=b)
    expand = (jnp.arange(LANES)[:, None] == (jnp.arange(d_ssd)[None, :] // SSD_HEAD_DIM)).astype(F32)
    xc, xdt, dec = _ssd_prep(pr["xbc"], conv_prev[:, 0], conv_prev[:, 1], conv_prev[:, 2], lp["conv_w"], lp["conv_b"],
                             pr["dt"], lp["dt_bias"], lp["a_log"], expand)
    xs, bm, cm = xc[:, :d_ssd], xc[:, d_ssd:d_ssd + gn], xc[:, d_ssd + gn:]
    st5 = ssm_prev.reshape((1, b, heads, SSD_HEAD_DIM, D_STATE))
    st_new, y = _ssd_step(xdt, dec, bm, cm, st5)
    y_ssd = _ssd_finish(y, xs, pr["z"], lp["dskip"], lp["norm_ssd"])
    conv_new = jnp.concatenate([conv_prev[:, 1:], pr["xbc"][:, None, :]], axis=1)
    n_pool, page_rows = cache_k.shape[0], cache_k.shape[1]
    n_pages = page_table.shape[1]
    past_len = n_pages * page_rows
    topk = min(TOPK_MAX, (past_len + 1) // 4)
    qi3 = pr["qi"].reshape(b, IDX_HEADS, IDX_DIM)
    wi3 = pr["wi"][:, :IDX_HEADS].reshape(b, IDX_HEADS, 1)
    eye = jnp.eye(PAGE_PACK, dtype=BF16)
    q8 = (eye[None, :, None, :, None] * qi3[:, None, :, None, :]).reshape(b, PAGE_PACK * IDX_HEADS,
                                                                         PAGE_PACK * IDX_DIM)
    w8 = jnp.tile(wi3, (1, PAGE_PACK, 1))
    tail_rows = -(n_pages + 1) % LANES + 1
    cache_kit = jnp.swapaxes(cache_ki, -1, -2)[None]
    s3 = _score_sample(page_table, q8, w8, qi3, wi3, pr["ki"].reshape(b, 1, IDX_DIM), cache_kit, tail_rows)
    thr, cut = _threshold_sample(s3.reshape(b, -1), topk)
    idx = _select_sample(s3, thr, cut, topk).reshape(b, topk)
    n_heads = pr["q"].shape[1] // ATT_HEAD_DIM
    q_per_kv = n_heads // KV_HEADS
    q4 = jnp.pad(pr["q"].reshape(b, KV_HEADS, q_per_kv, ATT_HEAD_DIM), ((0, 0), (0, 0), (0, q_per_kv), (0, 0)))
    nw3 = jnp.pad(lp["norm_att"].reshape(KV_HEADS, q_per_kv, ATT_HEAD_DIM), ((0, 0), (0, q_per_kv), (0, 0)))
    ck2 = cache_k.reshape(n_pool * page_rows * KV_HEADS, ATT_HEAD_DIM)
    cv2 = cache_v.reshape(n_pool * page_rows * KV_HEADS, ATT_HEAD_DIM)
    kn2 = pr["k"].reshape(b * KV_HEADS, ATT_HEAD_DIM)
    vn2 = pr["v"].reshape(b * KV_HEADS, ATT_HEAD_DIM)
    o4 = _attn_sample(idx, page_table, q4, nw3, ck2, cv2, kn2, vn2, past_len, page_rows)
    y_att = o4[:, :, :q_per_kv].reshape(b, n_heads * ATT_HEAD_DIM)
    routed = _route(x, y_ssd, y_att, mod, lp, tm=b)
    return routed, (pr["k"], pr["v"], pr["ki"], conv_new, st_new.reshape(ssm_prev.shape))


def kernel(x_prompt, x_sample, cache_k, cache_v, cache_k_idx, state_conv, state_ssm, page_table, c_prompt, c_sample, w_ada, b_ada, norm1_w, norm2_w, w_in, conv_w, conv_b, dt_bias, a_log, d_skip, norm_ssd_w, ln_kidx_w, ln_kidx_b, norm_att_w, w_out, w_router_g, b_router_g, w_router_e, b_router_e, w_exp_up, w_exp_down, norm_f_w):
    batch, seq, d = x_prompt.shape
    dec_batch, dec_seq, _ = x_sample.shape
    assert batch == 1 and dec_seq == 1, "one prompt sequence and one new token per sample sequence"
    depth = w_ada.shape[0]
    heads = (d // 2) // SSD_HEAD_DIM
    xp = x_prompt.reshape(seq, d)
    xs = x_sample.reshape(dec_batch, d)
    n_c = batch + dec_batch
    c_all = jnp.pad(jnp.concatenate([c_prompt, c_sample]), ((0, -n_c % 8), (0, 0)))
    nf = _row(norm_f_w)
    outs_p, outs_s = [], []
    yp = ys = None
    for l in range(depth):
        p = dict(w_in=w_in[l], conv_w=conv_w[l], conv_b=conv_b[l], dt_bias=dt_bias[l], a_log=a_log[l],
                 d_skip=d_skip[l], norm_ssd_w=norm_ssd_w[l], ln_kidx_w=ln_kidx_w[l], ln_kidx_b=ln_kidx_b[l],
                 norm_att_w=norm_att_w[l], w_out=w_out[l], w_router_g=w_router_g[l], b_router_g=b_router_g[l],
                 w_router_e=w_router_e[l], b_router_e=b_router_e[l], w_exp_up=w_exp_up[l],
                 w_exp_down=w_exp_down[l], norm1_w=norm1_w[l], norm2_w=norm2_w[l])
        lp = _layer_params(p)
        mod = _ada_mod(c_all, w_ada[l], b_ada[l])
        mod_p = [mod[0:1, k * d:(k + 1) * d] for k in range(6)]
        mod_s = [mod[batch:n_c, k * d:(k + 1) * d] for k in range(6)]
        routed_p, st_p = _prompt_layer(xp, mod_p, lp, p)
        routed_s, st_s = _sample_layer(xs, mod_s, lp, p, cache_k[l], cache_v[l], cache_k_idx[l], state_conv[l],
                                       state_ssm[l], page_table)
        (xp, yp), (xs, ys) = _moe_and_norm(routed_p, routed_s, mod_p[5], mod_s[5], p, nf)
        outs_p.append(st_p)
        outs_s.append(st_s)

    def stack(outs, n_rows, lead):
        k = jnp.stack([o[0].reshape(lead + (n_rows, KV_HEADS, ATT_HEAD_DIM)) for o in outs])
        v = jnp.stack([o[1].reshape(lead + (n_rows, KV_HEADS, ATT_HEAD_DIM)) for o in outs])
        ki = jnp.stack([o[2].reshape(lead + (n_rows, IDX_DIM)) for o in outs])
        return k, v, ki

    k_p, v_p, ki_p = stack(outs_p, seq, (batch,))
    conv_p = jnp.stack([o[3][None] for o in outs_p])
    ssm_p = jnp.stack([o[4].reshape(batch, heads, SSD_HEAD_DIM, D_STATE) for o in outs_p])
    k_s = jnp.stack([o[0].reshape(dec_batch, dec_seq, KV_HEADS, ATT_HEAD_DIM) for o in outs_s])
    v_s = jnp.stack([o[1].reshape(dec_batch, dec_seq, KV_HEADS, ATT_HEAD_DIM) for o in outs_s])
    ki_s = jnp.stack([o[2].reshape(dec_batch, dec_seq, IDX_DIM) for o in outs_s])
    conv_s = jnp.stack([o[3] for o in outs_s])
    ssm_s = jnp.stack([o[4] for o in outs_s])
    return (yp.reshape(batch, seq, d), ys.reshape(dec_batch, dec_seq, d), k_p, v_p, ki_p, conv_p, ssm_p,
            k_s, v_s, ki_s, conv_s, ssm_s)
```
